```python
import jax, jax.numpy as jnp
from jax import lax
import numpy as np

D_MODEL = 1024
BATCH = 16
SEQ = 4096
DEPTH = 1

HEAD_DIM = 64
N_HEADS_SB = 8
N_HEADS_FOX = 8
D_SB = N_HEADS_SB * HEAD_DIM
D_FOX = N_HEADS_FOX * HEAD_DIM
D_FF = 2816
CONV_WIDTH = 3
Q_BLOCK = 128
LN_EPS = 1e-5
DEEPNORM_ALPHA = (2.0 * DEPTH) ** 0.25
DEEPNORM_BETA = (8.0 * DEPTH) ** -0.25
FORGET_BIAS_MEAN = 3.0

SPLIT_SIZES = (D_SB, D_SB, D_SB, D_FOX, D_FOX, D_FOX, N_HEADS_FOX, D_MODEL, D_MODEL)
SPLIT_POINTS = tuple(int(v) for v in np.cumsum(SPLIT_SIZES)[:-1])
N_IN = int(sum(SPLIT_SIZES))

kernel_name = 'hybrid_stickbreak_fox_convglu_deepnorm'


def layer_norm(x, g, b):
    xf = x.astype(jnp.float32)
    mean = jnp.mean(xf, axis=-1, keepdims=True)
    var = jnp.mean(jnp.square(xf - mean), axis=-1, keepdims=True)
    y = (xf - mean) * lax.rsqrt(var + LN_EPS)
    return (y * g.astype(jnp.float32) + b.astype(jnp.float32)).astype(x.dtype)


def split_heads(t, n_heads):
    b, s, _ = t.shape
    return t.reshape(b, s, n_heads, HEAD_DIM).transpose(0, 2, 1, 3)


def merge_heads(t):
    b, h, s, d = t.shape
    return t.transpose(0, 2, 1, 3).reshape(b, s, h * d)


def stick_breaking_attention(q, k, v):
    seq = q.shape[2]
    scale = HEAD_DIM ** -0.5
    outs = []
    for i in range(seq // Q_BLOCK):
        end = (i + 1) * Q_BLOCK
        qb = q[:, :, i * Q_BLOCK:end]
        kb, vb = k[:, :, :end], v[:, :, :end]
        z = jnp.einsum('bhqd,bhkd->bhqk', qb, kb).astype(jnp.float32) * scale
        q_pos = jnp.arange(i * Q_BLOCK, end)
        k_pos = jnp.arange(end)
        strict = k_pos[None, :] < q_pos[:, None]
        log_beta = jax.nn.log_sigmoid(z)
        log_one_minus = jnp.where(strict, jax.nn.log_sigmoid(-z), 0.0)
        suffix = lax.cumsum(log_one_minus, axis=3, reverse=True) - log_one_minus
        weights = jnp.where(strict, jnp.exp(log_beta + suffix), 0.0)
        outs.append(jnp.einsum('bhqk,bhkd->bhqd', weights.astype(vb.dtype), vb))
    return jnp.concatenate(outs, axis=2)


def forgetting_attention(q, k, v, cum_log_f):
    seq = q.shape[2]
    scale = HEAD_DIM ** -0.5
    outs = []
    for i in range(seq // Q_BLOCK):
        end = (i + 1) * Q_BLOCK
        qb = q[:, :, i * Q_BLOCK:end]
        kb, vb = k[:, :, :end], v[:, :, :end]
        c_q = cum_log_f[:, :, i * Q_BLOCK:end]
        c_k = cum_log_f[:, :, :end]
        z = jnp.einsum('bhqd,bhkd->bhqk', qb, kb).astype(jnp.float32) * scale
        z = z + c_q[..., :, None] - c_k[..., None, :]
        q_pos = jnp.arange(i * Q_BLOCK, end)
        k_pos = jnp.arange(end)
        causal = k_pos[None, :] <= q_pos[:, None]
        p = jax.nn.softmax(jnp.where(causal, z, -jnp.inf), axis=-1)
        outs.append(jnp.einsum('bhqk,bhkd->bhqd', p.astype(vb.dtype), vb))
    return jnp.concatenate(outs, axis=2)


def causal_depthwise_conv(u, w_conv, b_conv):
    seq = u.shape[1]
    u_pad = jnp.pad(u, ((0, 0), (CONV_WIDTH - 1, 0), (0, 0)))
    y = b_conv
    for tap in range(CONV_WIDTH):
        y = y + w_conv[tap] * u_pad[:, tap:tap + seq]
    return y


def _fwd_setup_inputs(seed: int = 0) -> dict:
    key = jax.random.key(seed)
    ks = jax.random.split(key, 16)
    f32 = jnp.float32
    x = jax.random.normal(ks[0], (BATCH, SEQ, D_MODEL), f32)
    col_scale = np.ones((N_IN,), np.float32)
    bounds = (0,) + SPLIT_POINTS + (N_IN,)
    for idx in (2, 5):
        col_scale[bounds[idx]:bounds[idx + 1]] = DEEPNORM_BETA
    bias_offset = np.zeros((N_IN,), np.float32)
    bias_offset[bounds[6]:bounds[7]] = FORGET_BIAS_MEAN
    w_in = jax.random.normal(ks[1], (DEPTH, D_MODEL, N_IN), f32) * D_MODEL ** -0.5 * jnp.asarray(col_scale)
    b_in = 0.02 * jax.random.normal(ks[2], (DEPTH, N_IN), f32) + jnp.asarray(bias_offset)
    w_proj_sb = jax.random.normal(ks[3], (DEPTH, D_SB, D_MODEL), f32) * D_SB ** -0.5 * DEEPNORM_BETA
    w_proj_fox = jax.random.normal(ks[4], (DEPTH, D_FOX, D_MODEL), f32) * D_FOX ** -0.5 * DEEPNORM_BETA
    w_out = jax.random.normal(ks[5], (DEPTH, D_MODEL, D_MODEL), f32) * D_MODEL ** -0.5 * DEEPNORM_BETA
    ln1_g = 1.0 + 0.02 * jax.random.normal(ks[6], (DEPTH, D_MODEL), f32)
    ln1_b = 0.02 * jax.random.normal(ks[7], (DEPTH, D_MODEL), f32)
    w_up = jax.random.normal(ks[8], (DEPTH, D_MODEL, 2 * D_FF), f32) * D_MODEL ** -0.5
    w_conv = jax.random.normal(ks[9], (DEPTH, CONV_WIDTH, D_FF), f32) * CONV_WIDTH ** -0.5
    b_conv = 0.02 * jax.random.normal(ks[10], (DEPTH, D_FF), f32)
    w_down = jax.random.normal(ks[11], (DEPTH, D_FF, D_MODEL), f32) * D_FF ** -0.5 * DEEPNORM_BETA
    ln2_g = 1.0 + 0.02 * jax.random.normal(ks[12], (DEPTH, D_MODEL), f32)
    ln2_b = 0.02 * jax.random.normal(ks[13], (DEPTH, D_MODEL), f32)
    return {'x': x, 'w_in': w_in, 'b_in': b_in, 'w_proj_sb': w_proj_sb,
            'w_proj_fox': w_proj_fox, 'w_out': w_out, 'ln1_g': ln1_g, 'ln1_b': ln1_b,
            'w_up': w_up, 'w_conv': w_conv, 'b_conv': b_conv, 'w_down': w_down,
            'ln2_g': ln2_g, 'ln2_b': ln2_b}


def _fwd_reference(x, w_in, b_in, w_proj_sb, w_proj_fox, w_out, ln1_g, ln1_b,
              w_up, w_conv, b_conv, w_down, ln2_g, ln2_b):
    for l in range(DEPTH):
        h = jnp.einsum('bsd,dn->bsn', x, w_in[l]) + b_in[l]
        q_sb, k_sb, v_sb, q_fx, k_fx, v_fx, f_logit, g_sb, g_fx = jnp.split(h, SPLIT_POINTS, axis=-1)
        o_sb = stick_breaking_attention(split_heads(q_sb, N_HEADS_SB),
                                        split_heads(k_sb, N_HEADS_SB),
                                        split_heads(v_sb, N_HEADS_SB))
        cum_log_f = lax.cumsum(jax.nn.log_sigmoid(f_logit.astype(jnp.float32)), axis=1)
        o_fx = forgetting_attention(split_heads(q_fx, N_HEADS_FOX),
                                    split_heads(k_fx, N_HEADS_FOX),
                                    split_heads(v_fx, N_HEADS_FOX),
                                    cum_log_f.transpose(0, 2, 1))
        y_sb = jnp.einsum('bse,ed->bsd', merge_heads(o_sb), w_proj_sb[l])
        y_fx = jnp.einsum('bse,ed->bsd', merge_heads(o_fx), w_proj_fox[l])
        merged = jax.nn.sigmoid(g_sb) * y_sb + jax.nn.sigmoid(g_fx) * y_fx
        mix = jnp.einsum('bsd,de->bse', merged, w_out[l])
        x = layer_norm(DEEPNORM_ALPHA * x + mix, ln1_g[l], ln1_b[l])
        u = jnp.einsum('bsd,df->bsf', x, w_up[l])
        u_gate, u_val = jnp.split(u, 2, axis=-1)
        a = jax.nn.gelu(causal_depthwise_conv(u_gate, w_conv[l], b_conv[l]), approximate=False) * u_val
        ffn = jnp.einsum('bsf,fd->bsd', a, w_down[l])
        x = layer_norm(DEEPNORM_ALPHA * x + ffn, ln2_g[l], ln2_b[l])
    return x


import jax as _jax
import jax.numpy as _jnp

TWIN_FORMAT = 'train_step'
FWD_PARAMS = ['x', 'w_in', 'b_in', 'w_proj_sb', 'w_proj_fox', 'w_out', 'ln1_g', 'ln1_b', 'w_up', 'w_conv', 'b_conv', 'w_down', 'ln2_g', 'ln2_b']
TWIN_WEIGHTS = ['w_in', 'b_in', 'w_proj_sb', 'w_proj_fox', 'w_out', 'ln1_g', 'ln1_b', 'w_up', 'w_conv', 'b_conv', 'w_down', 'ln2_g', 'ln2_b']
TWIN_DIFF_INPUT = 'x'
TWIN_INPUTS = ['x', 'w_in', 'b_in', 'w_proj_sb', 'w_proj_fox', 'w_out', 'ln1_g', 'ln1_b', 'w_up', 'w_conv', 'b_conv', 'w_down', 'ln2_g', 'ln2_b', 'loss_target', 'm_w_in', 'm_b_in', 'm_w_proj_sb', 'm_w_proj_fox', 'm_w_out', 'm_ln1_g', 'm_ln1_b', 'm_w_up', 'm_w_conv', 'm_b_conv', 'm_w_down', 'm_ln2_g', 'm_ln2_b', 'v_w_in', 'v_b_in', 'v_w_proj_sb', 'v_w_proj_fox', 'v_w_out', 'v_ln1_g', 'v_ln1_b', 'v_w_up', 'v_w_conv', 'v_b_conv', 'v_w_down', 'v_ln2_g', 'v_ln2_b']
TWIN_OUTPUTS = ['loss', 'grad_x', 'grad_w_in', 'grad_b_in', 'grad_w_proj_sb', 'grad_w_proj_fox', 'grad_w_out', 'grad_ln1_g', 'grad_ln1_b', 'grad_w_up', 'grad_w_conv', 'grad_b_conv', 'grad_w_down', 'grad_ln2_g', 'grad_ln2_b', 'delta_w_in', 'delta_b_in', 'delta_w_proj_sb', 'delta_w_proj_fox', 'delta_w_out', 'delta_ln1_g', 'delta_ln1_b', 'delta_w_up', 'delta_w_conv', 'delta_b_conv', 'delta_w_down', 'delta_ln2_g', 'delta_ln2_b', 'new_m_w_in', 'new_m_b_in', 'new_m_w_proj_sb', 'new_m_w_proj_fox', 'new_m_w_out', 'new_m_ln1_g', 'new_m_ln1_b', 'new_m_w_up', 'new_m_w_conv', 'new_m_b_conv', 'new_m_w_down', 'new_m_ln2_g', 'new_m_ln2_b', 'new_v_w_in', 'new_v_b_in', 'new_v_w_proj_sb', 'new_v_w_proj_fox', 'new_v_w_out', 'new_v_ln1_g', 'new_v_ln1_b', 'new_v_w_up', 'new_v_w_conv', 'new_v_b_conv', 'new_v_w_down', 'new_v_ln2_g', 'new_v_ln2_b']
TWIN_LEAF_KINDS = {'loss': 'loss', 'grad_x': 'grad_x', 'grad_w_in': 'grad_w', 'grad_b_in': 'grad_w', 'grad_w_proj_sb': 'grad_w', 'grad_w_proj_fox': 'grad_w', 'grad_w_out': 'grad_w', 'grad_ln1_g': 'grad_w', 'grad_ln1_b': 'grad_w', 'grad_w_up': 'grad_w', 'grad_w_conv': 'grad_w', 'grad_b_conv': 'grad_w', 'grad_w_down': 'grad_w', 'grad_ln2_g': 'grad_w', 'grad_ln2_b': 'grad_w', 'delta_w_in': 'delta_w', 'delta_b_in': 'delta_w', 'delta_w_proj_sb': 'delta_w', 'delta_w_proj_fox': 'delta_w', 'delta_w_out': 'delta_w', 'delta_ln1_g': 'delta_w', 'delta_ln1_b': 'delta_w', 'delta_w_up': 'delta_w', 'delta_w_conv': 'delta_w', 'delta_b_conv': 'delta_w', 'delta_w_down': 'delta_w', 'delta_ln2_g': 'delta_w', 'delta_ln2_b': 'delta_w', 'new_m_w_in': 'new_m', 'new_m_b_in': 'new_m', 'new_m_w_proj_sb': 'new_m', 'new_m_w_proj_fox': 'new_m', 'new_m_w_out': 'new_m', 'new_m_ln1_g': 'new_m', 'new_m_ln1_b': 'new_m', 'new_m_w_up': 'new_m', 'new_m_w_conv': 'new_m', 'new_m_b_conv': 'new_m', 'new_m_w_down': 'new_m', 'new_m_ln2_g': 'new_m', 'new_m_ln2_b': 'new_m', 'new_v_w_in': 'new_v', 'new_v_b_in': 'new_v', 'new_v_w_proj_sb': 'new_v', 'new_v_w_proj_fox': 'new_v', 'new_v_w_out': 'new_v', 'new_v_ln1_g': 'new_v', 'new_v_ln1_b': 'new_v', 'new_v_w_up': 'new_v', 'new_v_w_conv': 'new_v', 'new_v_b_conv': 'new_v', 'new_v_w_down': 'new_v', 'new_v_ln2_g': 'new_v', 'new_v_ln2_b': 'new_v'}


def _forward(args):
    return _fwd_reference(*[args[k] for k in FWD_PARAMS])


def _output_shape():
    out = _jax.eval_shape(lambda: _forward(_fwd_setup_inputs(0)))
    return out.shape, out.dtype

N_MICROBATCH = 1
ADAM_LR = 0.001
ADAM_B1 = 0.9
ADAM_B2 = 0.999
ADAM_EPS = 1e-08
ADAM_WD = 0.01
ADAM_STEP = 10
PER_EXAMPLE_BATCH_AXIS = {'x': 0, 'loss_target': 0}
SHARED_INPUTS = []
_WEIGHT_DTYPES = {'w_in': _jnp.float32, 'b_in': _jnp.float32, 'w_proj_sb': _jnp.float32, 'w_proj_fox': _jnp.float32, 'w_out': _jnp.float32, 'ln1_g': _jnp.float32, 'ln1_b': _jnp.float32, 'w_up': _jnp.float32, 'w_conv': _jnp.float32, 'b_conv': _jnp.float32, 'w_down': _jnp.float32, 'ln2_g': _jnp.float32, 'ln2_b': _jnp.float32}
MOMENT_SCALE = {'w_in': 1.522376e-02, 'b_in': 7.508679e-02, 'w_proj_sb': 2.753039e-02, 'w_proj_fox': 1.342779e-02, 'w_out': 3.034131e-02, 'ln1_g': 1.609804e+00, 'ln1_b': 7.416240e-01, 'w_up': 4.830315e-02, 'w_conv': 4.917636e-02, 'b_conv': 4.600346e-02, 'w_down': 1.326507e-01, 'ln2_g': 6.400419e+01, 'ln2_b': 1.075044e+00}


def _to_microbatches(a, axis):
    t = _jnp.moveaxis(a, axis, 0)
    t = t.reshape((N_MICROBATCH, t.shape[0] // N_MICROBATCH) + t.shape[1:])
    return _jnp.moveaxis(t, 1, axis + 1)


def setup_inputs(seed: int = 0) -> dict:
    inp = _fwd_setup_inputs(seed)
    key = _jax.random.fold_in(_jax.random.key(seed), 7919)
    shape, _ = _output_shape()
    out = dict(inp)
    out["loss_target"] = _jax.random.normal(_jax.random.fold_in(key, 0), shape, _jnp.float32)
    for i, name in enumerate(TWIN_WEIGHTS):
        w = inp[name].astype(_jnp.float32)
        if MOMENT_SCALE is None:
            s = _jnp.sqrt(_jnp.mean(_jnp.square(w)) + 1e-30)
        else:
            s = MOMENT_SCALE[name]
        km, kv = _jax.random.split(_jax.random.fold_in(key, i + 1))
        out[name] = w
        out["m_" + name] = s * _jax.random.normal(km, w.shape, _jnp.float32)
        out["v_" + name] = (s * s) * _jax.random.uniform(kv, w.shape, _jnp.float32, 0.5, 1.5)
    if N_MICROBATCH > 1:
        for name, axis in PER_EXAMPLE_BATCH_AXIS.items():
            out[name] = _to_microbatches(out[name], axis)
    return {'x': out['x'], 'w_in': out['w_in'], 'b_in': out['b_in'], 'w_proj_sb': out['w_proj_sb'], 'w_proj_fox': out['w_proj_fox'], 'w_out': out['w_out'], 'ln1_g': out['ln1_g'], 'ln1_b': out['ln1_b'], 'w_up': out['w_up'], 'w_conv': out['w_conv'], 'b_conv': out['b_conv'], 'w_down': out['w_down'], 'ln2_g': out['ln2_g'], 'ln2_b': out['ln2_b'], 'loss_target': out['loss_target'], 'm_w_in': out['m_w_in'], 'm_b_in': out['m_b_in'], 'm_w_proj_sb': out['m_w_proj_sb'], 'm_w_proj_fox': out['m_w_proj_fox'], 'm_w_out': out['m_w_out'], 'm_ln1_g': out['m_ln1_g'], 'm_ln1_b': out['m_ln1_b'], 'm_w_up': out['m_w_up'], 'm_w_conv': out['m_w_conv'], 'm_b_conv': out['m_b_conv'], 'm_w_down': out['m_w_down'], 'm_ln2_g': out['m_ln2_g'], 'm_ln2_b': out['m_ln2_b'], 'v_w_in': out['v_w_in'], 'v_b_in': out['v_b_in'], 'v_w_proj_sb': out['v_w_proj_sb'], 'v_w_proj_fox': out['v_w_proj_fox'], 'v_w_out': out['v_w_out'], 'v_ln1_g': out['v_ln1_g'], 'v_ln1_b': out['v_ln1_b'], 'v_w_up': out['v_w_up'], 'v_w_conv': out['v_w_conv'], 'v_b_conv': out['v_b_conv'], 'v_w_down': out['v_w_down'], 'v_ln2_g': out['v_ln2_g'], 'v_ln2_b': out['v_ln2_b']}


def _loss(weights, diff, rest, loss_target):
    with _jax.named_scope("forward"):
        args = {**rest, TWIN_DIFF_INPUT: diff, **{k: w.astype(_WEIGHT_DTYPES[k]) for k, w in weights.items()}}
        y = _forward(args)
    with _jax.named_scope("loss_head"):
        err = _jnp.square(y.astype(_jnp.float32) - loss_target)
        return 0.5 * _jnp.sum(_jnp.mean(err, axis=-1)) if err.ndim else 0.5 * err


def _adamw(w, g, m, v):
    m = ADAM_B1 * m + (1.0 - ADAM_B1) * g
    v = ADAM_B2 * v + (1.0 - ADAM_B2) * _jnp.square(g)
    m_hat = m / (1.0 - ADAM_B1 ** ADAM_STEP)
    v_hat = v / (1.0 - ADAM_B2 ** ADAM_STEP)
    delta = -ADAM_LR * (m_hat / (_jnp.sqrt(v_hat) + ADAM_EPS) + ADAM_WD * w)
    return delta, m, v


def reference(x, w_in, b_in, w_proj_sb, w_proj_fox, w_out, ln1_g, ln1_b, w_up, w_conv, b_conv, w_down, ln2_g, ln2_b, loss_target, m_w_in, m_b_in, m_w_proj_sb, m_w_proj_fox, m_w_out, m_ln1_g, m_ln1_b, m_w_up, m_w_conv, m_b_conv, m_w_down, m_ln2_g, m_ln2_b, v_w_in, v_b_in, v_w_proj_sb, v_w_proj_fox, v_w_out, v_ln1_g, v_ln1_b, v_w_up, v_w_conv, v_b_conv, v_w_down, v_ln2_g, v_ln2_b):
    given = dict(x=x, w_in=w_in, b_in=b_in, w_proj_sb=w_proj_sb, w_proj_fox=w_proj_fox, w_out=w_out, ln1_g=ln1_g, ln1_b=ln1_b, w_up=w_up, w_conv=w_conv, b_conv=b_conv, w_down=w_down, ln2_g=ln2_g, ln2_b=ln2_b, loss_target=loss_target, m_w_in=m_w_in, m_b_in=m_b_in, m_w_proj_sb=m_w_proj_sb, m_w_proj_fox=m_w_proj_fox, m_w_out=m_w_out, m_ln1_g=m_ln1_g, m_ln1_b=m_ln1_b, m_w_up=m_w_up, m_w_conv=m_w_conv, m_b_conv=m_b_conv, m_w_down=m_w_down, m_ln2_g=m_ln2_g, m_ln2_b=m_ln2_b, v_w_in=v_w_in, v_b_in=v_b_in, v_w_proj_sb=v_w_proj_sb, v_w_proj_fox=v_w_proj_fox, v_w_out=v_w_out, v_ln1_g=v_ln1_g, v_ln1_b=v_ln1_b, v_w_up=v_w_up, v_w_conv=v_w_conv, v_b_conv=v_b_conv, v_w_down=v_w_down, v_ln2_g=v_ln2_g, v_ln2_b=v_ln2_b)
    weights = {n: given[n] for n in TWIN_WEIGHTS}
    shared = {n: given[n] for n in SHARED_INPUTS}
    per_example = {n: given[n] for n in ['x']}
    grad_fn = _jax.value_and_grad(_loss, argnums=(0, 1))

    def one_microbatch(ex, loss_target):
        ex = dict(ex)
        diff = ex.pop(TWIN_DIFF_INPUT)
        return grad_fn(weights, diff, {**shared, **ex}, loss_target)

    if N_MICROBATCH == 1:
        loss, (grad_w, grad_x) = one_microbatch(per_example, given["loss_target"])
    else:
        def body(carry, xs):
            loss_sum, grad_sum = carry
            l_k, (gw_k, gx_k) = one_microbatch(xs[0], xs[1])
            with _jax.named_scope("update"):
                return (loss_sum + l_k, _jax.tree.map(_jnp.add, grad_sum, gw_k)), gx_k

        init = (_jnp.zeros((), _jnp.float32), _jax.tree.map(_jnp.zeros_like, weights))
        (loss, grad_w), grad_x = _jax.lax.scan(body, init, (per_example, given["loss_target"]))
    with _jax.named_scope("update"):
        delta_w, new_m, new_v = {}, {}, {}
        for n in TWIN_WEIGHTS:
            delta_w[n], new_m[n], new_v[n] = _adamw(weights[n], grad_w[n], given["m_" + n], given["v_" + n])
    return (loss, grad_x, *[grad_w[n] for n in TWIN_WEIGHTS], *[delta_w[n] for n in TWIN_WEIGHTS],
            *[new_m[n] for n in TWIN_WEIGHTS], *[new_v[n] for n in TWIN_WEIGHTS])
```

```python
import functools
import math

import jax
import jax.numpy as jnp
from jax import lax
from jax.experimental import pallas as pl
from jax.experimental.pallas import tpu as pltpu

f32, bf16 = jnp.float32, jnp.bfloat16

HEAD_DIM = 64
LANES = 128
LN_EPS = 1e-5
ALPHA = 2.0 ** 0.25
Q_SCALE = HEAD_DIM ** -0.5
ADAM_LR, ADAM_B1, ADAM_B2, ADAM_EPS, ADAM_WD, ADAM_STEP = 0.001, 0.9, 0.999, 1e-08, 0.01, 10
VMEM_LIMIT = 56 * 1024 * 1024
NEG = -1e30

_pcall = pl.pallas_call
_NT = (((1,), (1,)), ((), ()))
_TN = (((0,), (0,)), ((), ()))


def _params(sem=None):
    return pltpu.CompilerParams(dimension_semantics=sem, vmem_limit_bytes=VMEM_LIMIT)


def _tile(dim, target, unit=LANES):
    if dim <= target:
        return dim
    t = (target // unit) * unit
    while t > unit and dim % t:
        t -= unit
    assert dim % t == 0, (dim, target)
    return t


def _dot(a, b, dn=None):
    if dn is None:
        return jnp.dot(a, b, preferred_element_type=f32)
    return lax.dot_general(a, b, dn, preferred_element_type=f32)


def _split_dot(x, tri):
    hi = x.astype(bf16)
    lo = (x - hi.astype(f32)).astype(bf16)
    return _dot(hi, tri) + _dot(lo, tri)


def _matmul(a, b, *, name, ta=False, tb=False, bias=None, addend=None, addend_scale=1.0, colsum=False,
            out_dtype=f32, tm=512, tn=512, tk=1024):
    M, K = (a.shape[1], a.shape[0]) if ta else a.shape
    N = b.shape[0] if tb else b.shape[1]
    assert K == (b.shape[1] if tb else b.shape[0])
    assert not (colsum and tb)
    tm, tn, tk = _tile(M, tm), _tile(N, tn), _tile(K, tk)
    nk = K // tk
    n_in = 2 + (bias is not None) + (addend is not None)

    def body(*refs):
        a_ref, b_ref = refs[0], refs[1]
        bias_ref = refs[2] if bias is not None else None
        add_ref = refs[n_in - 1] if addend is not None else None
        o_ref = refs[n_in]
        cs_ref = refs[n_in + 1] if colsum else None
        acc = refs[-2] if colsum else refs[-1]
        cs_acc = refs[-1] if colsum else None
        k = pl.program_id(2)

        @pl.when(k == 0)
        def _():
            acc[...] = jnp.zeros_like(acc)
            if colsum:
                cs_acc[...] = jnp.zeros_like(cs_acc)

        dn = (((0 if ta else 1,), (1 if tb else 0,)), ((), ()))
        acc[...] += lax.dot_general(a_ref[...].astype(bf16), b_ref[...].astype(bf16), dn, preferred_element_type=f32)
        if colsum:
            cs_acc[...] += jnp.sum(b_ref[...].astype(f32), axis=0, keepdims=True)

        @pl.when(k == nk - 1)
        def _():
            r = acc[...]
            if bias is not None:
                r = r + bias_ref[...]
            if addend is not None:
                r = r + addend_scale * add_ref[...].astype(f32)
            o_ref[...] = r.astype(out_dtype)
            if colsum:
                cs_ref[0] = cs_acc[...]

    a_spec = pl.BlockSpec((tk, tm), lambda i, j, k: (k, i)) if ta else pl.BlockSpec((tm, tk), lambda i, j, k: (i, k))
    b_spec = pl.BlockSpec((tn, tk), lambda i, j, k: (j, k)) if tb else pl.BlockSpec((tk, tn), lambda i, j, k: (k, j))
    in_specs, args = [a_spec, b_spec], [a, b]
    if bias is not None:
        in_specs.append(pl.BlockSpec((1, tn), lambda i, j, k: (0, j)))
        args.append(bias.reshape(1, N).astype(f32))
    if addend is not None:
        in_specs.append(pl.BlockSpec((tm, tn), lambda i, j, k: (i, j)))
        args.append(addend)
    out_shape = [jax.ShapeDtypeStruct((M, N), out_dtype)]
    out_specs = [pl.BlockSpec((tm, tn), lambda i, j, k: (i, j))]
    scratch = [pltpu.VMEM((tm, tn), f32)]
    if colsum:
        out_shape.append(jax.ShapeDtypeStruct((M // tm, 1, N), f32))
        out_specs.append(pl.BlockSpec((1, 1, tn), lambda i, j, k: (i, 0, j)))
        scratch.append(pltpu.VMEM((1, tn), f32))
    res = _pcall(body, name=name, grid=(M // tm, N // tn, nk), in_specs=in_specs, out_specs=out_specs,
                 out_shape=out_shape, scratch_shapes=scratch,
                 compiler_params=_params(("arbitrary", "arbitrary", "arbitrary")))(*args)
    return (res[0], res[1][0]) if colsum else res[0]


def _cumlogf(fl, B, S):
    t = _tile(S, 256, 8)

    def body(fl_ref, c_ref, carry):
        @pl.when(pl.program_id(1) == 0)
        def _():
            carry[...] = jnp.zeros_like(carry)
        z = fl_ref[...]
        ls = jnp.minimum(z, 0.0) - jnp.log(1.0 + jnp.exp(-jnp.abs(z)))
        row = lax.broadcasted_iota(jnp.int32, (t, t), 0)
        col = lax.broadcasted_iota(jnp.int32, (t, t), 1)
        lower = (col <= row).astype(f32)
        c = jnp.dot(lower, ls, precision=lax.Precision.HIGHEST, preferred_element_type=f32) + carry[...]
        c_ref[...] = c
        carry[...] = c[t - 1:t, :]

    return _pcall(body, name="cumlogf", grid=(B, S // t),
                  in_specs=[pl.BlockSpec((t, LANES), lambda b, i: (b * (S // t) + i, 0))],
                  out_specs=pl.BlockSpec((t, LANES), lambda b, i: (b * (S // t) + i, 0)),
                  out_shape=jax.ShapeDtypeStruct(fl.shape, f32), scratch_shapes=[pltpu.VMEM((1, LANES), f32)],
                  compiler_params=_params(("arbitrary", "arbitrary")))(fl)


def _cumlogf_bwd(dc, fl, B, S):
    t = _tile(S, 256, 8)
    n = S // t

    def body(dc_ref, fl_ref, o_ref, carry):
        @pl.when(pl.program_id(1) == 0)
        def _():
            carry[...] = jnp.zeros_like(carry)
        row = lax.broadcasted_iota(jnp.int32, (t, t), 0)
        col = lax.broadcasted_iota(jnp.int32, (t, t), 1)
        upper = (col >= row).astype(f32)
        r = jnp.dot(upper, dc_ref[...], precision=lax.Precision.HIGHEST, preferred_element_type=f32) + carry[...]
        carry[...] = r[0:1, :]
        z = fl_ref[...]
        o_ref[...] = r / (1.0 + jnp.exp(z))

    spec = pl.BlockSpec((t, LANES), lambda b, i: (b * n + n - 1 - i, 0))
    return _pcall(body, name="cumlogf_bwd", grid=(B, n), in_specs=[spec, spec], out_specs=spec,
                  out_shape=jax.ShapeDtypeStruct(fl.shape, f32), scratch_shapes=[pltpu.VMEM((1, LANES), f32)],
                  compiler_params=_params(("arbitrary", "arbitrary")))(dc, fl)


def _head_masks():
    lane = lax.broadcasted_iota(jnp.int32, (1, LANES), 1)
    return lane < HEAD_DIM


def _by_head(m0, t):
    z = jnp.zeros_like(t)
    return [jnp.where(m0, t, z), jnp.where(m0, z, t)]


def _sb_terms(z):
    relu = jnp.maximum(z, 0.0)
    sp = jnp.log(1.0 + jnp.exp(-jnp.abs(z)))
    return (z - relu) - sp, -relu - sp


def _sb_fwd(qkv, B, S, n_pairs, qcol, kcol, vcol, tq):
    nq = S // tq
    T = B * S

    def body(q_ref, k_ref, v_ref, o_ref, lt_ref):
        i = pl.program_id(2)
        m0 = _head_masks()
        qh = _by_head(m0, q_ref[...])
        row = lax.broadcasted_iota(jnp.int32, (tq, tq), 0)
        col = lax.broadcasted_iota(jnp.int32, (tq, tq), 1)
        strict = col < row
        later = (row > col).astype(bf16)

        def tile(s0, R, acc, diag):
            kb = k_ref[pl.ds(s0, tq), :]
            vh = _by_head(m0, v_ref[pl.ds(s0, tq), :])
            R = list(R)
            for h in range(2):
                z = _dot(qh[h], kb, _NT)
                lb, l1m = _sb_terms(z)
                if diag:
                    l1m = jnp.where(strict, l1m, 0.0)
                suf = _split_dot(l1m, later) + R[h]
                w = jnp.exp(lb + suf)
                if diag:
                    w = jnp.where(strict, w, 0.0)
                acc = acc + _dot(w.astype(bf16), vh[h])
                R[h] = R[h] + jnp.sum(l1m, axis=1, keepdims=True)
            return R, acc

        zero = jnp.zeros((tq, 1), f32)
        R, acc = tile(pl.multiple_of(i * tq, tq), [zero, zero], jnp.zeros((tq, LANES), f32), True)

        def loop(n, carry):
            s0 = pl.multiple_of((i - 1 - n) * tq, tq)
            R, acc = tile(s0, carry[:2], carry[2], False)
            return R[0], R[1], acc

        R0, R1, acc = lax.fori_loop(0, i, loop, (R[0], R[1], acc))
        o_ref[...] = acc.astype(bf16)
        lt_ref[...] = jnp.where(m0, R0, R1)

    qs = lambda c: pl.BlockSpec((tq, LANES), lambda b, p, i: (b * nq + i, c + p))
    ks = lambda c: pl.BlockSpec((S, LANES), lambda b, p, i: (b, c + p))
    os_ = pl.BlockSpec((tq, LANES), lambda b, p, i: (b * nq + i, p))
    return _pcall(body, name="sb_fwd", grid=(B, n_pairs, nq), in_specs=[qs(qcol), ks(kcol), ks(vcol)],
                  out_specs=[os_, os_],
                  out_shape=[jax.ShapeDtypeStruct((T, n_pairs * LANES), bf16), jax.ShapeDtypeStruct((T, n_pairs * LANES), f32)],
                  compiler_params=_params(("arbitrary", "arbitrary", "arbitrary")))(qkv, qkv, qkv)


def _sb_bwd(qkv, do, o, lt, B, S, n_pairs, qcol, kcol, vcol, tq):
    nq = S // tq
    T = B * S

    def body(q_ref, k_ref, v_ref, do_ref, o_ref, lt_ref, dq_ref, dk_ref, dv_ref):
        i = pl.program_id(2)

        @pl.when(i == 0)
        def _():
            dk_ref[...] = jnp.zeros_like(dk_ref)
            dv_ref[...] = jnp.zeros_like(dv_ref)

        m0 = _head_masks()
        qh = _by_head(m0, q_ref[...])
        do2 = do_ref[...]
        doh = _by_head(m0, do2)
        prod = do2.astype(f32) * o_ref[...].astype(f32)
        delta = [jnp.sum(p, axis=1, keepdims=True) for p in _by_head(m0, prod)]
        lt = lt_ref[...]
        ltot = [lt[:, 0:1], lt[:, HEAD_DIM:HEAD_DIM + 1]]
        row = lax.broadcasted_iota(jnp.int32, (tq, tq), 0)
        col = lax.broadcasted_iota(jnp.int32, (tq, tq), 1)
        strict = col < row
        upto = (row <= col).astype(bf16)
        before = (row < col).astype(bf16)

        def tile(s0, CL, CP, dq, diag):
            kb = k_ref[pl.ds(s0, tq), :]
            vb = v_ref[pl.ds(s0, tq), :]
            kh = _by_head(m0, kb)
            CL, CP = list(CL), list(CP)
            dk = jnp.zeros((tq, LANES), f32)
            dv = jnp.zeros((tq, LANES), f32)
            for h in range(2):
                z = _dot(qh[h], kb, _NT)
                lb, l1m = _sb_terms(z)
                if diag:
                    l1m = jnp.where(strict, l1m, 0.0)
                suf = ltot[h] - CL[h] - _split_dot(l1m, upto)
                w = jnp.exp(lb + suf)
                if diag:
                    w = jnp.where(strict, w, 0.0)
                g = _dot(doh[h], vb, _NT) * w
                p = CP[h] + _split_dot(g, before)
                dz = g - jnp.exp(lb) * (g + p)
                if diag:
                    dz = jnp.where(strict, dz, 0.0)
                dzb = dz.astype(bf16)
                dq = dq + _dot(dzb, kh[h])
                dk = dk + _dot(dzb, qh[h], _TN)
                dv = dv + _dot(w.astype(bf16), doh[h], _TN)
                CL[h] = CL[h] + jnp.sum(l1m, axis=1, keepdims=True)
                CP[h] = CP[h] + jnp.sum(g, axis=1, keepdims=True)
            dk_ref[pl.ds(s0, tq), :] += dk
            dv_ref[pl.ds(s0, tq), :] += dv
            return CL, CP, dq

        zero = jnp.zeros((tq, 1), f32)

        def loop(n, carry):
            CL, CP, dq = tile(pl.multiple_of(n * tq, tq), carry[0:2], carry[2:4], carry[4], False)
            return CL[0], CL[1], CP[0], CP[1], dq

        c = lax.fori_loop(0, i, loop, (zero, zero, zero, zero, jnp.zeros((tq, LANES), f32)))
        _, _, dq = tile(pl.multiple_of(i * tq, tq), c[0:2], c[2:4], c[4], True)
        dq_ref[...] = dq * Q_SCALE

    qs = lambda c: pl.BlockSpec((tq, LANES), lambda b, p, i: (b * nq + i, c + p))
    ks = lambda c: pl.BlockSpec((S, LANES), lambda b, p, i: (b, c + p))
    ts = pl.BlockSpec((tq, LANES), lambda b, p, i: (b * nq + i, p))
    fs = pl.BlockSpec((S, LANES), lambda b, p, i: (b, p))
    shp = jax.ShapeDtypeStruct((T, n_pairs * LANES), f32)
    return _pcall(body, name="sb_bwd", grid=(B, n_pairs, nq),
                  in_specs=[qs(qcol), ks(kcol), ks(vcol), ts, ts, ts], out_specs=[ts, fs, fs], out_shape=[shp, shp, shp],
                  compiler_params=_params(("arbitrary", "arbitrary", "arbitrary")))(qkv, qkv, qkv, do, o, lt)


def _fox_fwd(qkv, c, cT, B, S, n_pairs, qcol, kcol, vcol, tq):
    nq = S // tq
    T = B * S

    def body(q_ref, k_ref, v_ref, cq_ref, ck_ref, o_ref, lse_ref):
        p_idx = pl.program_id(1)
        i = pl.program_id(2)
        m0 = _head_masks()
        lane = lax.broadcasted_iota(jnp.int32, (1, LANES), 1)
        qh = _by_head(m0, q_ref[...])
        cq_all = cq_ref[...]
        cq = [jnp.sum(jnp.where(lane == 2 * p_idx + h, cq_all, 0.0), axis=1, keepdims=True) for h in range(2)]
        row = lax.broadcasted_iota(jnp.int32, (tq, tq), 0)
        col = lax.broadcasted_iota(jnp.int32, (tq, tq), 1)
        causal = col <= row

        def tile(s0, m, l, acc, diag):
            kb = k_ref[pl.ds(s0, tq), :]
            vh = _by_head(m0, v_ref[pl.ds(s0, tq), :])
            m, l = list(m), list(l)
            scale, add = [], []
            for h in range(2):
                z = _dot(qh[h], kb, _NT) + (cq[h] - ck_ref[h, :, pl.ds(s0, tq)])
                if diag:
                    z = jnp.where(causal, z, NEG)
                m_new = jnp.maximum(m[h], jnp.max(z, axis=1, keepdims=True))
                p = jnp.exp(z - m_new)
                a = jnp.exp(m[h] - m_new)
                l[h] = a * l[h] + jnp.sum(p, axis=1, keepdims=True)
                m[h] = m_new
                scale.append(a)
                add.append(_dot(p.astype(bf16), vh[h]))
            acc = acc * jnp.where(m0, scale[0], scale[1]) + add[0] + add[1]
            return m, l, acc

        neg = jnp.full((tq, 1), NEG, f32)
        zero = jnp.zeros((tq, 1), f32)
        m, l, acc = tile(pl.multiple_of(i * tq, tq), [neg, neg], [zero, zero], jnp.zeros((tq, LANES), f32), True)

        def loop(n, carry):
            m, l, acc = tile(pl.multiple_of(n * tq, tq), carry[0:2], carry[2:4], carry[4], False)
            return m[0], m[1], l[0], l[1], acc

        m0_, m1_, l0, l1, acc = lax.fori_loop(0, i, loop, (m[0], m[1], l[0], l[1], acc))
        o_ref[...] = (acc * jnp.where(m0, 1.0 / l0, 1.0 / l1)).astype(bf16)
        lse_ref[...] = jnp.where(m0, m0_ + jnp.log(l0), m1_ + jnp.log(l1))

    qs = lambda cc: pl.BlockSpec((tq, LANES), lambda b, p, i: (b * nq + i, cc + p))
    ks = lambda cc: pl.BlockSpec((S, LANES), lambda b, p, i: (b, cc + p))
    cqs = pl.BlockSpec((tq, LANES), lambda b, p, i: (b * nq + i, 0))
    cks = pl.BlockSpec((2, 1, S), lambda b, p, i: (b * n_pairs + p, 0, 0))
    os_ = pl.BlockSpec((tq, LANES), lambda b, p, i: (b * nq + i, p))
    return _pcall(body, name="fox_fwd", grid=(B, n_pairs, nq), in_specs=[qs(qcol), ks(kcol), ks(vcol), cqs, cks],
                  out_specs=[os_, os_],
                  out_shape=[jax.ShapeDtypeStruct((T, n_pairs * LANES), bf16), jax.ShapeDtypeStruct((T, n_pairs * LANES), f32)],
                  compiler_params=_params(("arbitrary", "arbitrary", "arbitrary")))(qkv, qkv, qkv, c, cT)


def _fox_bwd(qkv, c, cT, do, o, lse, B, S, n_pairs, qcol, kcol, vcol, tq):
    nq = S // tq
    T = B * S

    def body(q_ref, k_ref, v_ref, cq_ref, ck_ref, do_ref, o_ref, lse_ref, dq_ref, dk_ref, dv_ref, dc_ref):
        p_idx = pl.program_id(1)
        i = pl.program_id(2)

        @pl.when(i == 0)
        def _():
            dk_ref[...] = jnp.zeros_like(dk_ref)
            dv_ref[...] = jnp.zeros_like(dv_ref)
            dc_ref[...] = jnp.zeros_like(dc_ref)

        m0 = _head_masks()
        lane = lax.broadcasted_iota(jnp.int32, (1, LANES), 1)
        qh = _by_head(m0, q_ref[...])
        do2 = do_ref[...]
        doh = _by_head(m0, do2)
        prod = do2.astype(f32) * o_ref[...].astype(f32)
        delta = [jnp.sum(p, axis=1, keepdims=True) for p in _by_head(m0, prod)]
        ls = lse_ref[...]
        lse = [ls[:, 0:1], ls[:, HEAD_DIM:HEAD_DIM + 1]]
        cq_all = cq_ref[...]
        cq = [jnp.sum(jnp.where(lane == 2 * p_idx + h, cq_all, 0.0), axis=1, keepdims=True) for h in range(2)]
        row = lax.broadcasted_iota(jnp.int32, (tq, tq), 0)
        col = lax.broadcasted_iota(jnp.int32, (tq, tq), 1)
        causal = col <= row

        def tile(s0, dq, diag):
            kb = k_ref[pl.ds(s0, tq), :]
            vb = v_ref[pl.ds(s0, tq), :]
            kh = _by_head(m0, kb)
            dk = jnp.zeros((tq, LANES), f32)
            dv = jnp.zeros((tq, LANES), f32)
            for h in range(2):
                z = _dot(qh[h], kb, _NT) + (cq[h] - ck_ref[h, :, pl.ds(s0, tq)])
                p = jnp.exp(z - lse[h])
                if diag:
                    p = jnp.where(causal, p, 0.0)
                ds = p * (_dot(doh[h], vb, _NT) - delta[h])
                dsb = ds.astype(bf16)
                dq = dq + _dot(dsb, kh[h])
                dk = dk + _dot(dsb, qh[h], _TN)
                dv = dv + _dot(p.astype(bf16), doh[h], _TN)
                dc_ref[h, :, pl.ds(s0, tq)] -= jnp.sum(ds, axis=0, keepdims=True)
            dk_ref[pl.ds(s0, tq), :] += dk
            dv_ref[pl.ds(s0, tq), :] += dv
            return dq

        dq = lax.fori_loop(0, i, lambda n, dq: tile(pl.multiple_of(n * tq, tq), dq, False), jnp.zeros((tq, LANES), f32))
        dq = tile(pl.multiple_of(i * tq, tq), dq, True)
        dq_ref[...] = dq * Q_SCALE

    qs = lambda cc: pl.BlockSpec((tq, LANES), lambda b, p, i: (b * nq + i, cc + p))
    ks = lambda cc: pl.BlockSpec((S, LANES), lambda b, p, i: (b, cc + p))
    cqs = pl.BlockSpec((tq, LANES), lambda b, p, i: (b * nq + i, 0))
    cks = pl.BlockSpec((2, 1, S), lambda b, p, i: (b * n_pairs + p, 0, 0))
    ts = pl.BlockSpec((tq, LANES), lambda b, p, i: (b * nq + i, p))
    fs = pl.BlockSpec((S, LANES), lambda b, p, i: (b, p))
    shp = jax.ShapeDtypeStruct((T, n_pairs * LANES), f32)
    return _pcall(body, name="fox_bwd", grid=(B, n_pairs, nq),
                  in_specs=[qs(qcol), ks(kcol), ks(vcol), cqs, cks, ts, ts, ts], out_specs=[ts, fs, fs, cks],
                  out_shape=[shp, shp, shp, jax.ShapeDtypeStruct(cT.shape, f32)],
                  compiler_params=_params(("arbitrary", "arbitrary", "arbitrary")))(qkv, qkv, qkv, c, cT, do, o, lse)


def _sigmoid(x):
    return 1.0 / (1.0 + jnp.exp(-x))


def _mix_fwd(o_sb, o_fx, g, x, wp_sb, wp_fx, w_out, ln_g, ln_b, tm):
    T, D = x.shape
    E = o_sb.shape[1]
    tm = _tile(T, tm, 8)

    def body(osb_ref, ofx_ref, gsb_ref, gfx_ref, x_ref, wsb_ref, wfx_ref, wo_ref, lg_ref, lb_ref,
             xhat_ref, rstd_ref, x1_ref, mg_ref):
        y_sb = _dot(osb_ref[...], wsb_ref[...])
        y_fx = _dot(ofx_ref[...], wfx_ref[...])
        merged = (_sigmoid(gsb_ref[...]) * y_sb + _sigmoid(gfx_ref[...]) * y_fx).astype(bf16)
        r = ALPHA * x_ref[...] + _dot(merged, wo_ref[...])
        mean = jnp.mean(r, axis=1, keepdims=True)
        cen = r - mean
        rstd = lax.rsqrt(jnp.mean(cen * cen, axis=1, keepdims=True) + LN_EPS)
        xhat = cen * rstd
        xhat_ref[...] = xhat
        rstd_ref[...] = rstd
        x1_ref[...] = (xhat * lg_ref[...] + lb_ref[...]).astype(bf16)
        mg_ref[...] = merged

    rows = lambda w, c=0: pl.BlockSpec((tm, w), lambda i: (i, c))
    full = lambda a: pl.BlockSpec(a.shape, lambda i: (0, 0))
    return _pcall(body, name="mix_fwd", grid=(T // tm,),
                  in_specs=[rows(E), rows(E), rows(D, 0), rows(D, 1), rows(D), full(wp_sb), full(wp_fx), full(w_out),
                            full(ln_g), full(ln_b)],
                  out_specs=[rows(D), rows(1), rows(D), rows(D)],
                  out_shape=[jax.ShapeDtypeStruct((T, D), f32), jax.ShapeDtypeStruct((T, 1), f32),
                             jax.ShapeDtypeStruct((T, D), bf16), jax.ShapeDtypeStruct((T, D), bf16)],
                  compiler_params=_params(("arbitrary",)))(o_sb, o_fx, g, g, x, wp_sb, wp_fx, w_out, ln_g, ln_b)


def _mix_bwd(dr1, o_sb, o_fx, g, wp_sb, wp_fx, w_out, tm):
    T, D = dr1.shape
    E = o_sb.shape[1]
    tm = _tile(T, tm, 8)

    def body(dr_ref, osb_ref, ofx_ref, gsb_ref, gfx_ref, wsb_ref, wfx_ref, wo_ref,
             dysb_ref, dyfx_ref, dgsb_ref, dgfx_ref, dosb_ref, dofx_ref):
        dm = _dot(dr_ref[...].astype(bf16), wo_ref[...], _NT)
        for o_ref, g_ref, w_ref, dy_ref, dg_ref, do_ref in (
                (osb_ref, gsb_ref, wsb_ref, dysb_ref, dgsb_ref, dosb_ref),
                (ofx_ref, gfx_ref, wfx_ref, dyfx_ref, dgfx_ref, dofx_ref)):
            y = _dot(o_ref[...], w_ref[...])
            s = _sigmoid(g_ref[...])
            dy = (dm * s).astype(bf16)
            dy_ref[...] = dy
            dg_ref[...] = (dm * y * s * (1.0 - s)).astype(bf16)
            do_ref[...] = _dot(dy, w_ref[...], _NT).astype(bf16)

    rows = lambda w, c=0: pl.BlockSpec((tm, w), lambda i: (i, c))
    full = lambda a: pl.BlockSpec(a.shape, lambda i: (0, 0))
    res = _pcall(body, name="mix_bwd", grid=(T // tm,),
                 in_specs=[rows(D), rows(E), rows(E), rows(D, 0), rows(D, 1), full(wp_sb), full(wp_fx), full(w_out)],
                 out_specs=[rows(D), rows(D), rows(D), rows(D), rows(E), rows(E)],
                 out_shape=[jax.ShapeDtypeStruct((T, D), bf16)] * 4 + [jax.ShapeDtypeStruct((T, E), bf16)] * 2,
                 compiler_params=_params(("arbitrary",)))(dr1, o_sb, o_fx, g, g, wp_sb, wp_fx, w_out)
    return res


def _ln_bwd(dy_a, dy_b, scale_b, xhat, rstd, ln_g, tm):
    T, D = xhat.shape
    tm = _tile(T, tm, 8)

    def body(a_ref, b_ref, xh_ref, rs_ref, g_ref, dr_ref, st_ref):
        @pl.when(pl.program_id(0) == 0)
        def _():
            st_ref[...] = jnp.zeros_like(st_ref)
        dy = a_ref[...] + scale_b * b_ref[...]
        xh = xh_ref[...]
        dxh = dy * g_ref[...]
        m1 = jnp.mean(dxh, axis=1, keepdims=True)
        m2 = jnp.mean(dxh * xh, axis=1, keepdims=True)
        dr_ref[...] = rs_ref[...] * (dxh - m1 - xh * m2)
        st_ref[0:1, :] += jnp.sum(dy * xh, axis=0, keepdims=True)
        st_ref[1:2, :] += jnp.sum(dy, axis=0, keepdims=True)

    rows = lambda w: pl.BlockSpec((tm, w), lambda i: (i, 0))
    return _pcall(body, name="ln1_bwd", grid=(T // tm,),
                  in_specs=[rows(D), rows(D), rows(D), rows(1), pl.BlockSpec((1, D), lambda i: (0, 0))],
                  out_specs=[rows(D), pl.BlockSpec((8, D), lambda i: (0, 0))],
                  out_shape=[jax.ShapeDtypeStruct((T, D), f32), jax.ShapeDtypeStruct((8, D), f32)],
                  compiler_params=_params(("arbitrary",)))(dy_a, dy_b, xhat, rstd, ln_g)


_INV_SQRT2 = 1.0 / math.sqrt(2.0)
_INV_SQRT2PI = 1.0 / math.sqrt(2.0 * math.pi)


def _conv_rows(ref, r0, rc, first, wc, bc):
    cur = ref[pl.ds(r0, rc), :]
    prev = ref[pl.ds(pl.multiple_of(jnp.maximum(r0 - 8, 0), 8), 8), :]
    prev = jnp.where(first, jnp.zeros_like(prev), prev)
    rid = lax.broadcasted_iota(jnp.int32, (rc, LANES), 0)
    s1 = jnp.where(rid == 0, prev[7:8, :], pltpu.roll(cur, 1, 0))
    s2 = jnp.where(rid == 0, prev[6:7, :], jnp.where(rid == 1, prev[7:8, :], pltpu.roll(cur, 2, 0)))
    conv = bc + wc[0:1, :] * s2 + wc[1:2, :] * s1 + wc[2:3, :] * cur
    return conv, (s2, s1, cur)


def _glu_fwd(u, w_conv, b_conv, B, S, rc=512):
    F = u.shape[1] // 2
    nf = F // LANES
    rc = _tile(S, rc, 8)

    def body(ug_ref, uv_ref, wc_ref, bc_ref, a_ref):
        wc, bc = wc_ref[...], bc_ref[...]

        def chunk(n, _):
            r0 = pl.multiple_of(n * rc, rc)
            c, _taps = _conv_rows(ug_ref, r0, rc, n == 0, wc, bc)
            gelu = 0.5 * c * (1.0 + lax.erf(c * _INV_SQRT2))
            a_ref[pl.ds(r0, rc), :] = (gelu * uv_ref[pl.ds(r0, rc), :]).astype(bf16)
            return 0

        lax.fori_loop(0, S // rc, chunk, 0)

    return _pcall(body, name="glu_fwd", grid=(B, nf),
                  in_specs=[pl.BlockSpec((S, LANES), lambda b, j: (b, j)), pl.BlockSpec((S, LANES), lambda b, j: (b, nf + j)),
                            pl.BlockSpec((3, LANES), lambda b, j: (0, j)), pl.BlockSpec((1, LANES), lambda b, j: (0, j))],
                  out_specs=pl.BlockSpec((S, LANES), lambda b, j: (b, j)),
                  out_shape=jax.ShapeDtypeStruct((B * S, F), bf16),
                  compiler_params=_params(("arbitrary", "arbitrary")))(u, u, w_conv, b_conv)


def _glu_bwd(u, da, w_conv, b_conv, B, S, rc=512):
    F = u.shape[1] // 2
    nf = F // LANES
    rc = _tile(S, rc, 8)
    nc = S // rc

    def body(ug_ref, uv_ref, da_ref, wc_ref, bc_ref, dug_ref, duv_ref, gw_ref, gb_ref, dc_ref):
        wc, bc = wc_ref[...], bc_ref[...]

        def chunk(n, carry):
            gw0, gw1, gw2, gb = carry
            r0 = pl.multiple_of(n * rc, rc)
            c, (s2, s1, cur) = _conv_rows(ug_ref, r0, rc, n == 0, wc, bc)
            cdf = 0.5 * (1.0 + lax.erf(c * _INV_SQRT2))
            da = da_ref[pl.ds(r0, rc), :]
            duv_ref[pl.ds(r0, rc), :] = (da * (c * cdf)).astype(bf16)
            dc = da * uv_ref[pl.ds(r0, rc), :] * (cdf + c * (_INV_SQRT2PI * jnp.exp(-0.5 * c * c)))
            dc_ref[pl.ds(r0, rc), :] = dc
            red = lambda t: jnp.sum(t, axis=0, keepdims=True)
            return gw0 + red(dc * s2), gw1 + red(dc * s1), gw2 + red(dc * cur), gb + red(dc)

        z = jnp.zeros((1, LANES), f32)
        gw0, gw1, gw2, gb = lax.fori_loop(0, nc, chunk, (z, z, z, z))
        gw_ref[0, 0:1, :] = gw0
        gw_ref[0, 1:2, :] = gw1
        gw_ref[0, 2:3, :] = gw2
        gb_ref[0] = gb

        def chunk2(n, _):
            r0 = pl.multiple_of(n * rc, rc)
            cur = dc_ref[pl.ds(r0, rc), :]
            nxt = dc_ref[pl.ds(pl.multiple_of(jnp.minimum(r0 + rc, S - 8), 8), 8), :]
            nxt = jnp.where(n == nc - 1, jnp.zeros_like(nxt), nxt)
            rid = lax.broadcasted_iota(jnp.int32, (rc, LANES), 0)
            a1 = jnp.where(rid == rc - 1, nxt[0:1, :], pltpu.roll(cur, rc - 1, 0))
            a2 = jnp.where(rid == rc - 1, nxt[1:2, :], jnp.where(rid == rc - 2, nxt[0:1, :], pltpu.roll(cur, rc - 2, 0)))
            dug_ref[pl.ds(r0, rc), :] = (wc[2:3, :] * cur + wc[1:2, :] * a1 + wc[0:1, :] * a2).astype(bf16)
            return 0

        lax.fori_loop(0, nc, chunk2, 0)

    blk = lambda off: pl.BlockSpec((S, LANES), lambda b, j: (b, off + j))
    return _pcall(body, name="glu_bwd", grid=(B, nf),
                  in_specs=[blk(0), blk(nf), blk(0), pl.BlockSpec((3, LANES), lambda b, j: (0, j)),
                            pl.BlockSpec((1, LANES), lambda b, j: (0, j))],
                  out_specs=[blk(0), blk(0), pl.BlockSpec((1, 3, LANES), lambda b, j: (b, 0, j)),
                             pl.BlockSpec((1, 1, LANES), lambda b, j: (b, 0, j))],
                  out_shape=[jax.ShapeDtypeStruct((B * S, F), bf16), jax.ShapeDtypeStruct((B * S, F), bf16),
                             jax.ShapeDtypeStruct((B, 3, F), f32), jax.ShapeDtypeStruct((B, 1, F), f32)],
                  scratch_shapes=[pltpu.VMEM((S, LANES), f32)],
                  compiler_params=_params(("arbitrary", "arbitrary")))(u, u, da, w_conv, b_conv)


def _down_loss(a, w_down, xhat1, ln1_g, ln1_b, ln2_g, ln2_b, tgt, tm):
    T, D = xhat1.shape
    F = a.shape[1]
    tm = _tile(T, tm, 8)

    def body(a_ref, w_ref, xh_ref, g1_ref, b1_ref, g2_ref, b2_ref, t_ref, dr_ref, st_ref):
        @pl.when(pl.program_id(0) == 0)
        def _():
            st_ref[...] = jnp.zeros_like(st_ref)
        x1 = xh_ref[...] * g1_ref[...] + b1_ref[...]
        r = ALPHA * x1 + _dot(a_ref[...], w_ref[...])
        mean = jnp.mean(r, axis=1, keepdims=True)
        cen = r - mean
        rstd = lax.rsqrt(jnp.mean(cen * cen, axis=1, keepdims=True) + LN_EPS)
        xh = cen * rstd
        err = (xh * g2_ref[...] + b2_ref[...]) - t_ref[...]
        dy = err * (1.0 / D)
        dxh = dy * g2_ref[...]
        m1 = jnp.mean(dxh, axis=1, keepdims=True)
        m2 = jnp.mean(dxh * xh, axis=1, keepdims=True)
        dr_ref[...] = rstd * (dxh - m1 - xh * m2)
        st_ref[0:1, :] += jnp.sum(dy * xh, axis=0, keepdims=True)
        st_ref[1:2, :] += jnp.sum(dy, axis=0, keepdims=True)
        st_ref[2:3, :] += jnp.sum(err * err, axis=0, keepdims=True)

    rows = lambda w: pl.BlockSpec((tm, w), lambda i: (i, 0))
    vec = pl.BlockSpec((1, D), lambda i: (0, 0))
    return _pcall(body, name="down_loss", grid=(T // tm,),
                  in_specs=[rows(F), pl.BlockSpec((F, D), lambda i: (0, 0)), rows(D), vec, vec, vec, vec, rows(D)],
                  out_specs=[rows(D), pl.BlockSpec((8, D), lambda i: (0, 0))],
                  out_shape=[jax.ShapeDtypeStruct((T, D), f32), jax.ShapeDtypeStruct((8, D), f32)],
                  compiler_params=_params(("arbitrary",)))(a, w_down, xhat1, ln1_g, ln1_b, ln2_g, ln2_b, tgt)


def _local_step(x, tgt, w_in, b_in, wp_sb, wp_fx, w_out, ln1_g, ln1_b, w_up, w_conv, b_conv, w_down, ln2_g, ln2_b,
                tq=256, tm=256):
    B, S, D = x.shape
    T = B * S
    E = wp_sb.shape[0]
    n_pairs = E // LANES
    NH = E // HEAD_DIM
    F = w_down.shape[0]
    x2 = x.reshape(T, D)
    tgt2 = tgt.reshape(T, D)
    tq = _tile(S, tq, 8)

    c_f, c_g = 6 * E, 6 * E + NH
    w_qkv = w_in[:, :c_f]
    qscale = jnp.concatenate([jnp.full((E,), Q_SCALE, f32), jnp.ones((2 * E,), f32)] * 2)
    w_qkv_s = (w_qkv.astype(f32) * qscale).astype(bf16)
    b_qkv_s = b_in[:, :c_f] * qscale
    w_f = jnp.pad(w_in[:, c_f:c_g], ((0, 0), (0, LANES - NH)))
    b_f = jnp.pad(b_in[:, c_f:c_g], ((0, 0), (0, LANES - NH)))
    w_g = w_in[:, c_g:]
    b_g = b_in[:, c_g:]

    xb = x2.astype(bf16)
    qkv = _matmul(xb, w_qkv_s, bias=b_qkv_s, out_dtype=bf16, tm=1024, tn=512, name="proj_qkv")
    g = _matmul(xb, w_g, bias=b_g, tm=1024, tn=512, name="proj_gate")
    fl = _matmul(xb, w_f, bias=b_f, tm=1024, name="proj_forget")
    c = _cumlogf(fl, B, S)
    cT = c.reshape(B, S, LANES)[:, :, :NH].transpose(0, 2, 1).reshape(B * NH, 1, S)
    P = n_pairs
    o_sb, lt_sb = _sb_fwd(qkv, B, S, P, 0, P, 2 * P, tq)
    o_fx, lse_fx = _fox_fwd(qkv, c, cT, B, S, P, 3 * P, 4 * P, 5 * P, tq)
    xhat1, rstd1, x1b, merged = _mix_fwd(o_sb, o_fx, g, x2, wp_sb, wp_fx, w_out, ln1_g, ln1_b, tm)
    u = _matmul(x1b, w_up, tm=1024, tn=512, name="ffn_up")
    a = _glu_fwd(u, w_conv, b_conv, B, S)
    dr2, st2 = _down_loss(a, w_down, xhat1, ln1_g, ln1_b, ln2_g, ln2_b, tgt2, tm)

    grads = {}
    grads["ln2_g"], grads["ln2_b"] = st2[0:1], st2[1:2]
    sq_err = st2[2:3]
    da = _matmul(dr2, w_down, tb=True, tm=1024, tn=512, name="ffn_da")
    grads["w_down"] = _matmul(a, dr2, ta=True, tm=512, tn=512, tk=1024, name="grad_w_down")
    du_g, du_v, gwc, gbc = _glu_bwd(u, da, w_conv, b_conv, B, S)
    grads["w_conv"] = jnp.sum(gwc, axis=0)
    grads["b_conv"] = jnp.sum(gbc, axis=0)
    grads["w_up"] = jnp.concatenate(
        [_matmul(x1b, du_g, ta=True, tm=512, tn=512, tk=1024, name="grad_w_up_gate"),
         _matmul(x1b, du_v, ta=True, tm=512, tn=512, tk=1024, name="grad_w_up_val")], axis=1)
    dx1 = _matmul(du_g, w_up[:, :F], tb=True, tm=1024, tn=512, tk=F, name="ffn_dx_gate")
    dx1 = _matmul(du_v, w_up[:, F:], tb=True, addend=dx1, tm=1024, tn=512, tk=F, name="ffn_dx_val")
    dr1, st1 = _ln_bwd(dx1, dr2, ALPHA, xhat1, rstd1, ln1_g, tm)
    grads["ln1_g"], grads["ln1_b"] = st1[0:1], st1[1:2]

    dy_sb, dy_fx, dg_sb, dg_fx, do_sb, do_fx = _mix_bwd(dr1, o_sb, o_fx, g, wp_sb, wp_fx, w_out, tm)
    grads["w_out"] = _matmul(merged, dr1, ta=True, tm=512, tn=512, tk=1024, name="grad_w_out")
    grads["w_proj_sb"] = _matmul(o_sb, dy_sb, ta=True, tm=512, tn=512, tk=1024, name="grad_w_proj_sb")
    grads["w_proj_fox"] = _matmul(o_fx, dy_fx, ta=True, tm=512, tn=512, tk=1024, name="grad_w_proj_fox")
    dq_s, dk_s, dv_s = _sb_bwd(qkv, do_sb, o_sb, lt_sb, B, S, P, 0, P, 2 * P, tq)
    dq_f, dk_f, dv_f, dcT = _fox_bwd(qkv, c, cT, do_fx, o_fx, lse_fx, B, S, P, 3 * P, 4 * P, 5 * P, tq)
    dc = jnp.pad(dcT.reshape(B, NH, S).transpose(0, 2, 1), ((0, 0), (0, 0), (0, LANES - NH))).reshape(T, LANES)
    dfl = _cumlogf_bwd(dc, fl, B, S)
    dqkv = jnp.concatenate([t.astype(bf16) for t in (dq_s, dk_s, dv_s, dq_f, dk_f, dv_f)], axis=1)
    dg = jnp.concatenate([dg_sb, dg_fx], axis=1)
    gw_qkv, gb_qkv = _matmul(x2, dqkv, ta=True, colsum=True, tm=512, tn=512, tk=1024, name="grad_w_qkv")
    gw_f, gb_f = _matmul(x2, dfl, ta=True, colsum=True, tm=512, tk=1024, name="grad_w_forget")
    gw_g, gb_g = _matmul(x2, dg, ta=True, colsum=True, tm=512, tn=512, tk=1024, name="grad_w_gate")
    grads["w_in"] = jnp.concatenate([gw_qkv, gw_f[:, :NH], gw_g], axis=1)
    grads["b_in"] = jnp.concatenate([gb_qkv, gb_f[:, :NH], gb_g], axis=1)
    dx = _matmul(dqkv, w_qkv, tb=True, addend=dr1, addend_scale=ALPHA, tm=1024, tn=512, tk=c_f, name="dx_qkv")
    dx = _matmul(dfl, w_f, tb=True, addend=dx, tm=1024, tn=512, name="dx_forget")
    dx = _matmul(dg, w_g, tb=True, addend=dx, tm=1024, tn=512, tk=2 * D, name="dx_gate")
    return sq_err, dx.reshape(B, S, D), grads


_ANY = pl.BlockSpec(memory_space=pl.ANY)
_MESH = pl.DeviceIdType.MESH


def _pos():
    return lax.axis_index("x"), lax.axis_index("y"), lax.axis_index("c")


def _other_chips(x, y):
    return [(1 - x, y), (x, 1 - y), (1 - x, 1 - y)]


def _gather_shards(flat_w, w_conv):
    def body(w_ref, wc_ref, ow_ref, owc_ref, send_sems, recv_sems, local_sems):
        x, y, c = _pos()
        k = 2 * x + y
        pairs = ((w_ref, ow_ref), (wc_ref, owc_ref))
        local = [pltpu.make_async_copy(src, dst.at[k], local_sems.at[a]) for a, (src, dst) in enumerate(pairs)]
        for cp in local:
            cp.start()

        def copy(a, j, chip, slot):
            src, dst = pairs[a]
            return pltpu.make_async_remote_copy(src_ref=src, dst_ref=dst.at[slot], send_sem=send_sems.at[a, j],
                                                recv_sem=recv_sems.at[a, j], device_id=(*chip, c), device_id_type=_MESH)

        chips = _other_chips(x, y)
        sends = [copy(a, j, chip, k) for j, chip in enumerate(chips) for a in range(2)]
        for cp in sends:
            cp.start()
        for j, chip in enumerate(chips):
            for a in range(2):
                copy(a, j, chip, 2 * chip[0] + chip[1]).wait_recv()
        for cp in sends:
            cp.wait_send()
        for cp in local:
            cp.wait()

    return _pcall(body, name="gather_weights", in_specs=[_ANY, _ANY], out_specs=[_ANY, _ANY],
                  out_shape=[jax.ShapeDtypeStruct((4,) + flat_w.shape, flat_w.dtype),
                             jax.ShapeDtypeStruct((4,) + w_conv.shape, w_conv.dtype)],
                  scratch_shapes=[pltpu.SemaphoreType.DMA((2, 3)), pltpu.SemaphoreType.DMA((2, 3)), pltpu.SemaphoreType.DMA((2,))],
                  compiler_params=pltpu.CompilerParams(has_side_effects=True))(flat_w, w_conv)


def _exchange_sibling_and_small(theirs, small):
    def body(th_ref, sm_ref, got_ref, sg_ref, big_sems, send_sems, recv_sems, local_sem):
        x, y, c = _pos()
        me = 4 * x + 2 * y + c
        big = pltpu.make_async_remote_copy(src_ref=th_ref, dst_ref=got_ref, send_sem=big_sems.at[0], recv_sem=big_sems.at[1],
                                           device_id=(x, y, 1 - c), device_id_type=_MESH)
        big.start()
        local = pltpu.make_async_copy(sm_ref, sg_ref.at[me], local_sem)
        local.start()
        peers = []
        for r in range(1, 8):
            flip = lambda v, bit: 1 - v if bit else v
            peers.append((flip(x, r & 4), flip(y, r & 2), flip(c, r & 1)))

        def copy(j, slot):
            return pltpu.make_async_remote_copy(src_ref=sm_ref, dst_ref=sg_ref.at[slot], send_sem=send_sems.at[j],
                                                recv_sem=recv_sems.at[j], device_id=peers[j], device_id_type=_MESH)

        sends = [copy(j, me) for j in range(7)]
        for cp in sends:
            cp.start()
        for j, (px, py, pc) in enumerate(peers):
            copy(j, 4 * px + 2 * py + pc).wait_recv()
        for cp in sends:
            cp.wait_send()
        local.wait()
        big.wait()

    return _pcall(body, name="exchange_sibling", in_specs=[_ANY, _ANY], out_specs=[_ANY, _ANY],
                  out_shape=[jax.ShapeDtypeStruct(theirs.shape, theirs.dtype), jax.ShapeDtypeStruct((8,) + small.shape, small.dtype)],
                  scratch_shapes=[pltpu.SemaphoreType.DMA((2,)), pltpu.SemaphoreType.DMA((7,)), pltpu.SemaphoreType.DMA((7,)),
                                  pltpu.SemaphoreType.DMA(())],
                  compiler_params=pltpu.CompilerParams(has_side_effects=True))(theirs, small)


def _exchange_chips(pieces):
    def body(p_ref, got_ref, send_sems, recv_sems, local_sem):
        x, y, c = _pos()
        k = 2 * x + y
        local = pltpu.make_async_copy(p_ref.at[k], got_ref.at[k], local_sem)
        local.start()
        chips = _other_chips(x, y)

        def copy(j, chip, piece, slot):
            return pltpu.make_async_remote_copy(src_ref=p_ref.at[piece], dst_ref=got_ref.at[slot], send_sem=send_sems.at[j],
                                                recv_sem=recv_sems.at[j], device_id=(*chip, c), device_id_type=_MESH)

        sends = [copy(j, chip, 2 * chip[0] + chip[1], k) for j, chip in enumerate(chips)]
        for cp in sends:
            cp.start()
        for j, chip in enumerate(chips):
            copy(j, chip, k, 2 * chip[0] + chip[1]).wait_recv()
        for cp in sends:
            cp.wait_send()
        local.wait()

    return _pcall(body, name="exchange_chips", in_specs=[_ANY], out_specs=_ANY,
                  out_shape=jax.ShapeDtypeStruct(pieces.shape, pieces.dtype),
                  scratch_shapes=[pltpu.SemaphoreType.DMA((3,)), pltpu.SemaphoreType.DMA((3,)), pltpu.SemaphoreType.DMA(())],
                  compiler_params=pltpu.CompilerParams(has_side_effects=True))(pieces)


def _share_halves(half):
    def body(h_ref, full_ref, sems, local_sem):
        x, y, c = _pos()
        local = pltpu.make_async_copy(h_ref, full_ref.at[c], local_sem)
        local.start()
        send = pltpu.make_async_remote_copy(src_ref=h_ref, dst_ref=full_ref.at[c], send_sem=sems.at[0], recv_sem=sems.at[1],
                                            device_id=(x, y, 1 - c), device_id_type=_MESH)
        send.start()
        pltpu.make_async_remote_copy(src_ref=h_ref, dst_ref=full_ref.at[1 - c], send_sem=sems.at[0], recv_sem=sems.at[1],
                                     device_id=(x, y, 1 - c), device_id_type=_MESH).wait_recv()
        send.wait_send()
        local.wait()

    return _pcall(body, name="share_halves", in_specs=[_ANY], out_specs=_ANY,
                  out_shape=jax.ShapeDtypeStruct((2,) + half.shape, half.dtype),
                  scratch_shapes=[pltpu.SemaphoreType.DMA((2,)), pltpu.SemaphoreType.DMA(())],
                  compiler_params=pltpu.CompilerParams(has_side_effects=True))(half)


def _add2(a, b, name):
    n = a.shape[0] * a.shape[1] if a.ndim == 3 else a.shape[0]
    a2, b2 = a.reshape(n, LANES), b.reshape(n, LANES)
    t = _tile(n, 4096, 8)

    def body(a_ref, b_ref, o_ref):
        o_ref[...] = a_ref[...] + b_ref[...]

    spec = pl.BlockSpec((t, LANES), lambda i: (i, 0))
    return _pcall(body, name=name, grid=(n // t,), in_specs=[spec, spec], out_specs=spec,
                  out_shape=jax.ShapeDtypeStruct((n, LANES), f32), compiler_params=_params(("arbitrary",)))(a2, b2).reshape(a.shape)


def _sum_slots(stack, name):
    k, n, _ = stack.shape
    t = _tile(n, 2048, 8)

    def body(s_ref, o_ref):
        acc = s_ref[0]
        for i in range(1, k):
            acc = acc + s_ref[i]
        o_ref[...] = acc

    return _pcall(body, name=name, grid=(n // t,), in_specs=[pl.BlockSpec((k, t, LANES), lambda i: (0, i, 0))],
                  out_specs=pl.BlockSpec((t, LANES), lambda i: (i, 0)), out_shape=jax.ShapeDtypeStruct((n, LANES), f32),
                  compiler_params=_params(("arbitrary",)))(stack)


def _adamw(w, g, m, v, name):
    n = w.shape[0]
    t = _tile(n, 2048, 8)
    c1 = 1.0 - ADAM_B1 ** ADAM_STEP
    c2 = 1.0 - ADAM_B2 ** ADAM_STEP

    def body(w_ref, g_ref, m_ref, v_ref, d_ref, nm_ref, nv_ref):
        g = g_ref[...]
        nm = ADAM_B1 * m_ref[...] + (1.0 - ADAM_B1) * g
        nv = ADAM_B2 * v_ref[...] + (1.0 - ADAM_B2) * (g * g)
        d_ref[...] = -ADAM_LR * ((nm / c1) / (jnp.sqrt(nv / c2) + ADAM_EPS) + ADAM_WD * w_ref[...])
        nm_ref[...] = nm
        nv_ref[...] = nv

    spec = pl.BlockSpec((t, LANES), lambda i: (i, 0))
    shp = jax.ShapeDtypeStruct((n, LANES), f32)
    return _pcall(body, name=name, grid=(n // t,), in_specs=[spec] * 4, out_specs=[spec] * 3, out_shape=[shp] * 3,
                  compiler_params=_params(("arbitrary",)))(w, g, m, v)


_MATS = (("w_in", 1), ("w_proj_sb", 1), ("w_proj_fox", 1), ("w_out", 0), ("w_up", 1), ("w_down", 0))
_SMALL = ("b_in", "ln1_g", "ln1_b", "b_conv", "ln2_g", "ln2_b")


def _pad_lanes(v):
    n = v.shape[-1]
    return jnp.pad(v, ((0, 0), (0, (-n) % LANES)))


def _pack_rows(vectors):
    flat = jnp.concatenate([_pad_lanes(v.reshape(1, -1)) for v in vectors], axis=1).reshape(-1, LANES)
    return jnp.pad(flat, ((0, (-flat.shape[0]) % 8), (0, 0)))


def _unpack_rows(packed, sizes):
    out, r = [], 0
    for n in sizes:
        rows = -(-n // LANES)
        out.append(packed[r:r + rows].reshape(1, rows * LANES)[:, :n])
        r += rows
    return out


def _pack_shards(shards):
    return jnp.concatenate([s.reshape(2, -1) for s in shards], axis=1).reshape(-1, LANES)


def _unpack_shards(flat, shapes):
    flat2 = flat.reshape(2, -1)
    out, o = [], 0
    for r, cols in shapes:
        n = r * cols // 2
        out.append(flat2[:, o:o + n].reshape(r, cols))
        o += n
    return out


def _pack_grads(grads, shapes):
    parts = []
    for (name, axis), (r, cols) in zip(_MATS, shapes):
        g = grads[name]
        if axis == 1:
            p = g.reshape(2, r // 2, 4, cols).transpose(0, 2, 1, 3)
        else:
            p = g.reshape(4, 2, r // 2, cols).transpose(1, 0, 2, 3)
        parts.append(p.reshape(2, 4, -1))
    return jnp.concatenate(parts, axis=2).reshape(2, 4, -1, LANES)


def kernel(x, w_in, b_in, w_proj_sb, w_proj_fox, w_out, ln1_g, ln1_b, w_up, w_conv, b_conv, w_down, ln2_g, ln2_b, loss_target, m_w_in, m_b_in, m_w_proj_sb, m_w_proj_fox, m_w_out, m_ln1_g, m_ln1_b, m_w_up, m_w_conv, m_b_conv, m_w_down, m_ln2_g, m_ln2_b, v_w_in, v_b_in, v_w_proj_sb, v_w_proj_fox, v_w_out, v_ln1_g, v_ln1_b, v_w_up, v_w_conv, v_b_conv, v_w_down, v_ln2_g, v_ln2_b):
    w = dict(w_in=w_in, b_in=b_in, w_proj_sb=w_proj_sb, w_proj_fox=w_proj_fox, w_out=w_out, ln1_g=ln1_g, ln1_b=ln1_b,
             w_up=w_up, w_conv=w_conv, b_conv=b_conv, w_down=w_down, ln2_g=ln2_g, ln2_b=ln2_b)
    m = dict(w_in=m_w_in, b_in=m_b_in, w_proj_sb=m_w_proj_sb, w_proj_fox=m_w_proj_fox, w_out=m_w_out, ln1_g=m_ln1_g,
             ln1_b=m_ln1_b, w_up=m_w_up, w_conv=m_w_conv, b_conv=m_b_conv, w_down=m_w_down, ln2_g=m_ln2_g, ln2_b=m_ln2_b)
    v = dict(w_in=v_w_in, b_in=v_b_in, w_proj_sb=v_w_proj_sb, w_proj_fox=v_w_proj_fox, w_out=v_w_out, ln1_g=v_ln1_g,
             ln1_b=v_ln1_b, w_up=v_w_up, w_conv=v_w_conv, b_conv=v_b_conv, w_down=v_w_down, ln2_g=v_ln2_g, ln2_b=v_ln2_b)
    order = ["w_in", "b_in", "w_proj_sb", "w_proj_fox", "w_out", "ln1_g", "ln1_b", "w_up", "w_conv", "b_conv", "w_down",
             "ln2_g", "ln2_b"]
    x_idx, y_idx, c_idx = _pos()
    chip = 2 * x_idx + y_idx
    D = x.shape[-1]
    shapes = [w[n].shape[1:] for n, _ in _MATS]

    flat_w = jnp.concatenate([w[n][0].astype(bf16).reshape(-1) for n, _ in _MATS]).reshape(-1, LANES)
    stack_w, stack_wc = _gather_shards(flat_w, w["w_conv"][0])
    full, o = {}, 0
    stack_flat = stack_w.reshape(4, -1)
    for (n, axis), (r, cols) in zip(_MATS, shapes):
        sh = stack_flat[:, o:o + r * cols].reshape(4, r, cols)
        full[n] = sh.transpose(1, 0, 2).reshape(r, 4 * cols) if axis == 1 else sh.reshape(4 * r, cols)
        o += r * cols
    w_conv_full = stack_wc.transpose(1, 0, 2).reshape(stack_wc.shape[1], -1)

    sq_err, grad_x, grads = _local_step(
        x, loss_target, full["w_in"], w["b_in"], full["w_proj_sb"], full["w_proj_fox"], full["w_out"], w["ln1_g"], w["ln1_b"],
        full["w_up"], w_conv_full, w["b_conv"], full["w_down"], w["ln2_g"], w["ln2_b"])
    loss_part = (0.5 / D) * jnp.sum(sq_err)

    packed = _pack_grads(grads, shapes)
    mine = lax.dynamic_index_in_dim(packed, c_idx, 0, keepdims=False)
    theirs = lax.dynamic_index_in_dim(packed, 1 - c_idx, 0, keepdims=False)
    small_names = list(_SMALL) + ["w_conv"]
    small = _pack_rows([jnp.full((1, 1), loss_part, f32)] + [grads[n] for n in small_names])
    got, small_all = _exchange_sibling_and_small(theirs, small)
    chip_sum = _add2(mine, got, "add_sibling")
    from_chips = _exchange_chips(chip_sum)
    half = _sum_slots(from_chips, "sum_chips")
    g_shard = _share_halves(half).reshape(-1, LANES)
    small_sum = _sum_slots(small_all, "sum_small")

    pack = lambda d: _pack_shards([d[n][0] for n, _ in _MATS])
    delta, new_m, new_v = _adamw(pack(w), g_shard, pack(m), pack(v), "adamw_shards")
    sizes = [1] + [int(grads[n].size) for n in small_names]
    sm = _unpack_rows(small_sum, sizes)
    loss = sm[0][0, 0]
    g_small = dict(zip(small_names, sm[1:]))
    F4 = w["w_conv"].shape[-1]
    g_small["w_conv"] = lax.dynamic_slice_in_dim(g_small["w_conv"].reshape(3, -1), chip * F4, F4, axis=1)
    pack_s = lambda d: _pack_rows([d[n].reshape(1, -1) for n in small_names])
    gs_packed = _pack_rows([g_small[n].reshape(1, -1) for n in small_names])
    s_delta, s_m, s_v = _adamw(pack_s(w), gs_packed, pack_s(m), pack_s(v), "adamw_small")
    s_sizes = [int(w[n].size) for n in small_names]

    out = {"grad": {}, "delta": {}, "m": {}, "v": {}}
    for key, flat in (("grad", g_shard), ("delta", delta), ("m", new_m), ("v", new_v)):
        for (n, _), t in zip(_MATS, _unpack_shards(flat, shapes)):
            out[key][n] = t.reshape(w[n].shape)
    for key, packed_s in (("grad", gs_packed), ("delta", s_delta), ("m", s_m), ("v", s_v)):
        for n, t in zip(small_names, _unpack_rows(packed_s, s_sizes)):
            out[key][n] = t.reshape(w[n].shape)
    return (loss, grad_x, *[out["grad"][n] for n in order], *[out["delta"][n] for n in order],
            *[out["m"][n] for n in order], *[out["v"][n] for n in order])
```

```python
import functools
import math

import jax
import jax.numpy as jnp
from jax import lax
from jax.experimental import pallas as pl
from jax.experimental.pallas import tpu as pltpu

f32, bf16 = jnp.float32, jnp.bfloat16

HEAD_DIM = 64
LANES = 128
LN_EPS = 1e-5
ALPHA = 2.0 ** 0.25
Q_SCALE = HEAD_DIM ** -0.5
ADAM_LR, ADAM_B1, ADAM_B2, ADAM_EPS, ADAM_WD, ADAM_STEP = 0.001, 0.9, 0.999, 1e-08, 0.01, 10
VMEM_LIMIT = 56 * 1024 * 1024
NEG = -1e30

_pcall = pl.pallas_call
_NT = (((1,), (1,)), ((), ()))
_TN = (((0,), (0,)), ((), ()))


def _params(sem=None):
    return pltpu.CompilerParams(dimension_semantics=sem, vmem_limit_bytes=VMEM_LIMIT)


def _tile(dim, target, unit=LANES):
    if dim <= target:
        return dim
    t = (target // unit) * unit
    while t > unit and dim % t:
        t -= unit
    assert dim % t == 0, (dim, target)
    return t


def _dot(a, b, dn=None):
    if dn is None:
        return jnp.dot(a, b, preferred_element_type=f32)
    return lax.dot_general(a, b, dn, preferred_element_type=f32)


def _split_dot(x, tri):
    hi = x.astype(bf16)
    lo = (x - hi.astype(f32)).astype(bf16)
    return _dot(hi, tri) + _dot(lo, tri)


def _matmul(a, b, *, name, ta=False, tb=False, bias=None, addend=None, addend_scale=1.0, colsum=False,
            out_dtype=f32, tm=512, tn=512, tk=1024):
    M, K = (a.shape[1], a.shape[0]) if ta else a.shape
    N = b.shape[0] if tb else b.shape[1]
    assert K == (b.shape[1] if tb else b.shape[0])
    assert not (colsum and tb)
    tm, tn, tk = _tile(M, tm), _tile(N, tn), _tile(K, tk)
    nk = K // tk
    n_in = 2 + (bias is not None) + (addend is not None)

    def body(*refs):
        a_ref, b_ref = refs[0], refs[1]
        bias_ref = refs[2] if bias is not None else None
        add_ref = refs[n_in - 1] if addend is not None else None
        o_ref = refs[n_in]
        cs_ref = refs[n_in + 1] if colsum else None
        acc = refs[-2] if colsum else refs[-1]
        cs_acc = refs[-1] if colsum else None
        k = pl.program_id(2)

        @pl.when(k == 0)
        def _():
            acc[...] = jnp.zeros_like(acc)
            if colsum:
                cs_acc[...] = jnp.zeros_like(cs_acc)

        dn = (((0 if ta else 1,), (1 if tb else 0,)), ((), ()))
        acc[...] += lax.dot_general(a_ref[...].astype(bf16), b_ref[...].astype(bf16), dn, preferred_element_type=f32)
        if colsum:
            cs_acc[...] += jnp.sum(b_ref[...].astype(f32), axis=0, keepdims=True)

        @pl.when(k == nk - 1)
        def _():
            r = acc[...]
            if bias is not None:
                r = r + bias_ref[...]
            if addend is not None:
                r = r + addend_scale * add_ref[...].astype(f32)
            o_ref[...] = r.astype(out_dtype)
            if colsum:
                cs_ref[0] = cs_acc[...]

    a_spec = pl.BlockSpec((tk, tm), lambda i, j, k: (k, i)) if ta else pl.BlockSpec((tm, tk), lambda i, j, k: (i, k))
    b_spec = pl.BlockSpec((tn, tk), lambda i, j, k: (j, k)) if tb else pl.BlockSpec((tk, tn), lambda i, j, k: (k, j))
    in_specs, args = [a_spec, b_spec], [a, b]
    if bias is not None:
        in_specs.append(pl.BlockSpec((1, tn), lambda i, j, k: (0, j)))
        args.append(bias.reshape(1, N).astype(f32))
    if addend is not None:
        in_specs.append(pl.BlockSpec((tm, tn), lambda i, j, k: (i, j)))
        args.append(addend)
    out_shape = [jax.ShapeDtypeStruct((M, N), out_dtype)]
    out_specs = [pl.BlockSpec((tm, tn), lambda i, j, k: (i, j))]
    scratch = [pltpu.VMEM((tm, tn), f32)]
    if colsum:
        out_shape.append(jax.ShapeDtypeStruct((M // tm, 1, N), f32))
        out_specs.append(pl.BlockSpec((1, 1, tn), lambda i, j, k: (i, 0, j)))
        scratch.append(pltpu.VMEM((1, tn), f32))
    res = _pcall(body, name=name, grid=(M // tm, N // tn, nk), in_specs=in_specs, out_specs=out_specs,
                 out_shape=out_shape, scratch_shapes=scratch,
                 compiler_params=_params(("arbitrary", "arbitrary", "arbitrary")))(*args)
    return (res[0], res[1][0]) if colsum else res[0]


def _cumlogf(fl, B, S):
    t = _tile(S, 256, 8)

    def body(fl_ref, c_ref, carry):
        @pl.when(pl.program_id(1) == 0)
        def _():
            carry[...] = jnp.zeros_like(carry)
        z = fl_ref[...]
        ls = jnp.minimum(z, 0.0) - jnp.log(1.0 + jnp.exp(-jnp.abs(z)))
        row = lax.broadcasted_iota(jnp.int32, (t, t), 0)
        col = lax.broadcasted_iota(jnp.int32, (t, t), 1)
        lower = (col <= row).astype(f32)
        c = jnp.dot(lower, ls, precision=lax.Precision.HIGHEST, preferred_element_type=f32) + carry[...]
        c_ref[...] = c
        carry[...] = c[t - 1:t, :]

    return _pcall(body, name="cumlogf", grid=(B, S // t),
                  in_specs=[pl.BlockSpec((t, LANES), lambda b, i: (b * (S // t) + i, 0))],
                  out_specs=pl.BlockSpec((t, LANES), lambda b, i: (b * (S // t) + i, 0)),
                  out_shape=jax.ShapeDtypeStruct(fl.shape, f32), scratch_shapes=[pltpu.VMEM((1, LANES), f32)],
                  compiler_params=_params(("arbitrary", "arbitrary")))(fl)


def _cumlogf_bwd(dc, fl, B, S):
    t = _tile(S, 256, 8)
    n = S // t

    def body(dc_ref, fl_ref, o_ref, carry):
        @pl.when(pl.program_id(1) == 0)
        def _():
            carry[...] = jnp.zeros_like(carry)
        row = lax.broadcasted_iota(jnp.int32, (t, t), 0)
        col = lax.broadcasted_iota(jnp.int32, (t, t), 1)
        upper = (col >= row).astype(f32)
        r = jnp.dot(upper, dc_ref[...], precision=lax.Precision.HIGHEST, preferred_element_type=f32) + carry[...]
        carry[...] = r[0:1, :]
        z = fl_ref[...]
        o_ref[...] = r / (1.0 + jnp.exp(z))

    spec = pl.BlockSpec((t, LANES), lambda b, i: (b * n + n - 1 - i, 0))
    return _pcall(body, name="cumlogf_bwd", grid=(B, n), in_specs=[spec, spec], out_specs=spec,
                  out_shape=jax.ShapeDtypeStruct(fl.shape, f32), scratch_shapes=[pltpu.VMEM((1, LANES), f32)],
                  compiler_params=_params(("arbitrary", "arbitrary")))(dc, fl)


def _head_masks():
    lane = lax.broadcasted_iota(jnp.int32, (1, LANES), 1)
    return lane < HEAD_DIM


def _by_head(m0, t):
    z = jnp.zeros_like(t)
    return [jnp.where(m0, t, z), jnp.where(m0, z, t)]


def _sb_terms(z):
    relu = jnp.maximum(z, 0.0)
    sp = jnp.log(1.0 + jnp.exp(-jnp.abs(z)))
    return (z - relu) - sp, -relu - sp


def _sb_fwd(qkv, B, S, n_pairs, qcol, kcol, vcol, tq):
    nq = S // tq
    T = B * S

    def body(q_ref, k_ref, v_ref, o_ref, lt_ref):
        i = pl.program_id(2)
        m0 = _head_masks()
        qh = _by_head(m0, q_ref[...])
        row = lax.broadcasted_iota(jnp.int32, (tq, tq), 0)
        col = lax.broadcasted_iota(jnp.int32, (tq, tq), 1)
        strict = col < row
        later = (row > col).astype(bf16)

        def tile(s0, R, acc, diag):
            kb = k_ref[pl.ds(s0, tq), :]
            vh = _by_head(m0, v_ref[pl.ds(s0, tq), :])
            R = list(R)
            for h in range(2):
                z = _dot(qh[h], kb, _NT)
                lb, l1m = _sb_terms(z)
                if diag:
                    l1m = jnp.where(strict, l1m, 0.0)
                suf = _split_dot(l1m, later) + R[h]
                w = jnp.exp(lb + suf)
                if diag:
                    w = jnp.where(strict, w, 0.0)
                acc = acc + _dot(w.astype(bf16), vh[h])
                R[h] = R[h] + jnp.sum(l1m, axis=1, keepdims=True)
            return R, acc

        zero = jnp.zeros((tq, 1), f32)
        R, acc = tile(pl.multiple_of(i * tq, tq), [zero, zero], jnp.zeros((tq, LANES), f32), True)

        def loop(n, carry):
            s0 = pl.multiple_of((i - 1 - n) * tq, tq)
            R, acc = tile(s0, carry[:2], carry[2], False)
            return R[0], R[1], acc

        R0, R1, acc = lax.fori_loop(0, i, loop, (R[0], R[1], acc))
        o_ref[...] = acc.astype(bf16)
        lt_ref[...] = jnp.where(m0, R0, R1)

    qs = lambda c: pl.BlockSpec((tq, LANES), lambda b, p, i: (b * nq + i, c + p))
    ks = lambda c: pl.BlockSpec((S, LANES), lambda b, p, i: (b, c + p))
    os_ = pl.BlockSpec((tq, LANES), lambda b, p, i: (b * nq + i, p))
    return _pcall(body, name="sb_fwd", grid=(B, n_pairs, nq), in_specs=[qs(qcol), ks(kcol), ks(vcol)],
                  out_specs=[os_, os_],
                  out_shape=[jax.ShapeDtypeStruct((T, n_pairs * LANES), bf16), jax.ShapeDtypeStruct((T, n_pairs * LANES), f32)],
                  compiler_params=_params(("arbitrary", "arbitrary", "arbitrary")))(qkv, qkv, qkv)


def _sb_bwd(qkv, do, o, lt, B, S, n_pairs, qcol, kcol, vcol, tq):
    nq = S // tq
    T = B * S

    def body(q_ref, k_ref, v_ref, do_ref, o_ref, lt_ref, dq_ref, dk_ref, dv_ref):
        i = pl.program_id(2)

        @pl.when(i == 0)
        def _():
            dk_ref[...] = jnp.zeros_like(dk_ref)
            dv_ref[...] = jnp.zeros_like(dv_ref)

        m0 = _head_masks()
        qh = _by_head(m0, q_ref[...])
        do2 = do_ref[...]
        doh = _by_head(m0, do2)
        prod = do2.astype(f32) * o_ref[...].astype(f32)
        delta = [jnp.sum(p, axis=1, keepdims=True) for p in _by_head(m0, prod)]
        lt = lt_ref[...]
        ltot = [lt[:, 0:1], lt[:, HEAD_DIM:HEAD_DIM + 1]]
        row = lax.broadcasted_iota(jnp.int32, (tq, tq), 0)
        col = lax.broadcasted_iota(jnp.int32, (tq, tq), 1)
        strict = col < row
        upto = (row <= col).astype(bf16)
        before = (row < col).astype(bf16)

        def tile(s0, CL, CP, dq, diag):
            kb = k_ref[pl.ds(s0, tq), :]
            vb = v_ref[pl.ds(s0, tq), :]
            kh = _by_head(m0, kb)
            CL, CP = list(CL), list(CP)
            dk = jnp.zeros((tq, LANES), f32)
            dv = jnp.zeros((tq, LANES), f32)
            for h in range(2):
                z = _dot(qh[h], kb, _NT)
                lb, l1m = _sb_terms(z)
                if diag:
                    l1m = jnp.where(strict, l1m, 0.0)
                suf = ltot[h] - CL[h] - _split_dot(l1m, upto)
                w = jnp.exp(lb + suf)
                if diag:
                    w = jnp.where(strict, w, 0.0)
                g = _dot(doh[h], vb, _NT) * w
                p = CP[h] + _split_dot(g, before)
                dz = g - jnp.exp(lb) * (g + p)
                if diag:
                    dz = jnp.where(strict, dz, 0.0)
                dzb = dz.astype(bf16)
                dq = dq + _dot(dzb, kh[h])
                dk = dk + _dot(dzb, qh[h], _TN)
                dv = dv + _dot(w.astype(bf16), doh[h], _TN)
                CL[h] = CL[h] + jnp.sum(l1m, axis=1, keepdims=True)
                CP[h] = CP[h] + jnp.sum(g, axis=1, keepdims=True)
            dk_ref[pl.ds(s0, tq), :] += dk
            dv_ref[pl.ds(s0, tq), :] += dv
            return CL, CP, dq

        zero = jnp.zeros((tq, 1), f32)

        def loop(n, carry):
            CL, CP, dq = tile(pl.multiple_of(n * tq, tq), carry[0:2], carry[2:4], carry[4], False)
            return CL[0], CL[1], CP[0], CP[1], dq

        c = lax.fori_loop(0, i, loop, (zero, zero, zero, zero, jnp.zeros((tq, LANES), f32)))
        _, _, dq = tile(pl.multiple_of(i * tq, tq), c[0:2], c[2:4], c[4], True)
        dq_ref[...] = dq * Q_SCALE

    qs = lambda c: pl.BlockSpec((tq, LANES), lambda b, p, i: (b * nq + i, c + p))
    ks = lambda c: pl.BlockSpec((S, LANES), lambda b, p, i: (b, c + p))
    ts = pl.BlockSpec((tq, LANES), lambda b, p, i: (b * nq + i, p))
    fs = pl.BlockSpec((S, LANES), lambda b, p, i: (b, p))
    shp = jax.ShapeDtypeStruct((T, n_pairs * LANES), f32)
    return _pcall(body, name="sb_bwd", grid=(B, n_pairs, nq),
                  in_specs=[qs(qcol), ks(kcol), ks(vcol), ts, ts, ts], out_specs=[ts, fs, fs], out_shape=[shp, shp, shp],
                  compiler_params=_params(("arbitrary", "arbitrary", "arbitrary")))(qkv, qkv, qkv, do, o, lt)


def _fox_fwd(qkv, c, cT, B, S, n_pairs, qcol, kcol, vcol, tq):
    nq = S // tq
    T = B * S

    def body(q_ref, k_ref, v_ref, cq_ref, ck_ref, o_ref, lse_ref):
        p_idx = pl.program_id(1)
        i = pl.program_id(2)
        m0 = _head_masks()
        lane = lax.broadcasted_iota(jnp.int32, (1, LANES), 1)
        qh = _by_head(m0, q_ref[...])
        cq_all = cq_ref[...]
        cq = [jnp.sum(jnp.where(lane == 2 * p_idx + h, cq_all, 0.0), axis=1, keepdims=True) for h in range(2)]
        row = lax.broadcasted_iota(jnp.int32, (tq, tq), 0)
        col = lax.broadcasted_iota(jnp.int32, (tq, tq), 1)
        causal = col <= row

        def tile(s0, m, l, acc, diag):
            kb = k_ref[pl.ds(s0, tq), :]
            vh = _by_head(m0, v_ref[pl.ds(s0, tq), :])
            m, l = list(m), list(l)
            scale, add = [], []
            for h in range(2):
                z = _dot(qh[h], kb, _NT) + (cq[h] - ck_ref[h, :, pl.ds(s0, tq)])
                if diag:
                    z = jnp.where(causal, z, NEG)
                m_new = jnp.maximum(m[h], jnp.max(z, axis=1, keepdims=True))
                p = jnp.exp(z - m_new)
                a = jnp.exp(m[h] - m_new)
                l[h] = a * l[h] + jnp.sum(p, axis=1, keepdims=True)
                m[h] = m_new
                scale.append(a)
                add.append(_dot(p.astype(bf16), vh[h]))
            acc = acc * jnp.where(m0, scale[0], scale[1]) + add[0] + add[1]
            return m, l, acc

        neg = jnp.full((tq, 1), NEG, f32)
        zero = jnp.zeros((tq, 1), f32)
        m, l, acc = tile(pl.multiple_of(i * tq, tq), [neg, neg], [zero, zero], jnp.zeros((tq, LANES), f32), True)

        def loop(n, carry):
            m, l, acc = tile(pl.multiple_of(n * tq, tq), carry[0:2], carry[2:4], carry[4], False)
            return m[0], m[1], l[0], l[1], acc

        m0_, m1_, l0, l1, acc = lax.fori_loop(0, i, loop, (m[0], m[1], l[0], l[1], acc))
        o_ref[...] = (acc * jnp.where(m0, 1.0 / l0, 1.0 / l1)).astype(bf16)
        lse_ref[...] = jnp.where(m0, m0_ + jnp.log(l0), m1_ + jnp.log(l1))

    qs = lambda cc: pl.BlockSpec((tq, LANES), lambda b, p, i: (b * nq + i, cc + p))
    ks = lambda cc: pl.BlockSpec((S, LANES), lambda b, p, i: (b, cc + p))
    cqs = pl.BlockSpec((tq, LANES), lambda b, p, i: (b * nq + i, 0))
    cks = pl.BlockSpec((2, 1, S), lambda b, p, i: (b * n_pairs + p, 0, 0))
    os_ = pl.BlockSpec((tq, LANES), lambda b, p, i: (b * nq + i, p))
    return _pcall(body, name="fox_fwd", grid=(B, n_pairs, nq), in_specs=[qs(qcol), ks(kcol), ks(vcol), cqs, cks],
                  out_specs=[os_, os_],
                  out_shape=[jax.ShapeDtypeStruct((T, n_pairs * LANES), bf16), jax.ShapeDtypeStruct((T, n_pairs * LANES), f32)],
                  compiler_params=_params(("arbitrary", "arbitrary", "arbitrary")))(qkv, qkv, qkv, c, cT)


def _fox_bwd(qkv, c, cT, do, o, lse, B, S, n_pairs, qcol, kcol, vcol, tq):
    nq = S // tq
    T = B * S

    def body(q_ref, k_ref, v_ref, cq_ref, ck_ref, do_ref, o_ref, lse_ref, dq_ref, dk_ref, dv_ref, dc_ref):
        p_idx = pl.program_id(1)
        i = pl.program_id(2)

        @pl.when(i == 0)
        def _():
            dk_ref[...] = jnp.zeros_like(dk_ref)
            dv_ref[...] = jnp.zeros_like(dv_ref)
            dc_ref[...] = jnp.zeros_like(dc_ref)

        m0 = _head_masks()
        lane = lax.broadcasted_iota(jnp.int32, (1, LANES), 1)
        qh = _by_head(m0, q_ref[...])
        do2 = do_ref[...]
        doh = _by_head(m0, do2)
        prod = do2.astype(f32) * o_ref[...].astype(f32)
        delta = [jnp.sum(p, axis=1, keepdims=True) for p in _by_head(m0, prod)]
        ls = lse_ref[...]
        lse = [ls[:, 0:1], ls[:, HEAD_DIM:HEAD_DIM + 1]]
        cq_all = cq_ref[...]
        cq = [jnp.sum(jnp.where(lane == 2 * p_idx + h, cq_all, 0.0), axis=1, keepdims=True) for h in range(2)]
        row = lax.broadcasted_iota(jnp.int32, (tq, tq), 0)
        col = lax.broadcasted_iota(jnp.int32, (tq, tq), 1)
        causal = col <= row

        def tile(s0, dq, diag):
            kb = k_ref[pl.ds(s0, tq), :]
            vb = v_ref[pl.ds(s0, tq), :]
            kh = _by_head(m0, kb)
            dk = jnp.zeros((tq, LANES), f32)
            dv = jnp.zeros((tq, LANES), f32)
            for h in range(2):
                z = _dot(qh[h], kb, _NT) + (cq[h] - ck_ref[h, :, pl.ds(s0, tq)])
                p = jnp.exp(z - lse[h])
                if diag:
                    p = jnp.where(causal, p, 0.0)
                ds = p * (_dot(doh[h], vb, _NT) - delta[h])
                dsb = ds.astype(bf16)
                dq = dq + _dot(dsb, kh[h])
                dk = dk + _dot(dsb, qh[h], _TN)
                dv = dv + _dot(p.astype(bf16), doh[h], _TN)
                dc_ref[h, :, pl.ds(s0, tq)] -= jnp.sum(ds, axis=0, keepdims=True)
            dk_ref[pl.ds(s0, tq), :] += dk
            dv_ref[pl.ds(s0, tq), :] += dv
            return dq

        dq = lax.fori_loop(0, i, lambda n, dq: tile(pl.multiple_of(n * tq, tq), dq, False), jnp.zeros((tq, LANES), f32))
        dq = tile(pl.multiple_of(i * tq, tq), dq, True)
        dq_ref[...] = dq * Q_SCALE

    qs = lambda cc: pl.BlockSpec((tq, LANES), lambda b, p, i: (b * nq + i, cc + p))
    ks = lambda cc: pl.BlockSpec((S, LANES), lambda b, p, i: (b, cc + p))
    cqs = pl.BlockSpec((tq, LANES), lambda b, p, i: (b * nq + i, 0))
    cks = pl.BlockSpec((2, 1, S), lambda b, p, i: (b * n_pairs + p, 0, 0))
    ts = pl.BlockSpec((tq, LANES), lambda b, p, i: (b * nq + i, p))
    fs = pl.BlockSpec((S, LANES), lambda b, p, i: (b, p))
    shp = jax.ShapeDtypeStruct((T, n_pairs * LANES), f32)
    return _pcall(body, name="fox_bwd", grid=(B, n_pairs, nq),
                  in_specs=[qs(qcol), ks(kcol), ks(vcol), cqs, cks, ts, ts, ts], out_specs=[ts, fs, fs, cks],
                  out_shape=[shp, shp, shp, jax.ShapeDtypeStruct(cT.shape, f32)],
                  compiler_params=_params(("arbitrary", "arbitrary", "arbitrary")))(qkv, qkv, qkv, c, cT, do, o, lse)


def _sigmoid(x):
    return 1.0 / (1.0 + jnp.exp(-x))


def _mix_fwd(o_sb, o_fx, g, x, wp_sb, wp_fx, w_out, ln_g, ln_b, tm):
    T, D = x.shape
    E = o_sb.shape[1]
    tm = _tile(T, tm, 8)

    def body(osb_ref, ofx_ref, gsb_ref, gfx_ref, x_ref, wsb_ref, wfx_ref, wo_ref, lg_ref, lb_ref,
             xhat_ref, rstd_ref, x1_ref, mg_ref):
        y_sb = _dot(osb_ref[...], wsb_ref[...])
        y_fx = _dot(ofx_ref[...], wfx_ref[...])
        merged = (_sigmoid(gsb_ref[...]) * y_sb + _sigmoid(gfx_ref[...]) * y_fx).astype(bf16)
        r = ALPHA * x_ref[...] + _dot(merged, wo_ref[...])
        mean = jnp.mean(r, axis=1, keepdims=True)
        cen = r - mean
        rstd = lax.rsqrt(jnp.mean(cen * cen, axis=1, keepdims=True) + LN_EPS)
        xhat = cen * rstd
        xhat_ref[...] = xhat
        rstd_ref[...] = rstd
        x1_ref[...] = (xhat * lg_ref[...] + lb_ref[...]).astype(bf16)
        mg_ref[...] = merged

    rows = lambda w, c=0: pl.BlockSpec((tm, w), lambda i: (i, c))
    full = lambda a: pl.BlockSpec(a.shape, lambda i: (0, 0))
    return _pcall(body, name="mix_fwd", grid=(T // tm,),
                  in_specs=[rows(E), rows(E), rows(D, 0), rows(D, 1), rows(D), full(wp_sb), full(wp_fx), full(w_out),
                            full(ln_g), full(ln_b)],
                  out_specs=[rows(D), rows(1), rows(D), rows(D)],
                  out_shape=[jax.ShapeDtypeStruct((T, D), f32), jax.ShapeDtypeStruct((T, 1), f32),
                             jax.ShapeDtypeStruct((T, D), bf16), jax.ShapeDtypeStruct((T, D), bf16)],
                  compiler_params=_params(("arbitrary",)))(o_sb, o_fx, g, g, x, wp_sb, wp_fx, w_out, ln_g, ln_b)


def _mix_bwd(dr1, o_sb, o_fx, g, wp_sb, wp_fx, w_out, tm):
    T, D = dr1.shape
    E = o_sb.shape[1]
    tm = _tile(T, tm, 8)

    def body(dr_ref, osb_ref, ofx_ref, gsb_ref, gfx_ref, wsb_ref, wfx_ref, wo_ref,
             dysb_ref, dyfx_ref, dgsb_ref, dgfx_ref, dosb_ref, dofx_ref, sumsb_ref, sumfx_ref):
        @pl.when(pl.program_id(0) == 0)
        def _():
            sumsb_ref[...] = jnp.zeros_like(sumsb_ref)
            sumfx_ref[...] = jnp.zeros_like(sumfx_ref)
        dm = _dot(dr_ref[...].astype(bf16), wo_ref[...], _NT)
        for o_ref, g_ref, w_ref, dy_ref, dg_ref, do_ref, sum_ref in (
                (osb_ref, gsb_ref, wsb_ref, dysb_ref, dgsb_ref, dosb_ref, sumsb_ref),
                (ofx_ref, gfx_ref, wfx_ref, dyfx_ref, dgfx_ref, dofx_ref, sumfx_ref)):
            y = _dot(o_ref[...], w_ref[...])
            s = _sigmoid(g_ref[...])
            dy = (dm * s).astype(bf16)
            dy_ref[...] = dy
            dg = dm * y * s * (1.0 - s)
            dg_ref[...] = dg.astype(bf16)
            sum_ref[0:1, :] += jnp.sum(dg, axis=0, keepdims=True)
            do_ref[...] = _dot(dy, w_ref[...], _NT).astype(bf16)

    rows = lambda w, c=0: pl.BlockSpec((tm, w), lambda i: (i, c))
    full = lambda a: pl.BlockSpec(a.shape, lambda i: (0, 0))
    acc = pl.BlockSpec((8, D), lambda i: (0, 0))
    res = _pcall(body, name="mix_bwd", grid=(T // tm,),
                 in_specs=[rows(D), rows(E), rows(E), rows(D, 0), rows(D, 1), full(wp_sb), full(wp_fx), full(w_out)],
                 out_specs=[rows(D), rows(D), rows(D), rows(D), rows(E), rows(E), acc, acc],
                 out_shape=[jax.ShapeDtypeStruct((T, D), bf16)] * 4 + [jax.ShapeDtypeStruct((T, E), bf16)] * 2
                 + [jax.ShapeDtypeStruct((8, D), f32)] * 2,
                 compiler_params=_params(("arbitrary",)))(dr1, o_sb, o_fx, g, g, wp_sb, wp_fx, w_out)
    return res


def _ln_bwd(dy_a, dy_b, scale_b, xhat, rstd, ln_g, tm):
    T, D = xhat.shape
    tm = _tile(T, tm, 8)

    def body(a_ref, b_ref, xh_ref, rs_ref, g_ref, dr_ref, st_ref):
        @pl.when(pl.program_id(0) == 0)
        def _():
            st_ref[...] = jnp.zeros_like(st_ref)
        dy = a_ref[...] + scale_b * b_ref[...]
        xh = xh_ref[...]
        dxh = dy * g_ref[...]
        m1 = jnp.mean(dxh, axis=1, keepdims=True)
        m2 = jnp.mean(dxh * xh, axis=1, keepdims=True)
        dr_ref[...] = rs_ref[...] * (dxh - m1 - xh * m2)
        st_ref[0:1, :] += jnp.sum(dy * xh, axis=0, keepdims=True)
        st_ref[1:2, :] += jnp.sum(dy, axis=0, keepdims=True)

    rows = lambda w: pl.BlockSpec((tm, w), lambda i: (i, 0))
    return _pcall(body, name="ln1_bwd", grid=(T // tm,),
                  in_specs=[rows(D), rows(D), rows(D), rows(1), pl.BlockSpec((1, D), lambda i: (0, 0))],
                  out_specs=[rows(D), pl.BlockSpec((8, D), lambda i: (0, 0))],
                  out_shape=[jax.ShapeDtypeStruct((T, D), f32), jax.ShapeDtypeStruct((8, D), f32)],
                  compiler_params=_params(("arbitrary",)))(dy_a, dy_b, xhat, rstd, ln_g)


_INV_SQRT2 = 1.0 / math.sqrt(2.0)
_INV_SQRT2PI = 1.0 / math.sqrt(2.0 * math.pi)


def _conv_rows(ref, r0, rc, first, wc, bc):
    cur = ref[pl.ds(r0, rc), :]
    prev = ref[pl.ds(pl.multiple_of(jnp.maximum(r0 - 8, 0), 8), 8), :]
    prev = jnp.where(first, jnp.zeros_like(prev), prev)
    rid = lax.broadcasted_iota(jnp.int32, (rc, LANES), 0)
    s1 = jnp.where(rid == 0, prev[7:8, :], pltpu.roll(cur, 1, 0))
    s2 = jnp.where(rid == 0, prev[6:7, :], jnp.where(rid == 1, prev[7:8, :], pltpu.roll(cur, 2, 0)))
    conv = bc + wc[0:1, :] * s2 + wc[1:2, :] * s1 + wc[2:3, :] * cur
    return conv, (s2, s1, cur)


def _glu_fwd(u, w_conv, b_conv, B, S, rc=512):
    F = u.shape[1] // 2
    nf = F // LANES
    rc = _tile(S, rc, 8)

    def body(ug_ref, uv_ref, wc_ref, bc_ref, a_ref):
        wc, bc = wc_ref[...], bc_ref[...]

        def chunk(n, _):
            r0 = pl.multiple_of(n * rc, rc)
            c, _taps = _conv_rows(ug_ref, r0, rc, n == 0, wc, bc)
            gelu = 0.5 * c * (1.0 + lax.erf(c * _INV_SQRT2))
            a_ref[pl.ds(r0, rc), :] = (gelu * uv_ref[pl.ds(r0, rc), :]).astype(bf16)
            return 0

        lax.fori_loop(0, S // rc, chunk, 0)

    return _pcall(body, name="glu_fwd", grid=(B, nf),
                  in_specs=[pl.BlockSpec((S, LANES), lambda b, j: (b, j)), pl.BlockSpec((S, LANES), lambda b, j: (b, nf + j)),
                            pl.BlockSpec((3, LANES), lambda b, j: (0, j)), pl.BlockSpec((1, LANES), lambda b, j: (0, j))],
                  out_specs=pl.BlockSpec((S, LANES), lambda b, j: (b, j)),
                  out_shape=jax.ShapeDtypeStruct((B * S, F), bf16),
                  compiler_params=_params(("arbitrary", "arbitrary")))(u, u, w_conv, b_conv)


def _glu_bwd(u, da, w_conv, b_conv, B, S, rc=512):
    F = u.shape[1] // 2
    nf = F // LANES
    rc = _tile(S, rc, 8)
    nc = S // rc

    def body(ug_ref, uv_ref, da_ref, wc_ref, bc_ref, dug_ref, duv_ref, gw_ref, gb_ref, dc_ref):
        wc, bc = wc_ref[...], bc_ref[...]

        def chunk(n, carry):
            gw0, gw1, gw2, gb = carry
            r0 = pl.multiple_of(n * rc, rc)
            c, (s2, s1, cur) = _conv_rows(ug_ref, r0, rc, n == 0, wc, bc)
            cdf = 0.5 * (1.0 + lax.erf(c * _INV_SQRT2))
            da = da_ref[pl.ds(r0, rc), :]
            duv_ref[pl.ds(r0, rc), :] = (da * (c * cdf)).astype(bf16)
            dc = da * uv_ref[pl.ds(r0, rc), :] * (cdf + c * (_INV_SQRT2PI * jnp.exp(-0.5 * c * c)))
            dc_ref[pl.ds(r0, rc), :] = dc
            red = lambda t: jnp.sum(t, axis=0, keepdims=True)
            return gw0 + red(dc * s2), gw1 + red(dc * s1), gw2 + red(dc * cur), gb + red(dc)

        z = jnp.zeros((1, LANES), f32)
        gw0, gw1, gw2, gb = lax.fori_loop(0, nc, chunk, (z, z, z, z))
        gw_ref[0, 0:1, :] = gw0
        gw_ref[0, 1:2, :] = gw1
        gw_ref[0, 2:3, :] = gw2
        gb_ref[0] = gb

        def chunk2(n, _):
            r0 = pl.multiple_of(n * rc, rc)
            cur = dc_ref[pl.ds(r0, rc), :]
            nxt = dc_ref[pl.ds(pl.multiple_of(jnp.minimum(r0 + rc, S - 8), 8), 8), :]
            nxt = jnp.where(n == nc - 1, jnp.zeros_like(nxt), nxt)
            rid = lax.broadcasted_iota(jnp.int32, (rc, LANES), 0)
            a1 = jnp.where(rid == rc - 1, nxt[0:1, :], pltpu.roll(cur, rc - 1, 0))
            a2 = jnp.where(rid == rc - 1, nxt[1:2, :], jnp.where(rid == rc - 2, nxt[0:1, :], pltpu.roll(cur, rc - 2, 0)))
            dug_ref[pl.ds(r0, rc), :] = (wc[2:3, :] * cur + wc[1:2, :] * a1 + wc[0:1, :] * a2).astype(bf16)
            return 0

        lax.fori_loop(0, nc, chunk2, 0)

    blk = lambda off: pl.BlockSpec((S, LANES), lambda b, j: (b, off + j))
    return _pcall(body, name="glu_bwd", grid=(B, nf),
                  in_specs=[blk(0), blk(nf), blk(0), pl.BlockSpec((3, LANES), lambda b, j: (0, j)),
                            pl.BlockSpec((1, LANES), lambda b, j: (0, j))],
                  out_specs=[blk(0), blk(0), pl.BlockSpec((1, 3, LANES), lambda b, j: (b, 0, j)),
                             pl.BlockSpec((1, 1, LANES), lambda b, j: (b, 0, j))],
                  out_shape=[jax.ShapeDtypeStruct((B * S, F), bf16), jax.ShapeDtypeStruct((B * S, F), bf16),
                             jax.ShapeDtypeStruct((B, 3, F), f32), jax.ShapeDtypeStruct((B, 1, F), f32)],
                  scratch_shapes=[pltpu.VMEM((S, LANES), f32)],
                  compiler_params=_params(("arbitrary", "arbitrary")))(u, u, da, w_conv, b_conv)


def _down_loss(a, w_down, xhat1, ln1_g, ln1_b, ln2_g, ln2_b, tgt, tm):
    T, D = xhat1.shape
    F = a.shape[1]
    tm = _tile(T, tm, 8)

    def body(a_ref, w_ref, xh_ref, g1_ref, b1_ref, g2_ref, b2_ref, t_ref, dr_ref, st_ref):
        @pl.when(pl.program_id(0) == 0)
        def _():
            st_ref[...] = jnp.zeros_like(st_ref)
        x1 = xh_ref[...] * g1_ref[...] + b1_ref[...]
        r = ALPHA * x1 + _dot(a_ref[...], w_ref[...])
        mean = jnp.mean(r, axis=1, keepdims=True)
        cen = r - mean
        rstd = lax.rsqrt(jnp.mean(cen * cen, axis=1, keepdims=True) + LN_EPS)
        xh = cen * rstd
        err = (xh * g2_ref[...] + b2_ref[...]) - t_ref[...]
        dy = err * (1.0 / D)
        dxh = dy * g2_ref[...]
        m1 = jnp.mean(dxh, axis=1, keepdims=True)
        m2 = jnp.mean(dxh * xh, axis=1, keepdims=True)
        dr_ref[...] = rstd * (dxh - m1 - xh * m2)
        st_ref[0:1, :] += jnp.sum(dy * xh, axis=0, keepdims=True)
        st_ref[1:2, :] += jnp.sum(dy, axis=0, keepdims=True)
        st_ref[2:3, :] += jnp.sum(err * err, axis=0, keepdims=True)

    rows = lambda w: pl.BlockSpec((tm, w), lambda i: (i, 0))
    vec = pl.BlockSpec((1, D), lambda i: (0, 0))
    return _pcall(body, name="down_loss", grid=(T // tm,),
                  in_specs=[rows(F), pl.BlockSpec((F, D), lambda i: (0, 0)), rows(D), vec, vec, vec, vec, rows(D)],
                  out_specs=[rows(D), pl.BlockSpec((8, D), lambda i: (0, 0))],
                  out_shape=[jax.ShapeDtypeStruct((T, D), f32), jax.ShapeDtypeStruct((8, D), f32)],
                  compiler_params=_params(("arbitrary",)))(a, w_down, xhat1, ln1_g, ln1_b, ln2_g, ln2_b, tgt)


def _local_step(x, tgt, w_in, b_in, wp_sb, wp_fx, w_out, ln1_g, ln1_b, w_up, w_conv, b_conv, w_down, ln2_g, ln2_b,
                tq=512, tm=256):
    B, S, D = x.shape
    T = B * S
    E = wp_sb.shape[0]
    n_pairs = E // LANES
    NH = E // HEAD_DIM
    F = w_down.shape[0]
    x2 = x.reshape(T, D)
    tgt2 = tgt.reshape(T, D)
    tq = _tile(S, tq, 8)

    c_f, c_g = 6 * E, 6 * E + NH
    w_qkv = w_in[:, :c_f]
    qscale = jnp.concatenate([jnp.full((E,), Q_SCALE, f32), jnp.ones((2 * E,), f32)] * 2)
    w_qkv_s = (w_qkv.astype(f32) * qscale).astype(bf16)
    b_qkv_s = b_in[:, :c_f] * qscale
    w_f = jnp.pad(w_in[:, c_f:c_g], ((0, 0), (0, LANES - NH)))
    b_f = jnp.pad(b_in[:, c_f:c_g], ((0, 0), (0, LANES - NH)))
    w_g = w_in[:, c_g:]
    b_g = b_in[:, c_g:]

    xb = x2.astype(bf16)
    qkv = _matmul(xb, w_qkv_s, bias=b_qkv_s, out_dtype=bf16, tm=1024, tn=512, name="proj_qkv")
    g = _matmul(xb, w_g, bias=b_g, tm=1024, tn=512, name="proj_gate")
    fl = _matmul(xb, w_f, bias=b_f, tm=1024, name="proj_forget")
    c = _cumlogf(fl, B, S)
    cT = c.reshape(B, S, LANES)[:, :, :NH].transpose(0, 2, 1).reshape(B * NH, 1, S)
    P = n_pairs
    o_sb, lt_sb = _sb_fwd(qkv, B, S, P, 0, P, 2 * P, tq)
    o_fx, lse_fx = _fox_fwd(qkv, c, cT, B, S, P, 3 * P, 4 * P, 5 * P, tq)
    xhat1, rstd1, x1b, merged = _mix_fwd(o_sb, o_fx, g, x2, wp_sb, wp_fx, w_out, ln1_g, ln1_b, tm)
    u = _matmul(x1b, w_up, tm=1024, tn=512, name="ffn_up")
    a = _glu_fwd(u, w_conv, b_conv, B, S)
    dr2, st2 = _down_loss(a, w_down, xhat1, ln1_g, ln1_b, ln2_g, ln2_b, tgt2, tm)

    grads = {}
    grads["ln2_g"], grads["ln2_b"] = st2[0:1], st2[1:2]
    sq_err = st2[2:3]
    da = _matmul(dr2, w_down, tb=True, tm=1024, tn=512, name="ffn_da")
    grads["w_down"] = _matmul(a, dr2, ta=True, tm=512, tn=512, tk=1024, name="grad_w_down")
    du_g, du_v, gwc, gbc = _glu_bwd(u, da, w_conv, b_conv, B, S)
    grads["w_conv"] = jnp.sum(gwc, axis=0)
    grads["b_conv"] = jnp.sum(gbc, axis=0)
    grads["w_up"] = jnp.concatenate(
        [_matmul(x1b, du_g, ta=True, tm=512, tn=512, tk=1024, name="grad_w_up_gate"),
         _matmul(x1b, du_v, ta=True, tm=512, tn=512, tk=1024, name="grad_w_up_val")], axis=1)
    dx1 = _matmul(du_g, w_up[:, :F], tb=True, tm=1024, tn=512, tk=F, name="ffn_dx_gate")
    dx1 = _matmul(du_v, w_up[:, F:], tb=True, addend=dx1, tm=1024, tn=512, tk=F, name="ffn_dx_val")
    dr1, st1 = _ln_bwd(dx1, dr2, ALPHA, xhat1, rstd1, ln1_g, tm)
    grads["ln1_g"], grads["ln1_b"] = st1[0:1], st1[1:2]

    dy_sb, dy_fx, dg_sb, dg_fx, do_sb, do_fx, gsum_sb, gsum_fx = _mix_bwd(dr1, o_sb, o_fx, g, wp_sb, wp_fx, w_out, tm)
    grads["w_out"] = _matmul(merged, dr1, ta=True, tm=512, tn=512, tk=1024, name="grad_w_out")
    grads["w_proj_sb"] = _matmul(o_sb, dy_sb, ta=True, tm=512, tn=512, tk=1024, name="grad_w_proj_sb")
    grads["w_proj_fox"] = _matmul(o_fx, dy_fx, ta=True, tm=512, tn=512, tk=1024, name="grad_w_proj_fox")
    dq_s, dk_s, dv_s = _sb_bwd(qkv, do_sb, o_sb, lt_sb, B, S, P, 0, P, 2 * P, tq)
    dq_f, dk_f, dv_f, dcT = _fox_bwd(qkv, c, cT, do_fx, o_fx, lse_fx, B, S, P, 3 * P, 4 * P, 5 * P, tq)
    dc = jnp.pad(dcT.reshape(B, NH, S).transpose(0, 2, 1), ((0, 0), (0, 0), (0, LANES - NH))).reshape(T, LANES)
    dfl = _cumlogf_bwd(dc, fl, B, S)
    dqkv = jnp.concatenate([dq_s, dk_s, dv_s, dq_f, dk_f, dv_f], axis=1)
    dg = jnp.concatenate([dg_sb, dg_fx], axis=1)
    gw_qkv, gb_qkv = _matmul(x2, dqkv, ta=True, colsum=True, tm=512, tn=512, tk=1024, name="grad_w_qkv")
    gw_f, gb_f = _matmul(x2, dfl, ta=True, colsum=True, tm=512, tk=1024, name="grad_w_forget")
    gw_g = _matmul(x2, dg, ta=True, tm=512, tn=512, tk=1024, name="grad_w_gate")
    grads["w_in"] = jnp.concatenate([gw_qkv, gw_f[:, :NH], gw_g], axis=1)
    grads["b_in"] = jnp.concatenate([gb_qkv, gb_f[:, :NH], gsum_sb[0:1], gsum_fx[0:1]], axis=1)
    dx = _matmul(dqkv, w_qkv, tb=True, addend=dr1, addend_scale=ALPHA, tm=512, tn=512, tk=c_f, name="dx_qkv")
    dx = _matmul(dfl, w_f, tb=True, addend=dx, tm=1024, tn=512, name="dx_forget")
    dx = _matmul(dg, w_g, tb=True, addend=dx, tm=1024, tn=512, tk=2 * D, name="dx_gate")
    return sq_err, dx.reshape(B, S, D), grads


_ANY = pl.BlockSpec(memory_space=pl.ANY)
_MESH = pl.DeviceIdType.MESH


def _pos():
    return lax.axis_index("x"), lax.axis_index("y"), lax.axis_index("c")


def _other_chips(x, y):
    return [(1 - x, y), (x, 1 - y), (1 - x, 1 - y)]


def _gather_shards(shards):
    n = len(shards)

    def body(*refs):
        srcs, dsts = refs[:n], refs[n:2 * n]
        send_sems, recv_sems, local_sems = refs[2 * n:]
        x, y, c = _pos()
        k = 2 * x + y
        local = [pltpu.make_async_copy(srcs[a], dsts[a].at[k], local_sems.at[a]) for a in range(n)]
        for cp in local:
            cp.start()

        def copy(a, j, chip, slot):
            return pltpu.make_async_remote_copy(src_ref=srcs[a], dst_ref=dsts[a].at[slot], send_sem=send_sems.at[a, j],
                                                recv_sem=recv_sems.at[a, j], device_id=(*chip, c), device_id_type=_MESH)

        chips = _other_chips(x, y)
        sends = [copy(a, j, chip, k) for a in range(n) for j, chip in enumerate(chips)]
        for cp in sends:
            cp.start()
        for a in range(n):
            for j, chip in enumerate(chips):
                copy(a, j, chip, 2 * chip[0] + chip[1]).wait_recv()
        for cp in sends:
            cp.wait_send()
        for cp in local:
            cp.wait()

    return _pcall(body, name="gather_weights", in_specs=[_ANY] * n, out_specs=[_ANY] * n,
                  out_shape=[jax.ShapeDtypeStruct((4,) + s.shape, s.dtype) for s in shards],
                  scratch_shapes=[pltpu.SemaphoreType.DMA((n, 3)), pltpu.SemaphoreType.DMA((n, 3)), pltpu.SemaphoreType.DMA((n,))],
                  compiler_params=pltpu.CompilerParams(has_side_effects=True))(*shards)


def _exchange_sibling_and_small(pieces, small):
    n = len(pieces)

    def body(*refs):
        p_refs, sm_ref = refs[:n], refs[n]
        got_refs, sg_ref = refs[n + 1:2 * n + 1], refs[2 * n + 1]
        big_send, big_recv, send_sems, recv_sems, local_sem = refs[2 * n + 2:]
        x, y, c = _pos()
        me = 4 * x + 2 * y + c
        big = []
        for a in range(n):
            h = pieces[a].shape[1] // 2
            src = p_refs[a].at[:, pl.ds(pl.multiple_of((1 - c) * h, 8), h), :]
            big.append(pltpu.make_async_remote_copy(src_ref=src, dst_ref=got_refs[a], send_sem=big_send.at[a],
                                                    recv_sem=big_recv.at[a], device_id=(x, y, 1 - c), device_id_type=_MESH))
        for cp in big:
            cp.start()
        local = pltpu.make_async_copy(sm_ref, sg_ref.at[me], local_sem)
        local.start()
        peers = []
        for r in range(1, 8):
            flip = lambda v, bit: 1 - v if bit else v
            peers.append((flip(x, r & 4), flip(y, r & 2), flip(c, r & 1)))

        def copy(j, slot):
            return pltpu.make_async_remote_copy(src_ref=sm_ref, dst_ref=sg_ref.at[slot], send_sem=send_sems.at[j],
                                                recv_sem=recv_sems.at[j], device_id=peers[j], device_id_type=_MESH)

        sends = [copy(j, me) for j in range(7)]
        for cp in sends:
            cp.start()
        for j, (px, py, pc) in enumerate(peers):
            copy(j, 4 * px + 2 * py + pc).wait_recv()
        for cp in sends:
            cp.wait_send()
        local.wait()
        for cp in big:
            cp.wait()

    halves = [jax.ShapeDtypeStruct((4, p.shape[1] // 2, p.shape[2]), p.dtype) for p in pieces]
    res = _pcall(body, name="exchange_sibling", in_specs=[_ANY] * (n + 1), out_specs=[_ANY] * (n + 1),
                 out_shape=halves + [jax.ShapeDtypeStruct((8,) + small.shape, small.dtype)],
                 scratch_shapes=[pltpu.SemaphoreType.DMA((n,)), pltpu.SemaphoreType.DMA((n,)), pltpu.SemaphoreType.DMA((7,)),
                                 pltpu.SemaphoreType.DMA((7,)), pltpu.SemaphoreType.DMA(())],
                 compiler_params=pltpu.CompilerParams(has_side_effects=True))(*pieces, small)
    return res[:n], res[n]


def _exchange_chips(pieces):
    n = len(pieces)

    def body(*refs):
        p_refs, got_refs = refs[:n], refs[n:2 * n]
        send_sems, recv_sems, local_sems = refs[2 * n:]
        x, y, c = _pos()
        k = 2 * x + y
        local = [pltpu.make_async_copy(p_refs[a].at[k], got_refs[a].at[k], local_sems.at[a]) for a in range(n)]
        for cp in local:
            cp.start()
        chips = _other_chips(x, y)

        def copy(a, j, chip, piece, slot):
            return pltpu.make_async_remote_copy(src_ref=p_refs[a].at[piece], dst_ref=got_refs[a].at[slot],
                                                send_sem=send_sems.at[a, j], recv_sem=recv_sems.at[a, j],
                                                device_id=(*chip, c), device_id_type=_MESH)

        sends = [copy(a, j, chip, 2 * chip[0] + chip[1], k) for a in range(n) for j, chip in enumerate(chips)]
        for cp in sends:
            cp.start()
        for a in range(n):
            for j, chip in enumerate(chips):
                copy(a, j, chip, k, 2 * chip[0] + chip[1]).wait_recv()
        for cp in sends:
            cp.wait_send()
        for cp in local:
            cp.wait()

    return _pcall(body, name="exchange_chips", in_specs=[_ANY] * n, out_specs=[_ANY] * n,
                  out_shape=[jax.ShapeDtypeStruct(p.shape, p.dtype) for p in pieces],
                  scratch_shapes=[pltpu.SemaphoreType.DMA((n, 3)), pltpu.SemaphoreType.DMA((n, 3)), pltpu.SemaphoreType.DMA((n,))],
                  compiler_params=pltpu.CompilerParams(has_side_effects=True))(*pieces)


def _share_halves(halves):
    n = len(halves)

    def body(*refs):
        h_refs, full_refs = refs[:n], refs[n:2 * n]
        send_sems, recv_sems, local_sems = refs[2 * n:]
        x, y, c = _pos()

        def rows(a, half):
            h = halves[a].shape[0]
            return full_refs[a].at[pl.ds(pl.multiple_of(half * h, 8), h), :]

        def copy(a, half):
            return pltpu.make_async_remote_copy(src_ref=h_refs[a], dst_ref=rows(a, half), send_sem=send_sems.at[a],
                                                recv_sem=recv_sems.at[a], device_id=(x, y, 1 - c), device_id_type=_MESH)

        local = [pltpu.make_async_copy(h_refs[a], rows(a, c), local_sems.at[a]) for a in range(n)]
        sends = [copy(a, c) for a in range(n)]
        for cp in local + sends:
            cp.start()
        for a in range(n):
            copy(a, 1 - c).wait_recv()
        for cp in sends:
            cp.wait_send()
        for cp in local:
            cp.wait()

    return _pcall(body, name="share_halves", in_specs=[_ANY] * n, out_specs=[_ANY] * n,
                  out_shape=[jax.ShapeDtypeStruct((2 * h.shape[0], h.shape[1]), h.dtype) for h in halves],
                  scratch_shapes=[pltpu.SemaphoreType.DMA((n,)), pltpu.SemaphoreType.DMA((n,)), pltpu.SemaphoreType.DMA((n,))],
                  compiler_params=pltpu.CompilerParams(has_side_effects=True))(*halves)


def _add_own_half(piece, got, core, name):
    _, r, cols = piece.shape
    h = r // 2

    def body(c_ref, a_ref, b_ref, o_ref):
        o_ref[...] = a_ref[0] + b_ref[...]

    grid_spec = pltpu.PrefetchScalarGridSpec(
        num_scalar_prefetch=1, grid=(4,),
        in_specs=[pl.BlockSpec((1, 1, h, cols), lambda k, c: (k, c[0], 0, 0)), pl.BlockSpec((1, h, cols), lambda k, c: (k, 0, 0))],
        out_specs=pl.BlockSpec((1, h, cols), lambda k, c: (k, 0, 0)))
    return _pcall(body, name=name, grid_spec=grid_spec, out_shape=jax.ShapeDtypeStruct((4, h, cols), f32),
                  compiler_params=_params(("arbitrary",)))(core, piece.reshape(4, 2, h, cols), got)


def _sum_slots(stack, name):
    k, n, cols = stack.shape
    t = _tile(n, 128, 8)

    def body(s_ref, o_ref):
        acc = s_ref[0]
        for i in range(1, k):
            acc = acc + s_ref[i]
        o_ref[...] = acc

    return _pcall(body, name=name, grid=(n // t,), in_specs=[pl.BlockSpec((k, t, cols), lambda i: (0, i, 0))],
                  out_specs=pl.BlockSpec((t, cols), lambda i: (i, 0)), out_shape=jax.ShapeDtypeStruct((n, cols), f32),
                  compiler_params=_params(("arbitrary",)))(stack)


def _adamw(w, g, m, v, name):
    n, cols = w.shape
    t = _tile(n, 128, 8)
    c1 = 1.0 - ADAM_B1 ** ADAM_STEP
    c2 = 1.0 - ADAM_B2 ** ADAM_STEP

    def body(w_ref, g_ref, m_ref, v_ref, d_ref, nm_ref, nv_ref):
        g = g_ref[...]
        nm = ADAM_B1 * m_ref[...] + (1.0 - ADAM_B1) * g
        nv = ADAM_B2 * v_ref[...] + (1.0 - ADAM_B2) * (g * g)
        d_ref[...] = -ADAM_LR * ((nm / c1) / (jnp.sqrt(nv / c2) + ADAM_EPS) + ADAM_WD * w_ref[...])
        nm_ref[...] = nm
        nv_ref[...] = nv

    spec = pl.BlockSpec((t, cols), lambda i: (i, 0))
    shp = jax.ShapeDtypeStruct((n, cols), f32)
    return _pcall(body, name=name, grid=(n // t,), in_specs=[spec] * 4, out_specs=[spec] * 3, out_shape=[shp] * 3,
                  compiler_params=_params(("arbitrary",)))(w, g, m, v)


_MATS = (("w_in", 1), ("w_proj_sb", 1), ("w_proj_fox", 1), ("w_out", 0), ("w_up", 1), ("w_down", 0))
_SMALL = ("b_in", "ln1_g", "ln1_b", "b_conv", "ln2_g", "ln2_b")


def _pad_lanes(v):
    n = v.shape[-1]
    return jnp.pad(v, ((0, 0), (0, (-n) % LANES)))


def _pack_rows(vectors):
    flat = jnp.concatenate([_pad_lanes(v.reshape(1, -1)) for v in vectors], axis=1).reshape(-1, LANES)
    return jnp.pad(flat, ((0, (-flat.shape[0]) % 8), (0, 0)))


def _unpack_rows(packed, sizes):
    out, r = [], 0
    for n in sizes:
        rows = -(-n // LANES)
        out.append(packed[r:r + rows].reshape(1, rows * LANES)[:, :n])
        r += rows
    return out


def _unstack(stack, axis):
    if axis == 0:
        return stack.reshape(-1, stack.shape[2])
    return jnp.concatenate([stack[k] for k in range(4)], axis=1)


def _pieces(g, axis):
    if axis == 0:
        return g.reshape(4, g.shape[0] // 4, g.shape[1])
    cols = g.shape[1] // 4
    return jnp.stack([g[:, k * cols:(k + 1) * cols] for k in range(4)])


def kernel(x, w_in, b_in, w_proj_sb, w_proj_fox, w_out, ln1_g, ln1_b, w_up, w_conv, b_conv, w_down, ln2_g, ln2_b, loss_target, m_w_in, m_b_in, m_w_proj_sb, m_w_proj_fox, m_w_out, m_ln1_g, m_ln1_b, m_w_up, m_w_conv, m_b_conv, m_w_down, m_ln2_g, m_ln2_b, v_w_in, v_b_in, v_w_proj_sb, v_w_proj_fox, v_w_out, v_ln1_g, v_ln1_b, v_w_up, v_w_conv, v_b_conv, v_w_down, v_ln2_g, v_ln2_b):
    w = dict(w_in=w_in, b_in=b_in, w_proj_sb=w_proj_sb, w_proj_fox=w_proj_fox, w_out=w_out, ln1_g=ln1_g, ln1_b=ln1_b,
             w_up=w_up, w_conv=w_conv, b_conv=b_conv, w_down=w_down, ln2_g=ln2_g, ln2_b=ln2_b)
    m = dict(w_in=m_w_in, b_in=m_b_in, w_proj_sb=m_w_proj_sb, w_proj_fox=m_w_proj_fox, w_out=m_w_out, ln1_g=m_ln1_g,
             ln1_b=m_ln1_b, w_up=m_w_up, w_conv=m_w_conv, b_conv=m_b_conv, w_down=m_w_down, ln2_g=m_ln2_g, ln2_b=m_ln2_b)
    v = dict(w_in=v_w_in, b_in=v_b_in, w_proj_sb=v_w_proj_sb, w_proj_fox=v_w_proj_fox, w_out=v_w_out, ln1_g=v_ln1_g,
             ln1_b=v_ln1_b, w_up=v_w_up, w_conv=v_w_conv, b_conv=v_b_conv, w_down=v_w_down, ln2_g=v_ln2_g, ln2_b=v_ln2_b)
    order = ["w_in", "b_in", "w_proj_sb", "w_proj_fox", "w_out", "ln1_g", "ln1_b", "w_up", "w_conv", "b_conv", "w_down",
             "ln2_g", "ln2_b"]
    x_idx, y_idx, c_idx = _pos()
    chip = 2 * x_idx + y_idx
    D = x.shape[-1]
    core = c_idx.astype(jnp.int32).reshape(1)

    stacks = _gather_shards([w[n][0].astype(bf16) for n, _ in _MATS] + [w["w_conv"][0]])
    full = {n: _unstack(s, axis) for (n, axis), s in zip(_MATS, stacks)}
    w_conv_full = _unstack(stacks[-1], 1)

    sq_err, grad_x, grads = _local_step(
        x, loss_target, full["w_in"], w["b_in"], full["w_proj_sb"], full["w_proj_fox"], full["w_out"], w["ln1_g"], w["ln1_b"],
        full["w_up"], w_conv_full, w["b_conv"], full["w_down"], w["ln2_g"], w["ln2_b"])
    loss_part = (0.5 / D) * jnp.sum(sq_err)

    pieces = [_pieces(grads[n], axis) for n, axis in _MATS]
    small_names = list(_SMALL) + ["w_conv"]
    small = _pack_rows([jnp.full((1, 1), loss_part, f32)] + [grads[n] for n in small_names])
    got, small_all = _exchange_sibling_and_small(pieces, small)
    chip_sums = [_add_own_half(p, g_, core, "add_sibling_" + n) for (n, _), p, g_ in zip(_MATS, pieces, got)]
    from_chips = _exchange_chips(chip_sums)
    halves = [_sum_slots(s, "sum_chips_" + n) for (n, _), s in zip(_MATS, from_chips)]
    g_shards = _share_halves(halves)
    small_sum = _sum_slots(small_all, "sum_small")

    out = {"grad": {}, "delta": {}, "m": {}, "v": {}}
    for (n, _), g_ in zip(_MATS, g_shards):
        d_, m_, v_ = _adamw(w[n][0], g_, m[n][0], v[n][0], "adamw_" + n)
        for key, t in (("grad", g_), ("delta", d_), ("m", m_), ("v", v_)):
            out[key][n] = t.reshape(w[n].shape)
    sizes = [1] + [int(grads[n].size) for n in small_names]
    sm = _unpack_rows(small_sum, sizes)
    loss = sm[0][0, 0]
    g_small = dict(zip(small_names, sm[1:]))
    F4 = w["w_conv"].shape[-1]
    g_small["w_conv"] = lax.dynamic_slice_in_dim(g_small["w_conv"].reshape(3, -1), chip * F4, F4, axis=1)
    pack_s = lambda d: _pack_rows([d[n].reshape(1, -1) for n in small_names])
    gs_packed = _pack_rows([g_small[n].reshape(1, -1) for n in small_names])
    s_delta, s_m, s_v = _adamw(pack_s(w), gs_packed, pack_s(m), pack_s(v), "adamw_small")
    s_sizes = [int(w[n].size) for n in small_names]

    for key, packed_s in (("grad", gs_packed), ("delta", s_delta), ("m", s_m), ("v", s_v)):
        for n, t in zip(small_names, _unpack_rows(packed_s, s_sizes)):
            out[key][n] = t.reshape(w[n].shape)
    return (loss, grad_x, *[out["grad"][n] for n in order], *[out["delta"][n] for n in order],
            *[out["m"][n] for n in order], *[out["v"][n] for n in order])
```

```python
import functools
import math

import jax
import jax.numpy as jnp
from jax import lax
from jax.experimental import pallas as pl
from jax.experimental.pallas import tpu as pltpu

f32, bf16 = jnp.float32, jnp.bfloat16

HEAD_DIM = 64
LANES = 128
LN_EPS = 1e-5
ALPHA = 2.0 ** 0.25
Q_SCALE = HEAD_DIM ** -0.5
ADAM_LR, ADAM_B1, ADAM_B2, ADAM_EPS, ADAM_WD, ADAM_STEP = 0.001, 0.9, 0.999, 1e-08, 0.01, 10
VMEM_LIMIT = 56 * 1024 * 1024
NEG = -1e30

_pcall = pl.pallas_call
_NT = (((1,), (1,)), ((), ()))
_TN = (((0,), (0,)), ((), ()))


def _params(sem=None):
    return pltpu.CompilerParams(dimension_semantics=sem, vmem_limit_bytes=VMEM_LIMIT)


def _tile(dim, target, unit=LANES):
    if dim <= target:
        return dim
    t = (target // unit) * unit
    while t > unit and dim % t:
        t -= unit
    assert dim % t == 0, (dim, target)
    return t


def _dot(a, b, dn=None):
    if dn is None:
        return jnp.dot(a, b, preferred_element_type=f32)
    return lax.dot_general(a, b, dn, preferred_element_type=f32)


def _split_dot(x, tri):
    hi = x.astype(bf16)
    lo = (x - hi.astype(f32)).astype(bf16)
    return _dot(hi, tri) + _dot(lo, tri)


SCAN_BLOCK = 256


def _scan_cols(x, tri, reverse):
    cb = tri.shape[0]
    nb = x.shape[1] // cb
    blocks = [x[:, b * cb:(b + 1) * cb] for b in range(nb)]
    outs, run = [None] * nb, None
    for b in (reversed(range(nb)) if reverse else range(nb)):
        o = _split_dot(blocks[b], tri)
        s = jnp.sum(blocks[b], axis=1, keepdims=True)
        outs[b] = o if run is None else o + run
        run = s if run is None else run + s
    return (outs[0] if nb == 1 else jnp.concatenate(outs, axis=1)), run


def _tri(cb, rel):
    row = lax.broadcasted_iota(jnp.int32, (cb, cb), 0)
    col = lax.broadcasted_iota(jnp.int32, (cb, cb), 1)
    return rel(row, col).astype(bf16)


def _matmul(a, b, *, name, ta=False, tb=False, bias=None, addend=None, addend_scale=1.0, colsum=False,
            out_dtype=f32, tm=512, tn=512, tk=1024):
    M, K = (a.shape[1], a.shape[0]) if ta else a.shape
    N = b.shape[0] if tb else b.shape[1]
    assert K == (b.shape[1] if tb else b.shape[0])
    assert not (colsum and tb)
    tm, tn, tk = _tile(M, tm), _tile(N, tn), _tile(K, tk)
    nk = K // tk
    n_in = 2 + (bias is not None) + (addend is not None)

    def body(*refs):
        a_ref, b_ref = refs[0], refs[1]
        bias_ref = refs[2] if bias is not None else None
        add_ref = refs[n_in - 1] if addend is not None else None
        o_ref = refs[n_in]
        cs_ref = refs[n_in + 1] if colsum else None
        acc = refs[-2] if colsum else refs[-1]
        cs_acc = refs[-1] if colsum else None
        k = pl.program_id(2)

        @pl.when(k == 0)
        def _():
            acc[...] = jnp.zeros_like(acc)
            if colsum:
                cs_acc[...] = jnp.zeros_like(cs_acc)

        dn = (((0 if ta else 1,), (1 if tb else 0,)), ((), ()))
        acc[...] += lax.dot_general(a_ref[...].astype(bf16), b_ref[...].astype(bf16), dn, preferred_element_type=f32)
        if colsum:
            cs_acc[...] += jnp.sum(b_ref[...].astype(f32), axis=0, keepdims=True)

        @pl.when(k == nk - 1)
        def _():
            r = acc[...]
            if bias is not None:
                r = r + bias_ref[...]
            if addend is not None:
                r = r + addend_scale * add_ref[...].astype(f32)
            o_ref[...] = r.astype(out_dtype)
            if colsum:
                cs_ref[0] = cs_acc[...]

    a_spec = pl.BlockSpec((tk, tm), lambda i, j, k: (k, i)) if ta else pl.BlockSpec((tm, tk), lambda i, j, k: (i, k))
    b_spec = pl.BlockSpec((tn, tk), lambda i, j, k: (j, k)) if tb else pl.BlockSpec((tk, tn), lambda i, j, k: (k, j))
    in_specs, args = [a_spec, b_spec], [a, b]
    if bias is not None:
        in_specs.append(pl.BlockSpec((1, tn), lambda i, j, k: (0, j)))
        args.append(bias.reshape(1, N).astype(f32))
    if addend is not None:
        in_specs.append(pl.BlockSpec((tm, tn), lambda i, j, k: (i, j)))
        args.append(addend)
    out_shape = [jax.ShapeDtypeStruct((M, N), out_dtype)]
    out_specs = [pl.BlockSpec((tm, tn), lambda i, j, k: (i, j))]
    scratch = [pltpu.VMEM((tm, tn), f32)]
    if colsum:
        out_shape.append(jax.ShapeDtypeStruct((M // tm, 1, N), f32))
        out_specs.append(pl.BlockSpec((1, 1, tn), lambda i, j, k: (i, 0, j)))
        scratch.append(pltpu.VMEM((1, tn), f32))
    res = _pcall(body, name=name, grid=(M // tm, N // tn, nk), in_specs=in_specs, out_specs=out_specs,
                 out_shape=out_shape, scratch_shapes=scratch,
                 compiler_params=_params(("arbitrary", "arbitrary", "arbitrary")))(*args)
    return (res[0], res[1][0]) if colsum else res[0]


def _cumlogf(fl, B, S):
    t = _tile(S, 256, 8)

    def body(fl_ref, c_ref, carry):
        @pl.when(pl.program_id(1) == 0)
        def _():
            carry[...] = jnp.zeros_like(carry)
        z = fl_ref[...]
        ls = jnp.minimum(z, 0.0) - jnp.log(1.0 + jnp.exp(-jnp.abs(z)))
        row = lax.broadcasted_iota(jnp.int32, (t, t), 0)
        col = lax.broadcasted_iota(jnp.int32, (t, t), 1)
        lower = (col <= row).astype(f32)
        c = jnp.dot(lower, ls, precision=lax.Precision.HIGHEST, preferred_element_type=f32) + carry[...]
        c_ref[...] = c
        carry[...] = c[t - 1:t, :]

    return _pcall(body, name="cumlogf", grid=(B, S // t),
                  in_specs=[pl.BlockSpec((t, LANES), lambda b, i: (b * (S // t) + i, 0))],
                  out_specs=pl.BlockSpec((t, LANES), lambda b, i: (b * (S // t) + i, 0)),
                  out_shape=jax.ShapeDtypeStruct(fl.shape, f32), scratch_shapes=[pltpu.VMEM((1, LANES), f32)],
                  compiler_params=_params(("arbitrary", "arbitrary")))(fl)


def _cumlogf_bwd(dc, fl, B, S):
    t = _tile(S, 256, 8)
    n = S // t

    def body(dc_ref, fl_ref, o_ref, carry):
        @pl.when(pl.program_id(1) == 0)
        def _():
            carry[...] = jnp.zeros_like(carry)
        row = lax.broadcasted_iota(jnp.int32, (t, t), 0)
        col = lax.broadcasted_iota(jnp.int32, (t, t), 1)
        upper = (col >= row).astype(f32)
        r = jnp.dot(upper, dc_ref[...], precision=lax.Precision.HIGHEST, preferred_element_type=f32) + carry[...]
        carry[...] = r[0:1, :]
        z = fl_ref[...]
        o_ref[...] = r / (1.0 + jnp.exp(z))

    spec = pl.BlockSpec((t, LANES), lambda b, i: (b * n + n - 1 - i, 0))
    return _pcall(body, name="cumlogf_bwd", grid=(B, n), in_specs=[spec, spec], out_specs=spec,
                  out_shape=jax.ShapeDtypeStruct(fl.shape, f32), scratch_shapes=[pltpu.VMEM((1, LANES), f32)],
                  compiler_params=_params(("arbitrary", "arbitrary")))(dc, fl)


def _head_masks():
    lane = lax.broadcasted_iota(jnp.int32, (1, LANES), 1)
    return lane < HEAD_DIM


def _by_head(m0, t):
    z = jnp.zeros_like(t)
    return [jnp.where(m0, t, z), jnp.where(m0, z, t)]


def _sb_terms(z):
    relu = jnp.maximum(z, 0.0)
    sp = jnp.log(1.0 + jnp.exp(-jnp.abs(z)))
    return (z - relu) - sp, -relu - sp


def _sb_fwd(qkv, B, S, n_pairs, qcol, kcol, vcol, tq):
    nq = S // tq
    T = B * S

    def body(q_ref, k_ref, v_ref, o_ref, lt_ref):
        i = pl.program_id(2)
        m0 = _head_masks()
        qh = _by_head(m0, q_ref[...])
        row = lax.broadcasted_iota(jnp.int32, (tq, tq), 0)
        col = lax.broadcasted_iota(jnp.int32, (tq, tq), 1)
        strict = col < row
        later = _tri(min(tq, SCAN_BLOCK), lambda j, s: j > s)

        def tile(s0, R, acc, diag):
            kb = k_ref[pl.ds(s0, tq), :]
            vh = _by_head(m0, v_ref[pl.ds(s0, tq), :])
            R = list(R)
            for h in range(2):
                z = _dot(qh[h], kb, _NT)
                lb, l1m = _sb_terms(z)
                if diag:
                    l1m = jnp.where(strict, l1m, 0.0)
                suf, total = _scan_cols(l1m, later, True)
                w = jnp.exp(lb + (suf + R[h]))
                if diag:
                    w = jnp.where(strict, w, 0.0)
                acc = acc + _dot(w.astype(bf16), vh[h])
                R[h] = R[h] + total
            return R, acc

        zero = jnp.zeros((tq, 1), f32)
        R, acc = tile(pl.multiple_of(i * tq, tq), [zero, zero], jnp.zeros((tq, LANES), f32), True)

        def loop(n, carry):
            s0 = pl.multiple_of((i - 1 - n) * tq, tq)
            R, acc = tile(s0, carry[:2], carry[2], False)
            return R[0], R[1], acc

        R0, R1, acc = lax.fori_loop(0, i, loop, (R[0], R[1], acc))
        o_ref[...] = acc.astype(bf16)
        lt_ref[...] = jnp.where(m0, R0, R1)

    qs = lambda c: pl.BlockSpec((tq, LANES), lambda b, p, i: (b * nq + i, c + p))
    ks = lambda c: pl.BlockSpec((S, LANES), lambda b, p, i: (b, c + p))
    os_ = pl.BlockSpec((tq, LANES), lambda b, p, i: (b * nq + i, p))
    return _pcall(body, name="sb_fwd", grid=(B, n_pairs, nq), in_specs=[qs(qcol), ks(kcol), ks(vcol)],
                  out_specs=[os_, os_],
                  out_shape=[jax.ShapeDtypeStruct((T, n_pairs * LANES), bf16), jax.ShapeDtypeStruct((T, n_pairs * LANES), f32)],
                  compiler_params=_params(("arbitrary", "arbitrary", "arbitrary")))(qkv, qkv, qkv)


def _sb_bwd(qkv, do, lt, B, S, n_pairs, qcol, kcol, vcol, tq):
    nq = S // tq
    T = B * S

    def body(q_ref, k_ref, v_ref, do_ref, lt_ref, dq_ref, dk_ref, dv_ref):
        i = pl.program_id(2)

        @pl.when(i == 0)
        def _():
            dk_ref[...] = jnp.zeros_like(dk_ref)
            dv_ref[...] = jnp.zeros_like(dv_ref)

        m0 = _head_masks()
        qh = _by_head(m0, q_ref[...])
        doh = _by_head(m0, do_ref[...])
        lt = lt_ref[...]
        ltot = [lt[:, 0:1], lt[:, HEAD_DIM:HEAD_DIM + 1]]
        row = lax.broadcasted_iota(jnp.int32, (tq, tq), 0)
        col = lax.broadcasted_iota(jnp.int32, (tq, tq), 1)
        strict = col < row
        upto = _tri(min(tq, SCAN_BLOCK), lambda j, s: j <= s)
        before = _tri(min(tq, SCAN_BLOCK), lambda j, s: j < s)

        def tile(s0, CL, CP, dq, diag):
            kb = k_ref[pl.ds(s0, tq), :]
            vb = v_ref[pl.ds(s0, tq), :]
            kh = _by_head(m0, kb)
            CL, CP = list(CL), list(CP)
            dk = jnp.zeros((tq, LANES), f32)
            dv = jnp.zeros((tq, LANES), f32)
            for h in range(2):
                z = _dot(qh[h], kb, _NT)
                lb, l1m = _sb_terms(z)
                if diag:
                    l1m = jnp.where(strict, l1m, 0.0)
                pre, l_total = _scan_cols(l1m, upto, False)
                w = jnp.exp(lb + ((ltot[h] - CL[h]) - pre))
                if diag:
                    w = jnp.where(strict, w, 0.0)
                g = _dot(doh[h], vb, _NT) * w
                p, g_total = _scan_cols(g, before, False)
                dz = g - jnp.exp(lb) * (g + (p + CP[h]))
                if diag:
                    dz = jnp.where(strict, dz, 0.0)
                dzb = dz.astype(bf16)
                dq = dq + _dot(dzb, kh[h])
                dk = dk + _dot(dzb, qh[h], _TN)
                dv = dv + _dot(w.astype(bf16), doh[h], _TN)
                CL[h] = CL[h] + l_total
                CP[h] = CP[h] + g_total
            dk_ref[pl.ds(s0, tq), :] += dk
            dv_ref[pl.ds(s0, tq), :] += dv
            return CL, CP, dq

        zero = jnp.zeros((tq, 1), f32)

        def loop(n, carry):
            CL, CP, dq = tile(pl.multiple_of(n * tq, tq), carry[0:2], carry[2:4], carry[4], False)
            return CL[0], CL[1], CP[0], CP[1], dq

        c = lax.fori_loop(0, i, loop, (zero, zero, zero, zero, jnp.zeros((tq, LANES), f32)))
        _, _, dq = tile(pl.multiple_of(i * tq, tq), c[0:2], c[2:4], c[4], True)
        dq_ref[...] = dq * Q_SCALE

    qs = lambda c: pl.BlockSpec((tq, LANES), lambda b, p, i: (b * nq + i, c + p))
    ks = lambda c: pl.BlockSpec((S, LANES), lambda b, p, i: (b, c + p))
    ts = pl.BlockSpec((tq, LANES), lambda b, p, i: (b * nq + i, p))
    fs = pl.BlockSpec((S, LANES), lambda b, p, i: (b, p))
    shp = jax.ShapeDtypeStruct((T, n_pairs * LANES), f32)
    return _pcall(body, name="sb_bwd", grid=(B, n_pairs, nq),
                  in_specs=[qs(qcol), ks(kcol), ks(vcol), ts, ts], out_specs=[ts, fs, fs], out_shape=[shp, shp, shp],
                  compiler_params=_params(("arbitrary", "arbitrary", "arbitrary")))(qkv, qkv, qkv, do, lt)


def _fox_fwd(qkv, c, cT, B, S, n_pairs, qcol, kcol, vcol, tq):
    nq = S // tq
    T = B * S

    def body(q_ref, k_ref, v_ref, cq_ref, ck_ref, o_ref, o32_ref, lse_ref):
        p_idx = pl.program_id(1)
        i = pl.program_id(2)
        m0 = _head_masks()
        lane = lax.broadcasted_iota(jnp.int32, (1, LANES), 1)
        qh = _by_head(m0, q_ref[...])
        cq_all = cq_ref[...]
        cq = [jnp.sum(jnp.where(lane == 2 * p_idx + h, cq_all, 0.0), axis=1, keepdims=True) for h in range(2)]
        row = lax.broadcasted_iota(jnp.int32, (tq, tq), 0)
        col = lax.broadcasted_iota(jnp.int32, (tq, tq), 1)
        causal = col <= row

        def tile(s0, m, l, acc, diag):
            kb = k_ref[pl.ds(s0, tq), :]
            vh = _by_head(m0, v_ref[pl.ds(s0, tq), :])
            m, l = list(m), list(l)
            scale, add = [], []
            for h in range(2):
                z = _dot(qh[h], kb, _NT) + (cq[h] - ck_ref[h, :, pl.ds(s0, tq)])
                if diag:
                    z = jnp.where(causal, z, NEG)
                m_new = jnp.maximum(m[h], jnp.max(z, axis=1, keepdims=True))
                p = jnp.exp(z - m_new)
                a = jnp.exp(m[h] - m_new)
                l[h] = a * l[h] + jnp.sum(p, axis=1, keepdims=True)
                m[h] = m_new
                scale.append(a)
                add.append(_dot(p.astype(bf16), vh[h]))
            acc = acc * jnp.where(m0, scale[0], scale[1]) + add[0] + add[1]
            return m, l, acc

        neg = jnp.full((tq, 1), NEG, f32)
        zero = jnp.zeros((tq, 1), f32)
        m, l, acc = tile(pl.multiple_of(i * tq, tq), [neg, neg], [zero, zero], jnp.zeros((tq, LANES), f32), True)

        def loop(n, carry):
            m, l, acc = tile(pl.multiple_of(n * tq, tq), carry[0:2], carry[2:4], carry[4], False)
            return m[0], m[1], l[0], l[1], acc

        m0_, m1_, l0, l1, acc = lax.fori_loop(0, i, loop, (m[0], m[1], l[0], l[1], acc))
        o = acc * jnp.where(m0, 1.0 / l0, 1.0 / l1)
        o_ref[...] = o.astype(bf16)
        o32_ref[...] = o
        lse_ref[...] = jnp.where(m0, m0_ + jnp.log(l0), m1_ + jnp.log(l1))

    qs = lambda cc: pl.BlockSpec((tq, LANES), lambda b, p, i: (b * nq + i, cc + p))
    ks = lambda cc: pl.BlockSpec((S, LANES), lambda b, p, i: (b, cc + p))
    cqs = pl.BlockSpec((tq, LANES), lambda b, p, i: (b * nq + i, 0))
    cks = pl.BlockSpec((2, 1, S), lambda b, p, i: (b * n_pairs + p, 0, 0))
    os_ = pl.BlockSpec((tq, LANES), lambda b, p, i: (b * nq + i, p))
    shp = jax.ShapeDtypeStruct((T, n_pairs * LANES), f32)
    return _pcall(body, name="fox_fwd", grid=(B, n_pairs, nq), in_specs=[qs(qcol), ks(kcol), ks(vcol), cqs, cks],
                  out_specs=[os_, os_, os_], out_shape=[jax.ShapeDtypeStruct((T, n_pairs * LANES), bf16), shp, shp],
                  compiler_params=_params(("arbitrary", "arbitrary", "arbitrary")))(qkv, qkv, qkv, c, cT)


def _fox_bwd(qkv, c, cT, do, o, lse, B, S, n_pairs, qcol, kcol, vcol, tq):
    nq = S // tq
    T = B * S

    def body(q_ref, k_ref, v_ref, cq_ref, ck_ref, do_ref, o_ref, lse_ref, dq_ref, dk_ref, dv_ref, dc_ref):
        p_idx = pl.program_id(1)
        i = pl.program_id(2)

        @pl.when(i == 0)
        def _():
            dk_ref[...] = jnp.zeros_like(dk_ref)
            dv_ref[...] = jnp.zeros_like(dv_ref)
            dc_ref[...] = jnp.zeros_like(dc_ref)

        m0 = _head_masks()
        lane = lax.broadcasted_iota(jnp.int32, (1, LANES), 1)
        qh = _by_head(m0, q_ref[...])
        do2 = do_ref[...]
        doh = _by_head(m0, do2)
        prod = do2.astype(f32) * o_ref[...].astype(f32)
        delta = [jnp.sum(p, axis=1, keepdims=True) for p in _by_head(m0, prod)]
        ls = lse_ref[...]
        lse = [ls[:, 0:1], ls[:, HEAD_DIM:HEAD_DIM + 1]]
        cq_all = cq_ref[...]
        cq = [jnp.sum(jnp.where(lane == 2 * p_idx + h, cq_all, 0.0), axis=1, keepdims=True) for h in range(2)]
        row = lax.broadcasted_iota(jnp.int32, (tq, tq), 0)
        col = lax.broadcasted_iota(jnp.int32, (tq, tq), 1)
        causal = col <= row

        def tile(s0, dq, diag):
            kb = k_ref[pl.ds(s0, tq), :]
            vb = v_ref[pl.ds(s0, tq), :]
            kh = _by_head(m0, kb)
            dk = jnp.zeros((tq, LANES), f32)
            dv = jnp.zeros((tq, LANES), f32)
            for h in range(2):
                z = _dot(qh[h], kb, _NT) + (cq[h] - ck_ref[h, :, pl.ds(s0, tq)])
                p = jnp.exp(z - lse[h])
                if diag:
                    p = jnp.where(causal, p, 0.0)
                ds = p * (_dot(doh[h], vb, _NT) - delta[h])
                dsb = ds.astype(bf16)
                dq = dq + _dot(dsb, kh[h])
                dk = dk + _dot(dsb, qh[h], _TN)
                dv = dv + _dot(p.astype(bf16), doh[h], _TN)
                dc_ref[h, :, pl.ds(s0, tq)] -= jnp.sum(ds, axis=0, keepdims=True)
            dk_ref[pl.ds(s0, tq), :] += dk
            dv_ref[pl.ds(s0, tq), :] += dv
            return dq

        dq = lax.fori_loop(0, i, lambda n, dq: tile(pl.multiple_of(n * tq, tq), dq, False), jnp.zeros((tq, LANES), f32))
        dq = tile(pl.multiple_of(i * tq, tq), dq, True)
        dq_ref[...] = dq * Q_SCALE

    qs = lambda cc: pl.BlockSpec((tq, LANES), lambda b, p, i: (b * nq + i, cc + p))
    ks = lambda cc: pl.BlockSpec((S, LANES), lambda b, p, i: (b, cc + p))
    cqs = pl.BlockSpec((tq, LANES), lambda b, p, i: (b * nq + i, 0))
    cks = pl.BlockSpec((2, 1, S), lambda b, p, i: (b * n_pairs + p, 0, 0))
    ts = pl.BlockSpec((tq, LANES), lambda b, p, i: (b * nq + i, p))
    fs = pl.BlockSpec((S, LANES), lambda b, p, i: (b, p))
    shp = jax.ShapeDtypeStruct((T, n_pairs * LANES), f32)
    return _pcall(body, name="fox_bwd", grid=(B, n_pairs, nq),
                  in_specs=[qs(qcol), ks(kcol), ks(vcol), cqs, cks, ts, ts, ts], out_specs=[ts, fs, fs, cks],
                  out_shape=[shp, shp, shp, jax.ShapeDtypeStruct(cT.shape, f32)],
                  compiler_params=_params(("arbitrary", "arbitrary", "arbitrary")))(qkv, qkv, qkv, c, cT, do, o, lse)


def _sigmoid(x):
    return 1.0 / (1.0 + jnp.exp(-x))


def _mix_fwd(o_sb, o_fx, g, x, wp_sb, wp_fx, w_out, ln_g, ln_b, tm):
    T, D = x.shape
    E = o_sb.shape[1]
    tm = _tile(T, tm, 8)

    def body(osb_ref, ofx_ref, gsb_ref, gfx_ref, x_ref, wsb_ref, wfx_ref, wo_ref, lg_ref, lb_ref,
             xhat_ref, rstd_ref, x1_ref, mg_ref):
        y_sb = _dot(osb_ref[...], wsb_ref[...])
        y_fx = _dot(ofx_ref[...], wfx_ref[...])
        merged = (_sigmoid(gsb_ref[...]) * y_sb + _sigmoid(gfx_ref[...]) * y_fx).astype(bf16)
        r = ALPHA * x_ref[...] + _dot(merged, wo_ref[...])
        mean = jnp.mean(r, axis=1, keepdims=True)
        cen = r - mean
        rstd = lax.rsqrt(jnp.mean(cen * cen, axis=1, keepdims=True) + LN_EPS)
        xhat = cen * rstd
        xhat_ref[...] = xhat
        rstd_ref[...] = rstd
        x1_ref[...] = (xhat * lg_ref[...] + lb_ref[...]).astype(bf16)
        mg_ref[...] = merged

    rows = lambda w, c=0: pl.BlockSpec((tm, w), lambda i: (i, c))
    full = lambda a: pl.BlockSpec(a.shape, lambda i: (0, 0))
    return _pcall(body, name="mix_fwd", grid=(T // tm,),
                  in_specs=[rows(E), rows(E), rows(D, 0), rows(D, 1), rows(D), full(wp_sb), full(wp_fx), full(w_out),
                            full(ln_g), full(ln_b)],
                  out_specs=[rows(D), rows(1), rows(D), rows(D)],
                  out_shape=[jax.ShapeDtypeStruct((T, D), f32), jax.ShapeDtypeStruct((T, 1), f32),
                             jax.ShapeDtypeStruct((T, D), bf16), jax.ShapeDtypeStruct((T, D), bf16)],
                  compiler_params=_params(("arbitrary",)))(o_sb, o_fx, g, g, x, wp_sb, wp_fx, w_out, ln_g, ln_b)


def _mix_bwd(dr1, o_sb, o_fx, g, wp_sb, wp_fx, w_out, tm):
    T, D = dr1.shape
    E = o_sb.shape[1]
    tm = _tile(T, tm, 8)

    def body(dr_ref, osb_ref, ofx_ref, gsb_ref, gfx_ref, wsb_ref, wfx_ref, wo_ref,
             dysb_ref, dyfx_ref, dgsb_ref, dgfx_ref, dosb_ref, dofx_ref, sumsb_ref, sumfx_ref):
        @pl.when(pl.program_id(0) == 0)
        def _():
            sumsb_ref[...] = jnp.zeros_like(sumsb_ref)
            sumfx_ref[...] = jnp.zeros_like(sumfx_ref)
        dm = _dot(dr_ref[...].astype(bf16), wo_ref[...], _NT)
        for o_ref, g_ref, w_ref, dy_ref, dg_ref, do_ref, sum_ref in (
                (osb_ref, gsb_ref, wsb_ref, dysb_ref, dgsb_ref, dosb_ref, sumsb_ref),
                (ofx_ref, gfx_ref, wfx_ref, dyfx_ref, dgfx_ref, dofx_ref, sumfx_ref)):
            y = _dot(o_ref[...], w_ref[...])
            s = _sigmoid(g_ref[...])
            dy = (dm * s).astype(bf16)
            dy_ref[...] = dy
            dg = dm * y * s * (1.0 - s)
            dg_ref[...] = dg.astype(bf16)
            sum_ref[0:1, :] += jnp.sum(dg, axis=0, keepdims=True)
            do_ref[...] = _dot(dy, w_ref[...], _NT).astype(bf16)

    rows = lambda w, c=0: pl.BlockSpec((tm, w), lambda i: (i, c))
    full = lambda a: pl.BlockSpec(a.shape, lambda i: (0, 0))
    acc = pl.BlockSpec((8, D), lambda i: (0, 0))
    res = _pcall(body, name="mix_bwd", grid=(T // tm,),
                 in_specs=[rows(D), rows(E), rows(E), rows(D, 0), rows(D, 1), full(wp_sb), full(wp_fx), full(w_out)],
                 out_specs=[rows(D), rows(D), rows(D), rows(D), rows(E), rows(E), acc, acc],
                 out_shape=[jax.ShapeDtypeStruct((T, D), bf16)] * 4 + [jax.ShapeDtypeStruct((T, E), bf16)] * 2
                 + [jax.ShapeDtypeStruct((8, D), f32)] * 2,
                 compiler_params=_params(("arbitrary",)))(dr1, o_sb, o_fx, g, g, wp_sb, wp_fx, w_out)
    return res


def _ln_bwd(dy_a, dy_b, scale_b, xhat, rstd, ln_g, tm):
    T, D = xhat.shape
    tm = _tile(T, tm, 8)

    def body(a_ref, b_ref, xh_ref, rs_ref, g_ref, dr_ref, st_ref):
        @pl.when(pl.program_id(0) == 0)
        def _():
            st_ref[...] = jnp.zeros_like(st_ref)
        dy = a_ref[...] + scale_b * b_ref[...]
        xh = xh_ref[...]
        dxh = dy * g_ref[...]
        m1 = jnp.mean(dxh, axis=1, keepdims=True)
        m2 = jnp.mean(dxh * xh, axis=1, keepdims=True)
        dr_ref[...] = rs_ref[...] * (dxh - m1 - xh * m2)
        st_ref[0:1, :] += jnp.sum(dy * xh, axis=0, keepdims=True)
        st_ref[1:2, :] += jnp.sum(dy, axis=0, keepdims=True)

    rows = lambda w: pl.BlockSpec((tm, w), lambda i: (i, 0))
    return _pcall(body, name="ln1_bwd", grid=(T // tm,),
                  in_specs=[rows(D), rows(D), rows(D), rows(1), pl.BlockSpec((1, D), lambda i: (0, 0))],
                  out_specs=[rows(D), pl.BlockSpec((8, D), lambda i: (0, 0))],
                  out_shape=[jax.ShapeDtypeStruct((T, D), f32), jax.ShapeDtypeStruct((8, D), f32)],
                  compiler_params=_params(("arbitrary",)))(dy_a, dy_b, xhat, rstd, ln_g)


_INV_SQRT2 = 1.0 / math.sqrt(2.0)
_INV_SQRT2PI = 1.0 / math.sqrt(2.0 * math.pi)


def _conv_rows(ref, r0, rc, first, wc, bc):
    cur = ref[pl.ds(r0, rc), :]
    prev = ref[pl.ds(pl.multiple_of(jnp.maximum(r0 - 8, 0), 8), 8), :]
    prev = jnp.where(first, jnp.zeros_like(prev), prev)
    rid = lax.broadcasted_iota(jnp.int32, (rc, LANES), 0)
    s1 = jnp.where(rid == 0, prev[7:8, :], pltpu.roll(cur, 1, 0))
    s2 = jnp.where(rid == 0, prev[6:7, :], jnp.where(rid == 1, prev[7:8, :], pltpu.roll(cur, 2, 0)))
    conv = bc + wc[0:1, :] * s2 + wc[1:2, :] * s1 + wc[2:3, :] * cur
    return conv, (s2, s1, cur)


def _glu_fwd(u, w_conv, b_conv, B, S, rc=512):
    F = u.shape[1] // 2
    nf = F // LANES
    rc = _tile(S, rc, 8)

    def body(ug_ref, uv_ref, wc_ref, bc_ref, a_ref):
        wc, bc = wc_ref[...], bc_ref[...]

        def chunk(n, _):
            r0 = pl.multiple_of(n * rc, rc)
            c, _taps = _conv_rows(ug_ref, r0, rc, n == 0, wc, bc)
            gelu = 0.5 * c * (1.0 + lax.erf(c * _INV_SQRT2))
            a_ref[pl.ds(r0, rc), :] = (gelu * uv_ref[pl.ds(r0, rc), :]).astype(bf16)
            return 0

        lax.fori_loop(0, S // rc, chunk, 0)

    return _pcall(body, name="glu_fwd", grid=(B, nf),
                  in_specs=[pl.BlockSpec((S, LANES), lambda b, j: (b, j)), pl.BlockSpec((S, LANES), lambda b, j: (b, nf + j)),
                            pl.BlockSpec((3, LANES), lambda b, j: (0, j)), pl.BlockSpec((1, LANES), lambda b, j: (0, j))],
                  out_specs=pl.BlockSpec((S, LANES), lambda b, j: (b, j)),
                  out_shape=jax.ShapeDtypeStruct((B * S, F), bf16),
                  compiler_params=_params(("arbitrary", "arbitrary")))(u, u, w_conv, b_conv)


def _glu_bwd(u, da, w_conv, b_conv, B, S, rc=512):
    F = u.shape[1] // 2
    nf = F // LANES
    rc = _tile(S, rc, 8)
    nc = S // rc

    def body(ug_ref, uv_ref, da_ref, wc_ref, bc_ref, dug_ref, duv_ref, gw_ref, gb_ref, dc_ref):
        wc, bc = wc_ref[...], bc_ref[...]

        def chunk(n, carry):
            gw0, gw1, gw2, gb = carry
            r0 = pl.multiple_of(n * rc, rc)
            c, (s2, s1, cur) = _conv_rows(ug_ref, r0, rc, n == 0, wc, bc)
            cdf = 0.5 * (1.0 + lax.erf(c * _INV_SQRT2))
            da = da_ref[pl.ds(r0, rc), :]
            duv_ref[pl.ds(r0, rc), :] = (da * (c * cdf)).astype(bf16)
            dc = da * uv_ref[pl.ds(r0, rc), :] * (cdf + c * (_INV_SQRT2PI * jnp.exp(-0.5 * c * c)))
            dc_ref[pl.ds(r0, rc), :] = dc
            red = lambda t: jnp.sum(t, axis=0, keepdims=True)
            return gw0 + red(dc * s2), gw1 + red(dc * s1), gw2 + red(dc * cur), gb + red(dc)

        z = jnp.zeros((1, LANES), f32)
        gw0, gw1, gw2, gb = lax.fori_loop(0, nc, chunk, (z, z, z, z))
        gw_ref[0, 0:1, :] = gw0
        gw_ref[0, 1:2, :] = gw1
        gw_ref[0, 2:3, :] = gw2
        gb_ref[0] = gb

        def chunk2(n, _):
            r0 = pl.multiple_of(n * rc, rc)
            cur = dc_ref[pl.ds(r0, rc), :]
            nxt = dc_ref[pl.ds(pl.multiple_of(jnp.minimum(r0 + rc, S - 8), 8), 8), :]
            nxt = jnp.where(n == nc - 1, jnp.zeros_like(nxt), nxt)
            rid = lax.broadcasted_iota(jnp.int32, (rc, LANES), 0)
            a1 = jnp.where(rid == rc - 1, nxt[0:1, :], pltpu.roll(cur, rc - 1, 0))
            a2 = jnp.where(rid == rc - 1, nxt[1:2, :], jnp.where(rid == rc - 2, nxt[0:1, :], pltpu.roll(cur, rc - 2, 0)))
            dug_ref[pl.ds(r0, rc), :] = (wc[2:3, :] * cur + wc[1:2, :] * a1 + wc[0:1, :] * a2).astype(bf16)
            return 0

        lax.fori_loop(0, nc, chunk2, 0)

    blk = lambda off: pl.BlockSpec((S, LANES), lambda b, j: (b, off + j))
    return _pcall(body, name="glu_bwd", grid=(B, nf),
                  in_specs=[blk(0), blk(nf), blk(0), pl.BlockSpec((3, LANES), lambda b, j: (0, j)),
                            pl.BlockSpec((1, LANES), lambda b, j: (0, j))],
                  out_specs=[blk(0), blk(0), pl.BlockSpec((1, 3, LANES), lambda b, j: (b, 0, j)),
                             pl.BlockSpec((1, 1, LANES), lambda b, j: (b, 0, j))],
                  out_shape=[jax.ShapeDtypeStruct((B * S, F), bf16), jax.ShapeDtypeStruct((B * S, F), bf16),
                             jax.ShapeDtypeStruct((B, 3, F), f32), jax.ShapeDtypeStruct((B, 1, F), f32)],
                  scratch_shapes=[pltpu.VMEM((S, LANES), f32)],
                  compiler_params=_params(("arbitrary", "arbitrary")))(u, u, da, w_conv, b_conv)


def _down_loss(a, w_down, xhat1, ln1_g, ln1_b, ln2_g, ln2_b, tgt, tm):
    T, D = xhat1.shape
    F = a.shape[1]
    tm = _tile(T, tm, 8)

    def body(a_ref, w_ref, xh_ref, g1_ref, b1_ref, g2_ref, b2_ref, t_ref, dr_ref, st_ref):
        @pl.when(pl.program_id(0) == 0)
        def _():
            st_ref[...] = jnp.zeros_like(st_ref)
        x1 = xh_ref[...] * g1_ref[...] + b1_ref[...]
        r = ALPHA * x1 + _dot(a_ref[...], w_ref[...])
        mean = jnp.mean(r, axis=1, keepdims=True)
        cen = r - mean
        rstd = lax.rsqrt(jnp.mean(cen * cen, axis=1, keepdims=True) + LN_EPS)
        xh = cen * rstd
        err = (xh * g2_ref[...] + b2_ref[...]) - t_ref[...]
        dy = err * (1.0 / D)
        dxh = dy * g2_ref[...]
        m1 = jnp.mean(dxh, axis=1, keepdims=True)
        m2 = jnp.mean(dxh * xh, axis=1, keepdims=True)
        dr_ref[...] = rstd * (dxh - m1 - xh * m2)
        st_ref[0:1, :] += jnp.sum(dy * xh, axis=0, keepdims=True)
        st_ref[1:2, :] += jnp.sum(dy, axis=0, keepdims=True)
        st_ref[2:3, :] += jnp.sum(err * err, axis=0, keepdims=True)

    rows = lambda w: pl.BlockSpec((tm, w), lambda i: (i, 0))
    vec = pl.BlockSpec((1, D), lambda i: (0, 0))
    return _pcall(body, name="down_loss", grid=(T // tm,),
                  in_specs=[rows(F), pl.BlockSpec((F, D), lambda i: (0, 0)), rows(D), vec, vec, vec, vec, rows(D)],
                  out_specs=[rows(D), pl.BlockSpec((8, D), lambda i: (0, 0))],
                  out_shape=[jax.ShapeDtypeStruct((T, D), f32), jax.ShapeDtypeStruct((8, D), f32)],
                  compiler_params=_params(("arbitrary",)))(a, w_down, xhat1, ln1_g, ln1_b, ln2_g, ln2_b, tgt)


def _local_step(x, tgt, w_in, b_in, wp_sb, wp_fx, w_out, ln1_g, ln1_b, w_up, w_conv, b_conv, w_down, ln2_g, ln2_b,
                tq=512, tm=256):
    B, S, D = x.shape
    T = B * S
    E = wp_sb.shape[0]
    n_pairs = E // LANES
    NH = E // HEAD_DIM
    F = w_down.shape[0]
    x2 = x.reshape(T, D)
    tgt2 = tgt.reshape(T, D)
    tq = _tile(S, tq, 8)

    c_f, c_g = 6 * E, 6 * E + NH
    w_qkv = w_in[:, :c_f]
    qscale = jnp.concatenate([jnp.full((E,), Q_SCALE, f32), jnp.ones((2 * E,), f32)] * 2)
    w_qkv_s = (w_qkv.astype(f32) * qscale).astype(bf16)
    b_qkv_s = b_in[:, :c_f] * qscale
    w_f = jnp.pad(w_in[:, c_f:c_g], ((0, 0), (0, LANES - NH)))
    b_f = jnp.pad(b_in[:, c_f:c_g], ((0, 0), (0, LANES - NH)))
    w_g = w_in[:, c_g:]
    b_g = b_in[:, c_g:]

    xb = x2.astype(bf16)
    qkv = _matmul(xb, w_qkv_s, bias=b_qkv_s, out_dtype=bf16, tm=1024, tn=512, name="proj_qkv")
    g = _matmul(xb, w_g, bias=b_g, tm=1024, tn=512, name="proj_gate")
    fl = _matmul(xb, w_f, bias=b_f, tm=1024, name="proj_forget")
    c = _cumlogf(fl, B, S)
    cT = c.reshape(B, S, LANES)[:, :, :NH].transpose(0, 2, 1).reshape(B * NH, 1, S)
    P = n_pairs
    o_sb, lt_sb = _sb_fwd(qkv, B, S, P, 0, P, 2 * P, tq)
    o_fx, o_fx32, lse_fx = _fox_fwd(qkv, c, cT, B, S, P, 3 * P, 4 * P, 5 * P, tq)
    xhat1, rstd1, x1b, merged = _mix_fwd(o_sb, o_fx, g, x2, wp_sb, wp_fx, w_out, ln1_g, ln1_b, tm)
    u = _matmul(x1b, w_up, tm=1024, tn=512, name="ffn_up")
    a = _glu_fwd(u, w_conv, b_conv, B, S)
    dr2, st2 = _down_loss(a, w_down, xhat1, ln1_g, ln1_b, ln2_g, ln2_b, tgt2, tm)

    grads = {}
    grads["ln2_g"], grads["ln2_b"] = st2[0:1], st2[1:2]
    sq_err = st2[2:3]
    da = _matmul(dr2, w_down, tb=True, tm=1024, tn=512, name="ffn_da")
    grads["w_down"] = _matmul(a, dr2, ta=True, tm=1408, tn=1024, tk=1024, name="grad_w_down")
    du_g, du_v, gwc, gbc = _glu_bwd(u, da, w_conv, b_conv, B, S)
    grads["w_conv"] = jnp.sum(gwc, axis=0)
    grads["b_conv"] = jnp.sum(gbc, axis=0)
    grads["w_up"] = jnp.concatenate(
        [_matmul(x1b, du_g, ta=True, tm=512, tn=2816, tk=1024, name="grad_w_up_gate"),
         _matmul(x1b, du_v, ta=True, tm=512, tn=2816, tk=1024, name="grad_w_up_val")], axis=1)
    dx1 = _matmul(du_g, w_up[:, :F], tb=True, tm=1024, tn=512, tk=F, name="ffn_dx_gate")
    dx1 = _matmul(du_v, w_up[:, F:], tb=True, addend=dx1, tm=1024, tn=512, tk=F, name="ffn_dx_val")
    dr1, st1 = _ln_bwd(dx1, dr2, ALPHA, xhat1, rstd1, ln1_g, tm)
    grads["ln1_g"], grads["ln1_b"] = st1[0:1], st1[1:2]

    dy_sb, dy_fx, dg_sb, dg_fx, do_sb, do_fx, gsum_sb, gsum_fx = _mix_bwd(dr1, o_sb, o_fx, g, wp_sb, wp_fx, w_out, tm)
    grads["w_out"] = _matmul(merged, dr1, ta=True, tm=512, tn=1024, tk=1024, name="grad_w_out")
    grads["w_proj_sb"] = _matmul(o_sb, dy_sb, ta=True, tm=512, tn=1024, tk=1024, name="grad_w_proj_sb")
    grads["w_proj_fox"] = _matmul(o_fx, dy_fx, ta=True, tm=512, tn=1024, tk=1024, name="grad_w_proj_fox")
    dq_s, dk_s, dv_s = _sb_bwd(qkv, do_sb, lt_sb, B, S, P, 0, P, 2 * P, tq)
    dq_f, dk_f, dv_f, dcT = _fox_bwd(qkv, c, cT, do_fx, o_fx32, lse_fx, B, S, P, 3 * P, 4 * P, 5 * P, tq)
    dc = jnp.pad(dcT.reshape(B, NH, S).transpose(0, 2, 1), ((0, 0), (0, 0), (0, LANES - NH))).reshape(T, LANES)
    dfl = _cumlogf_bwd(dc, fl, B, S)
    dqkv = jnp.concatenate([dq_s, dk_s, dv_s, dq_f, dk_f, dv_f], axis=1)
    dg = jnp.concatenate([dg_sb, dg_fx], axis=1)
    gw_qkv, gb_qkv = _matmul(x2, dqkv, ta=True, colsum=True, tm=512, tn=3072, tk=512, name="grad_w_qkv")
    gw_f, gb_f = _matmul(x2, dfl, ta=True, colsum=True, tm=512, tk=1024, name="grad_w_forget")
    gw_g = _matmul(x2, dg, ta=True, tm=512, tn=2048, tk=1024, name="grad_w_gate")
    grads["w_in"] = jnp.concatenate([gw_qkv, gw_f[:, :NH], gw_g], axis=1)
    grads["b_in"] = jnp.concatenate([gb_qkv, gb_f[:, :NH], gsum_sb[0:1], gsum_fx[0:1]], axis=1)
    dx = _matmul(dqkv, w_qkv, tb=True, addend=dr1, addend_scale=ALPHA, tm=512, tn=512, tk=c_f, name="dx_qkv")
    dx = _matmul(dfl, w_f, tb=True, addend=dx, tm=1024, tn=512, name="dx_forget")
    dx = _matmul(dg, w_g, tb=True, addend=dx, tm=1024, tn=512, tk=2 * D, name="dx_gate")
    return sq_err, dx.reshape(B, S, D), grads


_ANY = pl.BlockSpec(memory_space=pl.ANY)
_MESH = pl.DeviceIdType.MESH


def _pos():
    return lax.axis_index("x"), lax.axis_index("y"), lax.axis_index("c")


def _other_chips(x, y):
    return [(1 - x, y), (x, 1 - y), (1 - x, 1 - y)]


def _gather_shards(shards):
    n = len(shards)

    def body(*refs):
        srcs, dsts = refs[:n], refs[n:2 * n]
        send_sems, recv_sems, local_sems = refs[2 * n:]
        x, y, c = _pos()
        k = 2 * x + y
        local = [pltpu.make_async_copy(srcs[a], dsts[a].at[k], local_sems.at[a]) for a in range(n)]
        for cp in local:
            cp.start()

        def copy(a, j, chip, slot):
            return pltpu.make_async_remote_copy(src_ref=srcs[a], dst_ref=dsts[a].at[slot], send_sem=send_sems.at[a, j],
                                                recv_sem=recv_sems.at[a, j], device_id=(*chip, c), device_id_type=_MESH)

        chips = _other_chips(x, y)
        sends = [copy(a, j, chip, k) for a in range(n) for j, chip in enumerate(chips)]
        for cp in sends:
            cp.start()
        for a in range(n):
            for j, chip in enumerate(chips):
                copy(a, j, chip, 2 * chip[0] + chip[1]).wait_recv()
        for cp in sends:
            cp.wait_send()
        for cp in local:
            cp.wait()

    return _pcall(body, name="gather_weights", in_specs=[_ANY] * n, out_specs=[_ANY] * n,
                  out_shape=[jax.ShapeDtypeStruct((4,) + s.shape, s.dtype) for s in shards],
                  scratch_shapes=[pltpu.SemaphoreType.DMA((n, 3)), pltpu.SemaphoreType.DMA((n, 3)), pltpu.SemaphoreType.DMA((n,))],
                  compiler_params=pltpu.CompilerParams(has_side_effects=True))(*shards)


def _exchange_sibling_and_small(pieces, small):
    n = len(pieces)

    def body(*refs):
        p_refs, sm_ref = refs[:n], refs[n]
        got_refs, sg_ref = refs[n + 1:2 * n + 1], refs[2 * n + 1]
        big_send, big_recv, send_sems, recv_sems, local_sem = refs[2 * n + 2:]
        x, y, c = _pos()
        me = 4 * x + 2 * y + c
        big = []
        for a in range(n):
            h = pieces[a].shape[1] // 2
            src = p_refs[a].at[:, pl.ds(pl.multiple_of((1 - c) * h, 8), h), :]
            big.append(pltpu.make_async_remote_copy(src_ref=src, dst_ref=got_refs[a], send_sem=big_send.at[a],
                                                    recv_sem=big_recv.at[a], device_id=(x, y, 1 - c), device_id_type=_MESH))
        for cp in big:
            cp.start()
        local = pltpu.make_async_copy(sm_ref, sg_ref.at[me], local_sem)
        local.start()
        peers = []
        for r in range(1, 8):
            flip = lambda v, bit: 1 - v if bit else v
            peers.append((flip(x, r & 4), flip(y, r & 2), flip(c, r & 1)))

        def copy(j, slot):
            return pltpu.make_async_remote_copy(src_ref=sm_ref, dst_ref=sg_ref.at[slot], send_sem=send_sems.at[j],
                                                recv_sem=recv_sems.at[j], device_id=peers[j], device_id_type=_MESH)

        sends = [copy(j, me) for j in range(7)]
        for cp in sends:
            cp.start()
        for j, (px, py, pc) in enumerate(peers):
            copy(j, 4 * px + 2 * py + pc).wait_recv()
        for cp in sends:
            cp.wait_send()
        local.wait()
        for cp in big:
            cp.wait()

    halves = [jax.ShapeDtypeStruct((4, p.shape[1] // 2, p.shape[2]), p.dtype) for p in pieces]
    res = _pcall(body, name="exchange_sibling", in_specs=[_ANY] * (n + 1), out_specs=[_ANY] * (n + 1),
                 out_shape=halves + [jax.ShapeDtypeStruct((8,) + small.shape, small.dtype)],
                 scratch_shapes=[pltpu.SemaphoreType.DMA((n,)), pltpu.SemaphoreType.DMA((n,)), pltpu.SemaphoreType.DMA((7,)),
                                 pltpu.SemaphoreType.DMA((7,)), pltpu.SemaphoreType.DMA(())],
                 compiler_params=pltpu.CompilerParams(has_side_effects=True))(*pieces, small)
    return res[:n], res[n]


def _exchange_chips(pieces):
    n = len(pieces)

    def body(*refs):
        p_refs, got_refs = refs[:n], refs[n:2 * n]
        send_sems, recv_sems, local_sems = refs[2 * n:]
        x, y, c = _pos()
        k = 2 * x + y
        local = [pltpu.make_async_copy(p_refs[a].at[k], got_refs[a].at[k], local_sems.at[a]) for a in range(n)]
        for cp in local:
            cp.start()
        chips = _other_chips(x, y)

        def copy(a, j, chip, piece, slot):
            return pltpu.make_async_remote_copy(src_ref=p_refs[a].at[piece], dst_ref=got_refs[a].at[slot],
                                                send_sem=send_sems.at[a, j], recv_sem=recv_sems.at[a, j],
                                                device_id=(*chip, c), device_id_type=_MESH)

        sends = [copy(a, j, chip, 2 * chip[0] + chip[1], k) for a in range(n) for j, chip in enumerate(chips)]
        for cp in sends:
            cp.start()
        for a in range(n):
            for j, chip in enumerate(chips):
                copy(a, j, chip, k, 2 * chip[0] + chip[1]).wait_recv()
        for cp in sends:
            cp.wait_send()
        for cp in local:
            cp.wait()

    return _pcall(body, name="exchange_chips", in_specs=[_ANY] * n, out_specs=[_ANY] * n,
                  out_shape=[jax.ShapeDtypeStruct(p.shape, p.dtype) for p in pieces],
                  scratch_shapes=[pltpu.SemaphoreType.DMA((n, 3)), pltpu.SemaphoreType.DMA((n, 3)), pltpu.SemaphoreType.DMA((n,))],
                  compiler_params=pltpu.CompilerParams(has_side_effects=True))(*pieces)


def _share_halves(halves):
    n = len(halves)

    def body(*refs):
        h_refs, full_refs = refs[:n], refs[n:2 * n]
        send_sems, recv_sems, local_sems = refs[2 * n:]
        x, y, c = _pos()

        def rows(a, half):
            h = halves[a].shape[0]
            return full_refs[a].at[pl.ds(pl.multiple_of(half * h, 8), h), :]

        def copy(a, half):
            return pltpu.make_async_remote_copy(src_ref=h_refs[a], dst_ref=rows(a, half), send_sem=send_sems.at[a],
                                                recv_sem=recv_sems.at[a], device_id=(x, y, 1 - c), device_id_type=_MESH)

        local = [pltpu.make_async_copy(h_refs[a], rows(a, c), local_sems.at[a]) for a in range(n)]
        sends = [copy(a, c) for a in range(n)]
        for cp in local + sends:
            cp.start()
        for a in range(n):
            copy(a, 1 - c).wait_recv()
        for cp in sends:
            cp.wait_send()
        for cp in local:
            cp.wait()

    return _pcall(body, name="share_halves", in_specs=[_ANY] * n, out_specs=[_ANY] * n,
                  out_shape=[jax.ShapeDtypeStruct((2 * h.shape[0], h.shape[1]), h.dtype) for h in halves],
                  scratch_shapes=[pltpu.SemaphoreType.DMA((n,)), pltpu.SemaphoreType.DMA((n,)), pltpu.SemaphoreType.DMA((n,))],
                  compiler_params=pltpu.CompilerParams(has_side_effects=True))(*halves)


def _add_own_half(piece, got, core, name):
    _, r, cols = piece.shape
    h = r // 2

    def body(c_ref, a_ref, b_ref, o_ref, o16_ref):
        s = a_ref[0] + b_ref[...]
        o_ref[...] = s
        o16_ref[...] = s.astype(bf16)

    out = pl.BlockSpec((1, h, cols), lambda k, c: (k, 0, 0))
    grid_spec = pltpu.PrefetchScalarGridSpec(
        num_scalar_prefetch=1, grid=(4,),
        in_specs=[pl.BlockSpec((1, 1, h, cols), lambda k, c: (k, c[0], 0, 0)), out], out_specs=[out, out])
    return _pcall(body, name=name, grid_spec=grid_spec,
                  out_shape=[jax.ShapeDtypeStruct((4, h, cols), f32), jax.ShapeDtypeStruct((4, h, cols), bf16)],
                  compiler_params=_params(("arbitrary",)))(core, piece.reshape(4, 2, h, cols), got)


def _sum_chips(own, got, chip, name):
    _, h, cols = own.shape
    t = _tile(h, 128, 16)

    def body(k_ref, own_ref, got_ref, o_ref):
        acc = None
        for k in range(4):
            term = jnp.where(k_ref[0] == k, own_ref[0], got_ref[k].astype(f32))
            acc = term if acc is None else acc + term
        o_ref[...] = acc

    grid_spec = pltpu.PrefetchScalarGridSpec(
        num_scalar_prefetch=1, grid=(h // t,),
        in_specs=[pl.BlockSpec((1, t, cols), lambda i, k: (k[0], i, 0)), pl.BlockSpec((4, t, cols), lambda i, k: (0, i, 0))],
        out_specs=pl.BlockSpec((t, cols), lambda i, k: (i, 0)))
    return _pcall(body, name=name, grid_spec=grid_spec, out_shape=jax.ShapeDtypeStruct((h, cols), f32),
                  compiler_params=_params(("arbitrary",)))(chip, own, got)


def _sum_slots(stack, name):
    k, n, cols = stack.shape
    t = _tile(n, 128, 8)

    def body(s_ref, o_ref):
        acc = s_ref[0]
        for i in range(1, k):
            acc = acc + s_ref[i]
        o_ref[...] = acc

    return _pcall(body, name=name, grid=(n // t,), in_specs=[pl.BlockSpec((k, t, cols), lambda i: (0, i, 0))],
                  out_specs=pl.BlockSpec((t, cols), lambda i: (i, 0)), out_shape=jax.ShapeDtypeStruct((n, cols), f32),
                  compiler_params=_params(("arbitrary",)))(stack)


def _adamw(w, g, m, v, name):
    n, cols = w.shape
    t = _tile(n, 128, 8)
    c1 = 1.0 - ADAM_B1 ** ADAM_STEP
    c2 = 1.0 - ADAM_B2 ** ADAM_STEP

    def body(w_ref, g_ref, m_ref, v_ref, d_ref, nm_ref, nv_ref):
        g = g_ref[...]
        nm = ADAM_B1 * m_ref[...] + (1.0 - ADAM_B1) * g
        nv = ADAM_B2 * v_ref[...] + (1.0 - ADAM_B2) * (g * g)
        d_ref[...] = -ADAM_LR * ((nm / c1) / (jnp.sqrt(nv / c2) + ADAM_EPS) + ADAM_WD * w_ref[...])
        nm_ref[...] = nm
        nv_ref[...] = nv

    spec = pl.BlockSpec((t, cols), lambda i: (i, 0))
    shp = jax.ShapeDtypeStruct((n, cols), f32)
    return _pcall(body, name=name, grid=(n // t,), in_specs=[spec] * 4, out_specs=[spec] * 3, out_shape=[shp] * 3,
                  compiler_params=_params(("arbitrary",)))(w, g, m, v)


_MATS = (("w_in", 1), ("w_proj_sb", 1), ("w_proj_fox", 1), ("w_out", 0), ("w_up", 1), ("w_down", 0))
_SMALL = ("b_in", "ln1_g", "ln1_b", "b_conv", "ln2_g", "ln2_b")


def _pad_lanes(v):
    n = v.shape[-1]
    return jnp.pad(v, ((0, 0), (0, (-n) % LANES)))


def _pack_rows(vectors):
    flat = jnp.concatenate([_pad_lanes(v.reshape(1, -1)) for v in vectors], axis=1).reshape(-1, LANES)
    return jnp.pad(flat, ((0, (-flat.shape[0]) % 8), (0, 0)))


def _unpack_rows(packed, sizes):
    out, r = [], 0
    for n in sizes:
        rows = -(-n // LANES)
        out.append(packed[r:r + rows].reshape(1, rows * LANES)[:, :n])
        r += rows
    return out


def _unstack(stack, axis):
    if axis == 0:
        return stack.reshape(-1, stack.shape[2])
    return jnp.concatenate([stack[k] for k in range(4)], axis=1)


def _pieces(g, axis):
    if axis == 0:
        return g.reshape(4, g.shape[0] // 4, g.shape[1])
    cols = g.shape[1] // 4
    return jnp.stack([g[:, k * cols:(k + 1) * cols] for k in range(4)])


def kernel(x, w_in, b_in, w_proj_sb, w_proj_fox, w_out, ln1_g, ln1_b, w_up, w_conv, b_conv, w_down, ln2_g, ln2_b, loss_target, m_w_in, m_b_in, m_w_proj_sb, m_w_proj_fox, m_w_out, m_ln1_g, m_ln1_b, m_w_up, m_w_conv, m_b_conv, m_w_down, m_ln2_g, m_ln2_b, v_w_in, v_b_in, v_w_proj_sb, v_w_proj_fox, v_w_out, v_ln1_g, v_ln1_b, v_w_up, v_w_conv, v_b_conv, v_w_down, v_ln2_g, v_ln2_b):
    w = dict(w_in=w_in, b_in=b_in, w_proj_sb=w_proj_sb, w_proj_fox=w_proj_fox, w_out=w_out, ln1_g=ln1_g, ln1_b=ln1_b,
             w_up=w_up, w_conv=w_conv, b_conv=b_conv, w_down=w_down, ln2_g=ln2_g, ln2_b=ln2_b)
    m = dict(w_in=m_w_in, b_in=m_b_in, w_proj_sb=m_w_proj_sb, w_proj_fox=m_w_proj_fox, w_out=m_w_out, ln1_g=m_ln1_g,
             ln1_b=m_ln1_b, w_up=m_w_up, w_conv=m_w_conv, b_conv=m_b_conv, w_down=m_w_down, ln2_g=m_ln2_g, ln2_b=m_ln2_b)
    v = dict(w_in=v_w_in, b_in=v_b_in, w_proj_sb=v_w_proj_sb, w_proj_fox=v_w_proj_fox, w_out=v_w_out, ln1_g=v_ln1_g,
             ln1_b=v_ln1_b, w_up=v_w_up, w_conv=v_w_conv, b_conv=v_b_conv, w_down=v_w_down, ln2_g=v_ln2_g, ln2_b=v_ln2_b)
    order = ["w_in", "b_in", "w_proj_sb", "w_proj_fox", "w_out", "ln1_g", "ln1_b", "w_up", "w_conv", "b_conv", "w_down",
             "ln2_g", "ln2_b"]
    x_idx, y_idx, c_idx = _pos()
    chip = 2 * x_idx + y_idx
    D = x.shape[-1]
    core = c_idx.astype(jnp.int32).reshape(1)

    stacks = _gather_shards([w[n][0].astype(bf16) for n, _ in _MATS] + [w["w_conv"][0]])
    full = {n: _unstack(s, axis) for (n, axis), s in zip(_MATS, stacks)}
    w_conv_full = _unstack(stacks[-1], 1)

    sq_err, grad_x, grads = _local_step(
        x, loss_target, full["w_in"], w["b_in"], full["w_proj_sb"], full["w_proj_fox"], full["w_out"], w["ln1_g"], w["ln1_b"],
        full["w_up"], w_conv_full, w["b_conv"], full["w_down"], w["ln2_g"], w["ln2_b"])
    loss_part = (0.5 / D) * jnp.sum(sq_err)

    pieces = [_pieces(grads[n], axis) for n, axis in _MATS]
    small_names = list(_SMALL) + ["w_conv"]
    small = _pack_rows([jnp.full((1, 1), loss_part, f32)] + [grads[n] for n in small_names])
    got, small_all = _exchange_sibling_and_small(pieces, small)
    chip_sums = [_add_own_half(p, g_, core, "add_sibling_" + n) for (n, _), p, g_ in zip(_MATS, pieces, got)]
    from_chips = _exchange_chips([s16 for _, s16 in chip_sums])
    chip_arr = chip.astype(jnp.int32).reshape(1)
    halves = [_sum_chips(s32, r16, chip_arr, "sum_chips_" + n) for (n, _), (s32, _), r16 in zip(_MATS, chip_sums, from_chips)]
    g_shards = _share_halves(halves)
    small_sum = _sum_slots(small_all, "sum_small")

    out = {"grad": {}, "delta": {}, "m": {}, "v": {}}
    for (n, _), g_ in zip(_MATS, g_shards):
        d_, m_, v_ = _adamw(w[n][0], g_, m[n][0], v[n][0], "adamw_" + n)
        for key, t in (("grad", g_), ("delta", d_), ("m", m_), ("v", v_)):
            out[key][n] = t.reshape(w[n].shape)
    sizes = [1] + [int(grads[n].size) for n in small_names]
    sm = _unpack_rows(small_sum, sizes)
    loss = sm[0][0, 0]
    g_small = dict(zip(small_names, sm[1:]))
    F4 = w["w_conv"].shape[-1]
    g_small["w_conv"] = lax.dynamic_slice_in_dim(g_small["w_conv"].reshape(3, -1), chip * F4, F4, axis=1)
    pack_s = lambda d: _pack_rows([d[n].reshape(1, -1) for n in small_names])
    gs_packed = _pack_rows([g_small[n].reshape(1, -1) for n in small_names])
    s_delta, s_m, s_v = _adamw(pack_s(w), gs_packed, pack_s(m), pack_s(v), "adamw_small")
    s_sizes = [int(w[n].size) for n in small_names]

    for key, packed_s in (("grad", gs_packed), ("delta", s_delta), ("m", s_m), ("v", s_v)):
        for n, t in zip(small_names, _unpack_rows(packed_s, s_sizes)):
            out[key][n] = t.reshape(w[n].shape)
    return (loss, grad_x, *[out["grad"][n] for n in order], *[out["delta"][n] for n in order],
            *[out["m"][n] for n in order], *[out["v"][n] for n in order])
```

```python
import functools
import math

import jax
import jax.numpy as jnp
from jax import lax
from jax.experimental import pallas as pl
from jax.experimental.pallas import tpu as pltpu

f32, bf16 = jnp.float32, jnp.bfloat16

HEAD_DIM = 64
LANES = 128
LN_EPS = 1e-5
ALPHA = 2.0 ** 0.25
Q_SCALE = HEAD_DIM ** -0.5
ADAM_LR, ADAM_B1, ADAM_B2, ADAM_EPS, ADAM_WD, ADAM_STEP = 0.001, 0.9, 0.999, 1e-08, 0.01, 10
VMEM_LIMIT = 56 * 1024 * 1024
NEG = -1e30

_pcall = pl.pallas_call
_NT = (((1,), (1,)), ((), ()))
_TN = (((0,), (0,)), ((), ()))


def _params(sem=None):
    return pltpu.CompilerParams(dimension_semantics=sem, vmem_limit_bytes=VMEM_LIMIT)


def _tile(dim, target, unit=LANES):
    if dim <= target:
        return dim
    t = (target // unit) * unit
    while t > unit and dim % t:
        t -= unit
    assert dim % t == 0, (dim, target)
    return t


def _dot(a, b, dn=None):
    if dn is None:
        return jnp.dot(a, b, preferred_element_type=f32)
    return lax.dot_general(a, b, dn, preferred_element_type=f32)


def _split_dot(x, tri):
    hi = x.astype(bf16)
    lo = (x - hi.astype(f32)).astype(bf16)
    return _dot(hi, tri) + _dot(lo, tri)


SCAN_BLOCK = 256


def _scan_cols(x, tri, reverse):
    cb = tri.shape[0]
    nb = x.shape[1] // cb
    blocks = [x[:, b * cb:(b + 1) * cb] for b in range(nb)]
    outs, run = [None] * nb, None
    for b in (reversed(range(nb)) if reverse else range(nb)):
        o = _split_dot(blocks[b], tri)
        s = jnp.sum(blocks[b], axis=1, keepdims=True)
        outs[b] = o if run is None else o + run
        run = s if run is None else run + s
    return (outs[0] if nb == 1 else jnp.concatenate(outs, axis=1)), run


def _tri(cb, rel):
    row = lax.broadcasted_iota(jnp.int32, (cb, cb), 0)
    col = lax.broadcasted_iota(jnp.int32, (cb, cb), 1)
    return rel(row, col).astype(bf16)


def _matmul(a, b, *, name, ta=False, tb=False, bias=None, addend=None, addend_scale=1.0, colsum=False,
            out_dtype=f32, tm=512, tn=512, tk=1024):
    M, K = (a.shape[1], a.shape[0]) if ta else a.shape
    N = b.shape[0] if tb else b.shape[1]
    assert K == (b.shape[1] if tb else b.shape[0])
    assert not (colsum and tb)
    tm, tn, tk = _tile(M, tm), _tile(N, tn), _tile(K, tk)
    nk = K // tk
    n_in = 2 + (bias is not None) + (addend is not None)

    def body(*refs):
        a_ref, b_ref = refs[0], refs[1]
        bias_ref = refs[2] if bias is not None else None
        add_ref = refs[n_in - 1] if addend is not None else None
        o_ref = refs[n_in]
        cs_ref = refs[n_in + 1] if colsum else None
        acc = refs[-2] if colsum else refs[-1]
        cs_acc = refs[-1] if colsum else None
        k = pl.program_id(2)

        @pl.when(k == 0)
        def _():
            acc[...] = jnp.zeros_like(acc)
            if colsum:
                cs_acc[...] = jnp.zeros_like(cs_acc)

        dn = (((0 if ta else 1,), (1 if tb else 0,)), ((), ()))
        acc[...] += lax.dot_general(a_ref[...].astype(bf16), b_ref[...].astype(bf16), dn, preferred_element_type=f32)
        if colsum:
            cs_acc[...] += jnp.sum(b_ref[...].astype(f32), axis=0, keepdims=True)

        @pl.when(k == nk - 1)
        def _():
            r = acc[...]
            if bias is not None:
                r = r + bias_ref[...]
            if addend is not None:
                r = r + addend_scale * add_ref[...].astype(f32)
            o_ref[...] = r.astype(out_dtype)
            if colsum:
                cs_ref[0] = cs_acc[...]

    a_spec = pl.BlockSpec((tk, tm), lambda i, j, k: (k, i)) if ta else pl.BlockSpec((tm, tk), lambda i, j, k: (i, k))
    b_spec = pl.BlockSpec((tn, tk), lambda i, j, k: (j, k)) if tb else pl.BlockSpec((tk, tn), lambda i, j, k: (k, j))
    in_specs, args = [a_spec, b_spec], [a, b]
    if bias is not None:
        in_specs.append(pl.BlockSpec((1, tn), lambda i, j, k: (0, j)))
        args.append(bias.reshape(1, N).astype(f32))
    if addend is not None:
        in_specs.append(pl.BlockSpec((tm, tn), lambda i, j, k: (i, j)))
        args.append(addend)
    out_shape = [jax.ShapeDtypeStruct((M, N), out_dtype)]
    out_specs = [pl.BlockSpec((tm, tn), lambda i, j, k: (i, j))]
    scratch = [pltpu.VMEM((tm, tn), f32)]
    if colsum:
        out_shape.append(jax.ShapeDtypeStruct((M // tm, 1, N), f32))
        out_specs.append(pl.BlockSpec((1, 1, tn), lambda i, j, k: (i, 0, j)))
        scratch.append(pltpu.VMEM((1, tn), f32))
    res = _pcall(body, name=name, grid=(M // tm, N // tn, nk), in_specs=in_specs, out_specs=out_specs,
                 out_shape=out_shape, scratch_shapes=scratch,
                 compiler_params=_params(("arbitrary", "arbitrary", "arbitrary")))(*args)
    return (res[0], res[1][0]) if colsum else res[0]


def _cumlogf(fl, B, S):
    t = _tile(S, 256, 8)

    def body(fl_ref, c_ref, carry):
        @pl.when(pl.program_id(1) == 0)
        def _():
            carry[...] = jnp.zeros_like(carry)
        z = fl_ref[...]
        ls = jnp.minimum(z, 0.0) - jnp.log(1.0 + jnp.exp(-jnp.abs(z)))
        row = lax.broadcasted_iota(jnp.int32, (t, t), 0)
        col = lax.broadcasted_iota(jnp.int32, (t, t), 1)
        lower = (col <= row).astype(f32)
        c = jnp.dot(lower, ls, precision=lax.Precision.HIGHEST, preferred_element_type=f32) + carry[...]
        c_ref[...] = c
        carry[...] = c[t - 1:t, :]

    return _pcall(body, name="cumlogf", grid=(B, S // t),
                  in_specs=[pl.BlockSpec((t, LANES), lambda b, i: (b * (S // t) + i, 0))],
                  out_specs=pl.BlockSpec((t, LANES), lambda b, i: (b * (S // t) + i, 0)),
                  out_shape=jax.ShapeDtypeStruct(fl.shape, f32), scratch_shapes=[pltpu.VMEM((1, LANES), f32)],
                  compiler_params=_params(("arbitrary", "arbitrary")))(fl)


def _cumlogf_bwd(dc, fl, B, S):
    t = _tile(S, 256, 8)
    n = S // t

    def body(dc_ref, fl_ref, o_ref, carry):
        @pl.when(pl.program_id(1) == 0)
        def _():
            carry[...] = jnp.zeros_like(carry)
        row = lax.broadcasted_iota(jnp.int32, (t, t), 0)
        col = lax.broadcasted_iota(jnp.int32, (t, t), 1)
        upper = (col >= row).astype(f32)
        r = jnp.dot(upper, dc_ref[...], precision=lax.Precision.HIGHEST, preferred_element_type=f32) + carry[...]
        carry[...] = r[0:1, :]
        z = fl_ref[...]
        o_ref[...] = r / (1.0 + jnp.exp(z))

    spec = pl.BlockSpec((t, LANES), lambda b, i: (b * n + n - 1 - i, 0))
    return _pcall(body, name="cumlogf_bwd", grid=(B, n), in_specs=[spec, spec], out_specs=spec,
                  out_shape=jax.ShapeDtypeStruct(fl.shape, f32), scratch_shapes=[pltpu.VMEM((1, LANES), f32)],
                  compiler_params=_params(("arbitrary", "arbitrary")))(dc, fl)


def _head_masks():
    lane = lax.broadcasted_iota(jnp.int32, (1, LANES), 1)
    return lane < HEAD_DIM


def _by_head(m0, t):
    z = jnp.zeros_like(t)
    return [jnp.where(m0, t, z), jnp.where(m0, z, t)]


def _sb_terms(z):
    relu = jnp.maximum(z, 0.0)
    sp = jnp.log(1.0 + jnp.exp(-jnp.abs(z)))
    return (z - relu) - sp, -relu - sp


def _sb_fwd(qkv, B, S, n_pairs, qcol, kcol, vcol, tq, shards=()):
    nq = S // tq
    T = B * S
    n = len(shards)

    def body(q_ref, k_ref, v_ref, *rest):
        o_ref, lt_ref = rest[n], rest[n + 1]
        i = pl.program_id(2)
        if n:
            start, finish = _gather_copies(rest[:n], rest[n + 2:2 * n + 2], *rest[2 * n + 2:])
            step = (pl.program_id(0) * n_pairs + pl.program_id(1)) * nq + i
            pl.when(step == 0)(start)
        m0 = _head_masks()
        qh = _by_head(m0, q_ref[...])
        row = lax.broadcasted_iota(jnp.int32, (tq, tq), 0)
        col = lax.broadcasted_iota(jnp.int32, (tq, tq), 1)
        strict = col < row
        later = _tri(min(tq, SCAN_BLOCK), lambda j, s: j > s)

        def tile(s0, R, acc, diag):
            kb = k_ref[pl.ds(s0, tq), :]
            vh = _by_head(m0, v_ref[pl.ds(s0, tq), :])
            R = list(R)
            for h in range(2):
                z = _dot(qh[h], kb, _NT)
                lb, l1m = _sb_terms(z)
                if diag:
                    l1m = jnp.where(strict, l1m, 0.0)
                suf, total = _scan_cols(l1m, later, True)
                w = jnp.exp(lb + (suf + R[h]))
                if diag:
                    w = jnp.where(strict, w, 0.0)
                acc = acc + _dot(w.astype(bf16), vh[h])
                R[h] = R[h] + total
            return R, acc

        zero = jnp.zeros((tq, 1), f32)
        R, acc = tile(pl.multiple_of(i * tq, tq), [zero, zero], jnp.zeros((tq, LANES), f32), True)

        def loop(n, carry):
            s0 = pl.multiple_of((i - 1 - n) * tq, tq)
            R, acc = tile(s0, carry[:2], carry[2], False)
            return R[0], R[1], acc

        R0, R1, acc = lax.fori_loop(0, i, loop, (R[0], R[1], acc))
        o_ref[...] = acc.astype(bf16)
        lt_ref[...] = jnp.where(m0, R0, R1)
        if n:
            pl.when(step == B * n_pairs * nq - 1)(finish)

    qs = lambda c: pl.BlockSpec((tq, LANES), lambda b, p, i: (b * nq + i, c + p))
    ks = lambda c: pl.BlockSpec((S, LANES), lambda b, p, i: (b, c + p))
    os_ = pl.BlockSpec((tq, LANES), lambda b, p, i: (b * nq + i, p))
    res = _pcall(body, name="sb_fwd", grid=(B, n_pairs, nq), in_specs=[qs(qcol), ks(kcol), ks(vcol)] + [_ANY] * n,
                 out_specs=[os_, os_] + [_ANY] * n,
                 out_shape=[jax.ShapeDtypeStruct((T, n_pairs * LANES), bf16), jax.ShapeDtypeStruct((T, n_pairs * LANES), f32)]
                 + _gather_shapes(shards), scratch_shapes=_gather_sems(n) if n else [],
                 compiler_params=_params(("arbitrary", "arbitrary", "arbitrary")))(qkv, qkv, qkv, *shards)
    return res[0], res[1], res[2:]


def _sb_bwd(qkv, do, lt, B, S, n_pairs, qcol, kcol, vcol, tq):
    nq = S // tq
    T = B * S

    def body(q_ref, k_ref, v_ref, do_ref, lt_ref, dq_ref, dk_ref, dv_ref):
        i = pl.program_id(2)

        @pl.when(i == 0)
        def _():
            dk_ref[...] = jnp.zeros_like(dk_ref)
            dv_ref[...] = jnp.zeros_like(dv_ref)

        m0 = _head_masks()
        qh = _by_head(m0, q_ref[...])
        doh = _by_head(m0, do_ref[...])
        lt = lt_ref[...]
        ltot = [lt[:, 0:1], lt[:, HEAD_DIM:HEAD_DIM + 1]]
        row = lax.broadcasted_iota(jnp.int32, (tq, tq), 0)
        col = lax.broadcasted_iota(jnp.int32, (tq, tq), 1)
        strict = col < row
        upto = _tri(min(tq, SCAN_BLOCK), lambda j, s: j <= s)
        before = _tri(min(tq, SCAN_BLOCK), lambda j, s: j < s)

        def tile(s0, CL, CP, dq, diag):
            kb = k_ref[pl.ds(s0, tq), :]
            vb = v_ref[pl.ds(s0, tq), :]
            kh = _by_head(m0, kb)
            CL, CP = list(CL), list(CP)
            dk = jnp.zeros((tq, LANES), f32)
            dv = jnp.zeros((tq, LANES), f32)
            for h in range(2):
                z = _dot(qh[h], kb, _NT)
                lb, l1m = _sb_terms(z)
                if diag:
                    l1m = jnp.where(strict, l1m, 0.0)
                pre, l_total = _scan_cols(l1m, upto, False)
                w = jnp.exp(lb + ((ltot[h] - CL[h]) - pre))
                if diag:
                    w = jnp.where(strict, w, 0.0)
                g = _dot(doh[h], vb, _NT) * w
                p, g_total = _scan_cols(g, before, False)
                dz = g - jnp.exp(lb) * (g + (p + CP[h]))
                if diag:
                    dz = jnp.where(strict, dz, 0.0)
                dzb = dz.astype(bf16)
                dq = dq + _dot(dzb, kh[h])
                dk = dk + _dot(dzb, qh[h], _TN)
                dv = dv + _dot(w.astype(bf16), doh[h], _TN)
                CL[h] = CL[h] + l_total
                CP[h] = CP[h] + g_total
            dk_ref[pl.ds(s0, tq), :] += dk
            dv_ref[pl.ds(s0, tq), :] += dv
            return CL, CP, dq

        zero = jnp.zeros((tq, 1), f32)

        def loop(n, carry):
            CL, CP, dq = tile(pl.multiple_of(n * tq, tq), carry[0:2], carry[2:4], carry[4], False)
            return CL[0], CL[1], CP[0], CP[1], dq

        c = lax.fori_loop(0, i, loop, (zero, zero, zero, zero, jnp.zeros((tq, LANES), f32)))
        _, _, dq = tile(pl.multiple_of(i * tq, tq), c[0:2], c[2:4], c[4], True)
        dq_ref[...] = dq * Q_SCALE

    qs = lambda c: pl.BlockSpec((tq, LANES), lambda b, p, i: (b * nq + i, c + p))
    ks = lambda c: pl.BlockSpec((S, LANES), lambda b, p, i: (b, c + p))
    ts = pl.BlockSpec((tq, LANES), lambda b, p, i: (b * nq + i, p))
    fs = pl.BlockSpec((S, LANES), lambda b, p, i: (b, p))
    shp = jax.ShapeDtypeStruct((T, n_pairs * LANES), f32)
    return _pcall(body, name="sb_bwd", grid=(B, n_pairs, nq),
                  in_specs=[qs(qcol), ks(kcol), ks(vcol), ts, ts], out_specs=[ts, fs, fs], out_shape=[shp, shp, shp],
                  compiler_params=_params(("arbitrary", "arbitrary", "arbitrary")))(qkv, qkv, qkv, do, lt)


def _fox_fwd(qkv, c, cT, B, S, n_pairs, qcol, kcol, vcol, tq):
    nq = S // tq
    T = B * S

    def body(q_ref, k_ref, v_ref, cq_ref, ck_ref, o_ref, o32_ref, lse_ref):
        p_idx = pl.program_id(1)
        i = pl.program_id(2)
        m0 = _head_masks()
        lane = lax.broadcasted_iota(jnp.int32, (1, LANES), 1)
        qh = _by_head(m0, q_ref[...])
        cq_all = cq_ref[...]
        cq = [jnp.sum(jnp.where(lane == 2 * p_idx + h, cq_all, 0.0), axis=1, keepdims=True) for h in range(2)]
        row = lax.broadcasted_iota(jnp.int32, (tq, tq), 0)
        col = lax.broadcasted_iota(jnp.int32, (tq, tq), 1)
        causal = col <= row

        def tile(s0, m, l, acc, diag):
            kb = k_ref[pl.ds(s0, tq), :]
            vh = _by_head(m0, v_ref[pl.ds(s0, tq), :])
            m, l = list(m), list(l)
            scale, add = [], []
            for h in range(2):
                z = _dot(qh[h], kb, _NT) + (cq[h] - ck_ref[h, :, pl.ds(s0, tq)])
                if diag:
                    z = jnp.where(causal, z, NEG)
                m_new = jnp.maximum(m[h], jnp.max(z, axis=1, keepdims=True))
                p = jnp.exp(z - m_new)
                a = jnp.exp(m[h] - m_new)
                l[h] = a * l[h] + jnp.sum(p, axis=1, keepdims=True)
                m[h] = m_new
                scale.append(a)
                add.append(_dot(p.astype(bf16), vh[h]))
            acc = acc * jnp.where(m0, scale[0], scale[1]) + add[0] + add[1]
            return m, l, acc

        neg = jnp.full((tq, 1), NEG, f32)
        zero = jnp.zeros((tq, 1), f32)
        m, l, acc = tile(pl.multiple_of(i * tq, tq), [neg, neg], [zero, zero], jnp.zeros((tq, LANES), f32), True)

        def loop(n, carry):
            m, l, acc = tile(pl.multiple_of(n * tq, tq), carry[0:2], carry[2:4], carry[4], False)
            return m[0], m[1], l[0], l[1], acc

        m0_, m1_, l0, l1, acc = lax.fori_loop(0, i, loop, (m[0], m[1], l[0], l[1], acc))
        o = acc * jnp.where(m0, 1.0 / l0, 1.0 / l1)
        o_ref[...] = o.astype(bf16)
        o32_ref[...] = o
        lse_ref[...] = jnp.where(m0, m0_ + jnp.log(l0), m1_ + jnp.log(l1))

    qs = lambda cc: pl.BlockSpec((tq, LANES), lambda b, p, i: (b * nq + i, cc + p))
    ks = lambda cc: pl.BlockSpec((S, LANES), lambda b, p, i: (b, cc + p))
    cqs = pl.BlockSpec((tq, LANES), lambda b, p, i: (b * nq + i, 0))
    cks = pl.BlockSpec((2, 1, S), lambda b, p, i: (b * n_pairs + p, 0, 0))
    os_ = pl.BlockSpec((tq, LANES), lambda b, p, i: (b * nq + i, p))
    shp = jax.ShapeDtypeStruct((T, n_pairs * LANES), f32)
    return _pcall(body, name="fox_fwd", grid=(B, n_pairs, nq), in_specs=[qs(qcol), ks(kcol), ks(vcol), cqs, cks],
                  out_specs=[os_, os_, os_], out_shape=[jax.ShapeDtypeStruct((T, n_pairs * LANES), bf16), shp, shp],
                  compiler_params=_params(("arbitrary", "arbitrary", "arbitrary")))(qkv, qkv, qkv, c, cT)


def _fox_bwd(qkv, c, cT, do, o, lse, B, S, n_pairs, qcol, kcol, vcol, tq):
    nq = S // tq
    T = B * S

    def body(q_ref, k_ref, v_ref, cq_ref, ck_ref, do_ref, o_ref, lse_ref, dq_ref, dk_ref, dv_ref, dc_ref):
        p_idx = pl.program_id(1)
        i = pl.program_id(2)

        @pl.when(i == 0)
        def _():
            dk_ref[...] = jnp.zeros_like(dk_ref)
            dv_ref[...] = jnp.zeros_like(dv_ref)
            dc_ref[...] = jnp.zeros_like(dc_ref)

        m0 = _head_masks()
        lane = lax.broadcasted_iota(jnp.int32, (1, LANES), 1)
        qh = _by_head(m0, q_ref[...])
        do2 = do_ref[...]
        doh = _by_head(m0, do2)
        prod = do2.astype(f32) * o_ref[...].astype(f32)
        delta = [jnp.sum(p, axis=1, keepdims=True) for p in _by_head(m0, prod)]
        ls = lse_ref[...]
        lse = [ls[:, 0:1], ls[:, HEAD_DIM:HEAD_DIM + 1]]
        cq_all = cq_ref[...]
        cq = [jnp.sum(jnp.where(lane == 2 * p_idx + h, cq_all, 0.0), axis=1, keepdims=True) for h in range(2)]
        row = lax.broadcasted_iota(jnp.int32, (tq, tq), 0)
        col = lax.broadcasted_iota(jnp.int32, (tq, tq), 1)
        causal = col <= row

        def tile(s0, dq, diag):
            kb = k_ref[pl.ds(s0, tq), :]
            vb = v_ref[pl.ds(s0, tq), :]
            kh = _by_head(m0, kb)
            dk = jnp.zeros((tq, LANES), f32)
            dv = jnp.zeros((tq, LANES), f32)
            for h in range(2):
                z = _dot(qh[h], kb, _NT) + (cq[h] - ck_ref[h, :, pl.ds(s0, tq)])
                p = jnp.exp(z - lse[h])
                if diag:
                    p = jnp.where(causal, p, 0.0)
                ds = p * (_dot(doh[h], vb, _NT) - delta[h])
                dsb = ds.astype(bf16)
                dq = dq + _dot(dsb, kh[h])
                dk = dk + _dot(dsb, qh[h], _TN)
                dv = dv + _dot(p.astype(bf16), doh[h], _TN)
                dc_ref[h, :, pl.ds(s0, tq)] -= jnp.sum(ds, axis=0, keepdims=True)
            dk_ref[pl.ds(s0, tq), :] += dk
            dv_ref[pl.ds(s0, tq), :] += dv
            return dq

        dq = lax.fori_loop(0, i, lambda n, dq: tile(pl.multiple_of(n * tq, tq), dq, False), jnp.zeros((tq, LANES), f32))
        dq = tile(pl.multiple_of(i * tq, tq), dq, True)
        dq_ref[...] = dq * Q_SCALE

    qs = lambda cc: pl.BlockSpec((tq, LANES), lambda b, p, i: (b * nq + i, cc + p))
    ks = lambda cc: pl.BlockSpec((S, LANES), lambda b, p, i: (b, cc + p))
    cqs = pl.BlockSpec((tq, LANES), lambda b, p, i: (b * nq + i, 0))
    cks = pl.BlockSpec((2, 1, S), lambda b, p, i: (b * n_pairs + p, 0, 0))
    ts = pl.BlockSpec((tq, LANES), lambda b, p, i: (b * nq + i, p))
    fs = pl.BlockSpec((S, LANES), lambda b, p, i: (b, p))
    shp = jax.ShapeDtypeStruct((T, n_pairs * LANES), f32)
    return _pcall(body, name="fox_bwd", grid=(B, n_pairs, nq),
                  in_specs=[qs(qcol), ks(kcol), ks(vcol), cqs, cks, ts, ts, ts], out_specs=[ts, fs, fs, cks],
                  out_shape=[shp, shp, shp, jax.ShapeDtypeStruct(cT.shape, f32)],
                  compiler_params=_params(("arbitrary", "arbitrary", "arbitrary")))(qkv, qkv, qkv, c, cT, do, o, lse)


def _sigmoid(x):
    return 1.0 / (1.0 + jnp.exp(-x))


def _mix_fwd(o_sb, o_fx, g, x, wp_sb, wp_fx, w_out, ln_g, ln_b, tm):
    T, D = x.shape
    E = o_sb.shape[1]
    tm = _tile(T, tm, 8)

    def body(osb_ref, ofx_ref, gsb_ref, gfx_ref, x_ref, wsb_ref, wfx_ref, wo_ref, lg_ref, lb_ref,
             xhat_ref, rstd_ref, x1_ref, mg_ref):
        y_sb = _dot(osb_ref[...], wsb_ref[...])
        y_fx = _dot(ofx_ref[...], wfx_ref[...])
        merged = (_sigmoid(gsb_ref[...]) * y_sb + _sigmoid(gfx_ref[...]) * y_fx).astype(bf16)
        r = ALPHA * x_ref[...] + _dot(merged, wo_ref[...])
        mean = jnp.mean(r, axis=1, keepdims=True)
        cen = r - mean
        rstd = lax.rsqrt(jnp.mean(cen * cen, axis=1, keepdims=True) + LN_EPS)
        xhat = cen * rstd
        xhat_ref[...] = xhat
        rstd_ref[...] = rstd
        x1_ref[...] = (xhat * lg_ref[...] + lb_ref[...]).astype(bf16)
        mg_ref[...] = merged

    rows = lambda w, c=0: pl.BlockSpec((tm, w), lambda i: (i, c))
    full = lambda a: pl.BlockSpec(a.shape, lambda i: (0, 0))
    return _pcall(body, name="mix_fwd", grid=(T // tm,),
                  in_specs=[rows(E), rows(E), rows(D, 0), rows(D, 1), rows(D), full(wp_sb), full(wp_fx), full(w_out),
                            full(ln_g), full(ln_b)],
                  out_specs=[rows(D), rows(1), rows(D), rows(D)],
                  out_shape=[jax.ShapeDtypeStruct((T, D), f32), jax.ShapeDtypeStruct((T, 1), f32),
                             jax.ShapeDtypeStruct((T, D), bf16), jax.ShapeDtypeStruct((T, D), bf16)],
                  compiler_params=_params(("arbitrary",)))(o_sb, o_fx, g, g, x, wp_sb, wp_fx, w_out, ln_g, ln_b)


def _mix_bwd(dr1, o_sb, o_fx, g, wp_sb, wp_fx, w_out, tm):
    T, D = dr1.shape
    E = o_sb.shape[1]
    tm = _tile(T, tm, 8)

    def body(dr_ref, osb_ref, ofx_ref, gsb_ref, gfx_ref, wsb_ref, wfx_ref, wo_ref,
             dysb_ref, dyfx_ref, dgsb_ref, dgfx_ref, dosb_ref, dofx_ref, sumsb_ref, sumfx_ref):
        @pl.when(pl.program_id(0) == 0)
        def _():
            sumsb_ref[...] = jnp.zeros_like(sumsb_ref)
            sumfx_ref[...] = jnp.zeros_like(sumfx_ref)
        dm = _dot(dr_ref[...].astype(bf16), wo_ref[...], _NT)
        for o_ref, g_ref, w_ref, dy_ref, dg_ref, do_ref, sum_ref in (
                (osb_ref, gsb_ref, wsb_ref, dysb_ref, dgsb_ref, dosb_ref, sumsb_ref),
                (ofx_ref, gfx_ref, wfx_ref, dyfx_ref, dgfx_ref, dofx_ref, sumfx_ref)):
            y = _dot(o_ref[...], w_ref[...])
            s = _sigmoid(g_ref[...])
            dy = (dm * s).astype(bf16)
            dy_ref[...] = dy
            dg = dm * y * s * (1.0 - s)
            dg_ref[...] = dg.astype(bf16)
            sum_ref[0:1, :] += jnp.sum(dg, axis=0, keepdims=True)
            do_ref[...] = _dot(dy, w_ref[...], _NT).astype(bf16)

    rows = lambda w, c=0: pl.BlockSpec((tm, w), lambda i: (i, c))
    full = lambda a: pl.BlockSpec(a.shape, lambda i: (0, 0))
    acc = pl.BlockSpec((8, D), lambda i: (0, 0))
    res = _pcall(body, name="mix_bwd", grid=(T // tm,),
                 in_specs=[rows(D), rows(E), rows(E), rows(D, 0), rows(D, 1), full(wp_sb), full(wp_fx), full(w_out)],
                 out_specs=[rows(D), rows(D), rows(D), rows(D), rows(E), rows(E), acc, acc],
                 out_shape=[jax.ShapeDtypeStruct((T, D), bf16)] * 4 + [jax.ShapeDtypeStruct((T, E), bf16)] * 2
                 + [jax.ShapeDtypeStruct((8, D), f32)] * 2,
                 compiler_params=_params(("arbitrary",)))(dr1, o_sb, o_fx, g, g, wp_sb, wp_fx, w_out)
    return res


def _ln_bwd(dy_a, dy_b, scale_b, xhat, rstd, ln_g, tm):
    T, D = xhat.shape
    tm = _tile(T, tm, 8)

    def body(a_ref, b_ref, xh_ref, rs_ref, g_ref, dr_ref, st_ref):
        @pl.when(pl.program_id(0) == 0)
        def _():
            st_ref[...] = jnp.zeros_like(st_ref)
        dy = a_ref[...] + scale_b * b_ref[...]
        xh = xh_ref[...]
        dxh = dy * g_ref[...]
        m1 = jnp.mean(dxh, axis=1, keepdims=True)
        m2 = jnp.mean(dxh * xh, axis=1, keepdims=True)
        dr_ref[...] = rs_ref[...] * (dxh - m1 - xh * m2)
        st_ref[0:1, :] += jnp.sum(dy * xh, axis=0, keepdims=True)
        st_ref[1:2, :] += jnp.sum(dy, axis=0, keepdims=True)

    rows = lambda w: pl.BlockSpec((tm, w), lambda i: (i, 0))
    return _pcall(body, name="ln1_bwd", grid=(T // tm,),
                  in_specs=[rows(D), rows(D), rows(D), rows(1), pl.BlockSpec((1, D), lambda i: (0, 0))],
                  out_specs=[rows(D), pl.BlockSpec((8, D), lambda i: (0, 0))],
                  out_shape=[jax.ShapeDtypeStruct((T, D), f32), jax.ShapeDtypeStruct((8, D), f32)],
                  compiler_params=_params(("arbitrary",)))(dy_a, dy_b, xhat, rstd, ln_g)


_INV_SQRT2 = 1.0 / math.sqrt(2.0)
_INV_SQRT2PI = 1.0 / math.sqrt(2.0 * math.pi)


def _conv_rows(ref, r0, rc, first, wc, bc):
    cur = ref[pl.ds(r0, rc), :]
    prev = ref[pl.ds(pl.multiple_of(jnp.maximum(r0 - 8, 0), 8), 8), :]
    prev = jnp.where(first, jnp.zeros_like(prev), prev)
    rid = lax.broadcasted_iota(jnp.int32, (rc, LANES), 0)
    s1 = jnp.where(rid == 0, prev[7:8, :], pltpu.roll(cur, 1, 0))
    s2 = jnp.where(rid == 0, prev[6:7, :], jnp.where(rid == 1, prev[7:8, :], pltpu.roll(cur, 2, 0)))
    conv = bc + wc[0:1, :] * s2 + wc[1:2, :] * s1 + wc[2:3, :] * cur
    return conv, (s2, s1, cur)


def _glu_fwd(u, w_conv, b_conv, B, S, rc=512):
    F = u.shape[1] // 2
    nf = F // LANES
    rc = _tile(S, rc, 8)

    def body(ug_ref, uv_ref, wc_ref, bc_ref, a_ref):
        wc, bc = wc_ref[...], bc_ref[...]

        def chunk(n, _):
            r0 = pl.multiple_of(n * rc, rc)
            c, _taps = _conv_rows(ug_ref, r0, rc, n == 0, wc, bc)
            gelu = 0.5 * c * (1.0 + lax.erf(c * _INV_SQRT2))
            a_ref[pl.ds(r0, rc), :] = (gelu * uv_ref[pl.ds(r0, rc), :]).astype(bf16)
            return 0

        lax.fori_loop(0, S // rc, chunk, 0)

    return _pcall(body, name="glu_fwd", grid=(B, nf),
                  in_specs=[pl.BlockSpec((S, LANES), lambda b, j: (b, j)), pl.BlockSpec((S, LANES), lambda b, j: (b, nf + j)),
                            pl.BlockSpec((3, LANES), lambda b, j: (0, j)), pl.BlockSpec((1, LANES), lambda b, j: (0, j))],
                  out_specs=pl.BlockSpec((S, LANES), lambda b, j: (b, j)),
                  out_shape=jax.ShapeDtypeStruct((B * S, F), bf16),
                  compiler_params=_params(("arbitrary", "arbitrary")))(u, u, w_conv, b_conv)


def _glu_bwd(u, da, w_conv, b_conv, B, S, rc=512):
    F = u.shape[1] // 2
    nf = F // LANES
    rc = _tile(S, rc, 8)
    nc = S // rc

    def body(ug_ref, uv_ref, da_ref, wc_ref, bc_ref, dug_ref, duv_ref, gw_ref, gb_ref, dc_ref):
        wc, bc = wc_ref[...], bc_ref[...]

        def chunk(n, carry):
            gw0, gw1, gw2, gb = carry
            r0 = pl.multiple_of(n * rc, rc)
            c, (s2, s1, cur) = _conv_rows(ug_ref, r0, rc, n == 0, wc, bc)
            cdf = 0.5 * (1.0 + lax.erf(c * _INV_SQRT2))
            da = da_ref[pl.ds(r0, rc), :]
            duv_ref[pl.ds(r0, rc), :] = (da * (c * cdf)).astype(bf16)
            dc = da * uv_ref[pl.ds(r0, rc), :] * (cdf + c * (_INV_SQRT2PI * jnp.exp(-0.5 * c * c)))
            dc_ref[pl.ds(r0, rc), :] = dc
            red = lambda t: jnp.sum(t, axis=0, keepdims=True)
            return gw0 + red(dc * s2), gw1 + red(dc * s1), gw2 + red(dc * cur), gb + red(dc)

        z = jnp.zeros((1, LANES), f32)
        gw0, gw1, gw2, gb = lax.fori_loop(0, nc, chunk, (z, z, z, z))
        gw_ref[0, 0:1, :] = gw0
        gw_ref[0, 1:2, :] = gw1
        gw_ref[0, 2:3, :] = gw2
        gb_ref[0] = gb

        def chunk2(n, _):
            r0 = pl.multiple_of(n * rc, rc)
            cur = dc_ref[pl.ds(r0, rc), :]
            nxt = dc_ref[pl.ds(pl.multiple_of(jnp.minimum(r0 + rc, S - 8), 8), 8), :]
            nxt = jnp.where(n == nc - 1, jnp.zeros_like(nxt), nxt)
            rid = lax.broadcasted_iota(jnp.int32, (rc, LANES), 0)
            a1 = jnp.where(rid == rc - 1, nxt[0:1, :], pltpu.roll(cur, rc - 1, 0))
            a2 = jnp.where(rid == rc - 1, nxt[1:2, :], jnp.where(rid == rc - 2, nxt[0:1, :], pltpu.roll(cur, rc - 2, 0)))
            dug_ref[pl.ds(r0, rc), :] = (wc[2:3, :] * cur + wc[1:2, :] * a1 + wc[0:1, :] * a2).astype(bf16)
            return 0

        lax.fori_loop(0, nc, chunk2, 0)

    blk = lambda off: pl.BlockSpec((S, LANES), lambda b, j: (b, off + j))
    return _pcall(body, name="glu_bwd", grid=(B, nf),
                  in_specs=[blk(0), blk(nf), blk(0), pl.BlockSpec((3, LANES), lambda b, j: (0, j)),
                            pl.BlockSpec((1, LANES), lambda b, j: (0, j))],
                  out_specs=[blk(0), blk(0), pl.BlockSpec((1, 3, LANES), lambda b, j: (b, 0, j)),
                             pl.BlockSpec((1, 1, LANES), lambda b, j: (b, 0, j))],
                  out_shape=[jax.ShapeDtypeStruct((B * S, F), bf16), jax.ShapeDtypeStruct((B * S, F), bf16),
                             jax.ShapeDtypeStruct((B, 3, F), f32), jax.ShapeDtypeStruct((B, 1, F), f32)],
                  scratch_shapes=[pltpu.VMEM((S, LANES), f32)],
                  compiler_params=_params(("arbitrary", "arbitrary")))(u, u, da, w_conv, b_conv)


def _down_loss(a, w_down, xhat1, ln1_g, ln1_b, ln2_g, ln2_b, tgt, tm):
    T, D = xhat1.shape
    F = a.shape[1]
    tm = _tile(T, tm, 8)

    def body(a_ref, w_ref, xh_ref, g1_ref, b1_ref, g2_ref, b2_ref, t_ref, dr_ref, st_ref):
        @pl.when(pl.program_id(0) == 0)
        def _():
            st_ref[...] = jnp.zeros_like(st_ref)
        x1 = xh_ref[...] * g1_ref[...] + b1_ref[...]
        r = ALPHA * x1 + _dot(a_ref[...], w_ref[...])
        mean = jnp.mean(r, axis=1, keepdims=True)
        cen = r - mean
        rstd = lax.rsqrt(jnp.mean(cen * cen, axis=1, keepdims=True) + LN_EPS)
        xh = cen * rstd
        err = (xh * g2_ref[...] + b2_ref[...]) - t_ref[...]
        dy = err * (1.0 / D)
        dxh = dy * g2_ref[...]
        m1 = jnp.mean(dxh, axis=1, keepdims=True)
        m2 = jnp.mean(dxh * xh, axis=1, keepdims=True)
        dr_ref[...] = rstd * (dxh - m1 - xh * m2)
        st_ref[0:1, :] += jnp.sum(dy * xh, axis=0, keepdims=True)
        st_ref[1:2, :] += jnp.sum(dy, axis=0, keepdims=True)
        st_ref[2:3, :] += jnp.sum(err * err, axis=0, keepdims=True)

    rows = lambda w: pl.BlockSpec((tm, w), lambda i: (i, 0))
    vec = pl.BlockSpec((1, D), lambda i: (0, 0))
    return _pcall(body, name="down_loss", grid=(T // tm,),
                  in_specs=[rows(F), pl.BlockSpec((F, D), lambda i: (0, 0)), rows(D), vec, vec, vec, vec, rows(D)],
                  out_specs=[rows(D), pl.BlockSpec((8, D), lambda i: (0, 0))],
                  out_shape=[jax.ShapeDtypeStruct((T, D), f32), jax.ShapeDtypeStruct((8, D), f32)],
                  compiler_params=_params(("arbitrary",)))(a, w_down, xhat1, ln1_g, ln1_b, ln2_g, ln2_b, tgt)


def _local_step(x, tgt, w_in, b_in, ln1_g, ln1_b, b_conv, ln2_g, ln2_b, late_shards, late_weights, tq=512, tm=256):
    B, S, D = x.shape
    T = B * S
    E = (w_in.shape[1] - 2 * D) * HEAD_DIM // (6 * HEAD_DIM + 1)
    n_pairs = E // LANES
    NH = E // HEAD_DIM
    x2 = x.reshape(T, D)
    tgt2 = tgt.reshape(T, D)
    tq = _tile(S, tq, 8)

    c_f, c_g = 6 * E, 6 * E + NH
    w_qkv = w_in[:, :c_f]
    qscale = jnp.concatenate([jnp.full((E,), Q_SCALE, f32), jnp.ones((2 * E,), f32)] * 2)
    w_qkv_s = (w_qkv.astype(f32) * qscale).astype(bf16)
    b_qkv_s = b_in[:, :c_f] * qscale
    w_f = jnp.pad(w_in[:, c_f:c_g], ((0, 0), (0, LANES - NH)))
    b_f = jnp.pad(b_in[:, c_f:c_g], ((0, 0), (0, LANES - NH)))
    w_g = w_in[:, c_g:]
    b_g = b_in[:, c_g:]

    xb = x2.astype(bf16)
    qkv = _matmul(xb, w_qkv_s, bias=b_qkv_s, out_dtype=bf16, tm=1024, tn=512, name="proj_qkv")
    g = _matmul(xb, w_g, bias=b_g, tm=1024, tn=512, name="proj_gate")
    fl = _matmul(xb, w_f, bias=b_f, tm=1024, name="proj_forget")
    c = _cumlogf(fl, B, S)
    cT = c.reshape(B, S, LANES)[:, :, :NH].transpose(0, 2, 1).reshape(B * NH, 1, S)
    P = n_pairs
    o_sb, lt_sb, stacks = _sb_fwd(qkv, B, S, P, 0, P, 2 * P, tq, late_shards)
    wp_sb, wp_fx, w_out, w_up, w_conv, w_down = late_weights(stacks)
    F = w_down.shape[0]
    o_fx, o_fx32, lse_fx = _fox_fwd(qkv, c, cT, B, S, P, 3 * P, 4 * P, 5 * P, tq)
    xhat1, rstd1, x1b, merged = _mix_fwd(o_sb, o_fx, g, x2, wp_sb, wp_fx, w_out, ln1_g, ln1_b, tm)
    u = _matmul(x1b, w_up, tm=1024, tn=512, name="ffn_up")
    a = _glu_fwd(u, w_conv, b_conv, B, S)
    dr2, st2 = _down_loss(a, w_down, xhat1, ln1_g, ln1_b, ln2_g, ln2_b, tgt2, tm)

    grads = {}
    grads["ln2_g"], grads["ln2_b"] = st2[0:1], st2[1:2]
    sq_err = st2[2:3]
    da = _matmul(dr2, w_down, tb=True, tm=1024, tn=512, name="ffn_da")
    grads["w_down"] = _matmul(a, dr2, ta=True, tm=1408, tn=1024, tk=1024, name="grad_w_down")
    du_g, du_v, gwc, gbc = _glu_bwd(u, da, w_conv, b_conv, B, S)
    grads["w_conv"] = jnp.sum(gwc, axis=0)
    grads["b_conv"] = jnp.sum(gbc, axis=0)
    grads["w_up"] = jnp.concatenate(
        [_matmul(x1b, du_g, ta=True, tm=512, tn=2816, tk=1024, name="grad_w_up_gate"),
         _matmul(x1b, du_v, ta=True, tm=512, tn=2816, tk=1024, name="grad_w_up_val")], axis=1)
    dx1 = _matmul(du_g, w_up[:, :F], tb=True, tm=1024, tn=512, tk=F, name="ffn_dx_gate")
    dx1 = _matmul(du_v, w_up[:, F:], tb=True, addend=dx1, tm=1024, tn=512, tk=F, name="ffn_dx_val")
    dr1, st1 = _ln_bwd(dx1, dr2, ALPHA, xhat1, rstd1, ln1_g, tm)
    grads["ln1_g"], grads["ln1_b"] = st1[0:1], st1[1:2]

    dy_sb, dy_fx, dg_sb, dg_fx, do_sb, do_fx, gsum_sb, gsum_fx = _mix_bwd(dr1, o_sb, o_fx, g, wp_sb, wp_fx, w_out, tm)
    grads["w_out"] = _matmul(merged, dr1, ta=True, tm=512, tn=1024, tk=1024, name="grad_w_out")
    grads["w_proj_sb"] = _matmul(o_sb, dy_sb, ta=True, tm=512, tn=1024, tk=1024, name="grad_w_proj_sb")
    grads["w_proj_fox"] = _matmul(o_fx, dy_fx, ta=True, tm=512, tn=1024, tk=1024, name="grad_w_proj_fox")
    dq_s, dk_s, dv_s = _sb_bwd(qkv, do_sb, lt_sb, B, S, P, 0, P, 2 * P, tq)
    dq_f, dk_f, dv_f, dcT = _fox_bwd(qkv, c, cT, do_fx, o_fx32, lse_fx, B, S, P, 3 * P, 4 * P, 5 * P, tq)
    dc = jnp.pad(dcT.reshape(B, NH, S).transpose(0, 2, 1), ((0, 0), (0, 0), (0, LANES - NH))).reshape(T, LANES)
    dfl = _cumlogf_bwd(dc, fl, B, S)
    dqkv = jnp.concatenate([dq_s, dk_s, dv_s, dq_f, dk_f, dv_f], axis=1)
    dg = jnp.concatenate([dg_sb, dg_fx], axis=1)
    gw_qkv, gb_qkv = _matmul(x2, dqkv, ta=True, colsum=True, tm=512, tn=3072, tk=512, name="grad_w_qkv")
    gw_f, gb_f = _matmul(x2, dfl, ta=True, colsum=True, tm=512, tk=1024, name="grad_w_forget")
    gw_g = _matmul(x2, dg, ta=True, tm=512, tn=2048, tk=1024, name="grad_w_gate")
    grads["w_in"] = jnp.concatenate([gw_qkv, gw_f[:, :NH], gw_g], axis=1)
    grads["b_in"] = jnp.concatenate([gb_qkv, gb_f[:, :NH], gsum_sb[0:1], gsum_fx[0:1]], axis=1)
    dx = _matmul(dqkv, w_qkv, tb=True, addend=dr1, addend_scale=ALPHA, tm=512, tn=512, tk=c_f, name="dx_qkv")
    dx = _matmul(dfl, w_f, tb=True, addend=dx, tm=1024, tn=512, name="dx_forget")
    dx = _matmul(dg, w_g, tb=True, addend=dx, tm=1024, tn=512, tk=2 * D, name="dx_gate")
    return sq_err, dx.reshape(B, S, D), grads


_ANY = pl.BlockSpec(memory_space=pl.ANY)
_MESH = pl.DeviceIdType.MESH


def _pos():
    return lax.axis_index("x"), lax.axis_index("y"), lax.axis_index("c")


def _other_chips(x, y):
    return [(1 - x, y), (x, 1 - y), (1 - x, 1 - y)]


def _gather_shards(shards):
    n = len(shards)

    def body(*refs):
        start, finish = _gather_copies(refs[:n], refs[n:2 * n], *refs[2 * n:])
        start()
        finish()

    return _pcall(body, name="gather_weights", in_specs=[_ANY] * n, out_specs=[_ANY] * n,
                  out_shape=_gather_shapes(shards), scratch_shapes=_gather_sems(n),
                  compiler_params=pltpu.CompilerParams(has_side_effects=True))(*shards)


def _gather_shapes(shards):
    return [jax.ShapeDtypeStruct((4,) + s.shape, s.dtype) for s in shards]


def _gather_sems(n):
    return [pltpu.SemaphoreType.DMA((n, 3)), pltpu.SemaphoreType.DMA((n, 3)), pltpu.SemaphoreType.DMA((n,))]


def _gather_copies(srcs, dsts, send_sems, recv_sems, local_sems):
    n = len(srcs)
    x, y, c = _pos()
    k = 2 * x + y
    chips = _other_chips(x, y)

    def copy(a, j, chip, slot):
        return pltpu.make_async_remote_copy(src_ref=srcs[a], dst_ref=dsts[a].at[slot], send_sem=send_sems.at[a, j],
                                            recv_sem=recv_sems.at[a, j], device_id=(*chip, c), device_id_type=_MESH)

    def mine():
        return ([pltpu.make_async_copy(srcs[a], dsts[a].at[k], local_sems.at[a]) for a in range(n)],
                [copy(a, j, chip, k) for a in range(n) for j, chip in enumerate(chips)])

    def start():
        local, sends = mine()
        for cp in local + sends:
            cp.start()

    def finish():
        local, sends = mine()
        for a in range(n):
            for j, chip in enumerate(chips):
                copy(a, j, chip, 2 * chip[0] + chip[1]).wait_recv()
        for cp in sends:
            cp.wait_send()
        for cp in local:
            cp.wait()

    return start, finish


def _exchange_sibling_and_small(pieces, small):
    n = len(pieces)

    def body(*refs):
        p_refs, sm_ref = refs[:n], refs[n]
        got_refs, sg_ref = refs[n + 1:2 * n + 1], refs[2 * n + 1]
        big_send, big_recv, send_sems, recv_sems, local_sem = refs[2 * n + 2:]
        x, y, c = _pos()
        me = 4 * x + 2 * y + c
        big = []
        for a in range(n):
            h = pieces[a].shape[1] // 2
            src = p_refs[a].at[:, pl.ds(pl.multiple_of((1 - c) * h, 8), h), :]
            big.append(pltpu.make_async_remote_copy(src_ref=src, dst_ref=got_refs[a], send_sem=big_send.at[a],
                                                    recv_sem=big_recv.at[a], device_id=(x, y, 1 - c), device_id_type=_MESH))
        for cp in big:
            cp.start()
        local = pltpu.make_async_copy(sm_ref, sg_ref.at[me], local_sem)
        local.start()
        peers = []
        for r in range(1, 8):
            flip = lambda v, bit: 1 - v if bit else v
            peers.append((flip(x, r & 4), flip(y, r & 2), flip(c, r & 1)))

        def copy(j, slot):
            return pltpu.make_async_remote_copy(src_ref=sm_ref, dst_ref=sg_ref.at[slot], send_sem=send_sems.at[j],
                                                recv_sem=recv_sems.at[j], device_id=peers[j], device_id_type=_MESH)

        sends = [copy(j, me) for j in range(7)]
        for cp in sends:
            cp.start()
        for j, (px, py, pc) in enumerate(peers):
            copy(j, 4 * px + 2 * py + pc).wait_recv()
        for cp in sends:
            cp.wait_send()
        local.wait()
        for cp in big:
            cp.wait()

    halves = [jax.ShapeDtypeStruct((4, p.shape[1] // 2, p.shape[2]), p.dtype) for p in pieces]
    res = _pcall(body, name="exchange_sibling", in_specs=[_ANY] * (n + 1), out_specs=[_ANY] * (n + 1),
                 out_shape=halves + [jax.ShapeDtypeStruct((8,) + small.shape, small.dtype)],
                 scratch_shapes=[pltpu.SemaphoreType.DMA((n,)), pltpu.SemaphoreType.DMA((n,)), pltpu.SemaphoreType.DMA((7,)),
                                 pltpu.SemaphoreType.DMA((7,)), pltpu.SemaphoreType.DMA(())],
                 compiler_params=pltpu.CompilerParams(has_side_effects=True))(*pieces, small)
    return res[:n], res[n]


def _exchange_chips(pieces):
    n = len(pieces)

    def body(*refs):
        p_refs, got_refs = refs[:n], refs[n:2 * n]
        send_sems, recv_sems, local_sems = refs[2 * n:]
        x, y, c = _pos()
        k = 2 * x + y
        local = [pltpu.make_async_copy(p_refs[a].at[k], got_refs[a].at[k], local_sems.at[a]) for a in range(n)]
        for cp in local:
            cp.start()
        chips = _other_chips(x, y)

        def copy(a, j, chip, piece, slot):
            return pltpu.make_async_remote_copy(src_ref=p_refs[a].at[piece], dst_ref=got_refs[a].at[slot],
                                                send_sem=send_sems.at[a, j], recv_sem=recv_sems.at[a, j],
                                                device_id=(*chip, c), device_id_type=_MESH)

        sends = [copy(a, j, chip, 2 * chip[0] + chip[1], k) for a in range(n) for j, chip in enumerate(chips)]
        for cp in sends:
            cp.start()
        for a in range(n):
            for j, chip in enumerate(chips):
                copy(a, j, chip, k, 2 * chip[0] + chip[1]).wait_recv()
        for cp in sends:
            cp.wait_send()
        for cp in local:
            cp.wait()

    return _pcall(body, name="exchange_chips", in_specs=[_ANY] * n, out_specs=[_ANY] * n,
                  out_shape=[jax.ShapeDtypeStruct(p.shape, p.dtype) for p in pieces],
                  scratch_shapes=[pltpu.SemaphoreType.DMA((n, 3)), pltpu.SemaphoreType.DMA((n, 3)), pltpu.SemaphoreType.DMA((n,))],
                  compiler_params=pltpu.CompilerParams(has_side_effects=True))(*pieces)


def _share_halves(shards):
    n = len(shards)

    def body(*refs):
        full_refs = refs[n:2 * n]
        send_sems, recv_sems = refs[2 * n:]
        x, y, c = _pos()

        def copy(a, half):
            h = shards[a].shape[0] // 2
            rows = full_refs[a].at[pl.ds(pl.multiple_of(half * h, 8), h), :]
            return pltpu.make_async_remote_copy(src_ref=rows, dst_ref=rows, send_sem=send_sems.at[a],
                                                recv_sem=recv_sems.at[a], device_id=(x, y, 1 - c), device_id_type=_MESH)

        sends = [copy(a, c) for a in range(n)]
        for cp in sends:
            cp.start()
        for a in range(n):
            copy(a, 1 - c).wait_recv()
        for cp in sends:
            cp.wait_send()

    return _pcall(body, name="share_halves", in_specs=[_ANY] * n, out_specs=[_ANY] * n,
                  out_shape=[jax.ShapeDtypeStruct(s.shape, s.dtype) for s in shards],
                  input_output_aliases={a: a for a in range(n)},
                  scratch_shapes=[pltpu.SemaphoreType.DMA((n,)), pltpu.SemaphoreType.DMA((n,))],
                  compiler_params=pltpu.CompilerParams(has_side_effects=True))(*shards)


def _add_own_half(piece, got, core, name):
    _, r, cols = piece.shape
    h = r // 2

    def body(c_ref, a_ref, b_ref, o_ref, o16_ref):
        s = a_ref[0] + b_ref[...]
        o_ref[...] = s
        o16_ref[...] = s.astype(bf16)

    out = pl.BlockSpec((1, h, cols), lambda k, c: (k, 0, 0))
    grid_spec = pltpu.PrefetchScalarGridSpec(
        num_scalar_prefetch=1, grid=(4,),
        in_specs=[pl.BlockSpec((1, 1, h, cols), lambda k, c: (k, c[0], 0, 0)), out], out_specs=[out, out])
    return _pcall(body, name=name, grid_spec=grid_spec,
                  out_shape=[jax.ShapeDtypeStruct((4, h, cols), f32), jax.ShapeDtypeStruct((4, h, cols), bf16)],
                  compiler_params=_params(("arbitrary",)))(core, piece.reshape(4, 2, h, cols), got)


def _sum_chips(own, got, where, name):
    _, h, cols = own.shape
    t = _tile(h, 128, 16)
    nt = h // t

    def body(w_ref, own_ref, got_ref, o_ref):
        acc = None
        for k in range(4):
            term = jnp.where(w_ref[0] == k, own_ref[0], got_ref[k].astype(f32))
            acc = term if acc is None else acc + term
        o_ref[...] = acc

    grid_spec = pltpu.PrefetchScalarGridSpec(
        num_scalar_prefetch=1, grid=(nt,),
        in_specs=[pl.BlockSpec((1, t, cols), lambda i, w: (w[0], i, 0)), pl.BlockSpec((4, t, cols), lambda i, w: (0, i, 0))],
        out_specs=pl.BlockSpec((t, cols), lambda i, w: (w[1] * nt + i, 0)))
    return _pcall(body, name=name, grid_spec=grid_spec, out_shape=jax.ShapeDtypeStruct((2 * h, cols), f32),
                  compiler_params=_params(("arbitrary",)))(where, own, got)


def _sum_slots(stack, name):
    k, n, cols = stack.shape
    t = _tile(n, 128, 8)

    def body(s_ref, o_ref):
        acc = s_ref[0]
        for i in range(1, k):
            acc = acc + s_ref[i]
        o_ref[...] = acc

    return _pcall(body, name=name, grid=(n // t,), in_specs=[pl.BlockSpec((k, t, cols), lambda i: (0, i, 0))],
                  out_specs=pl.BlockSpec((t, cols), lambda i: (i, 0)), out_shape=jax.ShapeDtypeStruct((n, cols), f32),
                  compiler_params=_params(("arbitrary",)))(stack)


def _adamw(w, g, m, v, name):
    n, cols = w.shape
    t = _tile(n, 128, 8)
    c1 = 1.0 - ADAM_B1 ** ADAM_STEP
    c2 = 1.0 - ADAM_B2 ** ADAM_STEP

    def body(w_ref, g_ref, m_ref, v_ref, d_ref, nm_ref, nv_ref):
        g = g_ref[...]
        nm = ADAM_B1 * m_ref[...] + (1.0 - ADAM_B1) * g
        nv = ADAM_B2 * v_ref[...] + (1.0 - ADAM_B2) * (g * g)
        d_ref[...] = -ADAM_LR * ((nm / c1) / (jnp.sqrt(nv / c2) + ADAM_EPS) + ADAM_WD * w_ref[...])
        nm_ref[...] = nm
        nv_ref[...] = nv

    spec = pl.BlockSpec((t, cols), lambda i: (i, 0))
    shp = jax.ShapeDtypeStruct((n, cols), f32)
    return _pcall(body, name=name, grid=(n // t,), in_specs=[spec] * 4, out_specs=[spec] * 3, out_shape=[shp] * 3,
                  compiler_params=_params(("arbitrary",)))(w, g, m, v)


_MATS = (("w_in", 1), ("w_proj_sb", 1), ("w_proj_fox", 1), ("w_out", 0), ("w_up", 1), ("w_down", 0))
_SMALL = ("b_in", "ln1_g", "ln1_b", "b_conv", "ln2_g", "ln2_b")


def _pad_lanes(v):
    n = v.shape[-1]
    return jnp.pad(v, ((0, 0), (0, (-n) % LANES)))


def _pack_rows(vectors):
    flat = jnp.concatenate([_pad_lanes(v.reshape(1, -1)) for v in vectors], axis=1).reshape(-1, LANES)
    return jnp.pad(flat, ((0, (-flat.shape[0]) % 8), (0, 0)))


def _unpack_rows(packed, sizes):
    out, r = [], 0
    for n in sizes:
        rows = -(-n // LANES)
        out.append(packed[r:r + rows].reshape(1, rows * LANES)[:, :n])
        r += rows
    return out


def _unstack(stack, axis):
    if axis == 0:
        return stack.reshape(-1, stack.shape[2])
    return jnp.concatenate([stack[k] for k in range(4)], axis=1)


def _pieces(g, axis):
    if axis == 0:
        return g.reshape(4, g.shape[0] // 4, g.shape[1])
    cols = g.shape[1] // 4
    return jnp.stack([g[:, k * cols:(k + 1) * cols] for k in range(4)])


def kernel(x, w_in, b_in, w_proj_sb, w_proj_fox, w_out, ln1_g, ln1_b, w_up, w_conv, b_conv, w_down, ln2_g, ln2_b, loss_target, m_w_in, m_b_in, m_w_proj_sb, m_w_proj_fox, m_w_out, m_ln1_g, m_ln1_b, m_w_up, m_w_conv, m_b_conv, m_w_down, m_ln2_g, m_ln2_b, v_w_in, v_b_in, v_w_proj_sb, v_w_proj_fox, v_w_out, v_ln1_g, v_ln1_b, v_w_up, v_w_conv, v_b_conv, v_w_down, v_ln2_g, v_ln2_b):
    w = dict(w_in=w_in, b_in=b_in, w_proj_sb=w_proj_sb, w_proj_fox=w_proj_fox, w_out=w_out, ln1_g=ln1_g, ln1_b=ln1_b,
             w_up=w_up, w_conv=w_conv, b_conv=b_conv, w_down=w_down, ln2_g=ln2_g, ln2_b=ln2_b)
    m = dict(w_in=m_w_in, b_in=m_b_in, w_proj_sb=m_w_proj_sb, w_proj_fox=m_w_proj_fox, w_out=m_w_out, ln1_g=m_ln1_g,
             ln1_b=m_ln1_b, w_up=m_w_up, w_conv=m_w_conv, b_conv=m_b_conv, w_down=m_w_down, ln2_g=m_ln2_g, ln2_b=m_ln2_b)
    v = dict(w_in=v_w_in, b_in=v_b_in, w_proj_sb=v_w_proj_sb, w_proj_fox=v_w_proj_fox, w_out=v_w_out, ln1_g=v_ln1_g,
             ln1_b=v_ln1_b, w_up=v_w_up, w_conv=v_w_conv, b_conv=v_b_conv, w_down=v_w_down, ln2_g=v_ln2_g, ln2_b=v_ln2_b)
    order = ["w_in", "b_in", "w_proj_sb", "w_proj_fox", "w_out", "ln1_g", "ln1_b", "w_up", "w_conv", "b_conv", "w_down",
             "ln2_g", "ln2_b"]
    x_idx, y_idx, c_idx = _pos()
    chip = 2 * x_idx + y_idx
    D = x.shape[-1]
    core = c_idx.astype(jnp.int32).reshape(1)

    w_in_full = _unstack(_gather_shards([w["w_in"][0].astype(bf16)])[0], 1)
    late = (("w_proj_sb", 1), ("w_proj_fox", 1), ("w_out", 0), ("w_up", 1), ("w_conv", 1), ("w_down", 0))
    late_shards = [w[n][0] if n == "w_conv" else w[n][0].astype(bf16) for n, _ in late]
    late_weights = lambda stacks: [_unstack(s, axis) for (_, axis), s in zip(late, stacks)]

    sq_err, grad_x, grads = _local_step(x, loss_target, w_in_full, w["b_in"], w["ln1_g"], w["ln1_b"], w["b_conv"],
                                        w["ln2_g"], w["ln2_b"], late_shards, late_weights)
    loss_part = (0.5 / D) * jnp.sum(sq_err)

    pieces = [_pieces(grads[n], axis) for n, axis in _MATS]
    small_names = list(_SMALL) + ["w_conv"]
    small = _pack_rows([jnp.full((1, 1), loss_part, f32)] + [grads[n] for n in small_names])
    got, small_all = _exchange_sibling_and_small(pieces, small)
    chip_sums = [_add_own_half(p, g_, core, "add_sibling_" + n) for (n, _), p, g_ in zip(_MATS, pieces, got)]
    from_chips = _exchange_chips([s16 for _, s16 in chip_sums])
    where = jnp.stack([chip, c_idx]).astype(jnp.int32)
    halves = [_sum_chips(s32, r16, where, "sum_chips_" + n) for (n, _), (s32, _), r16 in zip(_MATS, chip_sums, from_chips)]
    g_shards = _share_halves(halves)
    small_sum = _sum_slots(small_all, "sum_small")

    out = {"grad": {}, "delta": {}, "m": {}, "v": {}}
    for (n, _), g_ in zip(_MATS, g_shards):
        d_, m_, v_ = _adamw(w[n][0], g_, m[n][0], v[n][0], "adamw_" + n)
        for key, t in (("grad", g_), ("delta", d_), ("m", m_), ("v", v_)):
            out[key][n] = t.reshape(w[n].shape)
    sizes = [1] + [int(grads[n].size) for n in small_names]
    sm = _unpack_rows(small_sum, sizes)
    loss = sm[0][0, 0]
    g_small = dict(zip(small_names, sm[1:]))
    F4 = w["w_conv"].shape[-1]
    g_small["w_conv"] = lax.dynamic_slice_in_dim(g_small["w_conv"].reshape(3, -1), chip * F4, F4, axis=1)
    pack_s = lambda d: _pack_rows([d[n].reshape(1, -1) for n in small_names])
    gs_packed = _pack_rows([g_small[n].reshape(1, -1) for n in small_names])
    s_delta, s_m, s_v = _adamw(pack_s(w), gs_packed, pack_s(m), pack_s(v), "adamw_small")
    s_sizes = [int(w[n].size) for n in small_names]

    for key, packed_s in (("grad", gs_packed), ("delta", s_delta), ("m", s_m), ("v", s_v)):
        for n, t in zip(small_names, _unpack_rows(packed_s, s_sizes)):
            out[key][n] = t.reshape(w[n].shape)
    return (loss, grad_x, *[out["grad"][n] for n in order], *[out["delta"][n] for n in order],
            *[out["m"][n] for n in order], *[out["v"][n] for n in order])
```

```python
import functools
import math

import jax
import jax.numpy as jnp
from jax import lax
from jax.experimental import pallas as pl
from jax.experimental.pallas import tpu as pltpu

f32, bf16 = jnp.float32, jnp.bfloat16

HEAD_DIM = 64
LANES = 128
LN_EPS = 1e-5
ALPHA = 2.0 ** 0.25
Q_SCALE = HEAD_DIM ** -0.5
ADAM_LR, ADAM_B1, ADAM_B2, ADAM_EPS, ADAM_WD, ADAM_STEP = 0.001, 0.9, 0.999, 1e-08, 0.01, 10
VMEM_LIMIT = 56 * 1024 * 1024
NEG = -1e30

_pcall = pl.pallas_call
_NT = (((1,), (1,)), ((), ()))
_TN = (((0,), (0,)), ((), ()))


def _params(sem=None):
    return pltpu.CompilerParams(dimension_semantics=sem, vmem_limit_bytes=VMEM_LIMIT)


def _tile(dim, target, unit=LANES):
    if dim <= target:
        return dim
    t = (target // unit) * unit
    while t > unit and dim % t:
        t -= unit
    assert dim % t == 0, (dim, target)
    return t


def _dot(a, b, dn=None):
    if dn is None:
        return jnp.dot(a, b, preferred_element_type=f32)
    return lax.dot_general(a, b, dn, preferred_element_type=f32)


def _split_dot(x, tri):
    hi = x.astype(bf16)
    lo = (x - hi.astype(f32)).astype(bf16)
    return _dot(hi, tri) + _dot(lo, tri)


SCAN_BLOCK = 256


def _scan_cols(x, tri, reverse):
    cb = tri.shape[0]
    nb = x.shape[1] // cb
    blocks = [x[:, b * cb:(b + 1) * cb] for b in range(nb)]
    outs, run = [None] * nb, None
    for b in (reversed(range(nb)) if reverse else range(nb)):
        o = _split_dot(blocks[b], tri)
        s = jnp.sum(blocks[b], axis=1, keepdims=True)
        outs[b] = o if run is None else o + run
        run = s if run is None else run + s
    return (outs[0] if nb == 1 else jnp.concatenate(outs, axis=1)), run


def _tri(cb, rel):
    row = lax.broadcasted_iota(jnp.int32, (cb, cb), 0)
    col = lax.broadcasted_iota(jnp.int32, (cb, cb), 1)
    return rel(row, col).astype(bf16)


def _matmul(a, b, *, name, ta=False, tb=False, bias=None, addend=None, addend_scale=1.0, colsum=False,
            out_dtype=f32, tm=512, tn=512, tk=1024):
    M, K = (a.shape[1], a.shape[0]) if ta else a.shape
    N = b.shape[0] if tb else b.shape[1]
    assert K == (b.shape[1] if tb else b.shape[0])
    assert not (colsum and tb)
    tm, tn, tk = _tile(M, tm), _tile(N, tn), _tile(K, tk)
    nk = K // tk
    n_in = 2 + (bias is not None) + (addend is not None)

    def body(*refs):
        a_ref, b_ref = refs[0], refs[1]
        bias_ref = refs[2] if bias is not None else None
        add_ref = refs[n_in - 1] if addend is not None else None
        o_ref = refs[n_in]
        cs_ref = refs[n_in + 1] if colsum else None
        acc = refs[-2] if colsum else refs[-1]
        cs_acc = refs[-1] if colsum else None
        k = pl.program_id(2)

        @pl.when(k == 0)
        def _():
            acc[...] = jnp.zeros_like(acc)
            if colsum:
                cs_acc[...] = jnp.zeros_like(cs_acc)

        dn = (((0 if ta else 1,), (1 if tb else 0,)), ((), ()))
        acc[...] += lax.dot_general(a_ref[...].astype(bf16), b_ref[...].astype(bf16), dn, preferred_element_type=f32)
        if colsum:
            cs_acc[...] += jnp.sum(b_ref[...].astype(f32), axis=0, keepdims=True)

        @pl.when(k == nk - 1)
        def _():
            r = acc[...]
            if bias is not None:
                r = r + bias_ref[...]
            if addend is not None:
                r = r + addend_scale * add_ref[...].astype(f32)
            o_ref[...] = r.astype(out_dtype)
            if colsum:
                cs_ref[0] = cs_acc[...]

    a_spec = pl.BlockSpec((tk, tm), lambda i, j, k: (k, i)) if ta else pl.BlockSpec((tm, tk), lambda i, j, k: (i, k))
    b_spec = pl.BlockSpec((tn, tk), lambda i, j, k: (j, k)) if tb else pl.BlockSpec((tk, tn), lambda i, j, k: (k, j))
    in_specs, args = [a_spec, b_spec], [a, b]
    if bias is not None:
        in_specs.append(pl.BlockSpec((1, tn), lambda i, j, k: (0, j)))
        args.append(bias.reshape(1, N).astype(f32))
    if addend is not None:
        in_specs.append(pl.BlockSpec((tm, tn), lambda i, j, k: (i, j)))
        args.append(addend)
    out_shape = [jax.ShapeDtypeStruct((M, N), out_dtype)]
    out_specs = [pl.BlockSpec((tm, tn), lambda i, j, k: (i, j))]
    scratch = [pltpu.VMEM((tm, tn), f32)]
    if colsum:
        out_shape.append(jax.ShapeDtypeStruct((M // tm, 1, N), f32))
        out_specs.append(pl.BlockSpec((1, 1, tn), lambda i, j, k: (i, 0, j)))
        scratch.append(pltpu.VMEM((1, tn), f32))
    res = _pcall(body, name=name, grid=(M // tm, N // tn, nk), in_specs=in_specs, out_specs=out_specs,
                 out_shape=out_shape, scratch_shapes=scratch,
                 compiler_params=_params(("arbitrary", "arbitrary", "arbitrary")))(*args)
    return (res[0], res[1][0]) if colsum else res[0]


def _cumlogf(fl, B, S):
    t = _tile(S, 256, 8)

    def body(fl_ref, c_ref, carry):
        @pl.when(pl.program_id(1) == 0)
        def _():
            carry[...] = jnp.zeros_like(carry)
        z = fl_ref[...]
        ls = jnp.minimum(z, 0.0) - jnp.log(1.0 + jnp.exp(-jnp.abs(z)))
        row = lax.broadcasted_iota(jnp.int32, (t, t), 0)
        col = lax.broadcasted_iota(jnp.int32, (t, t), 1)
        lower = (col <= row).astype(f32)
        c = jnp.dot(lower, ls, precision=lax.Precision.HIGHEST, preferred_element_type=f32) + carry[...]
        c_ref[...] = c
        carry[...] = c[t - 1:t, :]

    return _pcall(body, name="cumlogf", grid=(B, S // t),
                  in_specs=[pl.BlockSpec((t, LANES), lambda b, i: (b * (S // t) + i, 0))],
                  out_specs=pl.BlockSpec((t, LANES), lambda b, i: (b * (S // t) + i, 0)),
                  out_shape=jax.ShapeDtypeStruct(fl.shape, f32), scratch_shapes=[pltpu.VMEM((1, LANES), f32)],
                  compiler_params=_params(("arbitrary", "arbitrary")))(fl)


def _cumlogf_bwd(dc, fl, B, S):
    t = _tile(S, 256, 8)
    n = S // t

    def body(dc_ref, fl_ref, o_ref, carry):
        @pl.when(pl.program_id(1) == 0)
        def _():
            carry[...] = jnp.zeros_like(carry)
        row = lax.broadcasted_iota(jnp.int32, (t, t), 0)
        col = lax.broadcasted_iota(jnp.int32, (t, t), 1)
        upper = (col >= row).astype(f32)
        r = jnp.dot(upper, dc_ref[...], precision=lax.Precision.HIGHEST, preferred_element_type=f32) + carry[...]
        carry[...] = r[0:1, :]
        z = fl_ref[...]
        o_ref[...] = r / (1.0 + jnp.exp(z))

    spec = pl.BlockSpec((t, LANES), lambda b, i: (b * n + n - 1 - i, 0))
    return _pcall(body, name="cumlogf_bwd", grid=(B, n), in_specs=[spec, spec], out_specs=spec,
                  out_shape=jax.ShapeDtypeStruct(fl.shape, f32), scratch_shapes=[pltpu.VMEM((1, LANES), f32)],
                  compiler_params=_params(("arbitrary", "arbitrary")))(dc, fl)


def _head_masks():
    lane = lax.broadcasted_iota(jnp.int32, (1, LANES), 1)
    return lane < HEAD_DIM


def _by_head(m0, t):
    z = jnp.zeros_like(t)
    return [jnp.where(m0, t, z), jnp.where(m0, z, t)]


def _sb_terms(z):
    relu = jnp.maximum(z, 0.0)
    sp = jnp.log(1.0 + jnp.exp(-jnp.abs(z)))
    return (z - relu) - sp, -relu - sp


STRIP_ROWS = 32


def _strip_rows(tq):
    return STRIP_ROWS if tq % STRIP_ROWS == 0 else tq


def _strict(r, rs, tq):
    row = lax.broadcasted_iota(jnp.int32, (rs, tq), 0) + r
    col = lax.broadcasted_iota(jnp.int32, (rs, tq), 1)
    return col < row


def _score_scratch(tq, n_f32, n_bf16, n_sums):
    return ([pltpu.VMEM((tq, tq), f32)] * (2 * n_f32) + [pltpu.VMEM((tq, tq), bf16)] * (2 * n_bf16)
            + [pltpu.VMEM((tq, LANES), f32)] * (2 * n_sums))


def _by_pairs(refs):
    return [refs[i:i + 2] for i in range(0, len(refs), 2)]


def _sb_fwd(qkv, B, S, n_pairs, qcol, kcol, vcol, tq, shards=()):
    nq = S // tq
    T = B * S
    n = len(shards)

    cb = min(tq, SCAN_BLOCK)
    nb = tq // cb
    rs = _strip_rows(tq)

    def body(q_ref, k_ref, v_ref, *rest):
        o_ref, lt_ref = rest[n], rest[n + 1]
        z_s, suf_s, hi_s, lo_s, w_s, sum_s = _by_pairs(rest[len(rest) - 12:])
        i = pl.program_id(2)
        if n:
            start, finish = _gather_copies(rest[:n], rest[n + 2:2 * n + 2], *rest[2 * n + 2:2 * n + 5])
            step = (pl.program_id(0) * n_pairs + pl.program_id(1)) * nq + i
            pl.when(step == 0)(start)
        m0 = _head_masks()
        qh = _by_head(m0, q_ref[...])
        later = _tri(cb, lambda j, s: j > s)
        lane = lax.broadcasted_iota(jnp.int32, (1, LANES), 1)

        def tile(s0, R, acc, diag):
            kb = k_ref[pl.ds(s0, tq), :]
            vh = _by_head(m0, v_ref[pl.ds(s0, tq), :])
            R = list(R)
            for h in range(2):
                z_s[h][...] = _dot(qh[h], kb, _NT)
            for h in range(2):
                for r in range(0, tq, rs):
                    l1m = _sb_terms(z_s[h][r:r + rs, :])[1]
                    if diag:
                        l1m = jnp.where(_strict(r, rs, tq), l1m, 0.0)
                    hi = l1m.astype(bf16)
                    hi_s[h][r:r + rs, :] = hi
                    lo_s[h][r:r + rs, :] = (l1m - hi.astype(f32)).astype(bf16)
                    sums = jnp.zeros((rs, LANES), f32)
                    for b in range(nb):
                        sums = jnp.where(lane == b, jnp.sum(l1m[:, b * cb:(b + 1) * cb], axis=1, keepdims=True), sums)
                    sum_s[h][r:r + rs, :] = sums
            for h in range(2):
                for b in range(nb):
                    blk = slice(b * cb, (b + 1) * cb)
                    suf_s[h][:, blk] = _dot(hi_s[h][:, blk], later) + _dot(lo_s[h][:, blk], later)
            for h in range(2):
                for r in range(0, tq, rs):
                    sums = sum_s[h][r:r + rs, :]
                    after = R[h][r:r + rs]
                    for b in reversed(range(nb)):
                        blk = slice(b * cb, (b + 1) * cb)
                        l1m = hi_s[h][r:r + rs, blk].astype(f32) + lo_s[h][r:r + rs, blk].astype(f32)
                        w = jnp.exp((z_s[h][r:r + rs, blk] + l1m) + (suf_s[h][r:r + rs, blk] + after))
                        if diag:
                            w = jnp.where(_strict(r, rs, tq)[:, blk], w, 0.0)
                        w_s[h][r:r + rs, blk] = w.astype(bf16)
                        after = after + sums[:, b:b + 1]
            for h in range(2):
                acc = acc + _dot(w_s[h][...], vh[h])
                R[h] = R[h] + jnp.sum(sum_s[h][...], axis=1, keepdims=True)
            return R, acc

        zero = jnp.zeros((tq, 1), f32)
        R, acc = tile(pl.multiple_of(i * tq, tq), [zero, zero], jnp.zeros((tq, LANES), f32), True)

        def loop(n, carry):
            s0 = pl.multiple_of((i - 1 - n) * tq, tq)
            R, acc = tile(s0, carry[:2], carry[2], False)
            return R[0], R[1], acc

        R0, R1, acc = lax.fori_loop(0, i, loop, (R[0], R[1], acc))
        o_ref[...] = acc.astype(bf16)
        lt_ref[...] = jnp.where(m0, R0, R1)
        if n:
            pl.when(step == B * n_pairs * nq - 1)(finish)

    qs = lambda c: pl.BlockSpec((tq, LANES), lambda b, p, i: (b * nq + i, c + p))
    ks = lambda c: pl.BlockSpec((S, LANES), lambda b, p, i: (b, c + p))
    os_ = pl.BlockSpec((tq, LANES), lambda b, p, i: (b * nq + i, p))
    res = _pcall(body, name="sb_fwd", grid=(B, n_pairs, nq), in_specs=[qs(qcol), ks(kcol), ks(vcol)] + [_ANY] * n,
                 out_specs=[os_, os_] + [_ANY] * n,
                 out_shape=[jax.ShapeDtypeStruct((T, n_pairs * LANES), bf16), jax.ShapeDtypeStruct((T, n_pairs * LANES), f32)]
                 + _gather_shapes(shards), scratch_shapes=(_gather_sems(n) if n else []) + _score_scratch(tq, 2, 3, 1),
                 compiler_params=_params(("arbitrary", "arbitrary", "arbitrary")))(qkv, qkv, qkv, *shards)
    return res[0], res[1], res[2:]


def _sb_bwd(qkv, do, lt, B, S, n_pairs, qcol, kcol, vcol, tq):
    nq = S // tq
    T = B * S

    def body(q_ref, k_ref, v_ref, do_ref, lt_ref, dq_ref, dk_ref, dv_ref):
        i = pl.program_id(2)

        @pl.when(i == 0)
        def _():
            dk_ref[...] = jnp.zeros_like(dk_ref)
            dv_ref[...] = jnp.zeros_like(dv_ref)

        m0 = _head_masks()
        qh = _by_head(m0, q_ref[...])
        doh = _by_head(m0, do_ref[...])
        lt = lt_ref[...]
        ltot = [lt[:, 0:1], lt[:, HEAD_DIM:HEAD_DIM + 1]]
        row = lax.broadcasted_iota(jnp.int32, (tq, tq), 0)
        col = lax.broadcasted_iota(jnp.int32, (tq, tq), 1)
        strict = col < row
        upto = _tri(min(tq, SCAN_BLOCK), lambda j, s: j <= s)
        before = _tri(min(tq, SCAN_BLOCK), lambda j, s: j < s)

        def tile(s0, CL, CP, dq, diag):
            kb = k_ref[pl.ds(s0, tq), :]
            vb = v_ref[pl.ds(s0, tq), :]
            kh = _by_head(m0, kb)
            CL, CP = list(CL), list(CP)
            dk = jnp.zeros((tq, LANES), f32)
            dv = jnp.zeros((tq, LANES), f32)
            for h in range(2):
                z = _dot(qh[h], kb, _NT)
                lb, l1m = _sb_terms(z)
                if diag:
                    l1m = jnp.where(strict, l1m, 0.0)
                pre, l_total = _scan_cols(l1m, upto, False)
                w = jnp.exp(lb + ((ltot[h] - CL[h]) - pre))
                if diag:
                    w = jnp.where(strict, w, 0.0)
                g = _dot(doh[h], vb, _NT) * w
                p, g_total = _scan_cols(g, before, False)
                dz = g - jnp.exp(lb) * (g + (p + CP[h]))
                if diag:
                    dz = jnp.where(strict, dz, 0.0)
                dzb = dz.astype(bf16)
                dq = dq + _dot(dzb, kh[h])
                dk = dk + _dot(dzb, qh[h], _TN)
                dv = dv + _dot(w.astype(bf16), doh[h], _TN)
                CL[h] = CL[h] + l_total
                CP[h] = CP[h] + g_total
            dk_ref[pl.ds(s0, tq), :] += dk
            dv_ref[pl.ds(s0, tq), :] += dv
            return CL, CP, dq

        zero = jnp.zeros((tq, 1), f32)

        def loop(n, carry):
            CL, CP, dq = tile(pl.multiple_of(n * tq, tq), carry[0:2], carry[2:4], carry[4], False)
            return CL[0], CL[1], CP[0], CP[1], dq

        c = lax.fori_loop(0, i, loop, (zero, zero, zero, zero, jnp.zeros((tq, LANES), f32)))
        _, _, dq = tile(pl.multiple_of(i * tq, tq), c[0:2], c[2:4], c[4], True)
        dq_ref[...] = dq * Q_SCALE

    qs = lambda c: pl.BlockSpec((tq, LANES), lambda b, p, i: (b * nq + i, c + p))
    ks = lambda c: pl.BlockSpec((S, LANES), lambda b, p, i: (b, c + p))
    ts = pl.BlockSpec((tq, LANES), lambda b, p, i: (b * nq + i, p))
    fs = pl.BlockSpec((S, LANES), lambda b, p, i: (b, p))
    shp = jax.ShapeDtypeStruct((T, n_pairs * LANES), f32)
    return _pcall(body, name="sb_bwd", grid=(B, n_pairs, nq),
                  in_specs=[qs(qcol), ks(kcol), ks(vcol), ts, ts], out_specs=[ts, fs, fs], out_shape=[shp, shp, shp],
                  compiler_params=_params(("arbitrary", "arbitrary", "arbitrary")))(qkv, qkv, qkv, do, lt)


def _fox_fwd(qkv, c, cT, B, S, n_pairs, qcol, kcol, vcol, tq):
    nq = S // tq
    T = B * S

    def body(q_ref, k_ref, v_ref, cq_ref, ck_ref, o_ref, o32_ref, lse_ref):
        p_idx = pl.program_id(1)
        i = pl.program_id(2)
        m0 = _head_masks()
        lane = lax.broadcasted_iota(jnp.int32, (1, LANES), 1)
        qh = _by_head(m0, q_ref[...])
        cq_all = cq_ref[...]
        cq = [jnp.sum(jnp.where(lane == 2 * p_idx + h, cq_all, 0.0), axis=1, keepdims=True) for h in range(2)]
        row = lax.broadcasted_iota(jnp.int32, (tq, tq), 0)
        col = lax.broadcasted_iota(jnp.int32, (tq, tq), 1)
        causal = col <= row

        def tile(s0, m, l, acc, diag):
            kb = k_ref[pl.ds(s0, tq), :]
            vh = _by_head(m0, v_ref[pl.ds(s0, tq), :])
            m, l = list(m), list(l)
            scale, add = [], []
            for h in range(2):
                z = _dot(qh[h], kb, _NT) + (cq[h] - ck_ref[h, :, pl.ds(s0, tq)])
                if diag:
                    z = jnp.where(causal, z, NEG)
                m_new = jnp.maximum(m[h], jnp.max(z, axis=1, keepdims=True))
                p = jnp.exp(z - m_new)
                a = jnp.exp(m[h] - m_new)
                l[h] = a * l[h] + jnp.sum(p, axis=1, keepdims=True)
                m[h] = m_new
                scale.append(a)
                add.append(_dot(p.astype(bf16), vh[h]))
            acc = acc * jnp.where(m0, scale[0], scale[1]) + add[0] + add[1]
            return m, l, acc

        neg = jnp.full((tq, 1), NEG, f32)
        zero = jnp.zeros((tq, 1), f32)
        m, l, acc = tile(pl.multiple_of(i * tq, tq), [neg, neg], [zero, zero], jnp.zeros((tq, LANES), f32), True)

        def loop(n, carry):
            m, l, acc = tile(pl.multiple_of(n * tq, tq), carry[0:2], carry[2:4], carry[4], False)
            return m[0], m[1], l[0], l[1], acc

        m0_, m1_, l0, l1, acc = lax.fori_loop(0, i, loop, (m[0], m[1], l[0], l[1], acc))
        o = acc * jnp.where(m0, 1.0 / l0, 1.0 / l1)
        o_ref[...] = o.astype(bf16)
        o32_ref[...] = o
        lse_ref[...] = jnp.where(m0, m0_ + jnp.log(l0), m1_ + jnp.log(l1))

    qs = lambda cc: pl.BlockSpec((tq, LANES), lambda b, p, i: (b * nq + i, cc + p))
    ks = lambda cc: pl.BlockSpec((S, LANES), lambda b, p, i: (b, cc + p))
    cqs = pl.BlockSpec((tq, LANES), lambda b, p, i: (b * nq + i, 0))
    cks = pl.BlockSpec((2, 1, S), lambda b, p, i: (b * n_pairs + p, 0, 0))
    os_ = pl.BlockSpec((tq, LANES), lambda b, p, i: (b * nq + i, p))
    shp = jax.ShapeDtypeStruct((T, n_pairs * LANES), f32)
    return _pcall(body, name="fox_fwd", grid=(B, n_pairs, nq), in_specs=[qs(qcol), ks(kcol), ks(vcol), cqs, cks],
                  out_specs=[os_, os_, os_], out_shape=[jax.ShapeDtypeStruct((T, n_pairs * LANES), bf16), shp, shp],
                  compiler_params=_params(("arbitrary", "arbitrary", "arbitrary")))(qkv, qkv, qkv, c, cT)


def _fox_bwd(qkv, c, cT, do, o, lse, B, S, n_pairs, qcol, kcol, vcol, tq):
    nq = S // tq
    T = B * S

    def body(q_ref, k_ref, v_ref, cq_ref, ck_ref, do_ref, o_ref, lse_ref, dq_ref, dk_ref, dv_ref, dc_ref):
        p_idx = pl.program_id(1)
        i = pl.program_id(2)

        @pl.when(i == 0)
        def _():
            dk_ref[...] = jnp.zeros_like(dk_ref)
            dv_ref[...] = jnp.zeros_like(dv_ref)
            dc_ref[...] = jnp.zeros_like(dc_ref)

        m0 = _head_masks()
        lane = lax.broadcasted_iota(jnp.int32, (1, LANES), 1)
        qh = _by_head(m0, q_ref[...])
        do2 = do_ref[...]
        doh = _by_head(m0, do2)
        prod = do2.astype(f32) * o_ref[...].astype(f32)
        delta = [jnp.sum(p, axis=1, keepdims=True) for p in _by_head(m0, prod)]
        ls = lse_ref[...]
        lse = [ls[:, 0:1], ls[:, HEAD_DIM:HEAD_DIM + 1]]
        cq_all = cq_ref[...]
        cq = [jnp.sum(jnp.where(lane == 2 * p_idx + h, cq_all, 0.0), axis=1, keepdims=True) for h in range(2)]
        row = lax.broadcasted_iota(jnp.int32, (tq, tq), 0)
        col = lax.broadcasted_iota(jnp.int32, (tq, tq), 1)
        causal = col <= row

        def tile(s0, dq, diag):
            kb = k_ref[pl.ds(s0, tq), :]
            vb = v_ref[pl.ds(s0, tq), :]
            kh = _by_head(m0, kb)
            dk = jnp.zeros((tq, LANES), f32)
            dv = jnp.zeros((tq, LANES), f32)
            for h in range(2):
                z = _dot(qh[h], kb, _NT) + (cq[h] - ck_ref[h, :, pl.ds(s0, tq)])
                p = jnp.exp(z - lse[h])
                if diag:
                    p = jnp.where(causal, p, 0.0)
                ds = p * (_dot(doh[h], vb, _NT) - delta[h])
                dsb = ds.astype(bf16)
                dq = dq + _dot(dsb, kh[h])
                dk = dk + _dot(dsb, qh[h], _TN)
                dv = dv + _dot(p.astype(bf16), doh[h], _TN)
                dc_ref[h, :, pl.ds(s0, tq)] -= jnp.sum(ds, axis=0, keepdims=True)
            dk_ref[pl.ds(s0, tq), :] += dk
            dv_ref[pl.ds(s0, tq), :] += dv
            return dq

        dq = lax.fori_loop(0, i, lambda n, dq: tile(pl.multiple_of(n * tq, tq), dq, False), jnp.zeros((tq, LANES), f32))
        dq = tile(pl.multiple_of(i * tq, tq), dq, True)
        dq_ref[...] = dq * Q_SCALE

    qs = lambda cc: pl.BlockSpec((tq, LANES), lambda b, p, i: (b * nq + i, cc + p))
    ks = lambda cc: pl.BlockSpec((S, LANES), lambda b, p, i: (b, cc + p))
    cqs = pl.BlockSpec((tq, LANES), lambda b, p, i: (b * nq + i, 0))
    cks = pl.BlockSpec((2, 1, S), lambda b, p, i: (b * n_pairs + p, 0, 0))
    ts = pl.BlockSpec((tq, LANES), lambda b, p, i: (b * nq + i, p))
    fs = pl.BlockSpec((S, LANES), lambda b, p, i: (b, p))
    shp = jax.ShapeDtypeStruct((T, n_pairs * LANES), f32)
    return _pcall(body, name="fox_bwd", grid=(B, n_pairs, nq),
                  in_specs=[qs(qcol), ks(kcol), ks(vcol), cqs, cks, ts, ts, ts], out_specs=[ts, fs, fs, cks],
                  out_shape=[shp, shp, shp, jax.ShapeDtypeStruct(cT.shape, f32)],
                  compiler_params=_params(("arbitrary", "arbitrary", "arbitrary")))(qkv, qkv, qkv, c, cT, do, o, lse)


def _sigmoid(x):
    return 1.0 / (1.0 + jnp.exp(-x))


def _mix_fwd(o_sb, o_fx, g, x, wp_sb, wp_fx, w_out, ln_g, ln_b, tm):
    T, D = x.shape
    E = o_sb.shape[1]
    tm = _tile(T, tm, 8)

    def body(osb_ref, ofx_ref, gsb_ref, gfx_ref, x_ref, wsb_ref, wfx_ref, wo_ref, lg_ref, lb_ref,
             xhat_ref, rstd_ref, x1_ref, mg_ref):
        y_sb = _dot(osb_ref[...], wsb_ref[...])
        y_fx = _dot(ofx_ref[...], wfx_ref[...])
        merged = (_sigmoid(gsb_ref[...]) * y_sb + _sigmoid(gfx_ref[...]) * y_fx).astype(bf16)
        r = ALPHA * x_ref[...] + _dot(merged, wo_ref[...])
        mean = jnp.mean(r, axis=1, keepdims=True)
        cen = r - mean
        rstd = lax.rsqrt(jnp.mean(cen * cen, axis=1, keepdims=True) + LN_EPS)
        xhat = cen * rstd
        xhat_ref[...] = xhat
        rstd_ref[...] = rstd
        x1_ref[...] = (xhat * lg_ref[...] + lb_ref[...]).astype(bf16)
        mg_ref[...] = merged

    rows = lambda w, c=0: pl.BlockSpec((tm, w), lambda i: (i, c))
    full = lambda a: pl.BlockSpec(a.shape, lambda i: (0, 0))
    return _pcall(body, name="mix_fwd", grid=(T // tm,),
                  in_specs=[rows(E), rows(E), rows(D, 0), rows(D, 1), rows(D), full(wp_sb), full(wp_fx), full(w_out),
                            full(ln_g), full(ln_b)],
                  out_specs=[rows(D), rows(1), rows(D), rows(D)],
                  out_shape=[jax.ShapeDtypeStruct((T, D), f32), jax.ShapeDtypeStruct((T, 1), f32),
                             jax.ShapeDtypeStruct((T, D), bf16), jax.ShapeDtypeStruct((T, D), bf16)],
                  compiler_params=_params(("arbitrary",)))(o_sb, o_fx, g, g, x, wp_sb, wp_fx, w_out, ln_g, ln_b)


def _mix_bwd(dr1, o_sb, o_fx, g, wp_sb, wp_fx, w_out, tm):
    T, D = dr1.shape
    E = o_sb.shape[1]
    tm = _tile(T, tm, 8)

    def body(dr_ref, osb_ref, ofx_ref, gsb_ref, gfx_ref, wsb_ref, wfx_ref, wo_ref,
             dysb_ref, dyfx_ref, dgsb_ref, dgfx_ref, dosb_ref, dofx_ref, sumsb_ref, sumfx_ref):
        @pl.when(pl.program_id(0) == 0)
        def _():
            sumsb_ref[...] = jnp.zeros_like(sumsb_ref)
            sumfx_ref[...] = jnp.zeros_like(sumfx_ref)
        dm = _dot(dr_ref[...].astype(bf16), wo_ref[...], _NT)
        for o_ref, g_ref, w_ref, dy_ref, dg_ref, do_ref, sum_ref in (
                (osb_ref, gsb_ref, wsb_ref, dysb_ref, dgsb_ref, dosb_ref, sumsb_ref),
                (ofx_ref, gfx_ref, wfx_ref, dyfx_ref, dgfx_ref, dofx_ref, sumfx_ref)):
            y = _dot(o_ref[...], w_ref[...])
            s = _sigmoid(g_ref[...])
            dy = (dm * s).astype(bf16)
            dy_ref[...] = dy
            dg = dm * y * s * (1.0 - s)
            dg_ref[...] = dg.astype(bf16)
            sum_ref[0:1, :] += jnp.sum(dg, axis=0, keepdims=True)
            do_ref[...] = _dot(dy, w_ref[...], _NT).astype(bf16)

    rows = lambda w, c=0: pl.BlockSpec((tm, w), lambda i: (i, c))
    full = lambda a: pl.BlockSpec(a.shape, lambda i: (0, 0))
    acc = pl.BlockSpec((8, D), lambda i: (0, 0))
    res = _pcall(body, name="mix_bwd", grid=(T // tm,),
                 in_specs=[rows(D), rows(E), rows(E), rows(D, 0), rows(D, 1), full(wp_sb), full(wp_fx), full(w_out)],
                 out_specs=[rows(D), rows(D), rows(D), rows(D), rows(E), rows(E), acc, acc],
                 out_shape=[jax.ShapeDtypeStruct((T, D), bf16)] * 4 + [jax.ShapeDtypeStruct((T, E), bf16)] * 2
                 + [jax.ShapeDtypeStruct((8, D), f32)] * 2,
                 compiler_params=_params(("arbitrary",)))(dr1, o_sb, o_fx, g, g, wp_sb, wp_fx, w_out)
    return res


def _ln_bwd(dy_a, dy_b, scale_b, xhat, rstd, ln_g, tm):
    T, D = xhat.shape
    tm = _tile(T, tm, 8)

    def body(a_ref, b_ref, xh_ref, rs_ref, g_ref, dr_ref, st_ref):
        @pl.when(pl.program_id(0) == 0)
        def _():
            st_ref[...] = jnp.zeros_like(st_ref)
        dy = a_ref[...] + scale_b * b_ref[...]
        xh = xh_ref[...]
        dxh = dy * g_ref[...]
        m1 = jnp.mean(dxh, axis=1, keepdims=True)
        m2 = jnp.mean(dxh * xh, axis=1, keepdims=True)
        dr_ref[...] = rs_ref[...] * (dxh - m1 - xh * m2)
        st_ref[0:1, :] += jnp.sum(dy * xh, axis=0, keepdims=True)
        st_ref[1:2, :] += jnp.sum(dy, axis=0, keepdims=True)

    rows = lambda w: pl.BlockSpec((tm, w), lambda i: (i, 0))
    return _pcall(body, name="ln1_bwd", grid=(T // tm,),
                  in_specs=[rows(D), rows(D), rows(D), rows(1), pl.BlockSpec((1, D), lambda i: (0, 0))],
                  out_specs=[rows(D), pl.BlockSpec((8, D), lambda i: (0, 0))],
                  out_shape=[jax.ShapeDtypeStruct((T, D), f32), jax.ShapeDtypeStruct((8, D), f32)],
                  compiler_params=_params(("arbitrary",)))(dy_a, dy_b, xhat, rstd, ln_g)


_INV_SQRT2 = 1.0 / math.sqrt(2.0)
_INV_SQRT2PI = 1.0 / math.sqrt(2.0 * math.pi)


def _conv_rows(ref, r0, rc, first, wc, bc):
    cur = ref[pl.ds(r0, rc), :]
    prev = ref[pl.ds(pl.multiple_of(jnp.maximum(r0 - 8, 0), 8), 8), :]
    prev = jnp.where(first, jnp.zeros_like(prev), prev)
    rid = lax.broadcasted_iota(jnp.int32, (rc, LANES), 0)
    s1 = jnp.where(rid == 0, prev[7:8, :], pltpu.roll(cur, 1, 0))
    s2 = jnp.where(rid == 0, prev[6:7, :], jnp.where(rid == 1, prev[7:8, :], pltpu.roll(cur, 2, 0)))
    conv = bc + wc[0:1, :] * s2 + wc[1:2, :] * s1 + wc[2:3, :] * cur
    return conv, (s2, s1, cur)


def _glu_fwd(u, w_conv, b_conv, B, S, rc=512):
    F = u.shape[1] // 2
    nf = F // LANES
    rc = _tile(S, rc, 8)

    def body(ug_ref, uv_ref, wc_ref, bc_ref, a_ref):
        wc, bc = wc_ref[...], bc_ref[...]

        def chunk(n, _):
            r0 = pl.multiple_of(n * rc, rc)
            c, _taps = _conv_rows(ug_ref, r0, rc, n == 0, wc, bc)
            gelu = 0.5 * c * (1.0 + lax.erf(c * _INV_SQRT2))
            a_ref[pl.ds(r0, rc), :] = (gelu * uv_ref[pl.ds(r0, rc), :]).astype(bf16)
            return 0

        lax.fori_loop(0, S // rc, chunk, 0)

    return _pcall(body, name="glu_fwd", grid=(B, nf),
                  in_specs=[pl.BlockSpec((S, LANES), lambda b, j: (b, j)), pl.BlockSpec((S, LANES), lambda b, j: (b, nf + j)),
                            pl.BlockSpec((3, LANES), lambda b, j: (0, j)), pl.BlockSpec((1, LANES), lambda b, j: (0, j))],
                  out_specs=pl.BlockSpec((S, LANES), lambda b, j: (b, j)),
                  out_shape=jax.ShapeDtypeStruct((B * S, F), bf16),
                  compiler_params=_params(("arbitrary", "arbitrary")))(u, u, w_conv, b_conv)


def _glu_bwd(u, da, w_conv, b_conv, B, S, rc=512):
    F = u.shape[1] // 2
    nf = F // LANES
    rc = _tile(S, rc, 8)
    nc = S // rc

    def body(ug_ref, uv_ref, da_ref, wc_ref, bc_ref, dug_ref, duv_ref, gw_ref, gb_ref, dc_ref):
        wc, bc = wc_ref[...], bc_ref[...]

        def chunk(n, carry):
            gw0, gw1, gw2, gb = carry
            r0 = pl.multiple_of(n * rc, rc)
            c, (s2, s1, cur) = _conv_rows(ug_ref, r0, rc, n == 0, wc, bc)
            cdf = 0.5 * (1.0 + lax.erf(c * _INV_SQRT2))
            da = da_ref[pl.ds(r0, rc), :]
            duv_ref[pl.ds(r0, rc), :] = (da * (c * cdf)).astype(bf16)
            dc = da * uv_ref[pl.ds(r0, rc), :] * (cdf + c * (_INV_SQRT2PI * jnp.exp(-0.5 * c * c)))
            dc_ref[pl.ds(r0, rc), :] = dc
            red = lambda t: jnp.sum(t, axis=0, keepdims=True)
            return gw0 + red(dc * s2), gw1 + red(dc * s1), gw2 + red(dc * cur), gb + red(dc)

        z = jnp.zeros((1, LANES), f32)
        gw0, gw1, gw2, gb = lax.fori_loop(0, nc, chunk, (z, z, z, z))
        gw_ref[0, 0:1, :] = gw0
        gw_ref[0, 1:2, :] = gw1
        gw_ref[0, 2:3, :] = gw2
        gb_ref[0] = gb

        def chunk2(n, _):
            r0 = pl.multiple_of(n * rc, rc)
            cur = dc_ref[pl.ds(r0, rc), :]
            nxt = dc_ref[pl.ds(pl.multiple_of(jnp.minimum(r0 + rc, S - 8), 8), 8), :]
            nxt = jnp.where(n == nc - 1, jnp.zeros_like(nxt), nxt)
            rid = lax.broadcasted_iota(jnp.int32, (rc, LANES), 0)
            a1 = jnp.where(rid == rc - 1, nxt[0:1, :], pltpu.roll(cur, rc - 1, 0))
            a2 = jnp.where(rid == rc - 1, nxt[1:2, :], jnp.where(rid == rc - 2, nxt[0:1, :], pltpu.roll(cur, rc - 2, 0)))
            dug_ref[pl.ds(r0, rc), :] = (wc[2:3, :] * cur + wc[1:2, :] * a1 + wc[0:1, :] * a2).astype(bf16)
            return 0

        lax.fori_loop(0, nc, chunk2, 0)

    blk = lambda off: pl.BlockSpec((S, LANES), lambda b, j: (b, off + j))
    return _pcall(body, name="glu_bwd", grid=(B, nf),
                  in_specs=[blk(0), blk(nf), blk(0), pl.BlockSpec((3, LANES), lambda b, j: (0, j)),
                            pl.BlockSpec((1, LANES), lambda b, j: (0, j))],
                  out_specs=[blk(0), blk(0), pl.BlockSpec((1, 3, LANES), lambda b, j: (b, 0, j)),
                             pl.BlockSpec((1, 1, LANES), lambda b, j: (b, 0, j))],
                  out_shape=[jax.ShapeDtypeStruct((B * S, F), bf16), jax.ShapeDtypeStruct((B * S, F), bf16),
                             jax.ShapeDtypeStruct((B, 3, F), f32), jax.ShapeDtypeStruct((B, 1, F), f32)],
                  scratch_shapes=[pltpu.VMEM((S, LANES), f32)],
                  compiler_params=_params(("arbitrary", "arbitrary")))(u, u, da, w_conv, b_conv)


def _down_loss(a, w_down, xhat1, ln1_g, ln1_b, ln2_g, ln2_b, tgt, tm):
    T, D = xhat1.shape
    F = a.shape[1]
    tm = _tile(T, tm, 8)

    def body(a_ref, w_ref, xh_ref, g1_ref, b1_ref, g2_ref, b2_ref, t_ref, dr_ref, st_ref):
        @pl.when(pl.program_id(0) == 0)
        def _():
            st_ref[...] = jnp.zeros_like(st_ref)
        x1 = xh_ref[...] * g1_ref[...] + b1_ref[...]
        r = ALPHA * x1 + _dot(a_ref[...], w_ref[...])
        mean = jnp.mean(r, axis=1, keepdims=True)
        cen = r - mean
        rstd = lax.rsqrt(jnp.mean(cen * cen, axis=1, keepdims=True) + LN_EPS)
        xh = cen * rstd
        err = (xh * g2_ref[...] + b2_ref[...]) - t_ref[...]
        dy = err * (1.0 / D)
        dxh = dy * g2_ref[...]
        m1 = jnp.mean(dxh, axis=1, keepdims=True)
        m2 = jnp.mean(dxh * xh, axis=1, keepdims=True)
        dr_ref[...] = rstd * (dxh - m1 - xh * m2)
        st_ref[0:1, :] += jnp.sum(dy * xh, axis=0, keepdims=True)
        st_ref[1:2, :] += jnp.sum(dy, axis=0, keepdims=True)
        st_ref[2:3, :] += jnp.sum(err * err, axis=0, keepdims=True)

    rows = lambda w: pl.BlockSpec((tm, w), lambda i: (i, 0))
    vec = pl.BlockSpec((1, D), lambda i: (0, 0))
    return _pcall(body, name="down_loss", grid=(T // tm,),
                  in_specs=[rows(F), pl.BlockSpec((F, D), lambda i: (0, 0)), rows(D), vec, vec, vec, vec, rows(D)],
                  out_specs=[rows(D), pl.BlockSpec((8, D), lambda i: (0, 0))],
                  out_shape=[jax.ShapeDtypeStruct((T, D), f32), jax.ShapeDtypeStruct((8, D), f32)],
                  compiler_params=_params(("arbitrary",)))(a, w_down, xhat1, ln1_g, ln1_b, ln2_g, ln2_b, tgt)


def _local_step(x, tgt, w_in, b_in, ln1_g, ln1_b, b_conv, ln2_g, ln2_b, late_shards, late_weights, tq=512, tm=256):
    B, S, D = x.shape
    T = B * S
    E = (w_in.shape[1] - 2 * D) * HEAD_DIM // (6 * HEAD_DIM + 1)
    n_pairs = E // LANES
    NH = E // HEAD_DIM
    x2 = x.reshape(T, D)
    tgt2 = tgt.reshape(T, D)
    tq = _tile(S, tq, 8)

    c_f, c_g = 6 * E, 6 * E + NH
    w_qkv = w_in[:, :c_f]
    qscale = jnp.concatenate([jnp.full((E,), Q_SCALE, f32), jnp.ones((2 * E,), f32)] * 2)
    w_qkv_s = (w_qkv.astype(f32) * qscale).astype(bf16)
    b_qkv_s = b_in[:, :c_f] * qscale
    w_f = jnp.pad(w_in[:, c_f:c_g], ((0, 0), (0, LANES - NH)))
    b_f = jnp.pad(b_in[:, c_f:c_g], ((0, 0), (0, LANES - NH)))
    w_g = w_in[:, c_g:]
    b_g = b_in[:, c_g:]

    xb = x2.astype(bf16)
    qkv = _matmul(xb, w_qkv_s, bias=b_qkv_s, out_dtype=bf16, tm=1024, tn=512, name="proj_qkv")
    g = _matmul(xb, w_g, bias=b_g, tm=1024, tn=512, name="proj_gate")
    fl = _matmul(xb, w_f, bias=b_f, tm=1024, name="proj_forget")
    c = _cumlogf(fl, B, S)
    cT = c.reshape(B, S, LANES)[:, :, :NH].transpose(0, 2, 1).reshape(B * NH, 1, S)
    P = n_pairs
    o_sb, lt_sb, stacks = _sb_fwd(qkv, B, S, P, 0, P, 2 * P, tq, late_shards)
    wp_sb, wp_fx, w_out, w_up, w_conv, w_down = late_weights(stacks)
    F = w_down.shape[0]
    o_fx, o_fx32, lse_fx = _fox_fwd(qkv, c, cT, B, S, P, 3 * P, 4 * P, 5 * P, tq)
    xhat1, rstd1, x1b, merged = _mix_fwd(o_sb, o_fx, g, x2, wp_sb, wp_fx, w_out, ln1_g, ln1_b, tm)
    u = _matmul(x1b, w_up, tm=1024, tn=512, name="ffn_up")
    a = _glu_fwd(u, w_conv, b_conv, B, S)
    dr2, st2 = _down_loss(a, w_down, xhat1, ln1_g, ln1_b, ln2_g, ln2_b, tgt2, tm)

    grads = {}
    grads["ln2_g"], grads["ln2_b"] = st2[0:1], st2[1:2]
    sq_err = st2[2:3]
    da = _matmul(dr2, w_down, tb=True, tm=1024, tn=512, name="ffn_da")
    grads["w_down"] = _matmul(a, dr2, ta=True, tm=1408, tn=1024, tk=1024, name="grad_w_down")
    du_g, du_v, gwc, gbc = _glu_bwd(u, da, w_conv, b_conv, B, S)
    grads["w_conv"] = jnp.sum(gwc, axis=0)
    grads["b_conv"] = jnp.sum(gbc, axis=0)
    grads["w_up"] = jnp.concatenate(
        [_matmul(x1b, du_g, ta=True, tm=512, tn=2816, tk=1024, name="grad_w_up_gate"),
         _matmul(x1b, du_v, ta=True, tm=512, tn=2816, tk=1024, name="grad_w_up_val")], axis=1)
    dx1 = _matmul(du_g, w_up[:, :F], tb=True, tm=1024, tn=512, tk=F, name="ffn_dx_gate")
    dx1 = _matmul(du_v, w_up[:, F:], tb=True, addend=dx1, tm=1024, tn=512, tk=F, name="ffn_dx_val")
    dr1, st1 = _ln_bwd(dx1, dr2, ALPHA, xhat1, rstd1, ln1_g, tm)
    grads["ln1_g"], grads["ln1_b"] = st1[0:1], st1[1:2]

    dy_sb, dy_fx, dg_sb, dg_fx, do_sb, do_fx, gsum_sb, gsum_fx = _mix_bwd(dr1, o_sb, o_fx, g, wp_sb, wp_fx, w_out, tm)
    grads["w_out"] = _matmul(merged, dr1, ta=True, tm=512, tn=1024, tk=1024, name="grad_w_out")
    grads["w_proj_sb"] = _matmul(o_sb, dy_sb, ta=True, tm=512, tn=1024, tk=1024, name="grad_w_proj_sb")
    grads["w_proj_fox"] = _matmul(o_fx, dy_fx, ta=True, tm=512, tn=1024, tk=1024, name="grad_w_proj_fox")
    dq_s, dk_s, dv_s = _sb_bwd(qkv, do_sb, lt_sb, B, S, P, 0, P, 2 * P, tq)
    dq_f, dk_f, dv_f, dcT = _fox_bwd(qkv, c, cT, do_fx, o_fx32, lse_fx, B, S, P, 3 * P, 4 * P, 5 * P, tq)
    dc = jnp.pad(dcT.reshape(B, NH, S).transpose(0, 2, 1), ((0, 0), (0, 0), (0, LANES - NH))).reshape(T, LANES)
    dfl = _cumlogf_bwd(dc, fl, B, S)
    dqkv = jnp.concatenate([dq_s, dk_s, dv_s, dq_f, dk_f, dv_f], axis=1)
    dg = jnp.concatenate([dg_sb, dg_fx], axis=1)
    gw_qkv, gb_qkv = _matmul(x2, dqkv, ta=True, colsum=True, tm=512, tn=3072, tk=512, name="grad_w_qkv")
    gw_f, gb_f = _matmul(x2, dfl, ta=True, colsum=True, tm=512, tk=1024, name="grad_w_forget")
    gw_g = _matmul(x2, dg, ta=True, tm=512, tn=2048, tk=1024, name="grad_w_gate")
    grads["w_in"] = jnp.concatenate([gw_qkv, gw_f[:, :NH], gw_g], axis=1)
    grads["b_in"] = jnp.concatenate([gb_qkv, gb_f[:, :NH], gsum_sb[0:1], gsum_fx[0:1]], axis=1)
    dx = _matmul(dqkv, w_qkv, tb=True, addend=dr1, addend_scale=ALPHA, tm=512, tn=512, tk=c_f, name="dx_qkv")
    dx = _matmul(dfl, w_f, tb=True, addend=dx, tm=1024, tn=512, name="dx_forget")
    dx = _matmul(dg, w_g, tb=True, addend=dx, tm=1024, tn=512, tk=2 * D, name="dx_gate")
    return sq_err, dx.reshape(B, S, D), grads


_ANY = pl.BlockSpec(memory_space=pl.ANY)
_MESH = pl.DeviceIdType.MESH


def _pos():
    return lax.axis_index("x"), lax.axis_index("y"), lax.axis_index("c")


def _other_chips(x, y):
    return [(1 - x, y), (x, 1 - y), (1 - x, 1 - y)]


def _gather_shards(shards):
    n = len(shards)

    def body(*refs):
        start, finish = _gather_copies(refs[:n], refs[n:2 * n], *refs[2 * n:])
        start()
        finish()

    return _pcall(body, name="gather_weights", in_specs=[_ANY] * n, out_specs=[_ANY] * n,
                  out_shape=_gather_shapes(shards), scratch_shapes=_gather_sems(n),
                  compiler_params=pltpu.CompilerParams(has_side_effects=True))(*shards)


def _gather_shapes(shards):
    return [jax.ShapeDtypeStruct((4,) + s.shape, s.dtype) for s in shards]


def _gather_sems(n):
    return [pltpu.SemaphoreType.DMA((n, 3)), pltpu.SemaphoreType.DMA((n, 3)), pltpu.SemaphoreType.DMA((n,))]


def _gather_copies(srcs, dsts, send_sems, recv_sems, local_sems):
    n = len(srcs)
    x, y, c = _pos()
    k = 2 * x + y
    chips = _other_chips(x, y)

    def copy(a, j, chip, slot):
        return pltpu.make_async_remote_copy(src_ref=srcs[a], dst_ref=dsts[a].at[slot], send_sem=send_sems.at[a, j],
                                            recv_sem=recv_sems.at[a, j], device_id=(*chip, c), device_id_type=_MESH)

    def mine():
        return ([pltpu.make_async_copy(srcs[a], dsts[a].at[k], local_sems.at[a]) for a in range(n)],
                [copy(a, j, chip, k) for a in range(n) for j, chip in enumerate(chips)])

    def start():
        local, sends = mine()
        for cp in local + sends:
            cp.start()

    def finish():
        local, sends = mine()
        for a in range(n):
            for j, chip in enumerate(chips):
                copy(a, j, chip, 2 * chip[0] + chip[1]).wait_recv()
        for cp in sends:
            cp.wait_send()
        for cp in local:
            cp.wait()

    return start, finish


def _exchange_sibling_and_small(pieces, small):
    n = len(pieces)

    def body(*refs):
        p_refs, sm_ref = refs[:n], refs[n]
        got_refs, sg_ref = refs[n + 1:2 * n + 1], refs[2 * n + 1]
        big_send, big_recv, send_sems, recv_sems, local_sem = refs[2 * n + 2:]
        x, y, c = _pos()
        me = 4 * x + 2 * y + c
        big = []
        for a in range(n):
            h = pieces[a].shape[1] // 2
            src = p_refs[a].at[:, pl.ds(pl.multiple_of((1 - c) * h, 8), h), :]
            big.append(pltpu.make_async_remote_copy(src_ref=src, dst_ref=got_refs[a], send_sem=big_send.at[a],
                                                    recv_sem=big_recv.at[a], device_id=(x, y, 1 - c), device_id_type=_MESH))
        for cp in big:
            cp.start()
        local = pltpu.make_async_copy(sm_ref, sg_ref.at[me], local_sem)
        local.start()
        peers = []
        for r in range(1, 8):
            flip = lambda v, bit: 1 - v if bit else v
            peers.append((flip(x, r & 4), flip(y, r & 2), flip(c, r & 1)))

        def copy(j, slot):
            return pltpu.make_async_remote_copy(src_ref=sm_ref, dst_ref=sg_ref.at[slot], send_sem=send_sems.at[j],
                                                recv_sem=recv_sems.at[j], device_id=peers[j], device_id_type=_MESH)

        sends = [copy(j, me) for j in range(7)]
        for cp in sends:
            cp.start()
        for j, (px, py, pc) in enumerate(peers):
            copy(j, 4 * px + 2 * py + pc).wait_recv()
        for cp in sends:
            cp.wait_send()
        local.wait()
        for cp in big:
            cp.wait()

    halves = [jax.ShapeDtypeStruct((4, p.shape[1] // 2, p.shape[2]), p.dtype) for p in pieces]
    res = _pcall(body, name="exchange_sibling", in_specs=[_ANY] * (n + 1), out_specs=[_ANY] * (n + 1),
                 out_shape=halves + [jax.ShapeDtypeStruct((8,) + small.shape, small.dtype)],
                 scratch_shapes=[pltpu.SemaphoreType.DMA((n,)), pltpu.SemaphoreType.DMA((n,)), pltpu.SemaphoreType.DMA((7,)),
                                 pltpu.SemaphoreType.DMA((7,)), pltpu.SemaphoreType.DMA(())],
                 compiler_params=pltpu.CompilerParams(has_side_effects=True))(*pieces, small)
    return res[:n], res[n]


def _exchange_chips(pieces):
    n = len(pieces)

    def body(*refs):
        p_refs, got_refs = refs[:n], refs[n:2 * n]
        send_sems, recv_sems, local_sems = refs[2 * n:]
        x, y, c = _pos()
        k = 2 * x + y
        local = [pltpu.make_async_copy(p_refs[a].at[k], got_refs[a].at[k], local_sems.at[a]) for a in range(n)]
        for cp in local:
            cp.start()
        chips = _other_chips(x, y)

        def copy(a, j, chip, piece, slot):
            return pltpu.make_async_remote_copy(src_ref=p_refs[a].at[piece], dst_ref=got_refs[a].at[slot],
                                                send_sem=send_sems.at[a, j], recv_sem=recv_sems.at[a, j],
                                                device_id=(*chip, c), device_id_type=_MESH)

        sends = [copy(a, j, chip, 2 * chip[0] + chip[1], k) for a in range(n) for j, chip in enumerate(chips)]
        for cp in sends:
            cp.start()
        for a in range(n):
            for j, chip in enumerate(chips):
                copy(a, j, chip, k, 2 * chip[0] + chip[1]).wait_recv()
        for cp in sends:
            cp.wait_send()
        for cp in local:
            cp.wait()

    return _pcall(body, name="exchange_chips", in_specs=[_ANY] * n, out_specs=[_ANY] * n,
                  out_shape=[jax.ShapeDtypeStruct(p.shape, p.dtype) for p in pieces],
                  scratch_shapes=[pltpu.SemaphoreType.DMA((n, 3)), pltpu.SemaphoreType.DMA((n, 3)), pltpu.SemaphoreType.DMA((n,))],
                  compiler_params=pltpu.CompilerParams(has_side_effects=True))(*pieces)


def _share_halves(shards):
    n = len(shards)

    def body(*refs):
        full_refs = refs[n:2 * n]
        send_sems, recv_sems = refs[2 * n:]
        x, y, c = _pos()

        def copy(a, half):
            h = shards[a].shape[0] // 2
            rows = full_refs[a].at[pl.ds(pl.multiple_of(half * h, 8), h), :]
            return pltpu.make_async_remote_copy(src_ref=rows, dst_ref=rows, send_sem=send_sems.at[a],
                                                recv_sem=recv_sems.at[a], device_id=(x, y, 1 - c), device_id_type=_MESH)

        sends = [copy(a, c) for a in range(n)]
        for cp in sends:
            cp.start()
        for a in range(n):
            copy(a, 1 - c).wait_recv()
        for cp in sends:
            cp.wait_send()

    return _pcall(body, name="share_halves", in_specs=[_ANY] * n, out_specs=[_ANY] * n,
                  out_shape=[jax.ShapeDtypeStruct(s.shape, s.dtype) for s in shards],
                  input_output_aliases={a: a for a in range(n)},
                  scratch_shapes=[pltpu.SemaphoreType.DMA((n,)), pltpu.SemaphoreType.DMA((n,))],
                  compiler_params=pltpu.CompilerParams(has_side_effects=True))(*shards)


def _add_own_half(piece, got, core, name):
    _, r, cols = piece.shape
    h = r // 2

    def body(c_ref, a_ref, b_ref, o_ref, o16_ref):
        s = a_ref[0] + b_ref[...]
        o_ref[...] = s
        o16_ref[...] = s.astype(bf16)

    out = pl.BlockSpec((1, h, cols), lambda k, c: (k, 0, 0))
    grid_spec = pltpu.PrefetchScalarGridSpec(
        num_scalar_prefetch=1, grid=(4,),
        in_specs=[pl.BlockSpec((1, 1, h, cols), lambda k, c: (k, c[0], 0, 0)), out], out_specs=[out, out])
    return _pcall(body, name=name, grid_spec=grid_spec,
                  out_shape=[jax.ShapeDtypeStruct((4, h, cols), f32), jax.ShapeDtypeStruct((4, h, cols), bf16)],
                  compiler_params=_params(("arbitrary",)))(core, piece.reshape(4, 2, h, cols), got)


def _sum_chips(own, got, where, name):
    _, h, cols = own.shape
    t = _tile(h, 128, 16)
    nt = h // t

    def body(w_ref, own_ref, got_ref, o_ref):
        acc = None
        for k in range(4):
            term = jnp.where(w_ref[0] == k, own_ref[0], got_ref[k].astype(f32))
            acc = term if acc is None else acc + term
        o_ref[...] = acc

    grid_spec = pltpu.PrefetchScalarGridSpec(
        num_scalar_prefetch=1, grid=(nt,),
        in_specs=[pl.BlockSpec((1, t, cols), lambda i, w: (w[0], i, 0)), pl.BlockSpec((4, t, cols), lambda i, w: (0, i, 0))],
        out_specs=pl.BlockSpec((t, cols), lambda i, w: (w[1] * nt + i, 0)))
    return _pcall(body, name=name, grid_spec=grid_spec, out_shape=jax.ShapeDtypeStruct((2 * h, cols), f32),
                  compiler_params=_params(("arbitrary",)))(where, own, got)


def _sum_slots(stack, name):
    k, n, cols = stack.shape
    t = _tile(n, 128, 8)

    def body(s_ref, o_ref):
        acc = s_ref[0]
        for i in range(1, k):
            acc = acc + s_ref[i]
        o_ref[...] = acc

    return _pcall(body, name=name, grid=(n // t,), in_specs=[pl.BlockSpec((k, t, cols), lambda i: (0, i, 0))],
                  out_specs=pl.BlockSpec((t, cols), lambda i: (i, 0)), out_shape=jax.ShapeDtypeStruct((n, cols), f32),
                  compiler_params=_params(("arbitrary",)))(stack)


def _adamw(w, g, m, v, name):
    n, cols = w.shape
    t = _tile(n, 128, 8)
    c1 = 1.0 - ADAM_B1 ** ADAM_STEP
    c2 = 1.0 - ADAM_B2 ** ADAM_STEP

    def body(w_ref, g_ref, m_ref, v_ref, d_ref, nm_ref, nv_ref, g_out_ref):
        g = g_ref[...]
        nm = ADAM_B1 * m_ref[...] + (1.0 - ADAM_B1) * g
        nv = ADAM_B2 * v_ref[...] + (1.0 - ADAM_B2) * (g * g)
        d_ref[...] = -ADAM_LR * ((nm / c1) / (jnp.sqrt(nv / c2) + ADAM_EPS) + ADAM_WD * w_ref[...])
        nm_ref[...] = nm
        nv_ref[...] = nv
        g_out_ref[...] = g

    spec = pl.BlockSpec((t, cols), lambda i: (i, 0))
    shp = jax.ShapeDtypeStruct((n, cols), f32)
    return _pcall(body, name=name, grid=(n // t,), in_specs=[spec] * 4, out_specs=[spec] * 4, out_shape=[shp] * 4,
                  compiler_params=_params(("arbitrary",)))(w, g, m, v)


_MATS = (("w_in", 1), ("w_proj_sb", 1), ("w_proj_fox", 1), ("w_out", 0), ("w_up", 1), ("w_down", 0))
_SMALL = ("b_in", "ln1_g", "ln1_b", "b_conv", "ln2_g", "ln2_b")


def _pad_lanes(v):
    n = v.shape[-1]
    return jnp.pad(v, ((0, 0), (0, (-n) % LANES)))


def _pack_rows(vectors):
    flat = jnp.concatenate([_pad_lanes(v.reshape(1, -1)) for v in vectors], axis=1).reshape(-1, LANES)
    return jnp.pad(flat, ((0, (-flat.shape[0]) % 8), (0, 0)))


def _unpack_rows(packed, sizes):
    out, r = [], 0
    for n in sizes:
        rows = -(-n // LANES)
        out.append(packed[r:r + rows].reshape(1, rows * LANES)[:, :n])
        r += rows
    return out


def _unstack(stack, axis):
    if axis == 0:
        return stack.reshape(-1, stack.shape[2])
    return jnp.concatenate([stack[k] for k in range(4)], axis=1)


def _pieces(g, axis):
    if axis == 0:
        return g.reshape(4, g.shape[0] // 4, g.shape[1])
    cols = g.shape[1] // 4
    return jnp.stack([g[:, k * cols:(k + 1) * cols] for k in range(4)])


def kernel(x, w_in, b_in, w_proj_sb, w_proj_fox, w_out, ln1_g, ln1_b, w_up, w_conv, b_conv, w_down, ln2_g, ln2_b, loss_target, m_w_in, m_b_in, m_w_proj_sb, m_w_proj_fox, m_w_out, m_ln1_g, m_ln1_b, m_w_up, m_w_conv, m_b_conv, m_w_down, m_ln2_g, m_ln2_b, v_w_in, v_b_in, v_w_proj_sb, v_w_proj_fox, v_w_out, v_ln1_g, v_ln1_b, v_w_up, v_w_conv, v_b_conv, v_w_down, v_ln2_g, v_ln2_b):
    w = dict(w_in=w_in, b_in=b_in, w_proj_sb=w_proj_sb, w_proj_fox=w_proj_fox, w_out=w_out, ln1_g=ln1_g, ln1_b=ln1_b,
             w_up=w_up, w_conv=w_conv, b_conv=b_conv, w_down=w_down, ln2_g=ln2_g, ln2_b=ln2_b)
    m = dict(w_in=m_w_in, b_in=m_b_in, w_proj_sb=m_w_proj_sb, w_proj_fox=m_w_proj_fox, w_out=m_w_out, ln1_g=m_ln1_g,
             ln1_b=m_ln1_b, w_up=m_w_up, w_conv=m_w_conv, b_conv=m_b_conv, w_down=m_w_down, ln2_g=m_ln2_g, ln2_b=m_ln2_b)
    v = dict(w_in=v_w_in, b_in=v_b_in, w_proj_sb=v_w_proj_sb, w_proj_fox=v_w_proj_fox, w_out=v_w_out, ln1_g=v_ln1_g,
             ln1_b=v_ln1_b, w_up=v_w_up, w_conv=v_w_conv, b_conv=v_b_conv, w_down=v_w_down, ln2_g=v_ln2_g, ln2_b=v_ln2_b)
    order = ["w_in", "b_in", "w_proj_sb", "w_proj_fox", "w_out", "ln1_g", "ln1_b", "w_up", "w_conv", "b_conv", "w_down",
             "ln2_g", "ln2_b"]
    x_idx, y_idx, c_idx = _pos()
    chip = 2 * x_idx + y_idx
    D = x.shape[-1]
    core = c_idx.astype(jnp.int32).reshape(1)

    w_in_full = _unstack(_gather_shards([w["w_in"][0].astype(bf16)])[0], 1)
    late = (("w_proj_sb", 1), ("w_proj_fox", 1), ("w_out", 0), ("w_up", 1), ("w_conv", 1), ("w_down", 0))
    late_shards = [w[n][0] if n == "w_conv" else w[n][0].astype(bf16) for n, _ in late]
    late_weights = lambda stacks: [_unstack(s, axis) for (_, axis), s in zip(late, stacks)]

    sq_err, grad_x, grads = _local_step(x, loss_target, w_in_full, w["b_in"], w["ln1_g"], w["ln1_b"], w["b_conv"],
                                        w["ln2_g"], w["ln2_b"], late_shards, late_weights)
    loss_part = (0.5 / D) * jnp.sum(sq_err)

    pieces = [_pieces(grads[n], axis) for n, axis in _MATS]
    small_names = list(_SMALL) + ["w_conv"]
    small = _pack_rows([jnp.full((1, 1), loss_part, f32)] + [grads[n] for n in small_names])
    got, small_all = _exchange_sibling_and_small(pieces, small)
    chip_sums = [_add_own_half(p, g_, core, "add_sibling_" + n) for (n, _), p, g_ in zip(_MATS, pieces, got)]
    from_chips = _exchange_chips([s16 for _, s16 in chip_sums])
    where = jnp.stack([chip, c_idx]).astype(jnp.int32)
    halves = [_sum_chips(s32, r16, where, "sum_chips_" + n) for (n, _), (s32, _), r16 in zip(_MATS, chip_sums, from_chips)]
    g_shards = _share_halves(halves)
    small_sum = _sum_slots(small_all, "sum_small")

    out = {"grad": {}, "delta": {}, "m": {}, "v": {}}
    for (n, _), g_ in zip(_MATS, g_shards):
        d_, m_, v_, g_out = _adamw(w[n][0], g_, m[n][0], v[n][0], "adamw_" + n)
        for key, t in (("grad", g_out), ("delta", d_), ("m", m_), ("v", v_)):
            out[key][n] = t.reshape(w[n].shape)
    sizes = [1] + [int(grads[n].size) for n in small_names]
    sm = _unpack_rows(small_sum, sizes)
    loss = sm[0][0, 0]
    g_small = dict(zip(small_names, sm[1:]))
    F4 = w["w_conv"].shape[-1]
    g_small["w_conv"] = lax.dynamic_slice_in_dim(g_small["w_conv"].reshape(3, -1), chip * F4, F4, axis=1)
    pack_s = lambda d: _pack_rows([d[n].reshape(1, -1) for n in small_names])
    gs_packed = _pack_rows([g_small[n].reshape(1, -1) for n in small_names])
    s_delta, s_m, s_v, _ = _adamw(pack_s(w), gs_packed, pack_s(m), pack_s(v), "adamw_small")
    s_sizes = [int(w[n].size) for n in small_names]

    for key, packed_s in (("grad", gs_packed), ("delta", s_delta), ("m", s_m), ("v", s_v)):
        for n, t in zip(small_names, _unpack_rows(packed_s, s_sizes)):
            out[key][n] = t.reshape(w[n].shape)
    return (loss, grad_x, *[out["grad"][n] for n in order], *[out["delta"][n] for n in order],
            *[out["m"][n] for n in order], *[out["v"][n] for n in order])
```

```python
import functools
import math

import jax
import jax.numpy as jnp
from jax import lax
from jax.experimental import pallas as pl
from jax.experimental.pallas import tpu as pltpu

f32, bf16 = jnp.float32, jnp.bfloat16

HEAD_DIM = 64
LANES = 128
LN_EPS = 1e-5
ALPHA = 2.0 ** 0.25
Q_SCALE = HEAD_DIM ** -0.5
ADAM_LR, ADAM_B1, ADAM_B2, ADAM_EPS, ADAM_WD, ADAM_STEP = 0.001, 0.9, 0.999, 1e-08, 0.01, 10
VMEM_LIMIT = 56 * 1024 * 1024
NEG = -1e30

_pcall = pl.pallas_call
_NT = (((1,), (1,)), ((), ()))
_TN = (((0,), (0,)), ((), ()))


def _params(sem=None):
    return pltpu.CompilerParams(dimension_semantics=sem, vmem_limit_bytes=VMEM_LIMIT)


def _tile(dim, target, unit=LANES):
    if dim <= target:
        return dim
    t = (target // unit) * unit
    while t > unit and dim % t:
        t -= unit
    assert dim % t == 0, (dim, target)
    return t


def _dot(a, b, dn=None):
    if dn is None:
        return jnp.dot(a, b, preferred_element_type=f32)
    return lax.dot_general(a, b, dn, preferred_element_type=f32)


def _split_dot(x, tri):
    hi = x.astype(bf16)
    lo = (x - hi.astype(f32)).astype(bf16)
    return _dot(hi, tri) + _dot(lo, tri)


SCAN_BLOCK = 256


def _scan_cols(x, tri, reverse):
    cb = tri.shape[0]
    nb = x.shape[1] // cb
    blocks = [x[:, b * cb:(b + 1) * cb] for b in range(nb)]
    outs, run = [None] * nb, None
    for b in (reversed(range(nb)) if reverse else range(nb)):
        o = _split_dot(blocks[b], tri)
        s = jnp.sum(blocks[b], axis=1, keepdims=True)
        outs[b] = o if run is None else o + run
        run = s if run is None else run + s
    return (outs[0] if nb == 1 else jnp.concatenate(outs, axis=1)), run


def _tri(cb, rel):
    row = lax.broadcasted_iota(jnp.int32, (cb, cb), 0)
    col = lax.broadcasted_iota(jnp.int32, (cb, cb), 1)
    return rel(row, col).astype(bf16)


def _matmul(a, b, *, name, ta=False, tb=False, bias=None, addend=None, addend_scale=1.0, colsum=False,
            out_dtype=f32, tm=512, tn=512, tk=1024):
    M, K = (a.shape[1], a.shape[0]) if ta else a.shape
    N = b.shape[0] if tb else b.shape[1]
    assert K == (b.shape[1] if tb else b.shape[0])
    assert not (colsum and tb)
    tm, tn, tk = _tile(M, tm), _tile(N, tn), _tile(K, tk)
    nk = K // tk
    n_in = 2 + (bias is not None) + (addend is not None)

    def body(*refs):
        a_ref, b_ref = refs[0], refs[1]
        bias_ref = refs[2] if bias is not None else None
        add_ref = refs[n_in - 1] if addend is not None else None
        o_ref = refs[n_in]
        cs_ref = refs[n_in + 1] if colsum else None
        acc = refs[-2] if colsum else refs[-1]
        cs_acc = refs[-1] if colsum else None
        k = pl.program_id(2)

        @pl.when(k == 0)
        def _():
            acc[...] = jnp.zeros_like(acc)
            if colsum:
                cs_acc[...] = jnp.zeros_like(cs_acc)

        dn = (((0 if ta else 1,), (1 if tb else 0,)), ((), ()))
        acc[...] += lax.dot_general(a_ref[...].astype(bf16), b_ref[...].astype(bf16), dn, preferred_element_type=f32)
        if colsum:
            cs_acc[...] += jnp.sum(b_ref[...].astype(f32), axis=0, keepdims=True)

        @pl.when(k == nk - 1)
        def _():
            r = acc[...]
            if bias is not None:
                r = r + bias_ref[...]
            if addend is not None:
                r = r + addend_scale * add_ref[...].astype(f32)
            o_ref[...] = r.astype(out_dtype)
            if colsum:
                cs_ref[0] = cs_acc[...]

    a_spec = pl.BlockSpec((tk, tm), lambda i, j, k: (k, i)) if ta else pl.BlockSpec((tm, tk), lambda i, j, k: (i, k))
    b_spec = pl.BlockSpec((tn, tk), lambda i, j, k: (j, k)) if tb else pl.BlockSpec((tk, tn), lambda i, j, k: (k, j))
    in_specs, args = [a_spec, b_spec], [a, b]
    if bias is not None:
        in_specs.append(pl.BlockSpec((1, tn), lambda i, j, k: (0, j)))
        args.append(bias.reshape(1, N).astype(f32))
    if addend is not None:
        in_specs.append(pl.BlockSpec((tm, tn), lambda i, j, k: (i, j)))
        args.append(addend)
    out_shape = [jax.ShapeDtypeStruct((M, N), out_dtype)]
    out_specs = [pl.BlockSpec((tm, tn), lambda i, j, k: (i, j))]
    scratch = [pltpu.VMEM((tm, tn), f32)]
    if colsum:
        out_shape.append(jax.ShapeDtypeStruct((M // tm, 1, N), f32))
        out_specs.append(pl.BlockSpec((1, 1, tn), lambda i, j, k: (i, 0, j)))
        scratch.append(pltpu.VMEM((1, tn), f32))
    res = _pcall(body, name=name, grid=(M // tm, N // tn, nk), in_specs=in_specs, out_specs=out_specs,
                 out_shape=out_shape, scratch_shapes=scratch,
                 compiler_params=_params(("arbitrary", "arbitrary", "arbitrary")))(*args)
    return (res[0], res[1][0]) if colsum else res[0]


def _cumlogf(fl, B, S):
    t = _tile(S, 256, 8)

    def body(fl_ref, c_ref, carry):
        @pl.when(pl.program_id(1) == 0)
        def _():
            carry[...] = jnp.zeros_like(carry)
        z = fl_ref[...]
        ls = jnp.minimum(z, 0.0) - jnp.log(1.0 + jnp.exp(-jnp.abs(z)))
        row = lax.broadcasted_iota(jnp.int32, (t, t), 0)
        col = lax.broadcasted_iota(jnp.int32, (t, t), 1)
        lower = (col <= row).astype(f32)
        c = jnp.dot(lower, ls, precision=lax.Precision.HIGHEST, preferred_element_type=f32) + carry[...]
        c_ref[...] = c
        carry[...] = c[t - 1:t, :]

    return _pcall(body, name="cumlogf", grid=(B, S // t),
                  in_specs=[pl.BlockSpec((t, LANES), lambda b, i: (b * (S // t) + i, 0))],
                  out_specs=pl.BlockSpec((t, LANES), lambda b, i: (b * (S // t) + i, 0)),
                  out_shape=jax.ShapeDtypeStruct(fl.shape, f32), scratch_shapes=[pltpu.VMEM((1, LANES), f32)],
                  compiler_params=_params(("arbitrary", "arbitrary")))(fl)


def _cumlogf_bwd(dc, fl, B, S):
    t = _tile(S, 256, 8)
    n = S // t

    def body(dc_ref, fl_ref, o_ref, carry):
        @pl.when(pl.program_id(1) == 0)
        def _():
            carry[...] = jnp.zeros_like(carry)
        row = lax.broadcasted_iota(jnp.int32, (t, t), 0)
        col = lax.broadcasted_iota(jnp.int32, (t, t), 1)
        upper = (col >= row).astype(f32)
        r = jnp.dot(upper, dc_ref[...], precision=lax.Precision.HIGHEST, preferred_element_type=f32) + carry[...]
        carry[...] = r[0:1, :]
        z = fl_ref[...]
        o_ref[...] = r / (1.0 + jnp.exp(z))

    spec = pl.BlockSpec((t, LANES), lambda b, i: (b * n + n - 1 - i, 0))
    return _pcall(body, name="cumlogf_bwd", grid=(B, n), in_specs=[spec, spec], out_specs=spec,
                  out_shape=jax.ShapeDtypeStruct(fl.shape, f32), scratch_shapes=[pltpu.VMEM((1, LANES), f32)],
                  compiler_params=_params(("arbitrary", "arbitrary")))(dc, fl)


def _head_masks():
    lane = lax.broadcasted_iota(jnp.int32, (1, LANES), 1)
    return lane < HEAD_DIM


def _by_head(m0, t):
    z = jnp.zeros_like(t)
    return [jnp.where(m0, t, z), jnp.where(m0, z, t)]


def _sb_terms(z):
    relu = jnp.maximum(z, 0.0)
    sp = jnp.log(1.0 + jnp.exp(-jnp.abs(z)))
    return (z - relu) - sp, -relu - sp


STRIP_ROWS = 32


def _strip_rows(tq):
    return STRIP_ROWS if tq % STRIP_ROWS == 0 else tq


def _strict(r, rs, tq):
    row = lax.broadcasted_iota(jnp.int32, (rs, tq), 0) + r
    col = lax.broadcasted_iota(jnp.int32, (rs, tq), 1)
    return col < row


def _score_scratch(tq, n_f32, n_bf16, n_sums):
    return ([pltpu.VMEM((tq, tq), f32)] * (2 * n_f32) + [pltpu.VMEM((tq, tq), bf16)] * (2 * n_bf16)
            + [pltpu.VMEM((tq, LANES), f32)] * (2 * n_sums))


def _by_pairs(refs):
    return [refs[i:i + 2] for i in range(0, len(refs), 2)]


def _sb_fwd(qkv, B, S, n_pairs, qcol, kcol, vcol, tq, shards=()):
    nq = S // tq
    T = B * S
    n = len(shards)

    cb = min(tq, SCAN_BLOCK)
    nb = tq // cb
    rs = _strip_rows(tq)

    def body(q_ref, k_ref, v_ref, *rest):
        o_ref, lt_ref = rest[n], rest[n + 1]
        z_s, suf_s, hi_s, lo_s, w_s, sum_s = _by_pairs(rest[len(rest) - 12:])
        i = pl.program_id(2)
        if n:
            start, finish = _gather_copies(rest[:n], rest[n + 2:2 * n + 2], *rest[2 * n + 2:2 * n + 5])
            step = (pl.program_id(0) * n_pairs + pl.program_id(1)) * nq + i
            pl.when(step == 0)(start)
        m0 = _head_masks()
        qh = _by_head(m0, q_ref[...])
        later = _tri(cb, lambda j, s: j > s)
        lane = lax.broadcasted_iota(jnp.int32, (1, LANES), 1)

        def tile(s0, R, acc, diag):
            kb = k_ref[pl.ds(s0, tq), :]
            vh = _by_head(m0, v_ref[pl.ds(s0, tq), :])
            R = list(R)
            for h in range(2):
                z_s[h][...] = _dot(qh[h], kb, _NT)
            for h in range(2):
                for r in range(0, tq, rs):
                    l1m = _sb_terms(z_s[h][r:r + rs, :])[1]
                    if diag:
                        l1m = jnp.where(_strict(r, rs, tq), l1m, 0.0)
                    hi = l1m.astype(bf16)
                    hi_s[h][r:r + rs, :] = hi
                    lo_s[h][r:r + rs, :] = (l1m - hi.astype(f32)).astype(bf16)
                    sums = jnp.zeros((rs, LANES), f32)
                    for b in range(nb):
                        sums = jnp.where(lane == b, jnp.sum(l1m[:, b * cb:(b + 1) * cb], axis=1, keepdims=True), sums)
                    sum_s[h][r:r + rs, :] = sums
            for h in range(2):
                for b in range(nb):
                    blk = slice(b * cb, (b + 1) * cb)
                    suf_s[h][:, blk] = _dot(hi_s[h][:, blk], later) + _dot(lo_s[h][:, blk], later)
            for h in range(2):
                for r in range(0, tq, rs):
                    sums = sum_s[h][r:r + rs, :]
                    after = R[h][r:r + rs]
                    for b in reversed(range(nb)):
                        blk = slice(b * cb, (b + 1) * cb)
                        l1m = hi_s[h][r:r + rs, blk].astype(f32) + lo_s[h][r:r + rs, blk].astype(f32)
                        w = jnp.exp((z_s[h][r:r + rs, blk] + l1m) + (suf_s[h][r:r + rs, blk] + after))
                        if diag:
                            w = jnp.where(_strict(r, rs, tq)[:, blk], w, 0.0)
                        w_s[h][r:r + rs, blk] = w.astype(bf16)
                        after = after + sums[:, b:b + 1]
            for h in range(2):
                acc = acc + _dot(w_s[h][...], vh[h])
                R[h] = R[h] + jnp.sum(sum_s[h][...], axis=1, keepdims=True)
            return R, acc

        zero = jnp.zeros((tq, 1), f32)
        R, acc = tile(pl.multiple_of(i * tq, tq), [zero, zero], jnp.zeros((tq, LANES), f32), True)

        def loop(n, carry):
            s0 = pl.multiple_of((i - 1 - n) * tq, tq)
            R, acc = tile(s0, carry[:2], carry[2], False)
            return R[0], R[1], acc

        R0, R1, acc = lax.fori_loop(0, i, loop, (R[0], R[1], acc))
        o_ref[...] = acc.astype(bf16)
        lt_ref[...] = jnp.where(m0, R0, R1)
        if n:
            pl.when(step == B * n_pairs * nq - 1)(finish)

    qs = lambda c: pl.BlockSpec((tq, LANES), lambda b, p, i: (b * nq + i, c + p))
    ks = lambda c: pl.BlockSpec((S, LANES), lambda b, p, i: (b, c + p))
    os_ = pl.BlockSpec((tq, LANES), lambda b, p, i: (b * nq + i, p))
    res = _pcall(body, name="sb_fwd", grid=(B, n_pairs, nq), in_specs=[qs(qcol), ks(kcol), ks(vcol)] + [_ANY] * n,
                 out_specs=[os_, os_] + [_ANY] * n,
                 out_shape=[jax.ShapeDtypeStruct((T, n_pairs * LANES), bf16), jax.ShapeDtypeStruct((T, n_pairs * LANES), f32)]
                 + _gather_shapes(shards), scratch_shapes=(_gather_sems(n) if n else []) + _score_scratch(tq, 2, 3, 1),
                 compiler_params=_params(("arbitrary", "arbitrary", "arbitrary")))(qkv, qkv, qkv, *shards)
    return res[0], res[1], res[2:]


def _sb_bwd(qkv, do, lt, B, S, n_pairs, qcol, kcol, vcol, tq, hosted=None):
    nq = S // tq
    T = B * S

    def body(*refs):
        i = pl.program_id(2)
        step = (pl.program_id(0) * n_pairs + pl.program_id(1)) * nq + i
        (q_ref, k_ref, v_ref, do_ref, lt_ref), (dq_ref, dk_ref, dv_ref), finish = _host(
            hosted, refs, 5, 3, step, B * n_pairs * nq - 1)

        @pl.when(i == 0)
        def _():
            dk_ref[...] = jnp.zeros_like(dk_ref)
            dv_ref[...] = jnp.zeros_like(dv_ref)

        m0 = _head_masks()
        qh = _by_head(m0, q_ref[...])
        doh = _by_head(m0, do_ref[...])
        lt = lt_ref[...]
        ltot = [lt[:, 0:1], lt[:, HEAD_DIM:HEAD_DIM + 1]]
        row = lax.broadcasted_iota(jnp.int32, (tq, tq), 0)
        col = lax.broadcasted_iota(jnp.int32, (tq, tq), 1)
        strict = col < row
        upto = _tri(min(tq, SCAN_BLOCK), lambda j, s: j <= s)
        before = _tri(min(tq, SCAN_BLOCK), lambda j, s: j < s)

        def tile(s0, CL, CP, dq, diag):
            kb = k_ref[pl.ds(s0, tq), :]
            vb = v_ref[pl.ds(s0, tq), :]
            kh = _by_head(m0, kb)
            CL, CP = list(CL), list(CP)
            dk = jnp.zeros((tq, LANES), f32)
            dv = jnp.zeros((tq, LANES), f32)
            for h in range(2):
                z = _dot(qh[h], kb, _NT)
                lb, l1m = _sb_terms(z)
                if diag:
                    l1m = jnp.where(strict, l1m, 0.0)
                pre, l_total = _scan_cols(l1m, upto, False)
                w = jnp.exp(lb + ((ltot[h] - CL[h]) - pre))
                if diag:
                    w = jnp.where(strict, w, 0.0)
                g = _dot(doh[h], vb, _NT) * w
                p, g_total = _scan_cols(g, before, False)
                dz = g - jnp.exp(lb) * (g + (p + CP[h]))
                if diag:
                    dz = jnp.where(strict, dz, 0.0)
                dzb = dz.astype(bf16)
                dq = dq + _dot(dzb, kh[h])
                dk = dk + _dot(dzb, qh[h], _TN)
                dv = dv + _dot(w.astype(bf16), doh[h], _TN)
                CL[h] = CL[h] + l_total
                CP[h] = CP[h] + g_total
            dk_ref[pl.ds(s0, tq), :] += dk
            dv_ref[pl.ds(s0, tq), :] += dv
            return CL, CP, dq

        zero = jnp.zeros((tq, 1), f32)

        def loop(n, carry):
            CL, CP, dq = tile(pl.multiple_of(n * tq, tq), carry[0:2], carry[2:4], carry[4], False)
            return CL[0], CL[1], CP[0], CP[1], dq

        c = lax.fori_loop(0, i, loop, (zero, zero, zero, zero, jnp.zeros((tq, LANES), f32)))
        _, _, dq = tile(pl.multiple_of(i * tq, tq), c[0:2], c[2:4], c[4], True)
        dq_ref[...] = dq * Q_SCALE
        finish()

    qs = lambda c: pl.BlockSpec((tq, LANES), lambda b, p, i: (b * nq + i, c + p))
    ks = lambda c: pl.BlockSpec((S, LANES), lambda b, p, i: (b, c + p))
    ts = pl.BlockSpec((tq, LANES), lambda b, p, i: (b * nq + i, p))
    fs = pl.BlockSpec((S, LANES), lambda b, p, i: (b, p))
    shp = jax.ShapeDtypeStruct((T, n_pairs * LANES), f32)
    return _host_call(body, "sb_bwd", (B, n_pairs, nq), [qs(qcol), ks(kcol), ks(vcol), ts, ts], [qkv, qkv, qkv, do, lt],
                      [ts, fs, fs], [shp, shp, shp], hosted)


def _fox_fwd(qkv, c, cT, B, S, n_pairs, qcol, kcol, vcol, tq):
    nq = S // tq
    T = B * S

    def body(q_ref, k_ref, v_ref, cq_ref, ck_ref, o_ref, o32_ref, lse_ref):
        p_idx = pl.program_id(1)
        i = pl.program_id(2)
        m0 = _head_masks()
        lane = lax.broadcasted_iota(jnp.int32, (1, LANES), 1)
        qh = _by_head(m0, q_ref[...])
        cq_all = cq_ref[...]
        cq = [jnp.sum(jnp.where(lane == 2 * p_idx + h, cq_all, 0.0), axis=1, keepdims=True) for h in range(2)]
        row = lax.broadcasted_iota(jnp.int32, (tq, tq), 0)
        col = lax.broadcasted_iota(jnp.int32, (tq, tq), 1)
        causal = col <= row

        def tile(s0, m, l, acc, diag):
            kb = k_ref[pl.ds(s0, tq), :]
            vh = _by_head(m0, v_ref[pl.ds(s0, tq), :])
            m, l = list(m), list(l)
            scale, add = [], []
            for h in range(2):
                z = _dot(qh[h], kb, _NT) + (cq[h] - ck_ref[h, :, pl.ds(s0, tq)])
                if diag:
                    z = jnp.where(causal, z, NEG)
                m_new = jnp.maximum(m[h], jnp.max(z, axis=1, keepdims=True))
                p = jnp.exp(z - m_new)
                a = jnp.exp(m[h] - m_new)
                l[h] = a * l[h] + jnp.sum(p, axis=1, keepdims=True)
                m[h] = m_new
                scale.append(a)
                add.append(_dot(p.astype(bf16), vh[h]))
            acc = acc * jnp.where(m0, scale[0], scale[1]) + add[0] + add[1]
            return m, l, acc

        neg = jnp.full((tq, 1), NEG, f32)
        zero = jnp.zeros((tq, 1), f32)
        m, l, acc = tile(pl.multiple_of(i * tq, tq), [neg, neg], [zero, zero], jnp.zeros((tq, LANES), f32), True)

        def loop(n, carry):
            m, l, acc = tile(pl.multiple_of(n * tq, tq), carry[0:2], carry[2:4], carry[4], False)
            return m[0], m[1], l[0], l[1], acc

        m0_, m1_, l0, l1, acc = lax.fori_loop(0, i, loop, (m[0], m[1], l[0], l[1], acc))
        o = acc * jnp.where(m0, 1.0 / l0, 1.0 / l1)
        o_ref[...] = o.astype(bf16)
        o32_ref[...] = o
        lse_ref[...] = jnp.where(m0, m0_ + jnp.log(l0), m1_ + jnp.log(l1))

    qs = lambda cc: pl.BlockSpec((tq, LANES), lambda b, p, i: (b * nq + i, cc + p))
    ks = lambda cc: pl.BlockSpec((S, LANES), lambda b, p, i: (b, cc + p))
    cqs = pl.BlockSpec((tq, LANES), lambda b, p, i: (b * nq + i, 0))
    cks = pl.BlockSpec((2, 1, S), lambda b, p, i: (b * n_pairs + p, 0, 0))
    os_ = pl.BlockSpec((tq, LANES), lambda b, p, i: (b * nq + i, p))
    shp = jax.ShapeDtypeStruct((T, n_pairs * LANES), f32)
    return _pcall(body, name="fox_fwd", grid=(B, n_pairs, nq), in_specs=[qs(qcol), ks(kcol), ks(vcol), cqs, cks],
                  out_specs=[os_, os_, os_], out_shape=[jax.ShapeDtypeStruct((T, n_pairs * LANES), bf16), shp, shp],
                  compiler_params=_params(("arbitrary", "arbitrary", "arbitrary")))(qkv, qkv, qkv, c, cT)


def _fox_bwd(qkv, c, cT, do, o, lse, B, S, n_pairs, qcol, kcol, vcol, tq, hosted=None):
    nq = S // tq
    T = B * S

    def body(*refs):
        p_idx = pl.program_id(1)
        i = pl.program_id(2)
        step = (pl.program_id(0) * n_pairs + p_idx) * nq + i
        (q_ref, k_ref, v_ref, cq_ref, ck_ref, do_ref, o_ref, lse_ref), (dq_ref, dk_ref, dv_ref, dc_ref), finish = _host(
            hosted, refs, 8, 4, step, B * n_pairs * nq - 1)

        @pl.when(i == 0)
        def _():
            dk_ref[...] = jnp.zeros_like(dk_ref)
            dv_ref[...] = jnp.zeros_like(dv_ref)
            dc_ref[...] = jnp.zeros_like(dc_ref)

        m0 = _head_masks()
        lane = lax.broadcasted_iota(jnp.int32, (1, LANES), 1)
        qh = _by_head(m0, q_ref[...])
        do2 = do_ref[...]
        doh = _by_head(m0, do2)
        prod = do2.astype(f32) * o_ref[...].astype(f32)
        delta = [jnp.sum(p, axis=1, keepdims=True) for p in _by_head(m0, prod)]
        ls = lse_ref[...]
        lse = [ls[:, 0:1], ls[:, HEAD_DIM:HEAD_DIM + 1]]
        cq_all = cq_ref[...]
        cq = [jnp.sum(jnp.where(lane == 2 * p_idx + h, cq_all, 0.0), axis=1, keepdims=True) for h in range(2)]
        row = lax.broadcasted_iota(jnp.int32, (tq, tq), 0)
        col = lax.broadcasted_iota(jnp.int32, (tq, tq), 1)
        causal = col <= row

        def tile(s0, dq, diag):
            kb = k_ref[pl.ds(s0, tq), :]
            vb = v_ref[pl.ds(s0, tq), :]
            kh = _by_head(m0, kb)
            dk = jnp.zeros((tq, LANES), f32)
            dv = jnp.zeros((tq, LANES), f32)
            for h in range(2):
                z = _dot(qh[h], kb, _NT) + (cq[h] - ck_ref[h, :, pl.ds(s0, tq)])
                p = jnp.exp(z - lse[h])
                if diag:
                    p = jnp.where(causal, p, 0.0)
                ds = p * (_dot(doh[h], vb, _NT) - delta[h])
                dsb = ds.astype(bf16)
                dq = dq + _dot(dsb, kh[h])
                dk = dk + _dot(dsb, qh[h], _TN)
                dv = dv + _dot(p.astype(bf16), doh[h], _TN)
                dc_ref[h, :, pl.ds(s0, tq)] -= jnp.sum(ds, axis=0, keepdims=True)
            dk_ref[pl.ds(s0, tq), :] += dk
            dv_ref[pl.ds(s0, tq), :] += dv
            return dq

        dq = lax.fori_loop(0, i, lambda n, dq: tile(pl.multiple_of(n * tq, tq), dq, False), jnp.zeros((tq, LANES), f32))
        dq = tile(pl.multiple_of(i * tq, tq), dq, True)
        dq_ref[...] = dq * Q_SCALE
        finish()

    qs = lambda cc: pl.BlockSpec((tq, LANES), lambda b, p, i: (b * nq + i, cc + p))
    ks = lambda cc: pl.BlockSpec((S, LANES), lambda b, p, i: (b, cc + p))
    cqs = pl.BlockSpec((tq, LANES), lambda b, p, i: (b * nq + i, 0))
    cks = pl.BlockSpec((2, 1, S), lambda b, p, i: (b * n_pairs + p, 0, 0))
    ts = pl.BlockSpec((tq, LANES), lambda b, p, i: (b * nq + i, p))
    fs = pl.BlockSpec((S, LANES), lambda b, p, i: (b, p))
    shp = jax.ShapeDtypeStruct((T, n_pairs * LANES), f32)
    return _host_call(body, "fox_bwd", (B, n_pairs, nq), [qs(qcol), ks(kcol), ks(vcol), cqs, cks, ts, ts, ts],
                      [qkv, qkv, qkv, c, cT, do, o, lse], [ts, fs, fs, cks],
                      [shp, shp, shp, jax.ShapeDtypeStruct(cT.shape, f32)], hosted)


def _sigmoid(x):
    return 1.0 / (1.0 + jnp.exp(-x))


def _mix_fwd(o_sb, o_fx, g, x, wp_sb, wp_fx, w_out, ln_g, ln_b, tm):
    T, D = x.shape
    E = o_sb.shape[1]
    tm = _tile(T, tm, 8)

    def body(osb_ref, ofx_ref, gsb_ref, gfx_ref, x_ref, wsb_ref, wfx_ref, wo_ref, lg_ref, lb_ref,
             xhat_ref, rstd_ref, x1_ref, mg_ref):
        y_sb = _dot(osb_ref[...], wsb_ref[...])
        y_fx = _dot(ofx_ref[...], wfx_ref[...])
        merged = (_sigmoid(gsb_ref[...]) * y_sb + _sigmoid(gfx_ref[...]) * y_fx).astype(bf16)
        r = ALPHA * x_ref[...] + _dot(merged, wo_ref[...])
        mean = jnp.mean(r, axis=1, keepdims=True)
        cen = r - mean
        rstd = lax.rsqrt(jnp.mean(cen * cen, axis=1, keepdims=True) + LN_EPS)
        xhat = cen * rstd
        xhat_ref[...] = xhat
        rstd_ref[...] = rstd
        x1_ref[...] = (xhat * lg_ref[...] + lb_ref[...]).astype(bf16)
        mg_ref[...] = merged

    rows = lambda w, c=0: pl.BlockSpec((tm, w), lambda i: (i, c))
    full = lambda a: pl.BlockSpec(a.shape, lambda i: (0, 0))
    return _pcall(body, name="mix_fwd", grid=(T // tm,),
                  in_specs=[rows(E), rows(E), rows(D, 0), rows(D, 1), rows(D), full(wp_sb), full(wp_fx), full(w_out),
                            full(ln_g), full(ln_b)],
                  out_specs=[rows(D), rows(1), rows(D), rows(D)],
                  out_shape=[jax.ShapeDtypeStruct((T, D), f32), jax.ShapeDtypeStruct((T, 1), f32),
                             jax.ShapeDtypeStruct((T, D), bf16), jax.ShapeDtypeStruct((T, D), bf16)],
                  compiler_params=_params(("arbitrary",)))(o_sb, o_fx, g, g, x, wp_sb, wp_fx, w_out, ln_g, ln_b)


def _mix_bwd(dr1, o_sb, o_fx, g, wp_sb, wp_fx, w_out, tm):
    T, D = dr1.shape
    E = o_sb.shape[1]
    tm = _tile(T, tm, 8)

    def body(dr_ref, osb_ref, ofx_ref, gsb_ref, gfx_ref, wsb_ref, wfx_ref, wo_ref,
             dysb_ref, dyfx_ref, dgsb_ref, dgfx_ref, dosb_ref, dofx_ref, sumsb_ref, sumfx_ref):
        @pl.when(pl.program_id(0) == 0)
        def _():
            sumsb_ref[...] = jnp.zeros_like(sumsb_ref)
            sumfx_ref[...] = jnp.zeros_like(sumfx_ref)
        dm = _dot(dr_ref[...].astype(bf16), wo_ref[...], _NT)
        for o_ref, g_ref, w_ref, dy_ref, dg_ref, do_ref, sum_ref in (
                (osb_ref, gsb_ref, wsb_ref, dysb_ref, dgsb_ref, dosb_ref, sumsb_ref),
                (ofx_ref, gfx_ref, wfx_ref, dyfx_ref, dgfx_ref, dofx_ref, sumfx_ref)):
            y = _dot(o_ref[...], w_ref[...])
            s = _sigmoid(g_ref[...])
            dy = (dm * s).astype(bf16)
            dy_ref[...] = dy
            dg = dm * y * s * (1.0 - s)
            dg_ref[...] = dg.astype(bf16)
            sum_ref[0:1, :] += jnp.sum(dg, axis=0, keepdims=True)
            do_ref[...] = _dot(dy, w_ref[...], _NT).astype(bf16)

    rows = lambda w, c=0: pl.BlockSpec((tm, w), lambda i: (i, c))
    full = lambda a: pl.BlockSpec(a.shape, lambda i: (0, 0))
    acc = pl.BlockSpec((8, D), lambda i: (0, 0))
    res = _pcall(body, name="mix_bwd", grid=(T // tm,),
                 in_specs=[rows(D), rows(E), rows(E), rows(D, 0), rows(D, 1), full(wp_sb), full(wp_fx), full(w_out)],
                 out_specs=[rows(D), rows(D), rows(D), rows(D), rows(E), rows(E), acc, acc],
                 out_shape=[jax.ShapeDtypeStruct((T, D), bf16)] * 4 + [jax.ShapeDtypeStruct((T, E), bf16)] * 2
                 + [jax.ShapeDtypeStruct((8, D), f32)] * 2,
                 compiler_params=_params(("arbitrary",)))(dr1, o_sb, o_fx, g, g, wp_sb, wp_fx, w_out)
    return res


def _ln_bwd(dy_a, dy_b, scale_b, xhat, rstd, ln_g, tm):
    T, D = xhat.shape
    tm = _tile(T, tm, 8)

    def body(a_ref, b_ref, xh_ref, rs_ref, g_ref, dr_ref, st_ref):
        @pl.when(pl.program_id(0) == 0)
        def _():
            st_ref[...] = jnp.zeros_like(st_ref)
        dy = a_ref[...] + scale_b * b_ref[...]
        xh = xh_ref[...]
        dxh = dy * g_ref[...]
        m1 = jnp.mean(dxh, axis=1, keepdims=True)
        m2 = jnp.mean(dxh * xh, axis=1, keepdims=True)
        dr_ref[...] = rs_ref[...] * (dxh - m1 - xh * m2)
        st_ref[0:1, :] += jnp.sum(dy * xh, axis=0, keepdims=True)
        st_ref[1:2, :] += jnp.sum(dy, axis=0, keepdims=True)

    rows = lambda w: pl.BlockSpec((tm, w), lambda i: (i, 0))
    return _pcall(body, name="ln1_bwd", grid=(T // tm,),
                  in_specs=[rows(D), rows(D), rows(D), rows(1), pl.BlockSpec((1, D), lambda i: (0, 0))],
                  out_specs=[rows(D), pl.BlockSpec((8, D), lambda i: (0, 0))],
                  out_shape=[jax.ShapeDtypeStruct((T, D), f32), jax.ShapeDtypeStruct((8, D), f32)],
                  compiler_params=_params(("arbitrary",)))(dy_a, dy_b, xhat, rstd, ln_g)


_INV_SQRT2 = 1.0 / math.sqrt(2.0)
_INV_SQRT2PI = 1.0 / math.sqrt(2.0 * math.pi)


def _conv_rows(ref, r0, rc, first, wc, bc):
    cur = ref[pl.ds(r0, rc), :]
    prev = ref[pl.ds(pl.multiple_of(jnp.maximum(r0 - 8, 0), 8), 8), :]
    prev = jnp.where(first, jnp.zeros_like(prev), prev)
    rid = lax.broadcasted_iota(jnp.int32, (rc, LANES), 0)
    s1 = jnp.where(rid == 0, prev[7:8, :], pltpu.roll(cur, 1, 0))
    s2 = jnp.where(rid == 0, prev[6:7, :], jnp.where(rid == 1, prev[7:8, :], pltpu.roll(cur, 2, 0)))
    conv = bc + wc[0:1, :] * s2 + wc[1:2, :] * s1 + wc[2:3, :] * cur
    return conv, (s2, s1, cur)


def _glu_fwd(u, w_conv, b_conv, B, S, rc=512):
    F = u.shape[1] // 2
    nf = F // LANES
    rc = _tile(S, rc, 8)

    def body(ug_ref, uv_ref, wc_ref, bc_ref, a_ref):
        wc, bc = wc_ref[...], bc_ref[...]

        def chunk(n, _):
            r0 = pl.multiple_of(n * rc, rc)
            c, _taps = _conv_rows(ug_ref, r0, rc, n == 0, wc, bc)
            gelu = 0.5 * c * (1.0 + lax.erf(c * _INV_SQRT2))
            a_ref[pl.ds(r0, rc), :] = (gelu * uv_ref[pl.ds(r0, rc), :]).astype(bf16)
            return 0

        lax.fori_loop(0, S // rc, chunk, 0)

    return _pcall(body, name="glu_fwd", grid=(B, nf),
                  in_specs=[pl.BlockSpec((S, LANES), lambda b, j: (b, j)), pl.BlockSpec((S, LANES), lambda b, j: (b, nf + j)),
                            pl.BlockSpec((3, LANES), lambda b, j: (0, j)), pl.BlockSpec((1, LANES), lambda b, j: (0, j))],
                  out_specs=pl.BlockSpec((S, LANES), lambda b, j: (b, j)),
                  out_shape=jax.ShapeDtypeStruct((B * S, F), bf16),
                  compiler_params=_params(("arbitrary", "arbitrary")))(u, u, w_conv, b_conv)


def _glu_bwd(u, da, w_conv, b_conv, B, S, rc=512):
    F = u.shape[1] // 2
    nf = F // LANES
    rc = _tile(S, rc, 8)
    nc = S // rc

    def body(ug_ref, uv_ref, da_ref, wc_ref, bc_ref, dug_ref, duv_ref, gw_ref, gb_ref, dc_ref):
        wc, bc = wc_ref[...], bc_ref[...]

        def chunk(n, carry):
            gw0, gw1, gw2, gb = carry
            r0 = pl.multiple_of(n * rc, rc)
            c, (s2, s1, cur) = _conv_rows(ug_ref, r0, rc, n == 0, wc, bc)
            cdf = 0.5 * (1.0 + lax.erf(c * _INV_SQRT2))
            da = da_ref[pl.ds(r0, rc), :]
            duv_ref[pl.ds(r0, rc), :] = (da * (c * cdf)).astype(bf16)
            dc = da * uv_ref[pl.ds(r0, rc), :] * (cdf + c * (_INV_SQRT2PI * jnp.exp(-0.5 * c * c)))
            dc_ref[pl.ds(r0, rc), :] = dc
            red = lambda t: jnp.sum(t, axis=0, keepdims=True)
            return gw0 + red(dc * s2), gw1 + red(dc * s1), gw2 + red(dc * cur), gb + red(dc)

        z = jnp.zeros((1, LANES), f32)
        gw0, gw1, gw2, gb = lax.fori_loop(0, nc, chunk, (z, z, z, z))
        gw_ref[0, 0:1, :] = gw0
        gw_ref[0, 1:2, :] = gw1
        gw_ref[0, 2:3, :] = gw2
        gb_ref[0] = gb

        def chunk2(n, _):
            r0 = pl.multiple_of(n * rc, rc)
            cur = dc_ref[pl.ds(r0, rc), :]
            nxt = dc_ref[pl.ds(pl.multiple_of(jnp.minimum(r0 + rc, S - 8), 8), 8), :]
            nxt = jnp.where(n == nc - 1, jnp.zeros_like(nxt), nxt)
            rid = lax.broadcasted_iota(jnp.int32, (rc, LANES), 0)
            a1 = jnp.where(rid == rc - 1, nxt[0:1, :], pltpu.roll(cur, rc - 1, 0))
            a2 = jnp.where(rid == rc - 1, nxt[1:2, :], jnp.where(rid == rc - 2, nxt[0:1, :], pltpu.roll(cur, rc - 2, 0)))
            dug_ref[pl.ds(r0, rc), :] = (wc[2:3, :] * cur + wc[1:2, :] * a1 + wc[0:1, :] * a2).astype(bf16)
            return 0

        lax.fori_loop(0, nc, chunk2, 0)

    blk = lambda off: pl.BlockSpec((S, LANES), lambda b, j: (b, off + j))
    return _pcall(body, name="glu_bwd", grid=(B, nf),
                  in_specs=[blk(0), blk(nf), blk(0), pl.BlockSpec((3, LANES), lambda b, j: (0, j)),
                            pl.BlockSpec((1, LANES), lambda b, j: (0, j))],
                  out_specs=[blk(0), blk(0), pl.BlockSpec((1, 3, LANES), lambda b, j: (b, 0, j)),
                             pl.BlockSpec((1, 1, LANES), lambda b, j: (b, 0, j))],
                  out_shape=[jax.ShapeDtypeStruct((B * S, F), bf16), jax.ShapeDtypeStruct((B * S, F), bf16),
                             jax.ShapeDtypeStruct((B, 3, F), f32), jax.ShapeDtypeStruct((B, 1, F), f32)],
                  scratch_shapes=[pltpu.VMEM((S, LANES), f32)],
                  compiler_params=_params(("arbitrary", "arbitrary")))(u, u, da, w_conv, b_conv)


def _down_loss(a, w_down, xhat1, ln1_g, ln1_b, ln2_g, ln2_b, tgt, tm):
    T, D = xhat1.shape
    F = a.shape[1]
    tm = _tile(T, tm, 8)

    def body(a_ref, w_ref, xh_ref, g1_ref, b1_ref, g2_ref, b2_ref, t_ref, dr_ref, st_ref):
        @pl.when(pl.program_id(0) == 0)
        def _():
            st_ref[...] = jnp.zeros_like(st_ref)
        x1 = xh_ref[...] * g1_ref[...] + b1_ref[...]
        r = ALPHA * x1 + _dot(a_ref[...], w_ref[...])
        mean = jnp.mean(r, axis=1, keepdims=True)
        cen = r - mean
        rstd = lax.rsqrt(jnp.mean(cen * cen, axis=1, keepdims=True) + LN_EPS)
        xh = cen * rstd
        err = (xh * g2_ref[...] + b2_ref[...]) - t_ref[...]
        dy = err * (1.0 / D)
        dxh = dy * g2_ref[...]
        m1 = jnp.mean(dxh, axis=1, keepdims=True)
        m2 = jnp.mean(dxh * xh, axis=1, keepdims=True)
        dr_ref[...] = rstd * (dxh - m1 - xh * m2)
        st_ref[0:1, :] += jnp.sum(dy * xh, axis=0, keepdims=True)
        st_ref[1:2, :] += jnp.sum(dy, axis=0, keepdims=True)
        st_ref[2:3, :] += jnp.sum(err * err, axis=0, keepdims=True)

    rows = lambda w: pl.BlockSpec((tm, w), lambda i: (i, 0))
    vec = pl.BlockSpec((1, D), lambda i: (0, 0))
    return _pcall(body, name="down_loss", grid=(T // tm,),
                  in_specs=[rows(F), pl.BlockSpec((F, D), lambda i: (0, 0)), rows(D), vec, vec, vec, vec, rows(D)],
                  out_specs=[rows(D), pl.BlockSpec((8, D), lambda i: (0, 0))],
                  out_shape=[jax.ShapeDtypeStruct((T, D), f32), jax.ShapeDtypeStruct((8, D), f32)],
                  compiler_params=_params(("arbitrary",)))(a, w_down, xhat1, ln1_g, ln1_b, ln2_g, ln2_b, tgt)


def _local_step(x, tgt, w_in, b_in, ln1_g, ln1_b, b_conv, ln2_g, ln2_b, late_shards, late_weights, reducer=None,
                tq=512, tm=256):
    B, S, D = x.shape
    T = B * S
    E = (w_in.shape[1] - 2 * D) * HEAD_DIM // (6 * HEAD_DIM + 1)
    n_pairs = E // LANES
    NH = E // HEAD_DIM
    x2 = x.reshape(T, D)
    tgt2 = tgt.reshape(T, D)
    tq = _tile(S, tq, 8)

    c_f, c_g = 6 * E, 6 * E + NH
    w_qkv = w_in[:, :c_f]
    qscale = jnp.concatenate([jnp.full((E,), Q_SCALE, f32), jnp.ones((2 * E,), f32)] * 2)
    w_qkv_s = (w_qkv.astype(f32) * qscale).astype(bf16)
    b_qkv_s = b_in[:, :c_f] * qscale
    w_f = jnp.pad(w_in[:, c_f:c_g], ((0, 0), (0, LANES - NH)))
    b_f = jnp.pad(b_in[:, c_f:c_g], ((0, 0), (0, LANES - NH)))
    w_g = w_in[:, c_g:]
    b_g = b_in[:, c_g:]

    xb = x2.astype(bf16)
    qkv = _matmul(xb, w_qkv_s, bias=b_qkv_s, out_dtype=bf16, tm=1024, tn=1536, name="proj_qkv")
    g = _matmul(xb, w_g, bias=b_g, tm=1024, tn=1024, name="proj_gate")
    fl = _matmul(xb, w_f, bias=b_f, tm=1024, name="proj_forget")
    c = _cumlogf(fl, B, S)
    cT = c.reshape(B, S, LANES)[:, :, :NH].transpose(0, 2, 1).reshape(B * NH, 1, S)
    P = n_pairs
    o_sb, lt_sb, stacks = _sb_fwd(qkv, B, S, P, 0, P, 2 * P, tq, late_shards)
    wp_sb, wp_fx, w_out, w_up, w_conv, w_down = late_weights(stacks)
    F = w_down.shape[0]
    o_fx, o_fx32, lse_fx = _fox_fwd(qkv, c, cT, B, S, P, 3 * P, 4 * P, 5 * P, tq)
    xhat1, rstd1, x1b, merged = _mix_fwd(o_sb, o_fx, g, x2, wp_sb, wp_fx, w_out, ln1_g, ln1_b, tm)
    u = _matmul(x1b, w_up, tm=1024, tn=1408, name="ffn_up")
    a = _glu_fwd(u, w_conv, b_conv, B, S)
    dr2, st2 = _down_loss(a, w_down, xhat1, ln1_g, ln1_b, ln2_g, ln2_b, tgt2, tm)

    grads = {}
    grads["ln2_g"], grads["ln2_b"] = st2[0:1], st2[1:2]
    sq_err = st2[2:3]
    da = _matmul(dr2, w_down, tb=True, tm=1024, tn=1408, name="ffn_da")
    grads["w_down"] = _matmul(a, dr2, ta=True, tm=1408, tn=1024, tk=1024, name="grad_w_down")
    du_g, du_v, gwc, gbc = _glu_bwd(u, da, w_conv, b_conv, B, S)
    grads["w_conv"] = jnp.sum(gwc, axis=0)
    grads["b_conv"] = jnp.sum(gbc, axis=0)
    grads["w_up"] = jnp.concatenate(
        [_matmul(x1b, du_g, ta=True, tm=512, tn=2816, tk=1024, name="grad_w_up_gate"),
         _matmul(x1b, du_v, ta=True, tm=512, tn=2816, tk=1024, name="grad_w_up_val")], axis=1)
    dx1 = _matmul(du_g, w_up[:, :F], tb=True, tm=1024, tn=1024, tk=F, name="ffn_dx_gate")
    dx1 = _matmul(du_v, w_up[:, F:], tb=True, addend=dx1, tm=1024, tn=1024, tk=F, name="ffn_dx_val")
    dr1, st1 = _ln_bwd(dx1, dr2, ALPHA, xhat1, rstd1, ln1_g, tm)
    grads["ln1_g"], grads["ln1_b"] = st1[0:1], st1[1:2]

    dy_sb, dy_fx, dg_sb, dg_fx, do_sb, do_fx, gsum_sb, gsum_fx = _mix_bwd(dr1, o_sb, o_fx, g, wp_sb, wp_fx, w_out, tm)
    grads["w_out"] = _matmul(merged, dr1, ta=True, tm=512, tn=1024, tk=1024, name="grad_w_out")
    grads["w_proj_sb"] = _matmul(o_sb, dy_sb, ta=True, tm=512, tn=1024, tk=1024, name="grad_w_proj_sb")
    grads["w_proj_fox"] = _matmul(o_fx, dy_fx, ta=True, tm=512, tn=1024, tk=1024, name="grad_w_proj_fox")
    (dq_f, dk_f, dv_f, dcT), got = _fox_bwd(qkv, c, cT, do_fx, o_fx32, lse_fx, B, S, P, 3 * P, 4 * P, 5 * P, tq,
                                            reducer.to_sibling(grads) if reducer else None)
    (dq_s, dk_s, dv_s), got = _sb_bwd(qkv, do_sb, lt_sb, B, S, P, 0, P, 2 * P, tq, reducer.to_chips(got) if reducer else None)
    if reducer:
        reducer.from_chips(got)
    dc = jnp.pad(dcT.reshape(B, NH, S).transpose(0, 2, 1), ((0, 0), (0, 0), (0, LANES - NH))).reshape(T, LANES)
    dfl = _cumlogf_bwd(dc, fl, B, S)
    dqkv = jnp.concatenate([dq_s, dk_s, dv_s, dq_f, dk_f, dv_f], axis=1)
    dg = jnp.concatenate([dg_sb, dg_fx], axis=1)
    gw_qkv, gb_qkv = _matmul(x2, dqkv, ta=True, colsum=True, tm=512, tn=3072, tk=512, name="grad_w_qkv")
    gw_f, gb_f = _matmul(x2, dfl, ta=True, colsum=True, tm=512, tk=1024, name="grad_w_forget")
    gw_g = _matmul(x2, dg, ta=True, tm=512, tn=2048, tk=1024, name="grad_w_gate")
    grads["w_in"] = jnp.concatenate([gw_qkv, gw_f[:, :NH], gw_g], axis=1)
    grads["b_in"] = jnp.concatenate([gb_qkv, gb_f[:, :NH], gsum_sb[0:1], gsum_fx[0:1]], axis=1)
    dx = _matmul(dqkv, w_qkv, tb=True, addend=dr1, addend_scale=ALPHA, tm=512, tn=1024, tk=c_f, name="dx_qkv")
    dx = _matmul(dfl, w_f, tb=True, addend=dx, tm=1024, tn=1024, name="dx_forget")
    dx = _matmul(dg, w_g, tb=True, addend=dx, tm=1024, tn=1024, tk=2 * D, name="dx_gate")
    return sq_err, dx.reshape(B, S, D), grads


_ANY = pl.BlockSpec(memory_space=pl.ANY)
_MESH = pl.DeviceIdType.MESH


def _pos():
    return lax.axis_index("x"), lax.axis_index("y"), lax.axis_index("c")


def _other_chips(x, y):
    return [(1 - x, y), (x, 1 - y), (1 - x, 1 - y)]


def _gather_shards(shards):
    n = len(shards)

    def body(*refs):
        start, finish = _gather_copies(refs[:n], refs[n:2 * n], *refs[2 * n:])
        start()
        finish()

    return _pcall(body, name="gather_weights", in_specs=[_ANY] * n, out_specs=[_ANY] * n,
                  out_shape=_gather_shapes(shards), scratch_shapes=_gather_sems(n),
                  compiler_params=pltpu.CompilerParams(has_side_effects=True))(*shards)


def _gather_shapes(shards):
    return [jax.ShapeDtypeStruct((4,) + s.shape, s.dtype) for s in shards]


def _gather_sems(n):
    return [pltpu.SemaphoreType.DMA((n, 3)), pltpu.SemaphoreType.DMA((n, 3)), pltpu.SemaphoreType.DMA((n,))]


def _gather_copies(srcs, dsts, send_sems, recv_sems, local_sems):
    n = len(srcs)
    x, y, c = _pos()
    k = 2 * x + y
    chips = _other_chips(x, y)

    def copy(a, j, chip, slot):
        return pltpu.make_async_remote_copy(src_ref=srcs[a], dst_ref=dsts[a].at[slot], send_sem=send_sems.at[a, j],
                                            recv_sem=recv_sems.at[a, j], device_id=(*chip, c), device_id_type=_MESH)

    def mine():
        return ([pltpu.make_async_copy(srcs[a], dsts[a].at[k], local_sems.at[a]) for a in range(n)],
                [copy(a, j, chip, k) for a in range(n) for j, chip in enumerate(chips)])

    def start():
        local, sends = mine()
        for cp in local + sends:
            cp.start()

    def finish():
        local, sends = mine()
        for a in range(n):
            for j, chip in enumerate(chips):
                copy(a, j, chip, 2 * chip[0] + chip[1]).wait_recv()
        for cp in sends:
            cp.wait_send()
        for cp in local:
            cp.wait()

    return start, finish


class _Hosted:
    def __init__(self, inputs, outputs, sems, copies):
        self.inputs, self.outputs, self.sems, self.copies = list(inputs), list(outputs), list(sems), copies


def _host(hosted, refs, n_in, n_out, step, last):
    if hosted is None:
        return refs[:n_in], refs[n_in:n_in + n_out], lambda: None
    hi, ho = len(hosted.inputs), len(hosted.outputs)
    own_in, h_in = refs[:n_in], refs[n_in:n_in + hi]
    own_out = refs[n_in + hi:n_in + hi + n_out]
    h_out = refs[n_in + hi + n_out:n_in + hi + n_out + ho]
    start, finish = hosted.copies(h_in, h_out, *refs[n_in + hi + n_out + ho:])
    pl.when(step == 0)(start)
    return own_in, own_out, lambda: pl.when(step == last)(finish)


def _host_call(body, name, grid, in_specs, args, out_specs, out_shape, hosted):
    h_in = hosted.inputs if hosted else []
    h_out = hosted.outputs if hosted else []
    res = _pcall(body, name=name, grid=grid, in_specs=list(in_specs) + [_ANY] * len(h_in),
                 out_specs=list(out_specs) + [_ANY] * len(h_out), out_shape=list(out_shape) + list(h_out),
                 scratch_shapes=hosted.sems if hosted else [],
                 compiler_params=_params(("arbitrary",) * len(grid)))(*args, *h_in)
    return list(res[:len(out_shape)]), list(res[len(out_shape):])


def _sibling_copies(shapes):
    n = len(shapes)

    def copies(p_refs, got_refs, send_sems, recv_sems):
        x, y, c = _pos()

        def copy(a):
            h = shapes[a][1] // 2
            src = p_refs[a].at[:, pl.ds(pl.multiple_of((1 - c) * h, 8), h), :]
            return pltpu.make_async_remote_copy(src_ref=src, dst_ref=got_refs[a], send_sem=send_sems.at[a],
                                                recv_sem=recv_sems.at[a], device_id=(x, y, 1 - c), device_id_type=_MESH)

        def start():
            for a in range(n):
                copy(a).start()

        def finish():
            for a in range(n):
                copy(a).wait()

        return start, finish

    return copies


def _hosted_sibling(pieces):
    n = len(pieces)
    return _Hosted(pieces, [jax.ShapeDtypeStruct((4, p.shape[1] // 2, p.shape[2]), p.dtype) for p in pieces],
                   [pltpu.SemaphoreType.DMA((n,)), pltpu.SemaphoreType.DMA((n,))], _sibling_copies([p.shape for p in pieces]))


def _chips_copies(n):
    def copies(p_refs, got_refs, send_sems, recv_sems):
        x, y, c = _pos()
        k = 2 * x + y
        chips = _other_chips(x, y)

        def copy(a, j, chip, piece, slot):
            return pltpu.make_async_remote_copy(src_ref=p_refs[a].at[piece], dst_ref=got_refs[a].at[slot],
                                                send_sem=send_sems.at[a, j], recv_sem=recv_sems.at[a, j],
                                                device_id=(*chip, c), device_id_type=_MESH)

        def start():
            for a in range(n):
                for j, chip in enumerate(chips):
                    copy(a, j, chip, 2 * chip[0] + chip[1], k).start()

        def finish():
            for a in range(n):
                for j, chip in enumerate(chips):
                    copy(a, j, chip, k, 2 * chip[0] + chip[1]).wait_recv()
            for a in range(n):
                for j, chip in enumerate(chips):
                    copy(a, j, chip, 2 * chip[0] + chip[1], k).wait_send()

        return start, finish

    return copies


def _hosted_chips(pieces):
    n = len(pieces)
    return _Hosted(pieces, [jax.ShapeDtypeStruct(p.shape, p.dtype) for p in pieces],
                   [pltpu.SemaphoreType.DMA((n, 3)), pltpu.SemaphoreType.DMA((n, 3))], _chips_copies(n))


def _exchange_sibling_and_small(pieces, small):
    n = len(pieces)

    def body(*refs):
        p_refs, sm_ref = refs[:n], refs[n]
        got_refs, sg_ref = refs[n + 1:2 * n + 1], refs[2 * n + 1]
        big_send, big_recv, send_sems, recv_sems, local_sem = refs[2 * n + 2:]
        x, y, c = _pos()
        me = 4 * x + 2 * y + c
        start_big, finish_big = _sibling_copies([p.shape for p in pieces])(p_refs, got_refs, big_send, big_recv)
        start_big()
        local = pltpu.make_async_copy(sm_ref, sg_ref.at[me], local_sem)
        local.start()
        peers = []
        for r in range(1, 8):
            flip = lambda v, bit: 1 - v if bit else v
            peers.append((flip(x, r & 4), flip(y, r & 2), flip(c, r & 1)))

        def copy(j, slot):
            return pltpu.make_async_remote_copy(src_ref=sm_ref, dst_ref=sg_ref.at[slot], send_sem=send_sems.at[j],
                                                recv_sem=recv_sems.at[j], device_id=peers[j], device_id_type=_MESH)

        sends = [copy(j, me) for j in range(7)]
        for cp in sends:
            cp.start()
        for j, (px, py, pc) in enumerate(peers):
            copy(j, 4 * px + 2 * py + pc).wait_recv()
        for cp in sends:
            cp.wait_send()
        local.wait()
        finish_big()

    halves = [jax.ShapeDtypeStruct((4, p.shape[1] // 2, p.shape[2]), p.dtype) for p in pieces]
    res = _pcall(body, name="exchange_sibling", in_specs=[_ANY] * (n + 1), out_specs=[_ANY] * (n + 1),
                 out_shape=halves + [jax.ShapeDtypeStruct((8,) + small.shape, small.dtype)],
                 scratch_shapes=[pltpu.SemaphoreType.DMA((n,)), pltpu.SemaphoreType.DMA((n,)), pltpu.SemaphoreType.DMA((7,)),
                                 pltpu.SemaphoreType.DMA((7,)), pltpu.SemaphoreType.DMA(())],
                 compiler_params=pltpu.CompilerParams(has_side_effects=True))(*pieces, small)
    return res[:n], res[n]


def _exchange_chips(pieces):
    n = len(pieces)

    def body(*refs):
        start, finish = _chips_copies(n)(refs[:n], refs[n:2 * n], *refs[2 * n:])
        start()
        finish()

    return _pcall(body, name="exchange_chips", in_specs=[_ANY] * n, out_specs=[_ANY] * n,
                  out_shape=[jax.ShapeDtypeStruct(p.shape, p.dtype) for p in pieces],
                  scratch_shapes=[pltpu.SemaphoreType.DMA((n, 3)), pltpu.SemaphoreType.DMA((n, 3))],
                  compiler_params=pltpu.CompilerParams(has_side_effects=True))(*pieces)


def _share_halves(shards):
    n = len(shards)

    def body(*refs):
        full_refs = refs[n:2 * n]
        send_sems, recv_sems = refs[2 * n:]
        x, y, c = _pos()

        def copy(a, half):
            h = shards[a].shape[0] // 2
            rows = full_refs[a].at[pl.ds(pl.multiple_of(half * h, 8), h), :]
            return pltpu.make_async_remote_copy(src_ref=rows, dst_ref=rows, send_sem=send_sems.at[a],
                                                recv_sem=recv_sems.at[a], device_id=(x, y, 1 - c), device_id_type=_MESH)

        sends = [copy(a, c) for a in range(n)]
        for cp in sends:
            cp.start()
        for a in range(n):
            copy(a, 1 - c).wait_recv()
        for cp in sends:
            cp.wait_send()

    return _pcall(body, name="share_halves", in_specs=[_ANY] * n, out_specs=[_ANY] * n,
                  out_shape=[jax.ShapeDtypeStruct(s.shape, s.dtype) for s in shards],
                  input_output_aliases={a: a for a in range(n)},
                  scratch_shapes=[pltpu.SemaphoreType.DMA((n,)), pltpu.SemaphoreType.DMA((n,))],
                  compiler_params=pltpu.CompilerParams(has_side_effects=True))(*shards)


def _add_own_half(piece, got, core, name):
    _, r, cols = piece.shape
    h = r // 2

    def body(c_ref, a_ref, b_ref, o_ref, o16_ref):
        s = a_ref[0] + b_ref[...]
        o_ref[...] = s
        o16_ref[...] = s.astype(bf16)

    out = pl.BlockSpec((1, h, cols), lambda k, c: (k, 0, 0))
    grid_spec = pltpu.PrefetchScalarGridSpec(
        num_scalar_prefetch=1, grid=(4,),
        in_specs=[pl.BlockSpec((1, 1, h, cols), lambda k, c: (k, c[0], 0, 0)), out], out_specs=[out, out])
    return _pcall(body, name=name, grid_spec=grid_spec,
                  out_shape=[jax.ShapeDtypeStruct((4, h, cols), f32), jax.ShapeDtypeStruct((4, h, cols), bf16)],
                  compiler_params=_params(("arbitrary",)))(core, piece.reshape(4, 2, h, cols), got)


def _sum_chips(own, got, where, name):
    _, h, cols = own.shape
    t = _tile(h, 128, 16)
    nt = h // t

    def body(w_ref, own_ref, gx_ref, gy_ref, gxy_ref, o_ref):
        o_ref[...] = ((own_ref[0] + gx_ref[0].astype(f32)) + gy_ref[0].astype(f32)) + gxy_ref[0].astype(f32)

    slot = lambda j: pl.BlockSpec((1, t, cols), lambda i, w: (w[j], i, 0))
    grid_spec = pltpu.PrefetchScalarGridSpec(
        num_scalar_prefetch=1, grid=(nt,),
        in_specs=[slot(0), slot(2), slot(3), slot(4)],
        out_specs=pl.BlockSpec((t, cols), lambda i, w: (w[1] * nt + i, 0)))
    return _pcall(body, name=name, grid_spec=grid_spec, out_shape=jax.ShapeDtypeStruct((2 * h, cols), f32),
                  compiler_params=_params(("arbitrary",)))(where, own, got, got, got)


def _sum_slots(stack, name):
    k, n, cols = stack.shape
    t = _tile(n, 128, 8)

    def body(s_ref, o_ref):
        acc = s_ref[0]
        for i in range(1, k):
            acc = acc + s_ref[i]
        o_ref[...] = acc

    return _pcall(body, name=name, grid=(n // t,), in_specs=[pl.BlockSpec((k, t, cols), lambda i: (0, i, 0))],
                  out_specs=pl.BlockSpec((t, cols), lambda i: (i, 0)), out_shape=jax.ShapeDtypeStruct((n, cols), f32),
                  compiler_params=_params(("arbitrary",)))(stack)


def _adamw(w, g, m, v, name):
    n, cols = w.shape
    t = _tile(n, 128, 8)
    c1 = 1.0 - ADAM_B1 ** ADAM_STEP
    c2 = 1.0 - ADAM_B2 ** ADAM_STEP

    def body(w_ref, g_ref, m_ref, v_ref, d_ref, nm_ref, nv_ref, g_out_ref):
        g = g_ref[...]
        nm = ADAM_B1 * m_ref[...] + (1.0 - ADAM_B1) * g
        nv = ADAM_B2 * v_ref[...] + (1.0 - ADAM_B2) * (g * g)
        d_ref[...] = -ADAM_LR * ((nm / c1) / (jnp.sqrt(nv / c2) + ADAM_EPS) + ADAM_WD * w_ref[...])
        nm_ref[...] = nm
        nv_ref[...] = nv
        g_out_ref[...] = g

    spec = pl.BlockSpec((t, cols), lambda i: (i, 0))
    shp = jax.ShapeDtypeStruct((n, cols), f32)
    return _pcall(body, name=name, grid=(n // t,), in_specs=[spec] * 4, out_specs=[spec] * 4, out_shape=[shp] * 4,
                  compiler_params=_params(("arbitrary",)))(w, g, m, v)


_MATS = (("w_in", 1), ("w_proj_sb", 1), ("w_proj_fox", 1), ("w_out", 0), ("w_up", 1), ("w_down", 0))
_SMALL = ("b_in", "ln1_g", "ln1_b", "b_conv", "ln2_g", "ln2_b")


def _pad_lanes(v):
    n = v.shape[-1]
    return jnp.pad(v, ((0, 0), (0, (-n) % LANES)))


def _pack_rows(vectors):
    flat = jnp.concatenate([_pad_lanes(v.reshape(1, -1)) for v in vectors], axis=1).reshape(-1, LANES)
    return jnp.pad(flat, ((0, (-flat.shape[0]) % 8), (0, 0)))


def _unpack_rows(packed, sizes):
    out, r = [], 0
    for n in sizes:
        rows = -(-n // LANES)
        out.append(packed[r:r + rows].reshape(1, rows * LANES)[:, :n])
        r += rows
    return out


def _unstack(stack, axis):
    if axis == 0:
        return stack.reshape(-1, stack.shape[2])
    return jnp.concatenate([stack[k] for k in range(4)], axis=1)


def _pieces(g, axis):
    if axis == 0:
        return g.reshape(4, g.shape[0] // 4, g.shape[1])
    cols = g.shape[1] // 4
    return jnp.stack([g[:, k * cols:(k + 1) * cols] for k in range(4)])


class _Reducer:
    def __init__(self, mats, core, where):
        self.mats, self.core, self.where = mats, core, where

    def pieces(self, grads):
        self.local = [_pieces(grads[n], axis) for n, axis in self.mats]
        return self.local

    def to_sibling(self, grads):
        return _hosted_sibling(self.pieces(grads))

    def to_chips(self, got):
        self.sums = [_add_own_half(p, g, self.core, "add_sibling_" + n) for (n, _), p, g in zip(self.mats, self.local, got)]
        return _hosted_chips([s16 for _, s16 in self.sums])

    def from_chips(self, got):
        self.halves = [_sum_chips(s32, r16, self.where, "sum_chips_" + n)
                       for (n, _), (s32, _), r16 in zip(self.mats, self.sums, got)]


def kernel(x, w_in, b_in, w_proj_sb, w_proj_fox, w_out, ln1_g, ln1_b, w_up, w_conv, b_conv, w_down, ln2_g, ln2_b, loss_target, m_w_in, m_b_in, m_w_proj_sb, m_w_proj_fox, m_w_out, m_ln1_g, m_ln1_b, m_w_up, m_w_conv, m_b_conv, m_w_down, m_ln2_g, m_ln2_b, v_w_in, v_b_in, v_w_proj_sb, v_w_proj_fox, v_w_out, v_ln1_g, v_ln1_b, v_w_up, v_w_conv, v_b_conv, v_w_down, v_ln2_g, v_ln2_b):
    w = dict(w_in=w_in, b_in=b_in, w_proj_sb=w_proj_sb, w_proj_fox=w_proj_fox, w_out=w_out, ln1_g=ln1_g, ln1_b=ln1_b,
             w_up=w_up, w_conv=w_conv, b_conv=b_conv, w_down=w_down, ln2_g=ln2_g, ln2_b=ln2_b)
    m = dict(w_in=m_w_in, b_in=m_b_in, w_proj_sb=m_w_proj_sb, w_proj_fox=m_w_proj_fox, w_out=m_w_out, ln1_g=m_ln1_g,
             ln1_b=m_ln1_b, w_up=m_w_up, w_conv=m_w_conv, b_conv=m_b_conv, w_down=m_w_down, ln2_g=m_ln2_g, ln2_b=m_ln2_b)
    v = dict(w_in=v_w_in, b_in=v_b_in, w_proj_sb=v_w_proj_sb, w_proj_fox=v_w_proj_fox, w_out=v_w_out, ln1_g=v_ln1_g,
             ln1_b=v_ln1_b, w_up=v_w_up, w_conv=v_w_conv, b_conv=v_b_conv, w_down=v_w_down, ln2_g=v_ln2_g, ln2_b=v_ln2_b)
    order = ["w_in", "b_in", "w_proj_sb", "w_proj_fox", "w_out", "ln1_g", "ln1_b", "w_up", "w_conv", "b_conv", "w_down",
             "ln2_g", "ln2_b"]
    x_idx, y_idx, c_idx = _pos()
    chip = 2 * x_idx + y_idx
    D = x.shape[-1]
    core = c_idx.astype(jnp.int32).reshape(1)

    w_in_full = _unstack(_gather_shards([w["w_in"][0].astype(bf16)])[0], 1)
    late = (("w_proj_sb", 1), ("w_proj_fox", 1), ("w_out", 0), ("w_up", 1), ("w_conv", 1), ("w_down", 0))
    late_shards = [w[n][0] if n == "w_conv" else w[n][0].astype(bf16) for n, _ in late]
    late_weights = lambda stacks: [_unstack(s, axis) for (_, axis), s in zip(late, stacks)]

    where = jnp.stack([chip, c_idx, 2 * (1 - x_idx) + y_idx, 2 * x_idx + 1 - y_idx, 2 * (1 - x_idx) + 1 - y_idx]).astype(jnp.int32)
    early = _Reducer(_MATS[1:], core, where)
    sq_err, grad_x, grads = _local_step(x, loss_target, w_in_full, w["b_in"], w["ln1_g"], w["ln1_b"], w["b_conv"],
                                        w["ln2_g"], w["ln2_b"], late_shards, late_weights, early)
    loss_part = (0.5 / D) * jnp.sum(sq_err)

    last = _Reducer(_MATS[:1], core, where)
    small_names = list(_SMALL) + ["w_conv"]
    small = _pack_rows([jnp.full((1, 1), loss_part, f32)] + [grads[n] for n in small_names])
    got, small_all = _exchange_sibling_and_small(last.pieces(grads), small)
    last.from_chips(_exchange_chips(last.to_chips(got).inputs))
    g_shards = _share_halves(last.halves + early.halves)
    small_sum = _sum_slots(small_all, "sum_small")

    out = {"grad": {}, "delta": {}, "m": {}, "v": {}}
    for (n, _), g_ in zip(_MATS, g_shards):
        d_, m_, v_, g_out = _adamw(w[n][0], g_, m[n][0], v[n][0], "adamw_" + n)
        for key, t in (("grad", g_out), ("delta", d_), ("m", m_), ("v", v_)):
            out[key][n] = t.reshape(w[n].shape)
    sizes = [1] + [int(grads[n].size) for n in small_names]
    sm = _unpack_rows(small_sum, sizes)
    loss = sm[0][0, 0]
    g_small = dict(zip(small_names, sm[1:]))
    F4 = w["w_conv"].shape[-1]
    g_small["w_conv"] = lax.dynamic_slice_in_dim(g_small["w_conv"].reshape(3, -1), chip * F4, F4, axis=1)
    pack_s = lambda d: _pack_rows([d[n].reshape(1, -1) for n in small_names])
    gs_packed = _pack_rows([g_small[n].reshape(1, -1) for n in small_names])
    s_delta, s_m, s_v, _ = _adamw(pack_s(w), gs_packed, pack_s(m), pack_s(v), "adamw_small")
    s_sizes = [int(w[n].size) for n in small_names]

    for key, packed_s in (("grad", gs_packed), ("delta", s_delta), ("m", s_m), ("v", s_v)):
        for n, t in zip(small_names, _unpack_rows(packed_s, s_sizes)):
            out[key][n] = t.reshape(w[n].shape)
    return (loss, grad_x, *[out["grad"][n] for n in order], *[out["delta"][n] for n in order],
            *[out["m"][n] for n in order], *[out["v"][n] for n in order])
```

```python
import functools
import math

import jax
import jax.numpy as jnp
from jax import lax
from jax.experimental import pallas as pl
from jax.experimental.pallas import tpu as pltpu

f32, bf16 = jnp.float32, jnp.bfloat16

HEAD_DIM = 64
LANES = 128
LN_EPS = 1e-5
ALPHA = 2.0 ** 0.25
Q_SCALE = HEAD_DIM ** -0.5
ADAM_LR, ADAM_B1, ADAM_B2, ADAM_EPS, ADAM_WD, ADAM_STEP = 0.001, 0.9, 0.999, 1e-08, 0.01, 10
VMEM_LIMIT = 56 * 1024 * 1024
NEG = -1e30

_pcall = pl.pallas_call
_NT = (((1,), (1,)), ((), ()))
_TN = (((0,), (0,)), ((), ()))


def _params(sem=None):
    return pltpu.CompilerParams(dimension_semantics=sem, vmem_limit_bytes=VMEM_LIMIT)


def _tile(dim, target, unit=LANES):
    if dim <= target:
        return dim
    t = (target // unit) * unit
    while t > unit and dim % t:
        t -= unit
    assert dim % t == 0, (dim, target)
    return t


def _dot(a, b, dn=None):
    if dn is None:
        return jnp.dot(a, b, preferred_element_type=f32)
    return lax.dot_general(a, b, dn, preferred_element_type=f32)


def _split_dot(x, tri):
    hi = x.astype(bf16)
    lo = (x - hi.astype(f32)).astype(bf16)
    return _dot(hi, tri) + _dot(lo, tri)


SCAN_BLOCK = 256


def _scan_cols(x, tri, reverse, split=True):
    cb = tri.shape[0]
    nb = x.shape[1] // cb
    blocks = [x[:, b * cb:(b + 1) * cb] for b in range(nb)]
    outs, run = [None] * nb, None
    for b in (reversed(range(nb)) if reverse else range(nb)):
        o = _split_dot(blocks[b], tri) if split else _dot(blocks[b].astype(bf16), tri)
        s = jnp.sum(blocks[b], axis=1, keepdims=True)
        outs[b] = o if run is None else o + run
        run = s if run is None else run + s
    return (outs[0] if nb == 1 else jnp.concatenate(outs, axis=1)), run


def _tri(cb, rel):
    row = lax.broadcasted_iota(jnp.int32, (cb, cb), 0)
    col = lax.broadcasted_iota(jnp.int32, (cb, cb), 1)
    return rel(row, col).astype(bf16)


def _matmul(a, b, *, name, ta=False, tb=False, bias=None, addend=None, addend_scale=1.0, colsum=False,
            out_dtype=f32, tm=512, tn=512, tk=1024, hosted=None):
    M, K = (a.shape[1], a.shape[0]) if ta else a.shape
    N = b.shape[0] if tb else b.shape[1]
    assert K == (b.shape[1] if tb else b.shape[0])
    assert not (colsum and tb)
    tm, tn, tk = _tile(M, tm), _tile(N, tn), _tile(K, tk)
    nk = K // tk
    n_in = 2 + (bias is not None) + (addend is not None)
    n_out = 1 + colsum
    grid = (M // tm, N // tn, nk)

    def body(*refs):
        k = pl.program_id(2)
        step = (pl.program_id(0) * grid[1] + pl.program_id(1)) * nk + k
        scratch = refs[len(refs) - n_out - (len(hosted.sems) if hosted else 0):]
        own_in, own_out, finish = _host(hosted, refs[:len(refs) - len(scratch)] + scratch[n_out:], n_in, n_out, step,
                                        grid[0] * grid[1] * nk - 1)
        a_ref, b_ref = own_in[0], own_in[1]
        bias_ref = own_in[2] if bias is not None else None
        add_ref = own_in[n_in - 1] if addend is not None else None
        o_ref = own_out[0]
        cs_ref = own_out[1] if colsum else None
        acc = scratch[0]
        cs_acc = scratch[1] if colsum else None

        @pl.when(k == 0)
        def _():
            acc[...] = jnp.zeros_like(acc)
            if colsum:
                cs_acc[...] = jnp.zeros_like(cs_acc)

        dn = (((0 if ta else 1,), (1 if tb else 0,)), ((), ()))
        acc[...] += lax.dot_general(a_ref[...].astype(bf16), b_ref[...].astype(bf16), dn, preferred_element_type=f32)
        if colsum:
            cs_acc[...] += jnp.sum(b_ref[...].astype(f32), axis=0, keepdims=True)

        @pl.when(k == nk - 1)
        def _():
            r = acc[...]
            if bias is not None:
                r = r + bias_ref[...]
            if addend is not None:
                r = r + addend_scale * add_ref[...].astype(f32)
            o_ref[...] = r.astype(out_dtype)
            if colsum:
                cs_ref[0] = cs_acc[...]

        finish()

    a_spec = pl.BlockSpec((tk, tm), lambda i, j, k: (k, i)) if ta else pl.BlockSpec((tm, tk), lambda i, j, k: (i, k))
    b_spec = pl.BlockSpec((tn, tk), lambda i, j, k: (j, k)) if tb else pl.BlockSpec((tk, tn), lambda i, j, k: (k, j))
    in_specs, args = [a_spec, b_spec], [a, b]
    if bias is not None:
        in_specs.append(pl.BlockSpec((1, tn), lambda i, j, k: (0, j)))
        args.append(bias.reshape(1, N).astype(f32))
    if addend is not None:
        in_specs.append(pl.BlockSpec((tm, tn), lambda i, j, k: (i, j)))
        args.append(addend)
    out_shape = [jax.ShapeDtypeStruct((M, N), out_dtype)]
    out_specs = [pl.BlockSpec((tm, tn), lambda i, j, k: (i, j))]
    scratch = [pltpu.VMEM((tm, tn), f32)]
    if colsum:
        out_shape.append(jax.ShapeDtypeStruct((M // tm, 1, N), f32))
        out_specs.append(pl.BlockSpec((1, 1, tn), lambda i, j, k: (i, 0, j)))
        scratch.append(pltpu.VMEM((1, tn), f32))
    h_in = hosted.inputs if hosted else []
    h_out = hosted.outputs if hosted else []
    res = _pcall(body, name=name, grid=grid, in_specs=in_specs + [_ANY] * len(h_in),
                 out_specs=out_specs + [_ANY] * len(h_out), out_shape=out_shape + h_out,
                 scratch_shapes=scratch + (hosted.sems if hosted else []),
                 compiler_params=_params(("arbitrary", "arbitrary", "arbitrary")))(*args, *h_in)
    own = (res[0], res[1][0]) if colsum else res[0]
    return (own, list(res[n_out:])) if hosted else own


def _cumlogf(fl, B, S):
    t = _tile(S, 256, 8)

    def body(fl_ref, c_ref, carry):
        @pl.when(pl.program_id(1) == 0)
        def _():
            carry[...] = jnp.zeros_like(carry)
        z = fl_ref[...]
        ls = jnp.minimum(z, 0.0) - jnp.log(1.0 + jnp.exp(-jnp.abs(z)))
        row = lax.broadcasted_iota(jnp.int32, (t, t), 0)
        col = lax.broadcasted_iota(jnp.int32, (t, t), 1)
        lower = (col <= row).astype(f32)
        c = jnp.dot(lower, ls, precision=lax.Precision.HIGHEST, preferred_element_type=f32) + carry[...]
        c_ref[...] = c
        carry[...] = c[t - 1:t, :]

    return _pcall(body, name="cumlogf", grid=(B, S // t),
                  in_specs=[pl.BlockSpec((t, LANES), lambda b, i: (b * (S // t) + i, 0))],
                  out_specs=pl.BlockSpec((t, LANES), lambda b, i: (b * (S // t) + i, 0)),
                  out_shape=jax.ShapeDtypeStruct(fl.shape, f32), scratch_shapes=[pltpu.VMEM((1, LANES), f32)],
                  compiler_params=_params(("arbitrary", "arbitrary")))(fl)


def _cumlogf_bwd(dc, fl, B, S):
    t = _tile(S, 256, 8)
    n = S // t

    def body(dc_ref, fl_ref, o_ref, carry):
        @pl.when(pl.program_id(1) == 0)
        def _():
            carry[...] = jnp.zeros_like(carry)
        row = lax.broadcasted_iota(jnp.int32, (t, t), 0)
        col = lax.broadcasted_iota(jnp.int32, (t, t), 1)
        upper = (col >= row).astype(f32)
        r = jnp.dot(upper, dc_ref[...], precision=lax.Precision.HIGHEST, preferred_element_type=f32) + carry[...]
        carry[...] = r[0:1, :]
        z = fl_ref[...]
        o_ref[...] = r / (1.0 + jnp.exp(z))

    spec = pl.BlockSpec((t, LANES), lambda b, i: (b * n + n - 1 - i, 0))
    return _pcall(body, name="cumlogf_bwd", grid=(B, n), in_specs=[spec, spec], out_specs=spec,
                  out_shape=jax.ShapeDtypeStruct(fl.shape, f32), scratch_shapes=[pltpu.VMEM((1, LANES), f32)],
                  compiler_params=_params(("arbitrary", "arbitrary")))(dc, fl)


def _head_masks():
    lane = lax.broadcasted_iota(jnp.int32, (1, LANES), 1)
    return lane < HEAD_DIM


def _by_head(m0, t):
    z = jnp.zeros_like(t)
    return [jnp.where(m0, t, z), jnp.where(m0, z, t)]


def _sb_terms(z):
    softplus = jnp.maximum(z, 0.0) + jnp.log(1.0 + jnp.exp(-jnp.abs(z)))
    return z - softplus, -softplus


STRIP_ROWS = 32


def _strip_rows(tq):
    return STRIP_ROWS if tq % STRIP_ROWS == 0 else tq


def _strict(r, rs, tq):
    row = lax.broadcasted_iota(jnp.int32, (rs, tq), 0) + r
    col = lax.broadcasted_iota(jnp.int32, (rs, tq), 1)
    return col < row


def _score_scratch(tq, n_f32, n_bf16, n_sums):
    return ([pltpu.VMEM((tq, tq), f32)] * (2 * n_f32) + [pltpu.VMEM((tq, tq), bf16)] * (2 * n_bf16)
            + [pltpu.VMEM((tq, LANES), f32)] * (2 * n_sums))


def _by_pairs(refs):
    return [refs[i:i + 2] for i in range(0, len(refs), 2)]


def _sb_fwd(qkv, B, S, n_pairs, qcol, kcol, vcol, tq, shards=()):
    nq = S // tq
    T = B * S
    n = len(shards)

    cb = min(tq, SCAN_BLOCK)
    nb = tq // cb
    rs = _strip_rows(tq)

    def body(q_ref, k_ref, v_ref, *rest):
        o_ref, lt_ref = rest[n], rest[n + 1]
        z_s, suf_s, hi_s, lo_s, w_s, sum_s = _by_pairs(rest[len(rest) - 12:])
        i = pl.program_id(2)
        if n:
            start, finish = _gather_copies(rest[:n], rest[n + 2:2 * n + 2], *rest[2 * n + 2:2 * n + 5])
            step = (pl.program_id(0) * n_pairs + pl.program_id(1)) * nq + i
            pl.when(step == 0)(start)
        m0 = _head_masks()
        qh = _by_head(m0, q_ref[...])
        later = _tri(cb, lambda j, s: j > s)
        lane = lax.broadcasted_iota(jnp.int32, (1, LANES), 1)

        def tile(s0, R, acc, diag):
            kb = k_ref[pl.ds(s0, tq), :]
            vh = _by_head(m0, v_ref[pl.ds(s0, tq), :])
            R = list(R)
            for h in range(2):
                z_s[h][...] = _dot(qh[h], kb, _NT)
            for h in range(2):
                for r in range(0, tq, rs):
                    lb, l1m = _sb_terms(z_s[h][r:r + rs, :])
                    z_s[h][r:r + rs, :] = lb
                    if diag:
                        l1m = jnp.where(_strict(r, rs, tq), l1m, 0.0)
                    hi = l1m.astype(bf16)
                    hi_s[h][r:r + rs, :] = hi
                    lo_s[h][r:r + rs, :] = (l1m - hi.astype(f32)).astype(bf16)
                    sums = jnp.zeros((rs, LANES), f32)
                    for b in range(nb):
                        sums = jnp.where(lane == b, jnp.sum(l1m[:, b * cb:(b + 1) * cb], axis=1, keepdims=True), sums)
                    sum_s[h][r:r + rs, :] = sums
            for h in range(2):
                for b in range(nb):
                    blk = slice(b * cb, (b + 1) * cb)
                    suf_s[h][:, blk] = _dot(hi_s[h][:, blk], later) + _dot(lo_s[h][:, blk], later)
            for h in range(2):
                for r in range(0, tq, rs):
                    sums = sum_s[h][r:r + rs, :]
                    after = R[h][r:r + rs]
                    for b in reversed(range(nb)):
                        blk = slice(b * cb, (b + 1) * cb)
                        w = jnp.exp(z_s[h][r:r + rs, blk] + (suf_s[h][r:r + rs, blk] + after))
                        if diag:
                            w = jnp.where(_strict(r, rs, tq)[:, blk], w, 0.0)
                        w_s[h][r:r + rs, blk] = w.astype(bf16)
                        after = after + sums[:, b:b + 1]
            for h in range(2):
                acc = acc + _dot(w_s[h][...], vh[h])
                R[h] = R[h] + jnp.sum(sum_s[h][...], axis=1, keepdims=True)
            return R, acc

        zero = jnp.zeros((tq, 1), f32)
        R, acc = tile(pl.multiple_of(i * tq, tq), [zero, zero], jnp.zeros((tq, LANES), f32), True)

        def loop(n, carry):
            s0 = pl.multiple_of((i - 1 - n) * tq, tq)
            R, acc = tile(s0, carry[:2], carry[2], False)
            return R[0], R[1], acc

        R0, R1, acc = lax.fori_loop(0, i, loop, (R[0], R[1], acc))
        o_ref[...] = acc.astype(bf16)
        lt_ref[...] = jnp.where(m0, R0, R1)
        if n:
            pl.when(step == B * n_pairs * nq - 1)(finish)

    qs = lambda c: pl.BlockSpec((tq, LANES), lambda b, p, i: (b * nq + i, c + p))
    ks = lambda c: pl.BlockSpec((S, LANES), lambda b, p, i: (b, c + p))
    os_ = pl.BlockSpec((tq, LANES), lambda b, p, i: (b * nq + i, p))
    res = _pcall(body, name="sb_fwd", grid=(B, n_pairs, nq), in_specs=[qs(qcol), ks(kcol), ks(vcol)] + [_ANY] * n,
                 out_specs=[os_, os_] + [_ANY] * n,
                 out_shape=[jax.ShapeDtypeStruct((T, n_pairs * LANES), bf16), jax.ShapeDtypeStruct((T, n_pairs * LANES), f32)]
                 + _gather_shapes(shards), scratch_shapes=(_gather_sems(n) if n else []) + _score_scratch(tq, 2, 3, 1),
                 compiler_params=_params(("arbitrary", "arbitrary", "arbitrary")))(qkv, qkv, qkv, *shards)
    return res[0], res[1], res[2:]


def _sb_bwd(qkv, do, lt, B, S, n_pairs, qcol, kcol, vcol, tq, hosted=None):
    nq = S // tq
    T = B * S

    def body(*refs):
        i = pl.program_id(2)
        step = (pl.program_id(0) * n_pairs + pl.program_id(1)) * nq + i
        (q_ref, k_ref, v_ref, do_ref, lt_ref), (dq_ref, dk_ref, dv_ref), finish = _host(
            hosted, refs, 5, 3, step, B * n_pairs * nq - 1)

        @pl.when(i == 0)
        def _():
            dk_ref[...] = jnp.zeros_like(dk_ref)
            dv_ref[...] = jnp.zeros_like(dv_ref)

        m0 = _head_masks()
        qh = _by_head(m0, q_ref[...])
        doh = _by_head(m0, do_ref[...])
        lt = lt_ref[...]
        ltot = [lt[:, 0:1], lt[:, HEAD_DIM:HEAD_DIM + 1]]
        row = lax.broadcasted_iota(jnp.int32, (tq, tq), 0)
        col = lax.broadcasted_iota(jnp.int32, (tq, tq), 1)
        strict = col < row
        upto = _tri(min(tq, SCAN_BLOCK), lambda j, s: j <= s)
        before = _tri(min(tq, SCAN_BLOCK), lambda j, s: j < s)

        def tile(s0, CL, CP, dq, diag):
            kb = k_ref[pl.ds(s0, tq), :]
            vb = v_ref[pl.ds(s0, tq), :]
            kh = _by_head(m0, kb)
            CL, CP = list(CL), list(CP)
            dk = jnp.zeros((tq, LANES), f32)
            dv = jnp.zeros((tq, LANES), f32)
            for h in range(2):
                z = _dot(qh[h], kb, _NT)
                lb, l1m = _sb_terms(z)
                if diag:
                    l1m = jnp.where(strict, l1m, 0.0)
                pre, l_total = _scan_cols(l1m, upto, False)
                w = jnp.exp(lb + ((ltot[h] - CL[h]) - pre))
                if diag:
                    w = jnp.where(strict, w, 0.0)
                g = _dot(doh[h], vb, _NT) * w
                p, g_total = _scan_cols(g, before, False, split=False)
                dz = g - jnp.exp(lb) * (g + (p + CP[h]))
                if diag:
                    dz = jnp.where(strict, dz, 0.0)
                dzb = dz.astype(bf16)
                dq = dq + _dot(dzb, kh[h])
                dk = dk + _dot(dzb, qh[h], _TN)
                dv = dv + _dot(w.astype(bf16), doh[h], _TN)
                CL[h] = CL[h] + l_total
                CP[h] = CP[h] + g_total
            dk_ref[pl.ds(s0, tq), :] += dk
            dv_ref[pl.ds(s0, tq), :] += dv
            return CL, CP, dq

        zero = jnp.zeros((tq, 1), f32)

        def loop(n, carry):
            CL, CP, dq = tile(pl.multiple_of(n * tq, tq), carry[0:2], carry[2:4], carry[4], False)
            return CL[0], CL[1], CP[0], CP[1], dq

        c = lax.fori_loop(0, i, loop, (zero, zero, zero, zero, jnp.zeros((tq, LANES), f32)))
        _, _, dq = tile(pl.multiple_of(i * tq, tq), c[0:2], c[2:4], c[4], True)
        dq_ref[...] = dq * Q_SCALE
        finish()

    qs = lambda c: pl.BlockSpec((tq, LANES), lambda b, p, i: (b * nq + i, c + p))
    ks = lambda c: pl.BlockSpec((S, LANES), lambda b, p, i: (b, c + p))
    ts = pl.BlockSpec((tq, LANES), lambda b, p, i: (b * nq + i, p))
    fs = pl.BlockSpec((S, LANES), lambda b, p, i: (b, p))
    shp = jax.ShapeDtypeStruct((T, n_pairs * LANES), f32)
    return _host_call(body, "sb_bwd", (B, n_pairs, nq), [qs(qcol), ks(kcol), ks(vcol), ts, ts], [qkv, qkv, qkv, do, lt],
                      [ts, fs, fs], [shp, shp, shp], hosted)


def _fox_fwd(qkv, c, cT, B, S, n_pairs, qcol, kcol, vcol, tq):
    nq = S // tq
    T = B * S

    def body(q_ref, k_ref, v_ref, cq_ref, ck_ref, o_ref, o32_ref, lse_ref):
        p_idx = pl.program_id(1)
        i = pl.program_id(2)
        m0 = _head_masks()
        lane = lax.broadcasted_iota(jnp.int32, (1, LANES), 1)
        qh = _by_head(m0, q_ref[...])
        cq_all = cq_ref[...]
        cq = [jnp.sum(jnp.where(lane == 2 * p_idx + h, cq_all, 0.0), axis=1, keepdims=True) for h in range(2)]
        row = lax.broadcasted_iota(jnp.int32, (tq, tq), 0)
        col = lax.broadcasted_iota(jnp.int32, (tq, tq), 1)
        causal = col <= row

        def tile(s0, m, l, acc, diag):
            kb = k_ref[pl.ds(s0, tq), :]
            vh = _by_head(m0, v_ref[pl.ds(s0, tq), :])
            m, l = list(m), list(l)
            scale, add = [], []
            for h in range(2):
                z = _dot(qh[h], kb, _NT) + (cq[h] - ck_ref[h, :, pl.ds(s0, tq)])
                if diag:
                    z = jnp.where(causal, z, NEG)
                m_new = jnp.maximum(m[h], jnp.max(z, axis=1, keepdims=True))
                p = jnp.exp(z - m_new)
                a = jnp.exp(m[h] - m_new)
                l[h] = a * l[h] + jnp.sum(p, axis=1, keepdims=True)
                m[h] = m_new
                scale.append(a)
                add.append(_dot(p.astype(bf16), vh[h]))
            acc = acc * jnp.where(m0, scale[0], scale[1]) + add[0] + add[1]
            return m, l, acc

        neg = jnp.full((tq, 1), NEG, f32)
        zero = jnp.zeros((tq, 1), f32)
        m, l, acc = tile(pl.multiple_of(i * tq, tq), [neg, neg], [zero, zero], jnp.zeros((tq, LANES), f32), True)

        def loop(n, carry):
            m, l, acc = tile(pl.multiple_of(n * tq, tq), carry[0:2], carry[2:4], carry[4], False)
            return m[0], m[1], l[0], l[1], acc

        m0_, m1_, l0, l1, acc = lax.fori_loop(0, i, loop, (m[0], m[1], l[0], l[1], acc))
        o = acc * jnp.where(m0, 1.0 / l0, 1.0 / l1)
        o_ref[...] = o.astype(bf16)
        o32_ref[...] = o
        lse_ref[...] = jnp.where(m0, m0_ + jnp.log(l0), m1_ + jnp.log(l1))

    qs = lambda cc: pl.BlockSpec((tq, LANES), lambda b, p, i: (b * nq + i, cc + p))
    ks = lambda cc: pl.BlockSpec((S, LANES), lambda b, p, i: (b, cc + p))
    cqs = pl.BlockSpec((tq, LANES), lambda b, p, i: (b * nq + i, 0))
    cks = pl.BlockSpec((2, 1, S), lambda b, p, i: (b * n_pairs + p, 0, 0))
    os_ = pl.BlockSpec((tq, LANES), lambda b, p, i: (b * nq + i, p))
    shp = jax.ShapeDtypeStruct((T, n_pairs * LANES), f32)
    return _pcall(body, name="fox_fwd", grid=(B, n_pairs, nq), in_specs=[qs(qcol), ks(kcol), ks(vcol), cqs, cks],
                  out_specs=[os_, os_, os_], out_shape=[jax.ShapeDtypeStruct((T, n_pairs * LANES), bf16), shp, shp],
                  compiler_params=_params(("arbitrary", "arbitrary", "arbitrary")))(qkv, qkv, qkv, c, cT)


def _fox_bwd(qkv, c, cT, do, o, lse, B, S, n_pairs, qcol, kcol, vcol, tq, hosted=None):
    nq = S // tq
    T = B * S

    def body(*refs):
        p_idx = pl.program_id(1)
        i = pl.program_id(2)
        step = (pl.program_id(0) * n_pairs + p_idx) * nq + i
        (q_ref, k_ref, v_ref, cq_ref, ck_ref, do_ref, o_ref, lse_ref), (dq_ref, dk_ref, dv_ref, dc_ref), finish = _host(
            hosted, refs, 8, 4, step, B * n_pairs * nq - 1)

        @pl.when(i == 0)
        def _():
            dk_ref[...] = jnp.zeros_like(dk_ref)
            dv_ref[...] = jnp.zeros_like(dv_ref)
            dc_ref[...] = jnp.zeros_like(dc_ref)

        m0 = _head_masks()
        lane = lax.broadcasted_iota(jnp.int32, (1, LANES), 1)
        qh = _by_head(m0, q_ref[...])
        do2 = do_ref[...]
        doh = _by_head(m0, do2)
        prod = do2.astype(f32) * o_ref[...].astype(f32)
        delta = [jnp.sum(p, axis=1, keepdims=True) for p in _by_head(m0, prod)]
        ls = lse_ref[...]
        lse = [ls[:, 0:1], ls[:, HEAD_DIM:HEAD_DIM + 1]]
        cq_all = cq_ref[...]
        cq = [jnp.sum(jnp.where(lane == 2 * p_idx + h, cq_all, 0.0), axis=1, keepdims=True) for h in range(2)]
        row = lax.broadcasted_iota(jnp.int32, (tq, tq), 0)
        col = lax.broadcasted_iota(jnp.int32, (tq, tq), 1)
        causal = col <= row

        def tile(s0, dq, diag):
            kb = k_ref[pl.ds(s0, tq), :]
            vb = v_ref[pl.ds(s0, tq), :]
            kh = _by_head(m0, kb)
            dk = jnp.zeros((tq, LANES), f32)
            dv = jnp.zeros((tq, LANES), f32)
            for h in range(2):
                z = _dot(qh[h], kb, _NT) + (cq[h] - ck_ref[h, :, pl.ds(s0, tq)])
                p = jnp.exp(z - lse[h])
                if diag:
                    p = jnp.where(causal, p, 0.0)
                ds = p * (_dot(doh[h], vb, _NT) - delta[h])
                dsb = ds.astype(bf16)
                dq = dq + _dot(dsb, kh[h])
                dk = dk + _dot(dsb, qh[h], _TN)
                dv = dv + _dot(p.astype(bf16), doh[h], _TN)
                dc_ref[h, :, pl.ds(s0, tq)] -= jnp.sum(ds, axis=0, keepdims=True)
            dk_ref[pl.ds(s0, tq), :] += dk
            dv_ref[pl.ds(s0, tq), :] += dv
            return dq

        dq = lax.fori_loop(0, i, lambda n, dq: tile(pl.multiple_of(n * tq, tq), dq, False), jnp.zeros((tq, LANES), f32))
        dq = tile(pl.multiple_of(i * tq, tq), dq, True)
        dq_ref[...] = dq * Q_SCALE
        finish()

    qs = lambda cc: pl.BlockSpec((tq, LANES), lambda b, p, i: (b * nq + i, cc + p))
    ks = lambda cc: pl.BlockSpec((S, LANES), lambda b, p, i: (b, cc + p))
    cqs = pl.BlockSpec((tq, LANES), lambda b, p, i: (b * nq + i, 0))
    cks = pl.BlockSpec((2, 1, S), lambda b, p, i: (b * n_pairs + p, 0, 0))
    ts = pl.BlockSpec((tq, LANES), lambda b, p, i: (b * nq + i, p))
    fs = pl.BlockSpec((S, LANES), lambda b, p, i: (b, p))
    shp = jax.ShapeDtypeStruct((T, n_pairs * LANES), f32)
    return _host_call(body, "fox_bwd", (B, n_pairs, nq), [qs(qcol), ks(kcol), ks(vcol), cqs, cks, ts, ts, ts],
                      [qkv, qkv, qkv, c, cT, do, o, lse], [ts, fs, fs, cks],
                      [shp, shp, shp, jax.ShapeDtypeStruct(cT.shape, f32)], hosted)


def _sigmoid(x):
    return 1.0 / (1.0 + jnp.exp(-x))


def _mix_fwd(o_sb, o_fx, g, x, wp_sb, wp_fx, w_out, ln_g, ln_b, tm):
    T, D = x.shape
    E = o_sb.shape[1]
    tm = _tile(T, tm, 8)

    def body(osb_ref, ofx_ref, gsb_ref, gfx_ref, x_ref, wsb_ref, wfx_ref, wo_ref, lg_ref, lb_ref,
             xhat_ref, rstd_ref, x1_ref, mg_ref):
        y_sb = _dot(osb_ref[...], wsb_ref[...])
        y_fx = _dot(ofx_ref[...], wfx_ref[...])
        merged = (_sigmoid(gsb_ref[...]) * y_sb + _sigmoid(gfx_ref[...]) * y_fx).astype(bf16)
        r = ALPHA * x_ref[...] + _dot(merged, wo_ref[...])
        mean = jnp.mean(r, axis=1, keepdims=True)
        cen = r - mean
        rstd = lax.rsqrt(jnp.mean(cen * cen, axis=1, keepdims=True) + LN_EPS)
        xhat = cen * rstd
        xhat_ref[...] = xhat
        rstd_ref[...] = rstd
        x1_ref[...] = (xhat * lg_ref[...] + lb_ref[...]).astype(bf16)
        mg_ref[...] = merged

    rows = lambda w, c=0: pl.BlockSpec((tm, w), lambda i: (i, c))
    full = lambda a: pl.BlockSpec(a.shape, lambda i: (0, 0))
    return _pcall(body, name="mix_fwd", grid=(T // tm,),
                  in_specs=[rows(E), rows(E), rows(D, 0), rows(D, 1), rows(D), full(wp_sb), full(wp_fx), full(w_out),
                            full(ln_g), full(ln_b)],
                  out_specs=[rows(D), rows(1), rows(D), rows(D)],
                  out_shape=[jax.ShapeDtypeStruct((T, D), f32), jax.ShapeDtypeStruct((T, 1), f32),
                             jax.ShapeDtypeStruct((T, D), bf16), jax.ShapeDtypeStruct((T, D), bf16)],
                  compiler_params=_params(("arbitrary",)))(o_sb, o_fx, g, g, x, wp_sb, wp_fx, w_out, ln_g, ln_b)


def _mix_bwd(dr1, o_sb, o_fx, g, wp_sb, wp_fx, w_out, tm):
    T, D = dr1.shape
    E = o_sb.shape[1]
    tm = _tile(T, tm, 8)

    def body(dr_ref, osb_ref, ofx_ref, gsb_ref, gfx_ref, wsb_ref, wfx_ref, wo_ref,
             dysb_ref, dyfx_ref, dg_ref, dosb_ref, dofx_ref, sumsb_ref, sumfx_ref):
        @pl.when(pl.program_id(0) == 0)
        def _():
            sumsb_ref[...] = jnp.zeros_like(sumsb_ref)
            sumfx_ref[...] = jnp.zeros_like(sumfx_ref)
        dm = _dot(dr_ref[...].astype(bf16), wo_ref[...], _NT)
        for half, (o_ref, g_ref, w_ref, dy_ref, do_ref, sum_ref) in enumerate((
                (osb_ref, gsb_ref, wsb_ref, dysb_ref, dosb_ref, sumsb_ref),
                (ofx_ref, gfx_ref, wfx_ref, dyfx_ref, dofx_ref, sumfx_ref))):
            y = _dot(o_ref[...], w_ref[...])
            s = _sigmoid(g_ref[...])
            dy = (dm * s).astype(bf16)
            dy_ref[...] = dy
            dg = dm * y * s * (1.0 - s)
            dg_ref[:, half * D:(half + 1) * D] = dg.astype(bf16)
            sum_ref[0:1, :] += jnp.sum(dg, axis=0, keepdims=True)
            do_ref[...] = _dot(dy, w_ref[...], _NT).astype(bf16)

    rows = lambda w, c=0: pl.BlockSpec((tm, w), lambda i: (i, c))
    full = lambda a: pl.BlockSpec(a.shape, lambda i: (0, 0))
    acc = pl.BlockSpec((8, D), lambda i: (0, 0))
    res = _pcall(body, name="mix_bwd", grid=(T // tm,),
                 in_specs=[rows(D), rows(E), rows(E), rows(D, 0), rows(D, 1), full(wp_sb), full(wp_fx), full(w_out)],
                 out_specs=[rows(D), rows(D), rows(2 * D), rows(E), rows(E), acc, acc],
                 out_shape=[jax.ShapeDtypeStruct((T, D), bf16)] * 2 + [jax.ShapeDtypeStruct((T, 2 * D), bf16)]
                 + [jax.ShapeDtypeStruct((T, E), bf16)] * 2 + [jax.ShapeDtypeStruct((8, D), f32)] * 2,
                 compiler_params=_params(("arbitrary",)))(dr1, o_sb, o_fx, g, g, wp_sb, wp_fx, w_out)
    return res


def _ln_bwd(dy_a, dy_b, scale_b, xhat, rstd, ln_g, tm):
    T, D = xhat.shape
    tm = _tile(T, tm, 8)

    def body(a_ref, b_ref, xh_ref, rs_ref, g_ref, dr_ref, st_ref):
        @pl.when(pl.program_id(0) == 0)
        def _():
            st_ref[...] = jnp.zeros_like(st_ref)
        dy = a_ref[...] + scale_b * b_ref[...]
        xh = xh_ref[...]
        dxh = dy * g_ref[...]
        m1 = jnp.mean(dxh, axis=1, keepdims=True)
        m2 = jnp.mean(dxh * xh, axis=1, keepdims=True)
        dr_ref[...] = rs_ref[...] * (dxh - m1 - xh * m2)
        st_ref[0:1, :] += jnp.sum(dy * xh, axis=0, keepdims=True)
        st_ref[1:2, :] += jnp.sum(dy, axis=0, keepdims=True)

    rows = lambda w: pl.BlockSpec((tm, w), lambda i: (i, 0))
    return _pcall(body, name="ln1_bwd", grid=(T // tm,),
                  in_specs=[rows(D), rows(D), rows(D), rows(1), pl.BlockSpec((1, D), lambda i: (0, 0))],
                  out_specs=[rows(D), pl.BlockSpec((8, D), lambda i: (0, 0))],
                  out_shape=[jax.ShapeDtypeStruct((T, D), f32), jax.ShapeDtypeStruct((8, D), f32)],
                  compiler_params=_params(("arbitrary",)))(dy_a, dy_b, xhat, rstd, ln_g)


_INV_SQRT2 = 1.0 / math.sqrt(2.0)
_INV_SQRT2PI = 1.0 / math.sqrt(2.0 * math.pi)


def _conv_rows(ref, r0, rc, first, wc, bc):
    cur = ref[pl.ds(r0, rc), :]
    prev = ref[pl.ds(pl.multiple_of(jnp.maximum(r0 - 8, 0), 8), 8), :]
    prev = jnp.where(first, jnp.zeros_like(prev), prev)
    rid = lax.broadcasted_iota(jnp.int32, (rc, LANES), 0)
    s1 = jnp.where(rid == 0, prev[7:8, :], pltpu.roll(cur, 1, 0))
    s2 = jnp.where(rid == 0, prev[6:7, :], jnp.where(rid == 1, prev[7:8, :], pltpu.roll(cur, 2, 0)))
    conv = bc + wc[0:1, :] * s2 + wc[1:2, :] * s1 + wc[2:3, :] * cur
    return conv, (s2, s1, cur)


def _glu_fwd(u, w_conv, b_conv, B, S, rc=512):
    F = u.shape[1] // 2
    nf = F // LANES
    rc = _tile(S, rc, 8)

    def body(ug_ref, uv_ref, wc_ref, bc_ref, a_ref):
        wc, bc = wc_ref[...], bc_ref[...]

        def chunk(n, _):
            r0 = pl.multiple_of(n * rc, rc)
            c, _taps = _conv_rows(ug_ref, r0, rc, n == 0, wc, bc)
            gelu = 0.5 * c * (1.0 + lax.erf(c * _INV_SQRT2))
            a_ref[pl.ds(r0, rc), :] = (gelu * uv_ref[pl.ds(r0, rc), :]).astype(bf16)
            return 0

        lax.fori_loop(0, S // rc, chunk, 0)

    return _pcall(body, name="glu_fwd", grid=(B, nf),
                  in_specs=[pl.BlockSpec((S, LANES), lambda b, j: (b, j)), pl.BlockSpec((S, LANES), lambda b, j: (b, nf + j)),
                            pl.BlockSpec((3, LANES), lambda b, j: (0, j)), pl.BlockSpec((1, LANES), lambda b, j: (0, j))],
                  out_specs=pl.BlockSpec((S, LANES), lambda b, j: (b, j)),
                  out_shape=jax.ShapeDtypeStruct((B * S, F), bf16),
                  compiler_params=_params(("arbitrary", "arbitrary")))(u, u, w_conv, b_conv)


def _glu_bwd(u, da, w_conv, b_conv, B, S, rc=512):
    F = u.shape[1] // 2
    nf = F // LANES
    rc = _tile(S, rc, 8)
    nc = S // rc

    def body(ug_ref, uv_ref, da_ref, wc_ref, bc_ref, dug_ref, duv_ref, gw_ref, gb_ref, dc_ref):
        wc, bc = wc_ref[...], bc_ref[...]

        def chunk(n, carry):
            gw0, gw1, gw2, gb = carry
            r0 = pl.multiple_of(n * rc, rc)
            c, (s2, s1, cur) = _conv_rows(ug_ref, r0, rc, n == 0, wc, bc)
            cdf = 0.5 * (1.0 + lax.erf(c * _INV_SQRT2))
            da = da_ref[pl.ds(r0, rc), :]
            duv_ref[pl.ds(r0, rc), :] = (da * (c * cdf)).astype(bf16)
            dc = da * uv_ref[pl.ds(r0, rc), :] * (cdf + c * (_INV_SQRT2PI * jnp.exp(-0.5 * c * c)))
            dc_ref[pl.ds(r0, rc), :] = dc
            red = lambda t: jnp.sum(t, axis=0, keepdims=True)
            return gw0 + red(dc * s2), gw1 + red(dc * s1), gw2 + red(dc * cur), gb + red(dc)

        z = jnp.zeros((1, LANES), f32)
        gw0, gw1, gw2, gb = lax.fori_loop(0, nc, chunk, (z, z, z, z))
        gw_ref[0, 0:1, :] = gw0
        gw_ref[0, 1:2, :] = gw1
        gw_ref[0, 2:3, :] = gw2
        gb_ref[0] = gb

        def chunk2(n, _):
            r0 = pl.multiple_of(n * rc, rc)
            cur = dc_ref[pl.ds(r0, rc), :]
            nxt = dc_ref[pl.ds(pl.multiple_of(jnp.minimum(r0 + rc, S - 8), 8), 8), :]
            nxt = jnp.where(n == nc - 1, jnp.zeros_like(nxt), nxt)
            rid = lax.broadcasted_iota(jnp.int32, (rc, LANES), 0)
            a1 = jnp.where(rid == rc - 1, nxt[0:1, :], pltpu.roll(cur, rc - 1, 0))
            a2 = jnp.where(rid == rc - 1, nxt[1:2, :], jnp.where(rid == rc - 2, nxt[0:1, :], pltpu.roll(cur, rc - 2, 0)))
            dug_ref[pl.ds(r0, rc), :] = (wc[2:3, :] * cur + wc[1:2, :] * a1 + wc[0:1, :] * a2).astype(bf16)
            return 0

        lax.fori_loop(0, nc, chunk2, 0)

    blk = lambda off: pl.BlockSpec((S, LANES), lambda b, j: (b, off + j))
    return _pcall(body, name="glu_bwd", grid=(B, nf),
                  in_specs=[blk(0), blk(nf), blk(0), pl.BlockSpec((3, LANES), lambda b, j: (0, j)),
                            pl.BlockSpec((1, LANES), lambda b, j: (0, j))],
                  out_specs=[blk(0), blk(0), pl.BlockSpec((1, 3, LANES), lambda b, j: (b, 0, j)),
                             pl.BlockSpec((1, 1, LANES), lambda b, j: (b, 0, j))],
                  out_shape=[jax.ShapeDtypeStruct((B * S, F), bf16), jax.ShapeDtypeStruct((B * S, F), bf16),
                             jax.ShapeDtypeStruct((B, 3, F), f32), jax.ShapeDtypeStruct((B, 1, F), f32)],
                  scratch_shapes=[pltpu.VMEM((S, LANES), f32)],
                  compiler_params=_params(("arbitrary", "arbitrary")))(u, u, da, w_conv, b_conv)


def _down_loss(a, w_down, xhat1, ln1_g, ln1_b, ln2_g, ln2_b, tgt, tm):
    T, D = xhat1.shape
    F = a.shape[1]
    tm = _tile(T, tm, 8)

    def body(a_ref, w_ref, xh_ref, g1_ref, b1_ref, g2_ref, b2_ref, t_ref, dr_ref, st_ref):
        @pl.when(pl.program_id(0) == 0)
        def _():
            st_ref[...] = jnp.zeros_like(st_ref)
        x1 = xh_ref[...] * g1_ref[...] + b1_ref[...]
        r = ALPHA * x1 + _dot(a_ref[...], w_ref[...])
        mean = jnp.mean(r, axis=1, keepdims=True)
        cen = r - mean
        rstd = lax.rsqrt(jnp.mean(cen * cen, axis=1, keepdims=True) + LN_EPS)
        xh = cen * rstd
        err = (xh * g2_ref[...] + b2_ref[...]) - t_ref[...]
        dy = err * (1.0 / D)
        dxh = dy * g2_ref[...]
        m1 = jnp.mean(dxh, axis=1, keepdims=True)
        m2 = jnp.mean(dxh * xh, axis=1, keepdims=True)
        dr_ref[...] = rstd * (dxh - m1 - xh * m2)
        st_ref[0:1, :] += jnp.sum(dy * xh, axis=0, keepdims=True)
        st_ref[1:2, :] += jnp.sum(dy, axis=0, keepdims=True)
        st_ref[2:3, :] += jnp.sum(err * err, axis=0, keepdims=True)

    rows = lambda w: pl.BlockSpec((tm, w), lambda i: (i, 0))
    vec = pl.BlockSpec((1, D), lambda i: (0, 0))
    return _pcall(body, name="down_loss", grid=(T // tm,),
                  in_specs=[rows(F), pl.BlockSpec((F, D), lambda i: (0, 0)), rows(D), vec, vec, vec, vec, rows(D)],
                  out_specs=[rows(D), pl.BlockSpec((8, D), lambda i: (0, 0))],
                  out_shape=[jax.ShapeDtypeStruct((T, D), f32), jax.ShapeDtypeStruct((8, D), f32)],
                  compiler_params=_params(("arbitrary",)))(a, w_down, xhat1, ln1_g, ln1_b, ln2_g, ln2_b, tgt)


def _local_step(x, tgt, w_in, b_in, ln1_g, ln1_b, b_conv, ln2_g, ln2_b, late_shards, late_weights, reducer=None,
                tail=None, tq=512, tm=256):
    B, S, D = x.shape
    T = B * S
    E = (w_in.shape[1] - 2 * D) * HEAD_DIM // (6 * HEAD_DIM + 1)
    n_pairs = E // LANES
    NH = E // HEAD_DIM
    x2 = x.reshape(T, D)
    tgt2 = tgt.reshape(T, D)
    tq = _tile(S, tq, 8)

    c_f, c_g = 6 * E, 6 * E + NH
    w_qkv = w_in[:, :c_f]
    qscale = jnp.concatenate([jnp.full((E,), Q_SCALE, f32), jnp.ones((2 * E,), f32)] * 2)
    w_qkv_s = (w_qkv.astype(f32) * qscale).astype(bf16)
    b_qkv_s = b_in[:, :c_f] * qscale
    w_f = jnp.pad(w_in[:, c_f:c_g], ((0, 0), (0, LANES - NH)))
    b_f = jnp.pad(b_in[:, c_f:c_g], ((0, 0), (0, LANES - NH)))
    w_g = w_in[:, c_g:]
    b_g = b_in[:, c_g:]

    xb = x2.astype(bf16)
    qkv = _matmul(xb, w_qkv_s, bias=b_qkv_s, out_dtype=bf16, tm=1024, tn=1536, name="proj_qkv")
    g = _matmul(xb, w_g, bias=b_g, tm=1024, tn=1024, name="proj_gate")
    fl = _matmul(xb, w_f, bias=b_f, tm=1024, name="proj_forget")
    c = _cumlogf(fl, B, S)
    cT = c.reshape(B, S, LANES)[:, :, :NH].transpose(0, 2, 1).reshape(B * NH, 1, S)
    P = n_pairs
    o_sb, lt_sb, stacks = _sb_fwd(qkv, B, S, P, 0, P, 2 * P, tq, late_shards)
    wp_sb, wp_fx, w_out, w_up, w_conv, w_down = late_weights(stacks)
    F = w_down.shape[0]
    o_fx, o_fx32, lse_fx = _fox_fwd(qkv, c, cT, B, S, P, 3 * P, 4 * P, 5 * P, tq)
    xhat1, rstd1, x1b, merged = _mix_fwd(o_sb, o_fx, g, x2, wp_sb, wp_fx, w_out, ln1_g, ln1_b, tm)
    u = _matmul(x1b, w_up, tm=1024, tn=1408, name="ffn_up")
    a = _glu_fwd(u, w_conv, b_conv, B, S)
    dr2, st2 = _down_loss(a, w_down, xhat1, ln1_g, ln1_b, ln2_g, ln2_b, tgt2, tm)

    grads = {}
    grads["ln2_g"], grads["ln2_b"] = st2[0:1], st2[1:2]
    sq_err = st2[2:3]
    da = _matmul(dr2, w_down, tb=True, tm=1024, tn=1408, name="ffn_da")
    grads["w_down"] = _matmul(a, dr2, ta=True, tm=1408, tn=1024, tk=1024, name="grad_w_down")
    du_g, du_v, gwc, gbc = _glu_bwd(u, da, w_conv, b_conv, B, S)
    grads["w_conv"] = jnp.sum(gwc, axis=0)
    grads["b_conv"] = jnp.sum(gbc, axis=0)
    grads["w_up"] = jnp.concatenate(
        [_matmul(x1b, du_g, ta=True, tm=512, tn=2816, tk=1024, name="grad_w_up_gate"),
         _matmul(x1b, du_v, ta=True, tm=512, tn=2816, tk=1024, name="grad_w_up_val")], axis=1)
    dx1 = _matmul(du_g, w_up[:, :F], tb=True, tm=1024, tn=1024, tk=F, name="ffn_dx_gate")
    dx1 = _matmul(du_v, w_up[:, F:], tb=True, addend=dx1, tm=1024, tn=1024, tk=F, name="ffn_dx_val")
    dr1, st1 = _ln_bwd(dx1, dr2, ALPHA, xhat1, rstd1, ln1_g, tm)
    grads["ln1_g"], grads["ln1_b"] = st1[0:1], st1[1:2]

    dy_sb, dy_fx, dg, do_sb, do_fx, gsum_sb, gsum_fx = _mix_bwd(dr1, o_sb, o_fx, g, wp_sb, wp_fx, w_out, tm)
    grads["w_out"] = _matmul(merged, dr1, ta=True, tm=512, tn=1024, tk=1024, name="grad_w_out")
    grads["w_proj_sb"] = _matmul(o_sb, dy_sb, ta=True, tm=512, tn=1024, tk=1024, name="grad_w_proj_sb")
    grads["w_proj_fox"] = _matmul(o_fx, dy_fx, ta=True, tm=512, tn=1024, tk=1024, name="grad_w_proj_fox")
    (dq_f, dk_f, dv_f, dcT), got = _fox_bwd(qkv, c, cT, do_fx, o_fx32, lse_fx, B, S, P, 3 * P, 4 * P, 5 * P, tq,
                                            reducer.to_sibling(grads) if reducer else None)
    (dq_s, dk_s, dv_s), got = _sb_bwd(qkv, do_sb, lt_sb, B, S, P, 0, P, 2 * P, tq, reducer.to_chips(got) if reducer else None)
    if reducer:
        reducer.from_chips(got)
    dc = jnp.pad(dcT.reshape(B, NH, S).transpose(0, 2, 1), ((0, 0), (0, 0), (0, LANES - NH))).reshape(T, LANES)
    dfl = _cumlogf_bwd(dc, fl, B, S)
    dqkv = jnp.concatenate([dq_s, dk_s, dv_s, dq_f, dk_f, dv_f], axis=1)
    gw_qkv, gb_qkv = _matmul(x2, dqkv, ta=True, colsum=True, tm=512, tn=3072, tk=512, name="grad_w_qkv")
    gw_f, gb_f = _matmul(x2, dfl, ta=True, colsum=True, tm=512, tk=1024, name="grad_w_forget")
    gw_g = _matmul(x2, dg, ta=True, tm=512, tn=2048, tk=1024, name="grad_w_gate")
    grads["w_in"] = jnp.concatenate([gw_qkv, gw_f[:, :NH], gw_g], axis=1)
    grads["b_in"] = jnp.concatenate([gb_qkv, gb_f[:, :NH], gsum_sb[0:1], gsum_fx[0:1]], axis=1)
    dx = _matmul(dqkv, w_qkv, tb=True, addend=dr1, addend_scale=ALPHA, tm=512, tn=1024, tk=c_f, name="dx_qkv",
                 hosted=tail.to_sibling(grads, sq_err) if tail else None)
    if tail:
        dx, got = dx
        to_chips = tail.to_chips(got)
    dx = _matmul(dfl, w_f, tb=True, addend=dx, tm=1024, tn=1024, name="dx_forget")
    dx = _matmul(dg, w_g, tb=True, addend=dx, tm=1024, tn=1024, tk=2 * D, name="dx_gate", hosted=to_chips if tail else None)
    if tail:
        dx, got = dx
        tail.from_chips(got)
    return sq_err, dx.reshape(B, S, D), grads


_ANY = pl.BlockSpec(memory_space=pl.ANY)
_MESH = pl.DeviceIdType.MESH


def _pos():
    return lax.axis_index("x"), lax.axis_index("y"), lax.axis_index("c")


def _other_chips(x, y):
    return [(1 - x, y), (x, 1 - y), (1 - x, 1 - y)]


def _gather_shards(shards):
    n = len(shards)

    def body(*refs):
        start, finish = _gather_copies(refs[:n], refs[n:2 * n], *refs[2 * n:])
        start()
        finish()

    return _pcall(body, name="gather_weights", in_specs=[_ANY] * n, out_specs=[_ANY] * n,
                  out_shape=_gather_shapes(shards), scratch_shapes=_gather_sems(n),
                  compiler_params=pltpu.CompilerParams(has_side_effects=True))(*shards)


def _gather_shapes(shards):
    return [jax.ShapeDtypeStruct((4,) + s.shape, s.dtype) for s in shards]


def _gather_sems(n):
    return [pltpu.SemaphoreType.DMA((n, 3)), pltpu.SemaphoreType.DMA((n, 3)), pltpu.SemaphoreType.DMA((n,))]


def _gather_copies(srcs, dsts, send_sems, recv_sems, local_sems):
    n = len(srcs)
    x, y, c = _pos()
    k = 2 * x + y
    chips = _other_chips(x, y)

    def copy(a, j, chip, slot):
        return pltpu.make_async_remote_copy(src_ref=srcs[a], dst_ref=dsts[a].at[slot], send_sem=send_sems.at[a, j],
                                            recv_sem=recv_sems.at[a, j], device_id=(*chip, c), device_id_type=_MESH)

    def mine():
        return ([pltpu.make_async_copy(srcs[a], dsts[a].at[k], local_sems.at[a]) for a in range(n)],
                [copy(a, j, chip, k) for a in range(n) for j, chip in enumerate(chips)])

    def start():
        local, sends = mine()
        for cp in local + sends:
            cp.start()

    def finish():
        local, sends = mine()
        for a in range(n):
            for j, chip in enumerate(chips):
                copy(a, j, chip, 2 * chip[0] + chip[1]).wait_recv()
        for cp in sends:
            cp.wait_send()
        for cp in local:
            cp.wait()

    return start, finish


class _Hosted:
    def __init__(self, inputs, outputs, sems, copies):
        self.inputs, self.outputs, self.sems, self.copies = list(inputs), list(outputs), list(sems), copies


def _host(hosted, refs, n_in, n_out, step, last):
    if hosted is None:
        return refs[:n_in], refs[n_in:n_in + n_out], lambda: None
    hi, ho = len(hosted.inputs), len(hosted.outputs)
    own_in, h_in = refs[:n_in], refs[n_in:n_in + hi]
    own_out = refs[n_in + hi:n_in + hi + n_out]
    h_out = refs[n_in + hi + n_out:n_in + hi + n_out + ho]
    start, finish = hosted.copies(h_in, h_out, *refs[n_in + hi + n_out + ho:])
    pl.when(step == 0)(start)
    return own_in, own_out, lambda: pl.when(step == last)(finish)


def _host_call(body, name, grid, in_specs, args, out_specs, out_shape, hosted):
    h_in = hosted.inputs if hosted else []
    h_out = hosted.outputs if hosted else []
    res = _pcall(body, name=name, grid=grid, in_specs=list(in_specs) + [_ANY] * len(h_in),
                 out_specs=list(out_specs) + [_ANY] * len(h_out), out_shape=list(out_shape) + list(h_out),
                 scratch_shapes=hosted.sems if hosted else [],
                 compiler_params=_params(("arbitrary",) * len(grid)))(*args, *h_in)
    return list(res[:len(out_shape)]), list(res[len(out_shape):])


def _sibling_copies(shapes):
    n = len(shapes)

    def copies(p_refs, got_refs, send_sems, recv_sems):
        x, y, c = _pos()

        def copy(a):
            h = shapes[a][1] // 2
            src = p_refs[a].at[:, pl.ds(pl.multiple_of((1 - c) * h, 8), h), :]
            return pltpu.make_async_remote_copy(src_ref=src, dst_ref=got_refs[a], send_sem=send_sems.at[a],
                                                recv_sem=recv_sems.at[a], device_id=(x, y, 1 - c), device_id_type=_MESH)

        def start():
            for a in range(n):
                copy(a).start()

        def finish():
            for a in range(n):
                copy(a).wait()

        return start, finish

    return copies


def _hosted_sibling(pieces):
    n = len(pieces)
    return _Hosted(pieces, [jax.ShapeDtypeStruct((4, p.shape[1] // 2, p.shape[2]), p.dtype) for p in pieces],
                   [pltpu.SemaphoreType.DMA((n,)), pltpu.SemaphoreType.DMA((n,))], _sibling_copies([p.shape for p in pieces]))


def _chips_copies(n):
    def copies(p_refs, got_refs, send_sems, recv_sems):
        x, y, c = _pos()
        k = 2 * x + y
        chips = _other_chips(x, y)

        def copy(a, j, chip, piece, slot):
            return pltpu.make_async_remote_copy(src_ref=p_refs[a].at[piece], dst_ref=got_refs[a].at[slot],
                                                send_sem=send_sems.at[a, j], recv_sem=recv_sems.at[a, j],
                                                device_id=(*chip, c), device_id_type=_MESH)

        def start():
            for a in range(n):
                for j, chip in enumerate(chips):
                    copy(a, j, chip, 2 * chip[0] + chip[1], k).start()

        def finish():
            for a in range(n):
                for j, chip in enumerate(chips):
                    copy(a, j, chip, k, 2 * chip[0] + chip[1]).wait_recv()
            for a in range(n):
                for j, chip in enumerate(chips):
                    copy(a, j, chip, 2 * chip[0] + chip[1], k).wait_send()

        return start, finish

    return copies


def _hosted_chips(pieces):
    n = len(pieces)
    return _Hosted(pieces, [jax.ShapeDtypeStruct(p.shape, p.dtype) for p in pieces],
                   [pltpu.SemaphoreType.DMA((n, 3)), pltpu.SemaphoreType.DMA((n, 3))], _chips_copies(n))


def _small_copies(sm_ref, sg_ref, send_sems, recv_sems, local_sem):
    x, y, c = _pos()
    me = 4 * x + 2 * y + c
    flip = lambda v, bit: 1 - v if bit else v
    peers = [(flip(x, r & 4), flip(y, r & 2), flip(c, r & 1)) for r in range(1, 8)]

    def copy(j, slot):
        return pltpu.make_async_remote_copy(src_ref=sm_ref, dst_ref=sg_ref.at[slot], send_sem=send_sems.at[j],
                                            recv_sem=recv_sems.at[j], device_id=peers[j], device_id_type=_MESH)

    def local():
        return pltpu.make_async_copy(sm_ref, sg_ref.at[me], local_sem)

    def start():
        local().start()
        for j in range(7):
            copy(j, me).start()

    def finish():
        for j, (px, py, pc) in enumerate(peers):
            copy(j, 4 * px + 2 * py + pc).wait_recv()
        for j in range(7):
            copy(j, me).wait_send()
        local().wait()

    return start, finish


def _hosted_sibling_and_small(pieces, small):
    n = len(pieces)
    sibling = _hosted_sibling(pieces)

    def copies(in_refs, out_refs, big_send, big_recv, send_sems, recv_sems, local_sem):
        start_big, finish_big = sibling.copies(in_refs[:n], out_refs[:n], big_send, big_recv)
        start_small, finish_small = _small_copies(in_refs[n], out_refs[n], send_sems, recv_sems, local_sem)

        def start():
            start_big()
            start_small()

        def finish():
            finish_small()
            finish_big()

        return start, finish

    return _Hosted(pieces + [small], sibling.outputs + [jax.ShapeDtypeStruct((8,) + small.shape, small.dtype)],
                   sibling.sems + [pltpu.SemaphoreType.DMA((7,)), pltpu.SemaphoreType.DMA((7,)), pltpu.SemaphoreType.DMA(())],
                   copies)


def _share_halves(shards):
    n = len(shards)

    def body(*refs):
        full_refs = refs[n:2 * n]
        send_sems, recv_sems = refs[2 * n:]
        x, y, c = _pos()

        def copy(a, half):
            h = shards[a].shape[0] // 2
            rows = full_refs[a].at[pl.ds(pl.multiple_of(half * h, 8), h), :]
            return pltpu.make_async_remote_copy(src_ref=rows, dst_ref=rows, send_sem=send_sems.at[a],
                                                recv_sem=recv_sems.at[a], device_id=(x, y, 1 - c), device_id_type=_MESH)

        sends = [copy(a, c) for a in range(n)]
        for cp in sends:
            cp.start()
        for a in range(n):
            copy(a, 1 - c).wait_recv()
        for cp in sends:
            cp.wait_send()

    return _pcall(body, name="share_halves", in_specs=[_ANY] * n, out_specs=[_ANY] * n,
                  out_shape=[jax.ShapeDtypeStruct(s.shape, s.dtype) for s in shards],
                  input_output_aliases={a: a for a in range(n)},
                  scratch_shapes=[pltpu.SemaphoreType.DMA((n,)), pltpu.SemaphoreType.DMA((n,))],
                  compiler_params=pltpu.CompilerParams(has_side_effects=True))(*shards)


def _add_own_half(piece, got, core, name):
    _, r, cols = piece.shape
    h = r // 2

    def body(c_ref, a_ref, b_ref, o_ref, o16_ref):
        s = a_ref[0] + b_ref[...]
        o_ref[...] = s
        o16_ref[...] = s.astype(bf16)

    out = pl.BlockSpec((1, h, cols), lambda k, c: (k, 0, 0))
    grid_spec = pltpu.PrefetchScalarGridSpec(
        num_scalar_prefetch=1, grid=(4,),
        in_specs=[pl.BlockSpec((1, 1, h, cols), lambda k, c: (k, c[0], 0, 0)), out], out_specs=[out, out])
    return _pcall(body, name=name, grid_spec=grid_spec,
                  out_shape=[jax.ShapeDtypeStruct((4, h, cols), f32), jax.ShapeDtypeStruct((4, h, cols), bf16)],
                  compiler_params=_params(("arbitrary",)))(core, piece.reshape(4, 2, h, cols), got)


def _sum_chips(own, got, where, name):
    _, h, cols = own.shape
    t = _tile(h, 128, 16)
    nt = h // t

    def body(w_ref, own_ref, gx_ref, gy_ref, gxy_ref, o_ref):
        o_ref[...] = ((own_ref[0] + gx_ref[0].astype(f32)) + gy_ref[0].astype(f32)) + gxy_ref[0].astype(f32)

    slot = lambda j: pl.BlockSpec((1, t, cols), lambda i, w: (w[j], i, 0))
    grid_spec = pltpu.PrefetchScalarGridSpec(
        num_scalar_prefetch=1, grid=(nt,),
        in_specs=[slot(0), slot(2), slot(3), slot(4)],
        out_specs=pl.BlockSpec((t, cols), lambda i, w: (w[1] * nt + i, 0)))
    return _pcall(body, name=name, grid_spec=grid_spec, out_shape=jax.ShapeDtypeStruct((2 * h, cols), f32),
                  compiler_params=_params(("arbitrary",)))(where, own, got, got, got)


def _sum_slots(stack, name):
    k, n, cols = stack.shape
    t = _tile(n, 128, 8)

    def body(s_ref, o_ref):
        acc = s_ref[0]
        for i in range(1, k):
            acc = acc + s_ref[i]
        o_ref[...] = acc

    return _pcall(body, name=name, grid=(n // t,), in_specs=[pl.BlockSpec((k, t, cols), lambda i: (0, i, 0))],
                  out_specs=pl.BlockSpec((t, cols), lambda i: (i, 0)), out_shape=jax.ShapeDtypeStruct((n, cols), f32),
                  compiler_params=_params(("arbitrary",)))(stack)


def _adamw(w, g, m, v, name):
    n, cols = w.shape
    t = _tile(n, 128, 8)
    c1 = 1.0 - ADAM_B1 ** ADAM_STEP
    c2 = 1.0 - ADAM_B2 ** ADAM_STEP

    def body(w_ref, g_ref, m_ref, v_ref, d_ref, nm_ref, nv_ref, g_out_ref):
        g = g_ref[...]
        nm = ADAM_B1 * m_ref[...] + (1.0 - ADAM_B1) * g
        nv = ADAM_B2 * v_ref[...] + (1.0 - ADAM_B2) * (g * g)
        d_ref[...] = -ADAM_LR * ((nm / c1) / (jnp.sqrt(nv / c2) + ADAM_EPS) + ADAM_WD * w_ref[...])
        nm_ref[...] = nm
        nv_ref[...] = nv
        g_out_ref[...] = g

    spec = pl.BlockSpec((t, cols), lambda i: (i, 0))
    shp = jax.ShapeDtypeStruct((n, cols), f32)
    return _pcall(body, name=name, grid=(n // t,), in_specs=[spec] * 4, out_specs=[spec] * 4, out_shape=[shp] * 4,
                  compiler_params=_params(("arbitrary",)))(w, g, m, v)


_MATS = (("w_in", 1), ("w_proj_sb", 1), ("w_proj_fox", 1), ("w_out", 0), ("w_up", 1), ("w_down", 0))
_SMALL = ("b_in", "ln1_g", "ln1_b", "b_conv", "ln2_g", "ln2_b")


def _pad_lanes(v):
    n = v.shape[-1]
    return jnp.pad(v, ((0, 0), (0, (-n) % LANES)))


def _pack_rows(vectors):
    flat = jnp.concatenate([_pad_lanes(v.reshape(1, -1)) for v in vectors], axis=1).reshape(-1, LANES)
    return jnp.pad(flat, ((0, (-flat.shape[0]) % 8), (0, 0)))


def _unpack_rows(packed, sizes):
    out, r = [], 0
    for n in sizes:
        rows = -(-n // LANES)
        out.append(packed[r:r + rows].reshape(1, rows * LANES)[:, :n])
        r += rows
    return out


def _unstack(stack, axis):
    if axis == 0:
        return stack.reshape(-1, stack.shape[2])
    return jnp.concatenate([stack[k] for k in range(4)], axis=1)


def _pieces(g, axis):
    if axis == 0:
        return g.reshape(4, g.shape[0] // 4, g.shape[1])
    cols = g.shape[1] // 4
    return jnp.stack([g[:, k * cols:(k + 1) * cols] for k in range(4)])


class _Reducer:
    def __init__(self, mats, core, where, small=None):
        self.mats, self.core, self.where, self.small = mats, core, where, small

    def to_sibling(self, grads, sq_err=None):
        self.local = [_pieces(grads[n], axis) for n, axis in self.mats]
        if self.small is None:
            return _hosted_sibling(self.local)
        return _hosted_sibling_and_small(self.local, self.small(grads, sq_err))

    def to_chips(self, got):
        if self.small is not None:
            got, self.small_all = got[:-1], got[-1]
        self.sums = [_add_own_half(p, g, self.core, "add_sibling_" + n) for (n, _), p, g in zip(self.mats, self.local, got)]
        return _hosted_chips([s16 for _, s16 in self.sums])

    def from_chips(self, got):
        self.halves = [_sum_chips(s32, r16, self.where, "sum_chips_" + n)
                       for (n, _), (s32, _), r16 in zip(self.mats, self.sums, got)]


def kernel(x, w_in, b_in, w_proj_sb, w_proj_fox, w_out, ln1_g, ln1_b, w_up, w_conv, b_conv, w_down, ln2_g, ln2_b, loss_target, m_w_in, m_b_in, m_w_proj_sb, m_w_proj_fox, m_w_out, m_ln1_g, m_ln1_b, m_w_up, m_w_conv, m_b_conv, m_w_down, m_ln2_g, m_ln2_b, v_w_in, v_b_in, v_w_proj_sb, v_w_proj_fox, v_w_out, v_ln1_g, v_ln1_b, v_w_up, v_w_conv, v_b_conv, v_w_down, v_ln2_g, v_ln2_b):
    w = dict(w_in=w_in, b_in=b_in, w_proj_sb=w_proj_sb, w_proj_fox=w_proj_fox, w_out=w_out, ln1_g=ln1_g, ln1_b=ln1_b,
             w_up=w_up, w_conv=w_conv, b_conv=b_conv, w_down=w_down, ln2_g=ln2_g, ln2_b=ln2_b)
    m = dict(w_in=m_w_in, b_in=m_b_in, w_proj_sb=m_w_proj_sb, w_proj_fox=m_w_proj_fox, w_out=m_w_out, ln1_g=m_ln1_g,
             ln1_b=m_ln1_b, w_up=m_w_up, w_conv=m_w_conv, b_conv=m_b_conv, w_down=m_w_down, ln2_g=m_ln2_g, ln2_b=m_ln2_b)
    v = dict(w_in=v_w_in, b_in=v_b_in, w_proj_sb=v_w_proj_sb, w_proj_fox=v_w_proj_fox, w_out=v_w_out, ln1_g=v_ln1_g,
             ln1_b=v_ln1_b, w_up=v_w_up, w_conv=v_w_conv, b_conv=v_b_conv, w_down=v_w_down, ln2_g=v_ln2_g, ln2_b=v_ln2_b)
    order = ["w_in", "b_in", "w_proj_sb", "w_proj_fox", "w_out", "ln1_g", "ln1_b", "w_up", "w_conv", "b_conv", "w_down",
             "ln2_g", "ln2_b"]
    x_idx, y_idx, c_idx = _pos()
    chip = 2 * x_idx + y_idx
    D = x.shape[-1]
    core = c_idx.astype(jnp.int32).reshape(1)

    w_in_full = _unstack(_gather_shards([w["w_in"][0].astype(bf16)])[0], 1)
    late = (("w_proj_sb", 1), ("w_proj_fox", 1), ("w_out", 0), ("w_up", 1), ("w_conv", 1), ("w_down", 0))
    late_shards = [w[n][0] if n == "w_conv" else w[n][0].astype(bf16) for n, _ in late]
    late_weights = lambda stacks: [_unstack(s, axis) for (_, axis), s in zip(late, stacks)]

    where = jnp.stack([chip, c_idx, 2 * (1 - x_idx) + y_idx, 2 * x_idx + 1 - y_idx, 2 * (1 - x_idx) + 1 - y_idx]).astype(jnp.int32)
    small_names = list(_SMALL) + ["w_conv"]
    pack_small = lambda grads, sq_err: _pack_rows(
        [jnp.full((1, 1), (0.5 / D) * jnp.sum(sq_err), f32)] + [grads[n] for n in small_names])
    early = _Reducer(_MATS[1:], core, where)
    last = _Reducer(_MATS[:1], core, where, pack_small)
    sq_err, grad_x, grads = _local_step(x, loss_target, w_in_full, w["b_in"], w["ln1_g"], w["ln1_b"], w["b_conv"],
                                        w["ln2_g"], w["ln2_b"], late_shards, late_weights, early, last)
    g_shards = _share_halves(last.halves + early.halves)
    small_sum = _sum_slots(last.small_all, "sum_small")

    out = {"grad": {}, "delta": {}, "m": {}, "v": {}}
    for (n, _), g_ in zip(_MATS, g_shards):
        d_, m_, v_, g_out = _adamw(w[n][0], g_, m[n][0], v[n][0], "adamw_" + n)
        for key, t in (("grad", g_out), ("delta", d_), ("m", m_), ("v", v_)):
            out[key][n] = t.reshape(w[n].shape)
    sizes = [1] + [int(grads[n].size) for n in small_names]
    sm = _unpack_rows(small_sum, sizes)
    loss = sm[0][0, 0]
    g_small = dict(zip(small_names, sm[1:]))
    F4 = w["w_conv"].shape[-1]
    g_small["w_conv"] = lax.dynamic_slice_in_dim(g_small["w_conv"].reshape(3, -1), chip * F4, F4, axis=1)
    pack_s = lambda d: _pack_rows([d[n].reshape(1, -1) for n in small_names])
    gs_packed = _pack_rows([g_small[n].reshape(1, -1) for n in small_names])
    s_delta, s_m, s_v, _ = _adamw(pack_s(w), gs_packed, pack_s(m), pack_s(v), "adamw_small")
    s_sizes = [int(w[n].size) for n in small_names]

    for key, packed_s in (("grad", gs_packed), ("delta", s_delta), ("m", s_m), ("v", s_v)):
        for n, t in zip(small_names, _unpack_rows(packed_s, s_sizes)):
            out[key][n] = t.reshape(w[n].shape)
    return (loss, grad_x, *[out["grad"][n] for n in order], *[out["delta"][n] for n in order],
            *[out["m"][n] for n in order], *[out["v"][n] for n in order])
```

```python
import functools
import math

import jax
import jax.numpy as jnp
from jax import lax
from jax.experimental import pallas as pl
from jax.experimental.pallas import tpu as pltpu

f32, bf16 = jnp.float32, jnp.bfloat16

HEAD_DIM = 64
LANES = 128
LN_EPS = 1e-5
ALPHA = 2.0 ** 0.25
Q_SCALE = HEAD_DIM ** -0.5
ADAM_LR, ADAM_B1, ADAM_B2, ADAM_EPS, ADAM_WD, ADAM_STEP = 0.001, 0.9, 0.999, 1e-08, 0.01, 10
VMEM_LIMIT = 56 * 1024 * 1024
NEG = -1e30

_pcall = pl.pallas_call
_NT = (((1,), (1,)), ((), ()))
_TN = (((0,), (0,)), ((), ()))


def _params(sem=None):
    return pltpu.CompilerParams(dimension_semantics=sem, vmem_limit_bytes=VMEM_LIMIT)


def _tile(dim, target, unit=LANES):
    if dim <= target:
        return dim
    t = (target // unit) * unit
    while t > unit and dim % t:
        t -= unit
    assert dim % t == 0, (dim, target)
    return t


def _dot(a, b, dn=None):
    if dn is None:
        return jnp.dot(a, b, preferred_element_type=f32)
    return lax.dot_general(a, b, dn, preferred_element_type=f32)


def _split_dot(x, tri):
    hi = x.astype(bf16)
    lo = (x - hi.astype(f32)).astype(bf16)
    return _dot(hi, tri) + _dot(lo, tri)


SCAN_BLOCK = 256


def _scan_cols(x, tri, reverse, split=True):
    cb = tri.shape[0]
    nb = x.shape[1] // cb
    blocks = [x[:, b * cb:(b + 1) * cb] for b in range(nb)]
    outs, run = [None] * nb, None
    for b in (reversed(range(nb)) if reverse else range(nb)):
        o = _split_dot(blocks[b], tri) if split else _dot(blocks[b].astype(bf16), tri)
        s = jnp.sum(blocks[b], axis=1, keepdims=True)
        outs[b] = o if run is None else o + run
        run = s if run is None else run + s
    return (outs[0] if nb == 1 else jnp.concatenate(outs, axis=1)), run


def _tri(cb, rel):
    row = lax.broadcasted_iota(jnp.int32, (cb, cb), 0)
    col = lax.broadcasted_iota(jnp.int32, (cb, cb), 1)
    return rel(row, col).astype(bf16)


def _matmul(a, b, *, name, ta=False, tb=False, bias=None, addend=None, addend_scale=1.0, colsum=False,
            out_dtype=f32, tm=512, tn=512, tk=1024, hosted=None):
    M, K = (a.shape[1], a.shape[0]) if ta else a.shape
    N = b.shape[0] if tb else b.shape[1]
    assert K == (b.shape[1] if tb else b.shape[0])
    assert not (colsum and tb)
    tm, tn, tk = _tile(M, tm), _tile(N, tn), _tile(K, tk)
    nk = K // tk
    n_in = 2 + (bias is not None) + (addend is not None)
    n_out = 1 + colsum
    grid = (M // tm, N // tn, nk)

    def body(*refs):
        k = pl.program_id(2)
        step = (pl.program_id(0) * grid[1] + pl.program_id(1)) * nk + k
        scratch = refs[len(refs) - n_out - (len(hosted.sems) if hosted else 0):]
        own_in, own_out, finish = _host(hosted, refs[:len(refs) - len(scratch)] + scratch[n_out:], n_in, n_out, step,
                                        grid[0] * grid[1] * nk - 1)
        a_ref, b_ref = own_in[0], own_in[1]
        bias_ref = own_in[2] if bias is not None else None
        add_ref = own_in[n_in - 1] if addend is not None else None
        o_ref = own_out[0]
        cs_ref = own_out[1] if colsum else None
        acc = scratch[0]
        cs_acc = scratch[1] if colsum else None

        @pl.when(k == 0)
        def _():
            acc[...] = jnp.zeros_like(acc)
            if colsum:
                cs_acc[...] = jnp.zeros_like(cs_acc)

        dn = (((0 if ta else 1,), (1 if tb else 0,)), ((), ()))
        acc[...] += lax.dot_general(a_ref[...].astype(bf16), b_ref[...].astype(bf16), dn, preferred_element_type=f32)
        if colsum:
            cs_acc[...] += jnp.sum(b_ref[...].astype(f32), axis=0, keepdims=True)

        @pl.when(k == nk - 1)
        def _():
            r = acc[...]
            if bias is not None:
                r = r + bias_ref[...]
            if addend is not None:
                r = r + addend_scale * add_ref[...].astype(f32)
            o_ref[...] = r.astype(out_dtype)
            if colsum:
                cs_ref[0] = cs_acc[...]

        finish()

    a_spec = pl.BlockSpec((tk, tm), lambda i, j, k: (k, i)) if ta else pl.BlockSpec((tm, tk), lambda i, j, k: (i, k))
    b_spec = pl.BlockSpec((tn, tk), lambda i, j, k: (j, k)) if tb else pl.BlockSpec((tk, tn), lambda i, j, k: (k, j))
    in_specs, args = [a_spec, b_spec], [a, b]
    if bias is not None:
        in_specs.append(pl.BlockSpec((1, tn), lambda i, j, k: (0, j)))
        args.append(bias.reshape(1, N).astype(f32))
    if addend is not None:
        in_specs.append(pl.BlockSpec((tm, tn), lambda i, j, k: (i, j)))
        args.append(addend)
    out_shape = [jax.ShapeDtypeStruct((M, N), out_dtype)]
    out_specs = [pl.BlockSpec((tm, tn), lambda i, j, k: (i, j))]
    scratch = [pltpu.VMEM((tm, tn), f32)]
    if colsum:
        out_shape.append(jax.ShapeDtypeStruct((M // tm, 1, N), f32))
        out_specs.append(pl.BlockSpec((1, 1, tn), lambda i, j, k: (i, 0, j)))
        scratch.append(pltpu.VMEM((1, tn), f32))
    h_in = hosted.inputs if hosted else []
    h_out = hosted.outputs if hosted else []
    res = _pcall(body, name=name, grid=grid, in_specs=in_specs + [_ANY] * len(h_in),
                 out_specs=out_specs + [_ANY] * len(h_out), out_shape=out_shape + h_out,
                 scratch_shapes=scratch + (hosted.sems if hosted else []),
                 compiler_params=_params(("arbitrary", "arbitrary", "arbitrary")))(*args, *h_in)
    own = (res[0], res[1][0]) if colsum else res[0]
    return (own, list(res[n_out:])) if hosted else own


def _cumlogf(fl, B, S):
    t = _tile(S, 256, 8)

    def body(fl_ref, c_ref, carry):
        @pl.when(pl.program_id(1) == 0)
        def _():
            carry[...] = jnp.zeros_like(carry)
        z = fl_ref[...]
        ls = jnp.minimum(z, 0.0) - jnp.log(1.0 + jnp.exp(-jnp.abs(z)))
        row = lax.broadcasted_iota(jnp.int32, (t, t), 0)
        col = lax.broadcasted_iota(jnp.int32, (t, t), 1)
        lower = (col <= row).astype(f32)
        c = jnp.dot(lower, ls, precision=lax.Precision.HIGHEST, preferred_element_type=f32) + carry[...]
        c_ref[...] = c
        carry[...] = c[t - 1:t, :]

    return _pcall(body, name="cumlogf", grid=(B, S // t),
                  in_specs=[pl.BlockSpec((t, LANES), lambda b, i: (b * (S // t) + i, 0))],
                  out_specs=pl.BlockSpec((t, LANES), lambda b, i: (b * (S // t) + i, 0)),
                  out_shape=jax.ShapeDtypeStruct(fl.shape, f32), scratch_shapes=[pltpu.VMEM((1, LANES), f32)],
                  compiler_params=_params(("arbitrary", "arbitrary")))(fl)


def _cumlogf_bwd(dc, fl, B, S):
    t = _tile(S, 256, 8)
    n = S // t

    def body(dc_ref, fl_ref, o_ref, carry):
        @pl.when(pl.program_id(1) == 0)
        def _():
            carry[...] = jnp.zeros_like(carry)
        row = lax.broadcasted_iota(jnp.int32, (t, t), 0)
        col = lax.broadcasted_iota(jnp.int32, (t, t), 1)
        upper = (col >= row).astype(f32)
        r = jnp.dot(upper, dc_ref[...], precision=lax.Precision.HIGHEST, preferred_element_type=f32) + carry[...]
        carry[...] = r[0:1, :]
        z = fl_ref[...]
        o_ref[...] = r / (1.0 + jnp.exp(z))

    spec = pl.BlockSpec((t, LANES), lambda b, i: (b * n + n - 1 - i, 0))
    return _pcall(body, name="cumlogf_bwd", grid=(B, n), in_specs=[spec, spec], out_specs=spec,
                  out_shape=jax.ShapeDtypeStruct(fl.shape, f32), scratch_shapes=[pltpu.VMEM((1, LANES), f32)],
                  compiler_params=_params(("arbitrary", "arbitrary")))(dc, fl)


def _head_masks():
    lane = lax.broadcasted_iota(jnp.int32, (1, LANES), 1)
    return lane < HEAD_DIM


def _by_head(m0, t):
    z = jnp.zeros_like(t)
    return [jnp.where(m0, t, z), jnp.where(m0, z, t)]


def _sb_terms(z):
    softplus = jnp.maximum(z, 0.0) + jnp.log(1.0 + jnp.exp(-jnp.abs(z)))
    return z - softplus, -softplus


STRIP_ROWS = 32


def _strip_rows(tq):
    return STRIP_ROWS if tq % STRIP_ROWS == 0 else tq


def _strict(r, rs, tq):
    row = lax.broadcasted_iota(jnp.int32, (rs, tq), 0) + r
    col = lax.broadcasted_iota(jnp.int32, (rs, tq), 1)
    return col < row


def _score_scratch(tq, n_f32, n_bf16, n_sums):
    return ([pltpu.VMEM((tq, tq), f32)] * (2 * n_f32) + [pltpu.VMEM((tq, tq), bf16)] * (2 * n_bf16)
            + [pltpu.VMEM((tq, LANES), f32)] * (2 * n_sums))


def _by_pairs(refs):
    return [refs[i:i + 2] for i in range(0, len(refs), 2)]


def _sb_fwd(qkv, B, S, n_pairs, qcol, kcol, vcol, tq, shards=()):
    nq = S // tq
    T = B * S
    n = len(shards)

    cb = min(tq, SCAN_BLOCK)
    nb = tq // cb
    rs = _strip_rows(tq)

    def body(q_ref, k_ref, v_ref, *rest):
        o_ref, lt_ref = rest[n], rest[n + 1]
        z_s, suf_s, hi_s, lo_s, w_s, sum_s = _by_pairs(rest[len(rest) - 12:])
        i = pl.program_id(2)
        if n:
            start, finish = _gather_copies(rest[:n], rest[n + 2:2 * n + 2], *rest[2 * n + 2:2 * n + 5])
            step = (pl.program_id(0) * n_pairs + pl.program_id(1)) * nq + i
            pl.when(step == 0)(start)
        m0 = _head_masks()
        qh = _by_head(m0, q_ref[...])
        later = _tri(cb, lambda j, s: j > s)
        lane = lax.broadcasted_iota(jnp.int32, (1, LANES), 1)

        def tile(s0, R, acc, diag):
            kb = k_ref[pl.ds(s0, tq), :]
            vh = _by_head(m0, v_ref[pl.ds(s0, tq), :])
            R = list(R)
            for h in range(2):
                z_s[h][...] = _dot(qh[h], kb, _NT)
            for h in range(2):
                for r in range(0, tq, rs):
                    lb, l1m = _sb_terms(z_s[h][r:r + rs, :])
                    z_s[h][r:r + rs, :] = lb
                    if diag:
                        l1m = jnp.where(_strict(r, rs, tq), l1m, 0.0)
                    hi = l1m.astype(bf16)
                    hi_s[h][r:r + rs, :] = hi
                    lo_s[h][r:r + rs, :] = (l1m - hi.astype(f32)).astype(bf16)
                    sums = jnp.zeros((rs, LANES), f32)
                    for b in range(nb):
                        sums = jnp.where(lane == b, jnp.sum(l1m[:, b * cb:(b + 1) * cb], axis=1, keepdims=True), sums)
                    sum_s[h][r:r + rs, :] = sums
            for h in range(2):
                for b in range(nb):
                    blk = slice(b * cb, (b + 1) * cb)
                    suf_s[h][:, blk] = _dot(hi_s[h][:, blk], later) + _dot(lo_s[h][:, blk], later)
            for h in range(2):
                for r in range(0, tq, rs):
                    sums = sum_s[h][r:r + rs, :]
                    after = R[h][r:r + rs]
                    for b in reversed(range(nb)):
                        blk = slice(b * cb, (b + 1) * cb)
                        w = jnp.exp(z_s[h][r:r + rs, blk] + (suf_s[h][r:r + rs, blk] + after))
                        if diag:
                            w = jnp.where(_strict(r, rs, tq)[:, blk], w, 0.0)
                        w_s[h][r:r + rs, blk] = w.astype(bf16)
                        after = after + sums[:, b:b + 1]
            for h in range(2):
                acc = acc + _dot(w_s[h][...], vh[h])
                R[h] = R[h] + jnp.sum(sum_s[h][...], axis=1, keepdims=True)
            return R, acc

        zero = jnp.zeros((tq, 1), f32)
        R, acc = tile(pl.multiple_of(i * tq, tq), [zero, zero], jnp.zeros((tq, LANES), f32), True)

        def loop(n, carry):
            s0 = pl.multiple_of((i - 1 - n) * tq, tq)
            R, acc = tile(s0, carry[:2], carry[2], False)
            return R[0], R[1], acc

        R0, R1, acc = lax.fori_loop(0, i, loop, (R[0], R[1], acc))
        o_ref[...] = acc.astype(bf16)
        lt_ref[...] = jnp.where(m0, R0, R1)
        if n:
            pl.when(step == B * n_pairs * nq - 1)(finish)

    qs = lambda c: pl.BlockSpec((tq, LANES), lambda b, p, i: (b * nq + i, c + p))
    ks = lambda c: pl.BlockSpec((S, LANES), lambda b, p, i: (b, c + p))
    os_ = pl.BlockSpec((tq, LANES), lambda b, p, i: (b * nq + i, p))
    res = _pcall(body, name="sb_fwd", grid=(B, n_pairs, nq), in_specs=[qs(qcol), ks(kcol), ks(vcol)] + [_ANY] * n,
                 out_specs=[os_, os_] + [_ANY] * n,
                 out_shape=[jax.ShapeDtypeStruct((T, n_pairs * LANES), bf16), jax.ShapeDtypeStruct((T, n_pairs * LANES), f32)]
                 + _gather_shapes(shards), scratch_shapes=(_gather_sems(n) if n else []) + _score_scratch(tq, 2, 3, 1),
                 compiler_params=_params(("arbitrary", "arbitrary", "arbitrary")))(qkv, qkv, qkv, *shards)
    return res[0], res[1], res[2:]


def _sb_bwd(qkv, do, lt, B, S, n_pairs, qcol, kcol, vcol, tq, hosted=None):
    nq = S // tq
    T = B * S

    def body(*refs):
        i = pl.program_id(2)
        step = (pl.program_id(0) * n_pairs + pl.program_id(1)) * nq + i
        (q_ref, k_ref, v_ref, do_ref, lt_ref), (dq_ref, dk_ref, dv_ref), finish = _host(
            hosted, refs, 5, 3, step, B * n_pairs * nq - 1)

        @pl.when(i == 0)
        def _():
            dk_ref[...] = jnp.zeros_like(dk_ref)
            dv_ref[...] = jnp.zeros_like(dv_ref)

        m0 = _head_masks()
        qh = _by_head(m0, q_ref[...])
        doh = _by_head(m0, do_ref[...])
        lt = lt_ref[...]
        ltot = [lt[:, 0:1], lt[:, HEAD_DIM:HEAD_DIM + 1]]
        row = lax.broadcasted_iota(jnp.int32, (tq, tq), 0)
        col = lax.broadcasted_iota(jnp.int32, (tq, tq), 1)
        strict = col < row
        upto = _tri(min(tq, SCAN_BLOCK), lambda j, s: j <= s)
        before = _tri(min(tq, SCAN_BLOCK), lambda j, s: j < s)

        def tile(s0, CL, CP, dq, diag):
            kb = k_ref[pl.ds(s0, tq), :]
            vb = v_ref[pl.ds(s0, tq), :]
            kh = _by_head(m0, kb)
            CL, CP = list(CL), list(CP)
            dk = jnp.zeros((tq, LANES), f32)
            dv = jnp.zeros((tq, LANES), f32)
            for h in range(2):
                z = _dot(qh[h], kb, _NT)
                lb, l1m = _sb_terms(z)
                if diag:
                    l1m = jnp.where(strict, l1m, 0.0)
                pre, l_total = _scan_cols(l1m, upto, False)
                w = jnp.exp(lb + ((ltot[h] - CL[h]) - pre))
                if diag:
                    w = jnp.where(strict, w, 0.0)
                g = _dot(doh[h], vb, _NT) * w
                p, g_total = _scan_cols(g, before, False, split=False)
                dz = g - jnp.exp(lb) * (g + (p + CP[h]))
                if diag:
                    dz = jnp.where(strict, dz, 0.0)
                dzb = dz.astype(bf16)
                dq = dq + _dot(dzb, kh[h])
                dk = dk + _dot(dzb, qh[h], _TN)
                dv = dv + _dot(w.astype(bf16), doh[h], _TN)
                CL[h] = CL[h] + l_total
                CP[h] = CP[h] + g_total
            dk_ref[pl.ds(s0, tq), :] += dk
            dv_ref[pl.ds(s0, tq), :] += dv
            return CL, CP, dq

        zero = jnp.zeros((tq, 1), f32)

        def loop(n, carry):
            CL, CP, dq = tile(pl.multiple_of(n * tq, tq), carry[0:2], carry[2:4], carry[4], False)
            return CL[0], CL[1], CP[0], CP[1], dq

        c = lax.fori_loop(0, i, loop, (zero, zero, zero, zero, jnp.zeros((tq, LANES), f32)))
        _, _, dq = tile(pl.multiple_of(i * tq, tq), c[0:2], c[2:4], c[4], True)
        dq_ref[...] = dq * Q_SCALE
        finish()

    qs = lambda c: pl.BlockSpec((tq, LANES), lambda b, p, i: (b * nq + i, c + p))
    ks = lambda c: pl.BlockSpec((S, LANES), lambda b, p, i: (b, c + p))
    ts = pl.BlockSpec((tq, LANES), lambda b, p, i: (b * nq + i, p))
    fs = pl.BlockSpec((S, LANES), lambda b, p, i: (b, p))
    shp = jax.ShapeDtypeStruct((T, n_pairs * LANES), f32)
    return _host_call(body, "sb_bwd", (B, n_pairs, nq), [qs(qcol), ks(kcol), ks(vcol), ts, ts], [qkv, qkv, qkv, do, lt],
                      [ts, fs, fs], [shp, shp, shp], hosted)


def _fox_fwd(qkv, c, cT, B, S, n_pairs, qcol, kcol, vcol, tq):
    nq = S // tq
    T = B * S

    def body(q_ref, k_ref, v_ref, cq_ref, ck_ref, o_ref, o32_ref, lse_ref):
        p_idx = pl.program_id(1)
        i = pl.program_id(2)
        m0 = _head_masks()
        lane = lax.broadcasted_iota(jnp.int32, (1, LANES), 1)
        qh = _by_head(m0, q_ref[...])
        cq_all = cq_ref[...]
        cq = [jnp.sum(jnp.where(lane == 2 * p_idx + h, cq_all, 0.0), axis=1, keepdims=True) for h in range(2)]
        row = lax.broadcasted_iota(jnp.int32, (tq, tq), 0)
        col = lax.broadcasted_iota(jnp.int32, (tq, tq), 1)
        causal = col <= row

        def tile(s0, m, l, acc, diag):
            kb = k_ref[pl.ds(s0, tq), :]
            vh = _by_head(m0, v_ref[pl.ds(s0, tq), :])
            m, l = list(m), list(l)
            scale, add = [], []
            for h in range(2):
                z = _dot(qh[h], kb, _NT) + (cq[h] - ck_ref[h, :, pl.ds(s0, tq)])
                if diag:
                    z = jnp.where(causal, z, NEG)
                m_new = jnp.maximum(m[h], jnp.max(z, axis=1, keepdims=True))
                p = jnp.exp(z - m_new)
                a = jnp.exp(m[h] - m_new)
                l[h] = a * l[h] + jnp.sum(p, axis=1, keepdims=True)
                m[h] = m_new
                scale.append(a)
                add.append(_dot(p.astype(bf16), vh[h]))
            acc = acc * jnp.where(m0, scale[0], scale[1]) + add[0] + add[1]
            return m, l, acc

        neg = jnp.full((tq, 1), NEG, f32)
        zero = jnp.zeros((tq, 1), f32)
        m, l, acc = tile(pl.multiple_of(i * tq, tq), [neg, neg], [zero, zero], jnp.zeros((tq, LANES), f32), True)

        def loop(n, carry):
            m, l, acc = tile(pl.multiple_of(n * tq, tq), carry[0:2], carry[2:4], carry[4], False)
            return m[0], m[1], l[0], l[1], acc

        m0_, m1_, l0, l1, acc = lax.fori_loop(0, i, loop, (m[0], m[1], l[0], l[1], acc))
        o = acc * jnp.where(m0, 1.0 / l0, 1.0 / l1)
        o_ref[...] = o.astype(bf16)
        o32_ref[...] = o
        lse_ref[...] = jnp.where(m0, m0_ + jnp.log(l0), m1_ + jnp.log(l1))

    qs = lambda cc: pl.BlockSpec((tq, LANES), lambda b, p, i: (b * nq + i, cc + p))
    ks = lambda cc: pl.BlockSpec((S, LANES), lambda b, p, i: (b, cc + p))
    cqs = pl.BlockSpec((tq, LANES), lambda b, p, i: (b * nq + i, 0))
    cks = pl.BlockSpec((2, 1, S), lambda b, p, i: (b * n_pairs + p, 0, 0))
    os_ = pl.BlockSpec((tq, LANES), lambda b, p, i: (b * nq + i, p))
    shp = jax.ShapeDtypeStruct((T, n_pairs * LANES), f32)
    return _pcall(body, name="fox_fwd", grid=(B, n_pairs, nq), in_specs=[qs(qcol), ks(kcol), ks(vcol), cqs, cks],
                  out_specs=[os_, os_, os_], out_shape=[jax.ShapeDtypeStruct((T, n_pairs * LANES), bf16), shp, shp],
                  compiler_params=_params(("arbitrary", "arbitrary", "arbitrary")))(qkv, qkv, qkv, c, cT)


def _fox_bwd(qkv, c, cT, do, o, lse, B, S, n_pairs, qcol, kcol, vcol, tq, hosted=None):
    nq = S // tq
    T = B * S

    def body(*refs):
        p_idx = pl.program_id(1)
        i = pl.program_id(2)
        step = (pl.program_id(0) * n_pairs + p_idx) * nq + i
        (q_ref, k_ref, v_ref, cq_ref, ck_ref, do_ref, o_ref, lse_ref), (dq_ref, dk_ref, dv_ref, dc_ref), finish = _host(
            hosted, refs, 8, 4, step, B * n_pairs * nq - 1)

        @pl.when(i == 0)
        def _():
            dk_ref[...] = jnp.zeros_like(dk_ref)
            dv_ref[...] = jnp.zeros_like(dv_ref)
            dc_ref[...] = jnp.zeros_like(dc_ref)

        m0 = _head_masks()
        lane = lax.broadcasted_iota(jnp.int32, (1, LANES), 1)
        qh = _by_head(m0, q_ref[...])
        do2 = do_ref[...]
        doh = _by_head(m0, do2)
        prod = do2.astype(f32) * o_ref[...].astype(f32)
        delta = [jnp.sum(p, axis=1, keepdims=True) for p in _by_head(m0, prod)]
        ls = lse_ref[...]
        lse = [ls[:, 0:1], ls[:, HEAD_DIM:HEAD_DIM + 1]]
        cq_all = cq_ref[...]
        cq = [jnp.sum(jnp.where(lane == 2 * p_idx + h, cq_all, 0.0), axis=1, keepdims=True) for h in range(2)]
        row = lax.broadcasted_iota(jnp.int32, (tq, tq), 0)
        col = lax.broadcasted_iota(jnp.int32, (tq, tq), 1)
        causal = col <= row

        def tile(s0, dq, diag):
            kb = k_ref[pl.ds(s0, tq), :]
            vb = v_ref[pl.ds(s0, tq), :]
            kh = _by_head(m0, kb)
            dk = jnp.zeros((tq, LANES), f32)
            dv = jnp.zeros((tq, LANES), f32)
            for h in range(2):
                z = _dot(qh[h], kb, _NT) + (cq[h] - ck_ref[h, :, pl.ds(s0, tq)])
                p = jnp.exp(z - lse[h])
                if diag:
                    p = jnp.where(causal, p, 0.0)
                ds = p * (_dot(doh[h], vb, _NT) - delta[h])
                dsb = ds.astype(bf16)
                dq = dq + _dot(dsb, kh[h])
                dk = dk + _dot(dsb, qh[h], _TN)
                dv = dv + _dot(p.astype(bf16), doh[h], _TN)
                dc_ref[h, :, pl.ds(s0, tq)] -= jnp.sum(ds, axis=0, keepdims=True)
            dk_ref[pl.ds(s0, tq), :] += dk
            dv_ref[pl.ds(s0, tq), :] += dv
            return dq

        dq = lax.fori_loop(0, i, lambda n, dq: tile(pl.multiple_of(n * tq, tq), dq, False), jnp.zeros((tq, LANES), f32))
        dq = tile(pl.multiple_of(i * tq, tq), dq, True)
        dq_ref[...] = dq * Q_SCALE
        finish()

    qs = lambda cc: pl.BlockSpec((tq, LANES), lambda b, p, i: (b * nq + i, cc + p))
    ks = lambda cc: pl.BlockSpec((S, LANES), lambda b, p, i: (b, cc + p))
    cqs = pl.BlockSpec((tq, LANES), lambda b, p, i: (b * nq + i, 0))
    cks = pl.BlockSpec((2, 1, S), lambda b, p, i: (b * n_pairs + p, 0, 0))
    ts = pl.BlockSpec((tq, LANES), lambda b, p, i: (b * nq + i, p))
    fs = pl.BlockSpec((S, LANES), lambda b, p, i: (b, p))
    shp = jax.ShapeDtypeStruct((T, n_pairs * LANES), f32)
    return _host_call(body, "fox_bwd", (B, n_pairs, nq), [qs(qcol), ks(kcol), ks(vcol), cqs, cks, ts, ts, ts],
                      [qkv, qkv, qkv, c, cT, do, o, lse], [ts, fs, fs, cks],
                      [shp, shp, shp, jax.ShapeDtypeStruct(cT.shape, f32)], hosted)


def _sigmoid(x):
    return 1.0 / (1.0 + jnp.exp(-x))


def _mix_fwd(o_sb, o_fx, g, x, wp_sb, wp_fx, w_out, ln_g, ln_b, tm):
    T, D = x.shape
    E = o_sb.shape[1]
    tm = _tile(T, tm, 8)

    def body(osb_ref, ofx_ref, gsb_ref, gfx_ref, x_ref, wsb_ref, wfx_ref, wo_ref, lg_ref, lb_ref,
             xhat_ref, rstd_ref, x1_ref, mg_ref):
        y_sb = _dot(osb_ref[...], wsb_ref[...])
        y_fx = _dot(ofx_ref[...], wfx_ref[...])
        merged = (_sigmoid(gsb_ref[...]) * y_sb + _sigmoid(gfx_ref[...]) * y_fx).astype(bf16)
        r = ALPHA * x_ref[...] + _dot(merged, wo_ref[...])
        mean = jnp.mean(r, axis=1, keepdims=True)
        cen = r - mean
        rstd = lax.rsqrt(jnp.mean(cen * cen, axis=1, keepdims=True) + LN_EPS)
        xhat = cen * rstd
        xhat_ref[...] = xhat
        rstd_ref[...] = rstd
        x1_ref[...] = (xhat * lg_ref[...] + lb_ref[...]).astype(bf16)
        mg_ref[...] = merged

    rows = lambda w, c=0: pl.BlockSpec((tm, w), lambda i: (i, c))
    full = lambda a: pl.BlockSpec(a.shape, lambda i: (0, 0))
    return _pcall(body, name="mix_fwd", grid=(T // tm,),
                  in_specs=[rows(E), rows(E), rows(D, 0), rows(D, 1), rows(D), full(wp_sb), full(wp_fx), full(w_out),
                            full(ln_g), full(ln_b)],
                  out_specs=[rows(D), rows(1), rows(D), rows(D)],
                  out_shape=[jax.ShapeDtypeStruct((T, D), f32), jax.ShapeDtypeStruct((T, 1), f32),
                             jax.ShapeDtypeStruct((T, D), bf16), jax.ShapeDtypeStruct((T, D), bf16)],
                  compiler_params=_params(("arbitrary",)))(o_sb, o_fx, g, g, x, wp_sb, wp_fx, w_out, ln_g, ln_b)


def _mix_bwd(dr1, o_sb, o_fx, g, wp_sb, wp_fx, w_out, tm):
    T, D = dr1.shape
    E = o_sb.shape[1]
    tm = _tile(T, tm, 8)

    def body(dr_ref, osb_ref, ofx_ref, gsb_ref, gfx_ref, wsb_ref, wfx_ref, wo_ref,
             dysb_ref, dyfx_ref, dg_ref, dosb_ref, dofx_ref, sumsb_ref, sumfx_ref):
        @pl.when(pl.program_id(0) == 0)
        def _():
            sumsb_ref[...] = jnp.zeros_like(sumsb_ref)
            sumfx_ref[...] = jnp.zeros_like(sumfx_ref)
        dm = _dot(dr_ref[...].astype(bf16), wo_ref[...], _NT)
        for half, (o_ref, g_ref, w_ref, dy_ref, do_ref, sum_ref) in enumerate((
                (osb_ref, gsb_ref, wsb_ref, dysb_ref, dosb_ref, sumsb_ref),
                (ofx_ref, gfx_ref, wfx_ref, dyfx_ref, dofx_ref, sumfx_ref))):
            y = _dot(o_ref[...], w_ref[...])
            s = _sigmoid(g_ref[...])
            dy = (dm * s).astype(bf16)
            dy_ref[...] = dy
            dg = dm * y * s * (1.0 - s)
            dg_ref[:, half * D:(half + 1) * D] = dg.astype(bf16)
            sum_ref[0:1, :] += jnp.sum(dg, axis=0, keepdims=True)
            do_ref[...] = _dot(dy, w_ref[...], _NT).astype(bf16)

    rows = lambda w, c=0: pl.BlockSpec((tm, w), lambda i: (i, c))
    full = lambda a: pl.BlockSpec(a.shape, lambda i: (0, 0))
    acc = pl.BlockSpec((8, D), lambda i: (0, 0))
    res = _pcall(body, name="mix_bwd", grid=(T // tm,),
                 in_specs=[rows(D), rows(E), rows(E), rows(D, 0), rows(D, 1), full(wp_sb), full(wp_fx), full(w_out)],
                 out_specs=[rows(D), rows(D), rows(2 * D), rows(E), rows(E), acc, acc],
                 out_shape=[jax.ShapeDtypeStruct((T, D), bf16)] * 2 + [jax.ShapeDtypeStruct((T, 2 * D), bf16)]
                 + [jax.ShapeDtypeStruct((T, E), bf16)] * 2 + [jax.ShapeDtypeStruct((8, D), f32)] * 2,
                 compiler_params=_params(("arbitrary",)))(dr1, o_sb, o_fx, g, g, wp_sb, wp_fx, w_out)
    return res


def _ln_bwd(dy_a, dy_b, scale_b, xhat, rstd, ln_g, tm):
    T, D = xhat.shape
    tm = _tile(T, tm, 8)

    def body(a_ref, b_ref, xh_ref, rs_ref, g_ref, dr_ref, st_ref):
        @pl.when(pl.program_id(0) == 0)
        def _():
            st_ref[...] = jnp.zeros_like(st_ref)
        dy = a_ref[...] + scale_b * b_ref[...]
        xh = xh_ref[...]
        dxh = dy * g_ref[...]
        m1 = jnp.mean(dxh, axis=1, keepdims=True)
        m2 = jnp.mean(dxh * xh, axis=1, keepdims=True)
        dr_ref[...] = rs_ref[...] * (dxh - m1 - xh * m2)
        st_ref[0:1, :] += jnp.sum(dy * xh, axis=0, keepdims=True)
        st_ref[1:2, :] += jnp.sum(dy, axis=0, keepdims=True)

    rows = lambda w: pl.BlockSpec((tm, w), lambda i: (i, 0))
    return _pcall(body, name="ln1_bwd", grid=(T // tm,),
                  in_specs=[rows(D), rows(D), rows(D), rows(1), pl.BlockSpec((1, D), lambda i: (0, 0))],
                  out_specs=[rows(D), pl.BlockSpec((8, D), lambda i: (0, 0))],
                  out_shape=[jax.ShapeDtypeStruct((T, D), f32), jax.ShapeDtypeStruct((8, D), f32)],
                  compiler_params=_params(("arbitrary",)))(dy_a, dy_b, xhat, rstd, ln_g)


_INV_SQRT2 = 1.0 / math.sqrt(2.0)
_INV_SQRT2PI = 1.0 / math.sqrt(2.0 * math.pi)


def _conv_rows(ref, r0, rc, first, wc, bc):
    cur = ref[pl.ds(r0, rc), :]
    prev = ref[pl.ds(pl.multiple_of(jnp.maximum(r0 - 8, 0), 8), 8), :]
    prev = jnp.where(first, jnp.zeros_like(prev), prev)
    rid = lax.broadcasted_iota(jnp.int32, (rc, LANES), 0)
    s1 = jnp.where(rid == 0, prev[7:8, :], pltpu.roll(cur, 1, 0))
    s2 = jnp.where(rid == 0, prev[6:7, :], jnp.where(rid == 1, prev[7:8, :], pltpu.roll(cur, 2, 0)))
    conv = bc + wc[0:1, :] * s2 + wc[1:2, :] * s1 + wc[2:3, :] * cur
    return conv, (s2, s1, cur)


def _glu_fwd(u, w_conv, b_conv, B, S, rc=512):
    F = u.shape[1] // 2
    nf = F // LANES
    rc = _tile(S, rc, 8)

    def body(ug_ref, uv_ref, wc_ref, bc_ref, a_ref):
        wc, bc = wc_ref[...], bc_ref[...]

        def chunk(n, _):
            r0 = pl.multiple_of(n * rc, rc)
            c, _taps = _conv_rows(ug_ref, r0, rc, n == 0, wc, bc)
            gelu = 0.5 * c * (1.0 + lax.erf(c * _INV_SQRT2))
            a_ref[pl.ds(r0, rc), :] = (gelu * uv_ref[pl.ds(r0, rc), :]).astype(bf16)
            return 0

        lax.fori_loop(0, S // rc, chunk, 0)

    return _pcall(body, name="glu_fwd", grid=(B, nf),
                  in_specs=[pl.BlockSpec((S, LANES), lambda b, j: (b, j)), pl.BlockSpec((S, LANES), lambda b, j: (b, nf + j)),
                            pl.BlockSpec((3, LANES), lambda b, j: (0, j)), pl.BlockSpec((1, LANES), lambda b, j: (0, j))],
                  out_specs=pl.BlockSpec((S, LANES), lambda b, j: (b, j)),
                  out_shape=jax.ShapeDtypeStruct((B * S, F), bf16),
                  compiler_params=_params(("arbitrary", "arbitrary")))(u, u, w_conv, b_conv)


def _glu_bwd(u, da, w_conv, b_conv, B, S, rc=512):
    F = u.shape[1] // 2
    nf = F // LANES
    rc = _tile(S, rc, 8)
    nc = S // rc

    def body(ug_ref, uv_ref, da_ref, wc_ref, bc_ref, dug_ref, duv_ref, gw_ref, gb_ref, dc_ref):
        wc, bc = wc_ref[...], bc_ref[...]

        def chunk(n, carry):
            gw0, gw1, gw2, gb = carry
            r0 = pl.multiple_of(n * rc, rc)
            c, (s2, s1, cur) = _conv_rows(ug_ref, r0, rc, n == 0, wc, bc)
            cdf = 0.5 * (1.0 + lax.erf(c * _INV_SQRT2))
            da = da_ref[pl.ds(r0, rc), :]
            duv_ref[pl.ds(r0, rc), :] = (da * (c * cdf)).astype(bf16)
            dc = da * uv_ref[pl.ds(r0, rc), :] * (cdf + c * (_INV_SQRT2PI * jnp.exp(-0.5 * c * c)))
            dc_ref[pl.ds(r0, rc), :] = dc
            red = lambda t: jnp.sum(t, axis=0, keepdims=True)
            return gw0 + red(dc * s2), gw1 + red(dc * s1), gw2 + red(dc * cur), gb + red(dc)

        z = jnp.zeros((1, LANES), f32)
        gw0, gw1, gw2, gb = lax.fori_loop(0, nc, chunk, (z, z, z, z))
        gw_ref[0, 0:1, :] = gw0
        gw_ref[0, 1:2, :] = gw1
        gw_ref[0, 2:3, :] = gw2
        gb_ref[0] = gb

        def chunk2(n, _):
            r0 = pl.multiple_of(n * rc, rc)
            cur = dc_ref[pl.ds(r0, rc), :]
            nxt = dc_ref[pl.ds(pl.multiple_of(jnp.minimum(r0 + rc, S - 8), 8), 8), :]
            nxt = jnp.where(n == nc - 1, jnp.zeros_like(nxt), nxt)
            rid = lax.broadcasted_iota(jnp.int32, (rc, LANES), 0)
            a1 = jnp.where(rid == rc - 1, nxt[0:1, :], pltpu.roll(cur, rc - 1, 0))
            a2 = jnp.where(rid == rc - 1, nxt[1:2, :], jnp.where(rid == rc - 2, nxt[0:1, :], pltpu.roll(cur, rc - 2, 0)))
            dug_ref[pl.ds(r0, rc), :] = (wc[2:3, :] * cur + wc[1:2, :] * a1 + wc[0:1, :] * a2).astype(bf16)
            return 0

        lax.fori_loop(0, nc, chunk2, 0)

    blk = lambda off: pl.BlockSpec((S, LANES), lambda b, j: (b, off + j))
    return _pcall(body, name="glu_bwd", grid=(B, nf),
                  in_specs=[blk(0), blk(nf), blk(0), pl.BlockSpec((3, LANES), lambda b, j: (0, j)),
                            pl.BlockSpec((1, LANES), lambda b, j: (0, j))],
                  out_specs=[blk(0), blk(0), pl.BlockSpec((1, 3, LANES), lambda b, j: (b, 0, j)),
                             pl.BlockSpec((1, 1, LANES), lambda b, j: (b, 0, j))],
                  out_shape=[jax.ShapeDtypeStruct((B * S, F), bf16), jax.ShapeDtypeStruct((B * S, F), bf16),
                             jax.ShapeDtypeStruct((B, 3, F), f32), jax.ShapeDtypeStruct((B, 1, F), f32)],
                  scratch_shapes=[pltpu.VMEM((S, LANES), f32)],
                  compiler_params=_params(("arbitrary", "arbitrary")))(u, u, da, w_conv, b_conv)


def _down_loss(a, w_down, xhat1, ln1_g, ln1_b, ln2_g, ln2_b, tgt, tm):
    T, D = xhat1.shape
    F = a.shape[1]
    tm = _tile(T, tm, 8)

    def body(a_ref, w_ref, xh_ref, g1_ref, b1_ref, g2_ref, b2_ref, t_ref, dr_ref, st_ref):
        @pl.when(pl.program_id(0) == 0)
        def _():
            st_ref[...] = jnp.zeros_like(st_ref)
        x1 = xh_ref[...] * g1_ref[...] + b1_ref[...]
        r = ALPHA * x1 + _dot(a_ref[...], w_ref[...])
        mean = jnp.mean(r, axis=1, keepdims=True)
        cen = r - mean
        rstd = lax.rsqrt(jnp.mean(cen * cen, axis=1, keepdims=True) + LN_EPS)
        xh = cen * rstd
        err = (xh * g2_ref[...] + b2_ref[...]) - t_ref[...]
        dy = err * (1.0 / D)
        dxh = dy * g2_ref[...]
        m1 = jnp.mean(dxh, axis=1, keepdims=True)
        m2 = jnp.mean(dxh * xh, axis=1, keepdims=True)
        dr_ref[...] = rstd * (dxh - m1 - xh * m2)
        st_ref[0:1, :] += jnp.sum(dy * xh, axis=0, keepdims=True)
        st_ref[1:2, :] += jnp.sum(dy, axis=0, keepdims=True)
        st_ref[2:3, :] += jnp.sum(err * err, axis=0, keepdims=True)

    rows = lambda w: pl.BlockSpec((tm, w), lambda i: (i, 0))
    vec = pl.BlockSpec((1, D), lambda i: (0, 0))
    return _pcall(body, name="down_loss", grid=(T // tm,),
                  in_specs=[rows(F), pl.BlockSpec((F, D), lambda i: (0, 0)), rows(D), vec, vec, vec, vec, rows(D)],
                  out_specs=[rows(D), pl.BlockSpec((8, D), lambda i: (0, 0))],
                  out_shape=[jax.ShapeDtypeStruct((T, D), f32), jax.ShapeDtypeStruct((8, D), f32)],
                  compiler_params=_params(("arbitrary",)))(a, w_down, xhat1, ln1_g, ln1_b, ln2_g, ln2_b, tgt)


def _local_step(x, tgt, w_in, b_in, ln1_g, ln1_b, b_conv, ln2_g, ln2_b, late_shards, late_weights, reducer=None,
                tail=None, tq=512, tm=256):
    B, S, D = x.shape
    T = B * S
    E = (w_in.shape[1] - 2 * D) * HEAD_DIM // (6 * HEAD_DIM + 1)
    n_pairs = E // LANES
    NH = E // HEAD_DIM
    x2 = x.reshape(T, D)
    tgt2 = tgt.reshape(T, D)
    tq = _tile(S, tq, 8)

    c_f, c_g = 6 * E, 6 * E + NH
    w_qkv = w_in[:, :c_f]
    qscale = jnp.concatenate([jnp.full((E,), Q_SCALE, f32), jnp.ones((2 * E,), f32)] * 2)
    w_qkv_s = (w_qkv.astype(f32) * qscale).astype(bf16)
    b_qkv_s = b_in[:, :c_f] * qscale
    w_f = jnp.pad(w_in[:, c_f:c_g], ((0, 0), (0, LANES - NH)))
    b_f = jnp.pad(b_in[:, c_f:c_g], ((0, 0), (0, LANES - NH)))
    w_g = w_in[:, c_g:]
    b_g = b_in[:, c_g:]

    xb = x2.astype(bf16)
    qkv = _matmul(xb, w_qkv_s, bias=b_qkv_s, out_dtype=bf16, tm=1024, tn=1536, name="proj_qkv")
    g = _matmul(xb, w_g, bias=b_g, tm=1024, tn=1024, name="proj_gate")
    fl = _matmul(xb, w_f, bias=b_f, tm=1024, name="proj_forget")
    c = _cumlogf(fl, B, S)
    cT = c.reshape(B, S, LANES)[:, :, :NH].transpose(0, 2, 1).reshape(B * NH, 1, S)
    P = n_pairs
    o_sb, lt_sb, stacks = _sb_fwd(qkv, B, S, P, 0, P, 2 * P, tq, late_shards)
    wp_sb, wp_fx, w_out, w_up, w_conv, w_down = late_weights(stacks)
    F = w_down.shape[0]
    o_fx, o_fx32, lse_fx = _fox_fwd(qkv, c, cT, B, S, P, 3 * P, 4 * P, 5 * P, tq)
    xhat1, rstd1, x1b, merged = _mix_fwd(o_sb, o_fx, g, x2, wp_sb, wp_fx, w_out, ln1_g, ln1_b, tm)
    u = _matmul(x1b, w_up, tm=1024, tn=1408, name="ffn_up")
    a = _glu_fwd(u, w_conv, b_conv, B, S)
    dr2, st2 = _down_loss(a, w_down, xhat1, ln1_g, ln1_b, ln2_g, ln2_b, tgt2, tm)

    grads = {}
    grads["ln2_g"], grads["ln2_b"] = st2[0:1], st2[1:2]
    sq_err = st2[2:3]
    da = _matmul(dr2, w_down, tb=True, tm=1024, tn=1408, name="ffn_da")
    grads["w_down"] = _matmul(a, dr2, ta=True, tm=1408, tn=1024, tk=1024, name="grad_w_down")
    du_g, du_v, gwc, gbc = _glu_bwd(u, da, w_conv, b_conv, B, S)
    grads["w_conv"] = jnp.sum(gwc, axis=0)
    grads["b_conv"] = jnp.sum(gbc, axis=0)
    grads["w_up"] = jnp.concatenate(
        [_matmul(x1b, du_g, ta=True, tm=512, tn=2816, tk=1024, name="grad_w_up_gate"),
         _matmul(x1b, du_v, ta=True, tm=512, tn=2816, tk=1024, name="grad_w_up_val")], axis=1)
    dx1 = _matmul(du_g, w_up[:, :F], tb=True, tm=1024, tn=1024, tk=F, name="ffn_dx_gate")
    dx1 = _matmul(du_v, w_up[:, F:], tb=True, addend=dx1, tm=1024, tn=1024, tk=F, name="ffn_dx_val")
    dr1, st1 = _ln_bwd(dx1, dr2, ALPHA, xhat1, rstd1, ln1_g, tm)
    grads["ln1_g"], grads["ln1_b"] = st1[0:1], st1[1:2]

    dy_sb, dy_fx, dg, do_sb, do_fx, gsum_sb, gsum_fx = _mix_bwd(dr1, o_sb, o_fx, g, wp_sb, wp_fx, w_out, tm)
    grads["w_out"] = _matmul(merged, dr1, ta=True, tm=512, tn=1024, tk=1024, name="grad_w_out")
    grads["w_proj_sb"] = _matmul(o_sb, dy_sb, ta=True, tm=512, tn=1024, tk=1024, name="grad_w_proj_sb")
    grads["w_proj_fox"] = _matmul(o_fx, dy_fx, ta=True, tm=512, tn=1024, tk=1024, name="grad_w_proj_fox")
    (dq_f, dk_f, dv_f, dcT), got = _fox_bwd(qkv, c, cT, do_fx, o_fx32, lse_fx, B, S, P, 3 * P, 4 * P, 5 * P, tq,
                                            reducer.to_sibling(grads) if reducer else None)
    (dq_s, dk_s, dv_s), got = _sb_bwd(qkv, do_sb, lt_sb, B, S, P, 0, P, 2 * P, tq, reducer.to_chips(got) if reducer else None)
    if reducer:
        reducer.from_chips(got)
    dc = jnp.pad(dcT.reshape(B, NH, S).transpose(0, 2, 1), ((0, 0), (0, 0), (0, LANES - NH))).reshape(T, LANES)
    dfl = _cumlogf_bwd(dc, fl, B, S)
    dqkv = jnp.concatenate([dq_s, dk_s, dv_s, dq_f, dk_f, dv_f], axis=1)
    gw_qkv, gb_qkv = _matmul(x2, dqkv, ta=True, colsum=True, tm=512, tn=3072, tk=512, name="grad_w_qkv")
    gw_f, gb_f = _matmul(x2, dfl, ta=True, colsum=True, tm=512, tk=1024, name="grad_w_forget")
    gw_g = _matmul(x2, dg, ta=True, tm=512, tn=2048, tk=1024, name="grad_w_gate")
    grads["w_in"] = jnp.concatenate([gw_qkv, gw_f[:, :NH], gw_g], axis=1)
    grads["b_in"] = jnp.concatenate([gb_qkv, gb_f[:, :NH], gsum_sb[0:1], gsum_fx[0:1]], axis=1)
    dx = _matmul(dfl, w_f, tb=True, addend=dr1, addend_scale=ALPHA, tm=1024, tn=1024, name="dx_forget",
                 hosted=tail.to_sibling(grads, sq_err) if tail else None)
    if tail:
        dx, got = dx
        to_chips = tail.to_chips(got)
    dx = _matmul(dqkv, w_qkv, tb=True, addend=dx, tm=512, tn=1024, tk=c_f, name="dx_qkv", hosted=to_chips if tail else None)
    if tail:
        dx, got = dx
        tail.from_chips(got)
    dx = _matmul(dg, w_g, tb=True, addend=dx, tm=1024, tn=1024, tk=2 * D, name="dx_gate")
    return sq_err, dx.reshape(B, S, D), grads


_ANY = pl.BlockSpec(memory_space=pl.ANY)
_MESH = pl.DeviceIdType.MESH


def _pos():
    return lax.axis_index("x"), lax.axis_index("y"), lax.axis_index("c")


def _other_chips(x, y):
    return [(1 - x, y), (x, 1 - y), (1 - x, 1 - y)]


def _gather_split(shard):
    h = shard.shape[0] // 2

    def body(src_ref, dst_ref, send_sems, recv_sems, pass_send, pass_recv):
        x, y, c = _pos()
        k = 2 * x + y
        chips = _other_chips(x, y)
        slots = [2 * chip[0] + chip[1] for chip in chips]

        def rows(ref, half):
            return ref.at[pl.ds(pl.multiple_of(half * h, 16), h), :]

        def fetch(j, slot):
            return pltpu.make_async_remote_copy(src_ref=rows(src_ref, c), dst_ref=rows(dst_ref.at[slot], c),
                                                send_sem=send_sems.at[j], recv_sem=recv_sems.at[j],
                                                device_id=(*chips[j], c), device_id_type=_MESH)

        def pass_on(j, half):
            blk = rows(dst_ref.at[slots[j]], half)
            return pltpu.make_async_remote_copy(src_ref=blk, dst_ref=blk, send_sem=pass_send.at[j], recv_sem=pass_recv.at[j],
                                                device_id=(x, y, 1 - c), device_id_type=_MESH)

        for j in range(3):
            fetch(j, k).start()
        for j in range(3):
            fetch(j, slots[j]).wait_recv()
            pass_on(j, c).start()
        for j in range(3):
            pass_on(j, 1 - c).wait_recv()
        for j in range(3):
            fetch(j, k).wait_send()
            pass_on(j, c).wait_send()

    sems = pltpu.SemaphoreType.DMA((3,))
    return _pcall(body, name="gather_w_in", in_specs=[_ANY], out_specs=_ANY,
                  out_shape=jax.ShapeDtypeStruct((4,) + shard.shape, shard.dtype), scratch_shapes=[sems] * 4,
                  compiler_params=pltpu.CompilerParams(has_side_effects=True))(shard)


def _gather_shapes(shards):
    return [jax.ShapeDtypeStruct((4,) + s.shape, s.dtype) for s in shards]


def _gather_sems(n):
    return [pltpu.SemaphoreType.DMA((n, 3)), pltpu.SemaphoreType.DMA((n, 3)), pltpu.SemaphoreType.DMA((n,))]


def _gather_copies(srcs, dsts, send_sems, recv_sems, local_sems):
    n = len(srcs)
    x, y, c = _pos()
    k = 2 * x + y
    chips = _other_chips(x, y)

    def copy(a, j, chip, slot):
        return pltpu.make_async_remote_copy(src_ref=srcs[a], dst_ref=dsts[a].at[slot], send_sem=send_sems.at[a, j],
                                            recv_sem=recv_sems.at[a, j], device_id=(*chip, c), device_id_type=_MESH)

    def mine():
        return ([pltpu.make_async_copy(srcs[a], dsts[a].at[k], local_sems.at[a]) for a in range(n)],
                [copy(a, j, chip, k) for a in range(n) for j, chip in enumerate(chips)])

    def start():
        local, sends = mine()
        for cp in local + sends:
            cp.start()

    def finish():
        local, sends = mine()
        for a in range(n):
            for j, chip in enumerate(chips):
                copy(a, j, chip, 2 * chip[0] + chip[1]).wait_recv()
        for cp in sends:
            cp.wait_send()
        for cp in local:
            cp.wait()

    return start, finish


class _Hosted:
    def __init__(self, inputs, outputs, sems, copies):
        self.inputs, self.outputs, self.sems, self.copies = list(inputs), list(outputs), list(sems), copies


def _host(hosted, refs, n_in, n_out, step, last):
    if hosted is None:
        return refs[:n_in], refs[n_in:n_in + n_out], lambda: None
    hi, ho = len(hosted.inputs), len(hosted.outputs)
    own_in, h_in = refs[:n_in], refs[n_in:n_in + hi]
    own_out = refs[n_in + hi:n_in + hi + n_out]
    h_out = refs[n_in + hi + n_out:n_in + hi + n_out + ho]
    start, finish = hosted.copies(h_in, h_out, *refs[n_in + hi + n_out + ho:])
    pl.when(step == 0)(start)
    return own_in, own_out, lambda: pl.when(step == last)(finish)


def _host_call(body, name, grid, in_specs, args, out_specs, out_shape, hosted):
    h_in = hosted.inputs if hosted else []
    h_out = hosted.outputs if hosted else []
    res = _pcall(body, name=name, grid=grid, in_specs=list(in_specs) + [_ANY] * len(h_in),
                 out_specs=list(out_specs) + [_ANY] * len(h_out), out_shape=list(out_shape) + list(h_out),
                 scratch_shapes=hosted.sems if hosted else [],
                 compiler_params=_params(("arbitrary",) * len(grid)))(*args, *h_in)
    return list(res[:len(out_shape)]), list(res[len(out_shape):])


def _sibling_copies(shapes):
    n = len(shapes)

    def copies(p_refs, got_refs, send_sems, recv_sems):
        x, y, c = _pos()

        def copy(a):
            h = shapes[a][1] // 2
            src = p_refs[a].at[:, pl.ds(pl.multiple_of((1 - c) * h, 8), h), :]
            return pltpu.make_async_remote_copy(src_ref=src, dst_ref=got_refs[a], send_sem=send_sems.at[a],
                                                recv_sem=recv_sems.at[a], device_id=(x, y, 1 - c), device_id_type=_MESH)

        def start():
            for a in range(n):
                copy(a).start()

        def finish():
            for a in range(n):
                copy(a).wait()

        return start, finish

    return copies


def _hosted_sibling(pieces):
    n = len(pieces)
    return _Hosted(pieces, [jax.ShapeDtypeStruct((4, p.shape[1] // 2, p.shape[2]), p.dtype) for p in pieces],
                   [pltpu.SemaphoreType.DMA((n,)), pltpu.SemaphoreType.DMA((n,))], _sibling_copies([p.shape for p in pieces]))


def _chips_copies(n):
    def copies(p_refs, got_refs, send_sems, recv_sems):
        x, y, c = _pos()
        k = 2 * x + y
        chips = _other_chips(x, y)

        def copy(a, j, chip, piece, slot):
            return pltpu.make_async_remote_copy(src_ref=p_refs[a].at[piece], dst_ref=got_refs[a].at[slot],
                                                send_sem=send_sems.at[a, j], recv_sem=recv_sems.at[a, j],
                                                device_id=(*chip, c), device_id_type=_MESH)

        def start():
            for a in range(n):
                for j, chip in enumerate(chips):
                    copy(a, j, chip, 2 * chip[0] + chip[1], k).start()

        def finish():
            for a in range(n):
                for j, chip in enumerate(chips):
                    copy(a, j, chip, k, 2 * chip[0] + chip[1]).wait_recv()
            for a in range(n):
                for j, chip in enumerate(chips):
                    copy(a, j, chip, 2 * chip[0] + chip[1], k).wait_send()

        return start, finish

    return copies


def _hosted_chips(pieces):
    n = len(pieces)
    return _Hosted(pieces, [jax.ShapeDtypeStruct(p.shape, p.dtype) for p in pieces],
                   [pltpu.SemaphoreType.DMA((n, 3)), pltpu.SemaphoreType.DMA((n, 3))], _chips_copies(n))


def _small_copies(sm_ref, sg_ref, send_sems, recv_sems, local_sem):
    x, y, c = _pos()
    me = 4 * x + 2 * y + c
    flip = lambda v, bit: 1 - v if bit else v
    peers = [(flip(x, r & 4), flip(y, r & 2), flip(c, r & 1)) for r in range(1, 8)]

    def copy(j, slot):
        return pltpu.make_async_remote_copy(src_ref=sm_ref, dst_ref=sg_ref.at[slot], send_sem=send_sems.at[j],
                                            recv_sem=recv_sems.at[j], device_id=peers[j], device_id_type=_MESH)

    def local():
        return pltpu.make_async_copy(sm_ref, sg_ref.at[me], local_sem)

    def start():
        local().start()
        for j in range(7):
            copy(j, me).start()

    def finish():
        for j, (px, py, pc) in enumerate(peers):
            copy(j, 4 * px + 2 * py + pc).wait_recv()
        for j in range(7):
            copy(j, me).wait_send()
        local().wait()

    return start, finish


def _hosted_sibling_and_small(pieces, small):
    n = len(pieces)
    sibling = _hosted_sibling(pieces)

    def copies(in_refs, out_refs, big_send, big_recv, send_sems, recv_sems, local_sem):
        start_big, finish_big = sibling.copies(in_refs[:n], out_refs[:n], big_send, big_recv)
        start_small, finish_small = _small_copies(in_refs[n], out_refs[n], send_sems, recv_sems, local_sem)

        def start():
            start_big()
            start_small()

        def finish():
            finish_small()
            finish_big()

        return start, finish

    return _Hosted(pieces + [small], sibling.outputs + [jax.ShapeDtypeStruct((8,) + small.shape, small.dtype)],
                   sibling.sems + [pltpu.SemaphoreType.DMA((7,)), pltpu.SemaphoreType.DMA((7,)), pltpu.SemaphoreType.DMA(())],
                   copies)


def _share_halves(shards):
    n = len(shards)

    def body(*refs):
        full_refs = refs[n:2 * n]
        send_sems, recv_sems = refs[2 * n:]
        x, y, c = _pos()

        def copy(a, half):
            h = shards[a].shape[0] // 2
            rows = full_refs[a].at[pl.ds(pl.multiple_of(half * h, 8), h), :]
            return pltpu.make_async_remote_copy(src_ref=rows, dst_ref=rows, send_sem=send_sems.at[a],
                                                recv_sem=recv_sems.at[a], device_id=(x, y, 1 - c), device_id_type=_MESH)

        sends = [copy(a, c) for a in range(n)]
        for cp in sends:
            cp.start()
        for a in range(n):
            copy(a, 1 - c).wait_recv()
        for cp in sends:
            cp.wait_send()

    return _pcall(body, name="share_halves", in_specs=[_ANY] * n, out_specs=[_ANY] * n,
                  out_shape=[jax.ShapeDtypeStruct(s.shape, s.dtype) for s in shards],
                  input_output_aliases={a: a for a in range(n)},
                  scratch_shapes=[pltpu.SemaphoreType.DMA((n,)), pltpu.SemaphoreType.DMA((n,))],
                  compiler_params=pltpu.CompilerParams(has_side_effects=True))(*shards)


def _add_own_half(piece, got, core, name):
    _, r, cols = piece.shape
    h = r // 2

    def body(c_ref, a_ref, b_ref, o_ref, o16_ref):
        s = a_ref[0] + b_ref[...]
        o_ref[...] = s
        o16_ref[...] = s.astype(bf16)

    out = pl.BlockSpec((1, h, cols), lambda k, c: (k, 0, 0))
    grid_spec = pltpu.PrefetchScalarGridSpec(
        num_scalar_prefetch=1, grid=(4,),
        in_specs=[pl.BlockSpec((1, 1, h, cols), lambda k, c: (k, c[0], 0, 0)), out], out_specs=[out, out])
    return _pcall(body, name=name, grid_spec=grid_spec,
                  out_shape=[jax.ShapeDtypeStruct((4, h, cols), f32), jax.ShapeDtypeStruct((4, h, cols), bf16)],
                  compiler_params=_params(("arbitrary",)))(core, piece.reshape(4, 2, h, cols), got)


def _sum_chips(own, got, where, name):
    _, h, cols = own.shape
    t = _tile(h, 128, 16)
    nt = h // t

    def body(w_ref, own_ref, gx_ref, gy_ref, gxy_ref, o_ref):
        o_ref[...] = ((own_ref[0] + gx_ref[0].astype(f32)) + gy_ref[0].astype(f32)) + gxy_ref[0].astype(f32)

    slot = lambda j: pl.BlockSpec((1, t, cols), lambda i, w: (w[j], i, 0))
    grid_spec = pltpu.PrefetchScalarGridSpec(
        num_scalar_prefetch=1, grid=(nt,),
        in_specs=[slot(0), slot(2), slot(3), slot(4)],
        out_specs=pl.BlockSpec((t, cols), lambda i, w: (w[1] * nt + i, 0)))
    return _pcall(body, name=name, grid_spec=grid_spec, out_shape=jax.ShapeDtypeStruct((2 * h, cols), f32),
                  compiler_params=_params(("arbitrary",)))(where, own, got, got, got)


def _sum_slots(stack, name):
    k, n, cols = stack.shape
    t = _tile(n, 128, 8)

    def body(s_ref, o_ref):
        acc = s_ref[0]
        for i in range(1, k):
            acc = acc + s_ref[i]
        o_ref[...] = acc

    return _pcall(body, name=name, grid=(n // t,), in_specs=[pl.BlockSpec((k, t, cols), lambda i: (0, i, 0))],
                  out_specs=pl.BlockSpec((t, cols), lambda i: (i, 0)), out_shape=jax.ShapeDtypeStruct((n, cols), f32),
                  compiler_params=_params(("arbitrary",)))(stack)


def _adamw(w, g, m, v, name):
    n, cols = w.shape
    t = _tile(n, 128, 8)
    c1 = 1.0 - ADAM_B1 ** ADAM_STEP
    c2 = 1.0 - ADAM_B2 ** ADAM_STEP

    def body(w_ref, g_ref, m_ref, v_ref, d_ref, nm_ref, nv_ref, g_out_ref):
        g = g_ref[...]
        nm = ADAM_B1 * m_ref[...] + (1.0 - ADAM_B1) * g
        nv = ADAM_B2 * v_ref[...] + (1.0 - ADAM_B2) * (g * g)
        d_ref[...] = -ADAM_LR * ((nm / c1) / (jnp.sqrt(nv / c2) + ADAM_EPS) + ADAM_WD * w_ref[...])
        nm_ref[...] = nm
        nv_ref[...] = nv
        g_out_ref[...] = g

    spec = pl.BlockSpec((t, cols), lambda i: (i, 0))
    shp = jax.ShapeDtypeStruct((n, cols), f32)
    return _pcall(body, name=name, grid=(n // t,), in_specs=[spec] * 4, out_specs=[spec] * 4, out_shape=[shp] * 4,
                  compiler_params=_params(("arbitrary",)))(w, g, m, v)


_MATS = (("w_in", 1), ("w_proj_sb", 1), ("w_proj_fox", 1), ("w_out", 0), ("w_up", 1), ("w_down", 0))
_SMALL = ("b_in", "ln1_g", "ln1_b", "b_conv", "ln2_g", "ln2_b")


def _pad_lanes(v):
    n = v.shape[-1]
    return jnp.pad(v, ((0, 0), (0, (-n) % LANES)))


def _pack_rows(vectors):
    flat = jnp.concatenate([_pad_lanes(v.reshape(1, -1)) for v in vectors], axis=1).reshape(-1, LANES)
    return jnp.pad(flat, ((0, (-flat.shape[0]) % 8), (0, 0)))


def _unpack_rows(packed, sizes):
    out, r = [], 0
    for n in sizes:
        rows = -(-n // LANES)
        out.append(packed[r:r + rows].reshape(1, rows * LANES)[:, :n])
        r += rows
    return out


def _unstack(stack, axis):
    if axis == 0:
        return stack.reshape(-1, stack.shape[2])
    return jnp.concatenate([stack[k] for k in range(4)], axis=1)


def _pieces(g, axis):
    if axis == 0:
        return g.reshape(4, g.shape[0] // 4, g.shape[1])
    cols = g.shape[1] // 4
    return jnp.stack([g[:, k * cols:(k + 1) * cols] for k in range(4)])


class _Reducer:
    def __init__(self, mats, core, where, small=None):
        self.mats, self.core, self.where, self.small = mats, core, where, small

    def to_sibling(self, grads, sq_err=None):
        self.local = [_pieces(grads[n], axis) for n, axis in self.mats]
        if self.small is None:
            return _hosted_sibling(self.local)
        return _hosted_sibling_and_small(self.local, self.small(grads, sq_err))

    def to_chips(self, got):
        if self.small is not None:
            got, self.small_all = got[:-1], got[-1]
        self.sums = [_add_own_half(p, g, self.core, "add_sibling_" + n) for (n, _), p, g in zip(self.mats, self.local, got)]
        return _hosted_chips([s16 for _, s16 in self.sums])

    def from_chips(self, got):
        self.halves = [_sum_chips(s32, r16, self.where, "sum_chips_" + n)
                       for (n, _), (s32, _), r16 in zip(self.mats, self.sums, got)]


def kernel(x, w_in, b_in, w_proj_sb, w_proj_fox, w_out, ln1_g, ln1_b, w_up, w_conv, b_conv, w_down, ln2_g, ln2_b, loss_target, m_w_in, m_b_in, m_w_proj_sb, m_w_proj_fox, m_w_out, m_ln1_g, m_ln1_b, m_w_up, m_w_conv, m_b_conv, m_w_down, m_ln2_g, m_ln2_b, v_w_in, v_b_in, v_w_proj_sb, v_w_proj_fox, v_w_out, v_ln1_g, v_ln1_b, v_w_up, v_w_conv, v_b_conv, v_w_down, v_ln2_g, v_ln2_b):
    w = dict(w_in=w_in, b_in=b_in, w_proj_sb=w_proj_sb, w_proj_fox=w_proj_fox, w_out=w_out, ln1_g=ln1_g, ln1_b=ln1_b,
             w_up=w_up, w_conv=w_conv, b_conv=b_conv, w_down=w_down, ln2_g=ln2_g, ln2_b=ln2_b)
    m = dict(w_in=m_w_in, b_in=m_b_in, w_proj_sb=m_w_proj_sb, w_proj_fox=m_w_proj_fox, w_out=m_w_out, ln1_g=m_ln1_g,
             ln1_b=m_ln1_b, w_up=m_w_up, w_conv=m_w_conv, b_conv=m_b_conv, w_down=m_w_down, ln2_g=m_ln2_g, ln2_b=m_ln2_b)
    v = dict(w_in=v_w_in, b_in=v_b_in, w_proj_sb=v_w_proj_sb, w_proj_fox=v_w_proj_fox, w_out=v_w_out, ln1_g=v_ln1_g,
             ln1_b=v_ln1_b, w_up=v_w_up, w_conv=v_w_conv, b_conv=v_b_conv, w_down=v_w_down, ln2_g=v_ln2_g, ln2_b=v_ln2_b)
    order = ["w_in", "b_in", "w_proj_sb", "w_proj_fox", "w_out", "ln1_g", "ln1_b", "w_up", "w_conv", "b_conv", "w_down",
             "ln2_g", "ln2_b"]
    x_idx, y_idx, c_idx = _pos()
    chip = 2 * x_idx + y_idx
    D = x.shape[-1]
    core = c_idx.astype(jnp.int32).reshape(1)

    w_in_own = w["w_in"][0].astype(bf16)
    w_in_stack = _gather_split(w_in_own)
    w_in_full = jnp.concatenate([jnp.where(chip == k, w_in_own, w_in_stack[k]) for k in range(4)], axis=1)
    late = (("w_proj_sb", 1), ("w_proj_fox", 1), ("w_out", 0), ("w_up", 1), ("w_conv", 1), ("w_down", 0))
    late_shards = [w[n][0] if n == "w_conv" else w[n][0].astype(bf16) for n, _ in late]
    late_weights = lambda stacks: [_unstack(s, axis) for (_, axis), s in zip(late, stacks)]

    where = jnp.stack([chip, c_idx, 2 * (1 - x_idx) + y_idx, 2 * x_idx + 1 - y_idx, 2 * (1 - x_idx) + 1 - y_idx]).astype(jnp.int32)
    small_names = list(_SMALL) + ["w_conv"]
    pack_small = lambda grads, sq_err: _pack_rows(
        [jnp.full((1, 1), (0.5 / D) * jnp.sum(sq_err), f32)] + [grads[n] for n in small_names])
    early = _Reducer(_MATS[1:], core, where)
    last = _Reducer(_MATS[:1], core, where, pack_small)
    sq_err, grad_x, grads = _local_step(x, loss_target, w_in_full, w["b_in"], w["ln1_g"], w["ln1_b"], w["b_conv"],
                                        w["ln2_g"], w["ln2_b"], late_shards, late_weights, early, last)
    g_shards = _share_halves(last.halves + early.halves)
    small_sum = _sum_slots(last.small_all, "sum_small")

    out = {"grad": {}, "delta": {}, "m": {}, "v": {}}
    for (n, _), g_ in zip(_MATS, g_shards):
        d_, m_, v_, g_out = _adamw(w[n][0], g_, m[n][0], v[n][0], "adamw_" + n)
        for key, t in (("grad", g_out), ("delta", d_), ("m", m_), ("v", v_)):
            out[key][n] = t.reshape(w[n].shape)
    sizes = [1] + [int(grads[n].size) for n in small_names]
    sm = _unpack_rows(small_sum, sizes)
    loss = sm[0][0, 0]
    g_small = dict(zip(small_names, sm[1:]))
    F4 = w["w_conv"].shape[-1]
    g_small["w_conv"] = lax.dynamic_slice_in_dim(g_small["w_conv"].reshape(3, -1), chip * F4, F4, axis=1)
    pack_s = lambda d: _pack_rows([d[n].reshape(1, -1) for n in small_names])
    gs_packed = _pack_rows([g_small[n].reshape(1, -1) for n in small_names])
    s_delta, s_m, s_v, _ = _adamw(pack_s(w), gs_packed, pack_s(m), pack_s(v), "adamw_small")
    s_sizes = [int(w[n].size) for n in small_names]

    for key, packed_s in (("grad", gs_packed), ("delta", s_delta), ("m", s_m), ("v", s_v)):
        for n, t in zip(small_names, _unpack_rows(packed_s, s_sizes)):
            out[key][n] = t.reshape(w[n].shape)
    return (loss, grad_x, *[out["grad"][n] for n in order], *[out["delta"][n] for n in order],
            *[out["m"][n] for n in order], *[out["v"][n] for n in order])
```

```python
import functools
import math

import jax
import jax.numpy as jnp
from jax import lax
from jax.experimental import pallas as pl
from jax.experimental.pallas import tpu as pltpu

f32, bf16 = jnp.float32, jnp.bfloat16

HEAD_DIM = 64
LANES = 128
LN_EPS = 1e-5
ALPHA = 2.0 ** 0.25
Q_SCALE = HEAD_DIM ** -0.5
ADAM_LR, ADAM_B1, ADAM_B2, ADAM_EPS, ADAM_WD, ADAM_STEP = 0.001, 0.9, 0.999, 1e-08, 0.01, 10
VMEM_LIMIT = 56 * 1024 * 1024
NEG = -1e30

_pcall = pl.pallas_call
_NT = (((1,), (1,)), ((), ()))
_TN = (((0,), (0,)), ((), ()))


def _params(sem=None):
    return pltpu.CompilerParams(dimension_semantics=sem, vmem_limit_bytes=VMEM_LIMIT)


def _tile(dim, target, unit=LANES):
    if dim <= target:
        return dim
    t = (target // unit) * unit
    while t > unit and dim % t:
        t -= unit
    assert dim % t == 0, (dim, target)
    return t


def _dot(a, b, dn=None):
    if dn is None:
        return jnp.dot(a, b, preferred_element_type=f32)
    return lax.dot_general(a, b, dn, preferred_element_type=f32)


def _split_dot(x, tri):
    hi = x.astype(bf16)
    lo = (x - hi.astype(f32)).astype(bf16)
    return _dot(hi, tri) + _dot(lo, tri)


SCAN_BLOCK = 256


def _scan_cols(x, tri, reverse, split=True):
    cb = tri.shape[0]
    nb = x.shape[1] // cb
    blocks = [x[:, b * cb:(b + 1) * cb] for b in range(nb)]
    outs, run = [None] * nb, None
    for b in (reversed(range(nb)) if reverse else range(nb)):
        o = _split_dot(blocks[b], tri) if split else _dot(blocks[b].astype(bf16), tri)
        s = jnp.sum(blocks[b], axis=1, keepdims=True)
        outs[b] = o if run is None else o + run
        run = s if run is None else run + s
    return (outs[0] if nb == 1 else jnp.concatenate(outs, axis=1)), run


def _tri(cb, rel):
    row = lax.broadcasted_iota(jnp.int32, (cb, cb), 0)
    col = lax.broadcasted_iota(jnp.int32, (cb, cb), 1)
    return rel(row, col).astype(bf16)


def _matmul(a, b, *, name, ta=False, tb=False, bias=None, addend=None, addend_scale=1.0, colsum=False,
            out_dtype=f32, tm=512, tn=512, tk=1024, hosted=None):
    M, K = (a.shape[1], a.shape[0]) if ta else a.shape
    N = b.shape[0] if tb else b.shape[1]
    assert K == (b.shape[1] if tb else b.shape[0])
    assert not (colsum and tb)
    tm, tn, tk = _tile(M, tm), _tile(N, tn), _tile(K, tk)
    nk = K // tk
    n_in = 2 + (bias is not None) + (addend is not None)
    n_out = 1 + colsum
    grid = (M // tm, N // tn, nk)

    def body(*refs):
        k = pl.program_id(2)
        step = (pl.program_id(0) * grid[1] + pl.program_id(1)) * nk + k
        scratch = refs[len(refs) - n_out - (len(hosted.sems) if hosted else 0):]
        own_in, own_out, finish = _host(hosted, refs[:len(refs) - len(scratch)] + scratch[n_out:], n_in, n_out, step,
                                        grid[0] * grid[1] * nk - 1)
        a_ref, b_ref = own_in[0], own_in[1]
        bias_ref = own_in[2] if bias is not None else None
        add_ref = own_in[n_in - 1] if addend is not None else None
        o_ref = own_out[0]
        cs_ref = own_out[1] if colsum else None
        acc = scratch[0]
        cs_acc = scratch[1] if colsum else None

        @pl.when(k == 0)
        def _():
            acc[...] = jnp.zeros_like(acc)
            if colsum:
                cs_acc[...] = jnp.zeros_like(cs_acc)

        dn = (((0 if ta else 1,), (1 if tb else 0,)), ((), ()))
        acc[...] += lax.dot_general(a_ref[...].astype(bf16), b_ref[...].astype(bf16), dn, preferred_element_type=f32)
        if colsum:
            cs_acc[...] += jnp.sum(b_ref[...].astype(f32), axis=0, keepdims=True)

        @pl.when(k == nk - 1)
        def _():
            r = acc[...]
            if bias is not None:
                r = r + bias_ref[...]
            if addend is not None:
                r = r + addend_scale * add_ref[...].astype(f32)
            o_ref[...] = r.astype(out_dtype)
            if colsum:
                cs_ref[0] = cs_acc[...]

        finish()

    a_spec = pl.BlockSpec((tk, tm), lambda i, j, k: (k, i)) if ta else pl.BlockSpec((tm, tk), lambda i, j, k: (i, k))
    b_spec = pl.BlockSpec((tn, tk), lambda i, j, k: (j, k)) if tb else pl.BlockSpec((tk, tn), lambda i, j, k: (k, j))
    in_specs, args = [a_spec, b_spec], [a, b]
    if bias is not None:
        in_specs.append(pl.BlockSpec((1, tn), lambda i, j, k: (0, j)))
        args.append(bias.reshape(1, N).astype(f32))
    if addend is not None:
        in_specs.append(pl.BlockSpec((tm, tn), lambda i, j, k: (i, j)))
        args.append(addend)
    out_shape = [jax.ShapeDtypeStruct((M, N), out_dtype)]
    out_specs = [pl.BlockSpec((tm, tn), lambda i, j, k: (i, j))]
    scratch = [pltpu.VMEM((tm, tn), f32)]
    if colsum:
        out_shape.append(jax.ShapeDtypeStruct((M // tm, 1, N), f32))
        out_specs.append(pl.BlockSpec((1, 1, tn), lambda i, j, k: (i, 0, j)))
        scratch.append(pltpu.VMEM((1, tn), f32))
    h_in = hosted.inputs if hosted else []
    h_out = hosted.outputs if hosted else []
    res = _pcall(body, name=name, grid=grid, in_specs=in_specs + [_ANY] * len(h_in),
                 out_specs=out_specs + [_ANY] * len(h_out), out_shape=out_shape + h_out,
                 scratch_shapes=scratch + (hosted.sems if hosted else []),
                 compiler_params=_params(("arbitrary", "arbitrary", "arbitrary")))(*args, *h_in)
    own = (res[0], res[1][0]) if colsum else res[0]
    return (own, list(res[n_out:])) if hosted else own


def _cumlogf(fl, B, S):
    t = _tile(S, 256, 8)

    def body(fl_ref, c_ref, carry):
        @pl.when(pl.program_id(1) == 0)
        def _():
            carry[...] = jnp.zeros_like(carry)
        z = fl_ref[...]
        ls = jnp.minimum(z, 0.0) - jnp.log(1.0 + jnp.exp(-jnp.abs(z)))
        row = lax.broadcasted_iota(jnp.int32, (t, t), 0)
        col = lax.broadcasted_iota(jnp.int32, (t, t), 1)
        lower = (col <= row).astype(f32)
        c = jnp.dot(lower, ls, precision=lax.Precision.HIGHEST, preferred_element_type=f32) + carry[...]
        c_ref[...] = c
        carry[...] = c[t - 1:t, :]

    return _pcall(body, name="cumlogf", grid=(B, S // t),
                  in_specs=[pl.BlockSpec((t, LANES), lambda b, i: (b * (S // t) + i, 0))],
                  out_specs=pl.BlockSpec((t, LANES), lambda b, i: (b * (S // t) + i, 0)),
                  out_shape=jax.ShapeDtypeStruct(fl.shape, f32), scratch_shapes=[pltpu.VMEM((1, LANES), f32)],
                  compiler_params=_params(("arbitrary", "arbitrary")))(fl)


def _cumlogf_bwd(dc, fl, B, S):
    t = _tile(S, 256, 8)
    n = S // t

    def body(dc_ref, fl_ref, o_ref, carry):
        @pl.when(pl.program_id(1) == 0)
        def _():
            carry[...] = jnp.zeros_like(carry)
        row = lax.broadcasted_iota(jnp.int32, (t, t), 0)
        col = lax.broadcasted_iota(jnp.int32, (t, t), 1)
        upper = (col >= row).astype(f32)
        r = jnp.dot(upper, dc_ref[...], precision=lax.Precision.HIGHEST, preferred_element_type=f32) + carry[...]
        carry[...] = r[0:1, :]
        z = fl_ref[...]
        o_ref[...] = r / (1.0 + jnp.exp(z))

    spec = pl.BlockSpec((t, LANES), lambda b, i: (b * n + n - 1 - i, 0))
    return _pcall(body, name="cumlogf_bwd", grid=(B, n), in_specs=[spec, spec], out_specs=spec,
                  out_shape=jax.ShapeDtypeStruct(fl.shape, f32), scratch_shapes=[pltpu.VMEM((1, LANES), f32)],
                  compiler_params=_params(("arbitrary", "arbitrary")))(dc, fl)


def _head_masks():
    lane = lax.broadcasted_iota(jnp.int32, (1, LANES), 1)
    return lane < HEAD_DIM


def _by_head(m0, t):
    z = jnp.zeros_like(t)
    return [jnp.where(m0, t, z), jnp.where(m0, z, t)]


def _sb_terms(z):
    softplus = jnp.maximum(z, 0.0) + jnp.log(1.0 + jnp.exp(-jnp.abs(z)))
    return z - softplus, -softplus


STRIP_ROWS = 32


def _strip_rows(tq):
    return STRIP_ROWS if tq % STRIP_ROWS == 0 else tq


def _strict(r, rs, tq):
    row = lax.broadcasted_iota(jnp.int32, (rs, tq), 0) + r
    col = lax.broadcasted_iota(jnp.int32, (rs, tq), 1)
    return col < row


def _score_scratch(tq, n_f32, n_bf16, n_sums):
    return ([pltpu.VMEM((tq, tq), f32)] * (2 * n_f32) + [pltpu.VMEM((tq, tq), bf16)] * (2 * n_bf16)
            + [pltpu.VMEM((tq, LANES), f32)] * (2 * n_sums))


def _by_pairs(refs):
    return [refs[i:i + 2] for i in range(0, len(refs), 2)]


def _sb_fwd(qkv, B, S, n_pairs, qcol, kcol, vcol, tq, shards=()):
    nq = S // tq
    T = B * S
    n = len(shards)

    cb = min(tq, SCAN_BLOCK)
    nb = tq // cb
    rs = _strip_rows(tq)

    def body(q_ref, k_ref, v_ref, *rest):
        o_ref, lt_ref = rest[n], rest[n + 1]
        z_s, suf_s, hi_s, lo_s, w_s, sum_s = _by_pairs(rest[len(rest) - 12:])
        i = pl.program_id(2)
        if n:
            start, finish = _gather_copies(rest[:n], rest[n + 2:2 * n + 2], *rest[2 * n + 2:2 * n + 5])
            step = (pl.program_id(0) * n_pairs + pl.program_id(1)) * nq + i
            pl.when(step == 0)(start)
        m0 = _head_masks()
        qh = _by_head(m0, q_ref[...])
        later = _tri(cb, lambda j, s: j > s)
        lane = lax.broadcasted_iota(jnp.int32, (1, LANES), 1)

        def tile(s0, R, acc, diag):
            kb = k_ref[pl.ds(s0, tq), :]
            vh = _by_head(m0, v_ref[pl.ds(s0, tq), :])
            R = list(R)
            for h in range(2):
                z_s[h][...] = _dot(qh[h], kb, _NT)
            for h in range(2):
                for r in range(0, tq, rs):
                    lb, l1m = _sb_terms(z_s[h][r:r + rs, :])
                    z_s[h][r:r + rs, :] = lb
                    if diag:
                        l1m = jnp.where(_strict(r, rs, tq), l1m, 0.0)
                    hi = l1m.astype(bf16)
                    hi_s[h][r:r + rs, :] = hi
                    lo_s[h][r:r + rs, :] = (l1m - hi.astype(f32)).astype(bf16)
                    sums = jnp.zeros((rs, LANES), f32)
                    for b in range(nb):
                        sums = jnp.where(lane == b, jnp.sum(l1m[:, b * cb:(b + 1) * cb], axis=1, keepdims=True), sums)
                    sum_s[h][r:r + rs, :] = sums
            for h in range(2):
                for b in range(nb):
                    blk = slice(b * cb, (b + 1) * cb)
                    suf_s[h][:, blk] = _dot(hi_s[h][:, blk], later) + _dot(lo_s[h][:, blk], later)
            for h in range(2):
                for r in range(0, tq, rs):
                    sums = sum_s[h][r:r + rs, :]
                    after = R[h][r:r + rs]
                    for b in reversed(range(nb)):
                        blk = slice(b * cb, (b + 1) * cb)
                        w = jnp.exp(z_s[h][r:r + rs, blk] + (suf_s[h][r:r + rs, blk] + after))
                        if diag:
                            w = jnp.where(_strict(r, rs, tq)[:, blk], w, 0.0)
                        w_s[h][r:r + rs, blk] = w.astype(bf16)
                        after = after + sums[:, b:b + 1]
            for h in range(2):
                acc = acc + _dot(w_s[h][...], vh[h])
                R[h] = R[h] + jnp.sum(sum_s[h][...], axis=1, keepdims=True)
            return R, acc

        zero = jnp.zeros((tq, 1), f32)
        R, acc = tile(pl.multiple_of(i * tq, tq), [zero, zero], jnp.zeros((tq, LANES), f32), True)

        def loop(n, carry):
            s0 = pl.multiple_of((i - 1 - n) * tq, tq)
            R, acc = tile(s0, carry[:2], carry[2], False)
            return R[0], R[1], acc

        R0, R1, acc = lax.fori_loop(0, i, loop, (R[0], R[1], acc))
        o_ref[...] = acc.astype(bf16)
        lt_ref[...] = jnp.where(m0, R0, R1)
        if n:
            pl.when(step == B * n_pairs * nq - 1)(finish)

    qs = lambda c: pl.BlockSpec((tq, LANES), lambda b, p, i: (b * nq + i, c + p))
    ks = lambda c: pl.BlockSpec((S, LANES), lambda b, p, i: (b, c + p))
    os_ = pl.BlockSpec((tq, LANES), lambda b, p, i: (b * nq + i, p))
    res = _pcall(body, name="sb_fwd", grid=(B, n_pairs, nq), in_specs=[qs(qcol), ks(kcol), ks(vcol)] + [_ANY] * n,
                 out_specs=[os_, os_] + [_ANY] * n,
                 out_shape=[jax.ShapeDtypeStruct((T, n_pairs * LANES), bf16), jax.ShapeDtypeStruct((T, n_pairs * LANES), f32)]
                 + _gather_shapes(shards), scratch_shapes=(_gather_sems(n) if n else []) + _score_scratch(tq, 2, 3, 1),
                 compiler_params=_params(("arbitrary", "arbitrary", "arbitrary")))(qkv, qkv, qkv, *shards)
    return res[0], res[1], res[2:]


def _sb_bwd(qkv, do, lt, B, S, n_pairs, qcol, kcol, vcol, tq, hosted=None):
    nq = S // tq
    T = B * S

    def body(*refs):
        i = pl.program_id(2)
        step = (pl.program_id(0) * n_pairs + pl.program_id(1)) * nq + i
        (q_ref, k_ref, v_ref, do_ref, lt_ref), (dq_ref, dk_ref, dv_ref), finish = _host(
            hosted, refs, 5, 3, step, B * n_pairs * nq - 1)

        @pl.when(i == 0)
        def _():
            dk_ref[...] = jnp.zeros_like(dk_ref)
            dv_ref[...] = jnp.zeros_like(dv_ref)

        m0 = _head_masks()
        qh = _by_head(m0, q_ref[...])
        doh = _by_head(m0, do_ref[...])
        lt = lt_ref[...]
        ltot = [lt[:, 0:1], lt[:, HEAD_DIM:HEAD_DIM + 1]]
        row = lax.broadcasted_iota(jnp.int32, (tq, tq), 0)
        col = lax.broadcasted_iota(jnp.int32, (tq, tq), 1)
        strict = col < row
        upto = _tri(min(tq, SCAN_BLOCK), lambda j, s: j <= s)
        before = _tri(min(tq, SCAN_BLOCK), lambda j, s: j < s)

        def tile(s0, CL, CP, dq, diag):
            kb = k_ref[pl.ds(s0, tq), :]
            vb = v_ref[pl.ds(s0, tq), :]
            kh = _by_head(m0, kb)
            CL, CP = list(CL), list(CP)
            dk = jnp.zeros((tq, LANES), f32)
            dv = jnp.zeros((tq, LANES), f32)
            for h in range(2):
                z = _dot(qh[h], kb, _NT)
                lb, l1m = _sb_terms(z)
                if diag:
                    l1m = jnp.where(strict, l1m, 0.0)
                pre, l_total = _scan_cols(l1m, upto, False)
                w = jnp.exp(lb + ((ltot[h] - CL[h]) - pre))
                if diag:
                    w = jnp.where(strict, w, 0.0)
                g = _dot(doh[h], vb, _NT) * w
                p, g_total = _scan_cols(g, before, False, split=False)
                dz = g - jnp.exp(lb) * (g + (p + CP[h]))
                if diag:
                    dz = jnp.where(strict, dz, 0.0)
                dzb = dz.astype(bf16)
                dq = dq + _dot(dzb, kh[h])
                dk = dk + _dot(dzb, qh[h], _TN)
                dv = dv + _dot(w.astype(bf16), doh[h], _TN)
                CL[h] = CL[h] + l_total
                CP[h] = CP[h] + g_total
            dk_ref[pl.ds(s0, tq), :] += dk
            dv_ref[pl.ds(s0, tq), :] += dv
            return CL, CP, dq

        zero = jnp.zeros((tq, 1), f32)

        def loop(n, carry):
            CL, CP, dq = tile(pl.multiple_of(n * tq, tq), carry[0:2], carry[2:4], carry[4], False)
            return CL[0], CL[1], CP[0], CP[1], dq

        c = lax.fori_loop(0, i, loop, (zero, zero, zero, zero, jnp.zeros((tq, LANES), f32)))
        _, _, dq = tile(pl.multiple_of(i * tq, tq), c[0:2], c[2:4], c[4], True)
        dq_ref[...] = dq * Q_SCALE
        finish()

    qs = lambda c: pl.BlockSpec((tq, LANES), lambda b, p, i: (b * nq + i, c + p))
    ks = lambda c: pl.BlockSpec((S, LANES), lambda b, p, i: (b, c + p))
    ts = pl.BlockSpec((tq, LANES), lambda b, p, i: (b * nq + i, p))
    fs = pl.BlockSpec((S, LANES), lambda b, p, i: (b, p))
    shp = jax.ShapeDtypeStruct((T, n_pairs * LANES), f32)
    return _host_call(body, "sb_bwd", (B, n_pairs, nq), [qs(qcol), ks(kcol), ks(vcol), ts, ts], [qkv, qkv, qkv, do, lt],
                      [ts, fs, fs], [shp, shp, shp], hosted)


def _fox_fwd(qkv, c, cT, B, S, n_pairs, qcol, kcol, vcol, tq):
    nq = S // tq
    T = B * S

    def body(q_ref, k_ref, v_ref, cq_ref, ck_ref, o_ref, o32_ref, lse_ref):
        p_idx = pl.program_id(1)
        i = pl.program_id(2)
        m0 = _head_masks()
        lane = lax.broadcasted_iota(jnp.int32, (1, LANES), 1)
        qh = _by_head(m0, q_ref[...])
        cq_all = cq_ref[...]
        cq = [jnp.sum(jnp.where(lane == 2 * p_idx + h, cq_all, 0.0), axis=1, keepdims=True) for h in range(2)]
        row = lax.broadcasted_iota(jnp.int32, (tq, tq), 0)
        col = lax.broadcasted_iota(jnp.int32, (tq, tq), 1)
        causal = col <= row

        def tile(s0, m, l, acc, diag):
            kb = k_ref[pl.ds(s0, tq), :]
            vh = _by_head(m0, v_ref[pl.ds(s0, tq), :])
            m, l = list(m), list(l)
            scale, add = [], []
            for h in range(2):
                z = _dot(qh[h], kb, _NT) + (cq[h] - ck_ref[h, :, pl.ds(s0, tq)])
                if diag:
                    z = jnp.where(causal, z, NEG)
                m_new = jnp.maximum(m[h], jnp.max(z, axis=1, keepdims=True))
                p = jnp.exp(z - m_new)
                a = jnp.exp(m[h] - m_new)
                l[h] = a * l[h] + jnp.sum(p, axis=1, keepdims=True)
                m[h] = m_new
                scale.append(a)
                add.append(_dot(p.astype(bf16), vh[h]))
            acc = acc * jnp.where(m0, scale[0], scale[1]) + add[0] + add[1]
            return m, l, acc

        neg = jnp.full((tq, 1), NEG, f32)
        zero = jnp.zeros((tq, 1), f32)
        m, l, acc = tile(pl.multiple_of(i * tq, tq), [neg, neg], [zero, zero], jnp.zeros((tq, LANES), f32), True)

        def loop(n, carry):
            m, l, acc = tile(pl.multiple_of(n * tq, tq), carry[0:2], carry[2:4], carry[4], False)
            return m[0], m[1], l[0], l[1], acc

        m0_, m1_, l0, l1, acc = lax.fori_loop(0, i, loop, (m[0], m[1], l[0], l[1], acc))
        o = acc * jnp.where(m0, 1.0 / l0, 1.0 / l1)
        o_ref[...] = o.astype(bf16)
        o32_ref[...] = o
        lse_ref[...] = jnp.where(m0, m0_ + jnp.log(l0), m1_ + jnp.log(l1))

    qs = lambda cc: pl.BlockSpec((tq, LANES), lambda b, p, i: (b * nq + i, cc + p))
    ks = lambda cc: pl.BlockSpec((S, LANES), lambda b, p, i: (b, cc + p))
    cqs = pl.BlockSpec((tq, LANES), lambda b, p, i: (b * nq + i, 0))
    cks = pl.BlockSpec((2, 1, S), lambda b, p, i: (b * n_pairs + p, 0, 0))
    os_ = pl.BlockSpec((tq, LANES), lambda b, p, i: (b * nq + i, p))
    shp = jax.ShapeDtypeStruct((T, n_pairs * LANES), f32)
    return _pcall(body, name="fox_fwd", grid=(B, n_pairs, nq), in_specs=[qs(qcol), ks(kcol), ks(vcol), cqs, cks],
                  out_specs=[os_, os_, os_], out_shape=[jax.ShapeDtypeStruct((T, n_pairs * LANES), bf16), shp, shp],
                  compiler_params=_params(("arbitrary", "arbitrary", "arbitrary")))(qkv, qkv, qkv, c, cT)


def _fox_bwd(qkv, c, cT, do, o, lse, B, S, n_pairs, qcol, kcol, vcol, tq, hosted=None):
    nq = S // tq
    T = B * S

    def body(*refs):
        p_idx = pl.program_id(1)
        i = pl.program_id(2)
        step = (pl.program_id(0) * n_pairs + p_idx) * nq + i
        (q_ref, k_ref, v_ref, cq_ref, ck_ref, do_ref, o_ref, lse_ref), (dq_ref, dk_ref, dv_ref, dc_ref), finish = _host(
            hosted, refs, 8, 4, step, B * n_pairs * nq - 1)

        @pl.when(i == 0)
        def _():
            dk_ref[...] = jnp.zeros_like(dk_ref)
            dv_ref[...] = jnp.zeros_like(dv_ref)
            dc_ref[...] = jnp.zeros_like(dc_ref)

        m0 = _head_masks()
        lane = lax.broadcasted_iota(jnp.int32, (1, LANES), 1)
        qh = _by_head(m0, q_ref[...])
        do2 = do_ref[...]
        doh = _by_head(m0, do2)
        prod = do2.astype(f32) * o_ref[...].astype(f32)
        delta = [jnp.sum(p, axis=1, keepdims=True) for p in _by_head(m0, prod)]
        ls = lse_ref[...]
        lse = [ls[:, 0:1], ls[:, HEAD_DIM:HEAD_DIM + 1]]
        cq_all = cq_ref[...]
        cq = [jnp.sum(jnp.where(lane == 2 * p_idx + h, cq_all, 0.0), axis=1, keepdims=True) for h in range(2)]
        row = lax.broadcasted_iota(jnp.int32, (tq, tq), 0)
        col = lax.broadcasted_iota(jnp.int32, (tq, tq), 1)
        causal = col <= row

        def tile(s0, dq, diag):
            kb = k_ref[pl.ds(s0, tq), :]
            vb = v_ref[pl.ds(s0, tq), :]
            kh = _by_head(m0, kb)
            dk = jnp.zeros((tq, LANES), f32)
            dv = jnp.zeros((tq, LANES), f32)
            for h in range(2):
                z = _dot(qh[h], kb, _NT) + (cq[h] - ck_ref[h, :, pl.ds(s0, tq)])
                p = jnp.exp(z - lse[h])
                if diag:
                    p = jnp.where(causal, p, 0.0)
                ds = p * (_dot(doh[h], vb, _NT) - delta[h])
                dsb = ds.astype(bf16)
                dq = dq + _dot(dsb, kh[h])
                dk = dk + _dot(dsb, qh[h], _TN)
                dv = dv + _dot(p.astype(bf16), doh[h], _TN)
                dc_ref[h, :, pl.ds(s0, tq)] -= jnp.sum(ds, axis=0, keepdims=True)
            dk_ref[pl.ds(s0, tq), :] += dk
            dv_ref[pl.ds(s0, tq), :] += dv
            return dq

        dq = lax.fori_loop(0, i, lambda n, dq: tile(pl.multiple_of(n * tq, tq), dq, False), jnp.zeros((tq, LANES), f32))
        dq = tile(pl.multiple_of(i * tq, tq), dq, True)
        dq_ref[...] = dq * Q_SCALE
        finish()

    qs = lambda cc: pl.BlockSpec((tq, LANES), lambda b, p, i: (b * nq + i, cc + p))
    ks = lambda cc: pl.BlockSpec((S, LANES), lambda b, p, i: (b, cc + p))
    cqs = pl.BlockSpec((tq, LANES), lambda b, p, i: (b * nq + i, 0))
    cks = pl.BlockSpec((2, 1, S), lambda b, p, i: (b * n_pairs + p, 0, 0))
    ts = pl.BlockSpec((tq, LANES), lambda b, p, i: (b * nq + i, p))
    fs = pl.BlockSpec((S, LANES), lambda b, p, i: (b, p))
    shp = jax.ShapeDtypeStruct((T, n_pairs * LANES), f32)
    return _host_call(body, "fox_bwd", (B, n_pairs, nq), [qs(qcol), ks(kcol), ks(vcol), cqs, cks, ts, ts, ts],
                      [qkv, qkv, qkv, c, cT, do, o, lse], [ts, fs, fs, cks],
                      [shp, shp, shp, jax.ShapeDtypeStruct(cT.shape, f32)], hosted)


def _sigmoid(x):
    return 1.0 / (1.0 + jnp.exp(-x))


def _mix_fwd(o_sb, o_fx, g, x, wp_sb, wp_fx, w_out, ln_g, ln_b, tm):
    T, D = x.shape
    E = o_sb.shape[1]
    tm = _tile(T, tm, 8)

    def body(osb_ref, ofx_ref, gsb_ref, gfx_ref, x_ref, wsb_ref, wfx_ref, wo_ref, lg_ref, lb_ref,
             xhat_ref, rstd_ref, x1_ref, mg_ref):
        y_sb = _dot(osb_ref[...], wsb_ref[...])
        y_fx = _dot(ofx_ref[...], wfx_ref[...])
        merged = (_sigmoid(gsb_ref[...]) * y_sb + _sigmoid(gfx_ref[...]) * y_fx).astype(bf16)
        r = ALPHA * x_ref[...] + _dot(merged, wo_ref[...])
        mean = jnp.mean(r, axis=1, keepdims=True)
        cen = r - mean
        rstd = lax.rsqrt(jnp.mean(cen * cen, axis=1, keepdims=True) + LN_EPS)
        xhat = cen * rstd
        xhat_ref[...] = xhat
        rstd_ref[...] = rstd
        x1_ref[...] = (xhat * lg_ref[...] + lb_ref[...]).astype(bf16)
        mg_ref[...] = merged

    rows = lambda w, c=0: pl.BlockSpec((tm, w), lambda i: (i, c))
    full = lambda a: pl.BlockSpec(a.shape, lambda i: (0, 0))
    return _pcall(body, name="mix_fwd", grid=(T // tm,),
                  in_specs=[rows(E), rows(E), rows(D, 0), rows(D, 1), rows(D), full(wp_sb), full(wp_fx), full(w_out),
                            full(ln_g), full(ln_b)],
                  out_specs=[rows(D), rows(1), rows(D), rows(D)],
                  out_shape=[jax.ShapeDtypeStruct((T, D), f32), jax.ShapeDtypeStruct((T, 1), f32),
                             jax.ShapeDtypeStruct((T, D), bf16), jax.ShapeDtypeStruct((T, D), bf16)],
                  compiler_params=_params(("arbitrary",)))(o_sb, o_fx, g, g, x, wp_sb, wp_fx, w_out, ln_g, ln_b)


def _mix_bwd(dr1, o_sb, o_fx, g, wp_sb, wp_fx, w_out, tm):
    T, D = dr1.shape
    E = o_sb.shape[1]
    tm = _tile(T, tm, 8)

    def body(dr_ref, osb_ref, ofx_ref, gsb_ref, gfx_ref, wsb_ref, wfx_ref, wo_ref,
             dysb_ref, dyfx_ref, dg_ref, dosb_ref, dofx_ref, sumsb_ref, sumfx_ref):
        @pl.when(pl.program_id(0) == 0)
        def _():
            sumsb_ref[...] = jnp.zeros_like(sumsb_ref)
            sumfx_ref[...] = jnp.zeros_like(sumfx_ref)
        dm = _dot(dr_ref[...].astype(bf16), wo_ref[...], _NT)
        for half, (o_ref, g_ref, w_ref, dy_ref, do_ref, sum_ref) in enumerate((
                (osb_ref, gsb_ref, wsb_ref, dysb_ref, dosb_ref, sumsb_ref),
                (ofx_ref, gfx_ref, wfx_ref, dyfx_ref, dofx_ref, sumfx_ref))):
            y = _dot(o_ref[...], w_ref[...])
            s = _sigmoid(g_ref[...])
            dy = (dm * s).astype(bf16)
            dy_ref[...] = dy
            dg = dm * y * s * (1.0 - s)
            dg_ref[:, half * D:(half + 1) * D] = dg.astype(bf16)
            sum_ref[0:1, :] += jnp.sum(dg, axis=0, keepdims=True)
            do_ref[...] = _dot(dy, w_ref[...], _NT).astype(bf16)

    rows = lambda w, c=0: pl.BlockSpec((tm, w), lambda i: (i, c))
    full = lambda a: pl.BlockSpec(a.shape, lambda i: (0, 0))
    acc = pl.BlockSpec((8, D), lambda i: (0, 0))
    res = _pcall(body, name="mix_bwd", grid=(T // tm,),
                 in_specs=[rows(D), rows(E), rows(E), rows(D, 0), rows(D, 1), full(wp_sb), full(wp_fx), full(w_out)],
                 out_specs=[rows(D), rows(D), rows(2 * D), rows(E), rows(E), acc, acc],
                 out_shape=[jax.ShapeDtypeStruct((T, D), bf16)] * 2 + [jax.ShapeDtypeStruct((T, 2 * D), bf16)]
                 + [jax.ShapeDtypeStruct((T, E), bf16)] * 2 + [jax.ShapeDtypeStruct((8, D), f32)] * 2,
                 compiler_params=_params(("arbitrary",)))(dr1, o_sb, o_fx, g, g, wp_sb, wp_fx, w_out)
    return res


def _ln_bwd(dy_a, dy_b, scale_b, xhat, rstd, ln_g, tm):
    T, D = xhat.shape
    tm = _tile(T, tm, 8)

    def body(a_ref, b_ref, xh_ref, rs_ref, g_ref, dr_ref, st_ref):
        @pl.when(pl.program_id(0) == 0)
        def _():
            st_ref[...] = jnp.zeros_like(st_ref)
        dy = a_ref[...] + scale_b * b_ref[...]
        xh = xh_ref[...]
        dxh = dy * g_ref[...]
        m1 = jnp.mean(dxh, axis=1, keepdims=True)
        m2 = jnp.mean(dxh * xh, axis=1, keepdims=True)
        dr_ref[...] = rs_ref[...] * (dxh - m1 - xh * m2)
        st_ref[0:1, :] += jnp.sum(dy * xh, axis=0, keepdims=True)
        st_ref[1:2, :] += jnp.sum(dy, axis=0, keepdims=True)

    rows = lambda w: pl.BlockSpec((tm, w), lambda i: (i, 0))
    return _pcall(body, name="ln1_bwd", grid=(T // tm,),
                  in_specs=[rows(D), rows(D), rows(D), rows(1), pl.BlockSpec((1, D), lambda i: (0, 0))],
                  out_specs=[rows(D), pl.BlockSpec((8, D), lambda i: (0, 0))],
                  out_shape=[jax.ShapeDtypeStruct((T, D), f32), jax.ShapeDtypeStruct((8, D), f32)],
                  compiler_params=_params(("arbitrary",)))(dy_a, dy_b, xhat, rstd, ln_g)


_INV_SQRT2 = 1.0 / math.sqrt(2.0)
_INV_SQRT2PI = 1.0 / math.sqrt(2.0 * math.pi)


def _conv_rows(ref, r0, rc, first, wc, bc):
    cur = ref[pl.ds(r0, rc), :]
    prev = ref[pl.ds(pl.multiple_of(jnp.maximum(r0 - 8, 0), 8), 8), :]
    prev = jnp.where(first, jnp.zeros_like(prev), prev)
    rid = lax.broadcasted_iota(jnp.int32, (rc, LANES), 0)
    s1 = jnp.where(rid == 0, prev[7:8, :], pltpu.roll(cur, 1, 0))
    s2 = jnp.where(rid == 0, prev[6:7, :], jnp.where(rid == 1, prev[7:8, :], pltpu.roll(cur, 2, 0)))
    conv = bc + wc[0:1, :] * s2 + wc[1:2, :] * s1 + wc[2:3, :] * cur
    return conv, (s2, s1, cur)


def _glu_fwd(u, w_conv, b_conv, B, S, rc=512):
    F = u.shape[1] // 2
    nf = F // LANES
    rc = _tile(S, rc, 8)

    def body(ug_ref, uv_ref, wc_ref, bc_ref, a_ref):
        wc, bc = wc_ref[...], bc_ref[...]

        def chunk(n, _):
            r0 = pl.multiple_of(n * rc, rc)
            c, _taps = _conv_rows(ug_ref, r0, rc, n == 0, wc, bc)
            gelu = 0.5 * c * (1.0 + lax.erf(c * _INV_SQRT2))
            a_ref[pl.ds(r0, rc), :] = (gelu * uv_ref[pl.ds(r0, rc), :]).astype(bf16)
            return 0

        lax.fori_loop(0, S // rc, chunk, 0)

    return _pcall(body, name="glu_fwd", grid=(B, nf),
                  in_specs=[pl.BlockSpec((S, LANES), lambda b, j: (b, j)), pl.BlockSpec((S, LANES), lambda b, j: (b, nf + j)),
                            pl.BlockSpec((3, LANES), lambda b, j: (0, j)), pl.BlockSpec((1, LANES), lambda b, j: (0, j))],
                  out_specs=pl.BlockSpec((S, LANES), lambda b, j: (b, j)),
                  out_shape=jax.ShapeDtypeStruct((B * S, F), bf16),
                  compiler_params=_params(("arbitrary", "arbitrary")))(u, u, w_conv, b_conv)


def _glu_bwd(u, da, w_conv, b_conv, B, S, rc=512):
    F = u.shape[1] // 2
    nf = F // LANES
    rc = _tile(S, rc, 8)
    nc = S // rc

    def body(ug_ref, uv_ref, da_ref, wc_ref, bc_ref, dug_ref, duv_ref, gw_ref, gb_ref, dc_ref):
        wc, bc = wc_ref[...], bc_ref[...]

        def chunk(n, carry):
            gw0, gw1, gw2, gb = carry
            r0 = pl.multiple_of(n * rc, rc)
            c, (s2, s1, cur) = _conv_rows(ug_ref, r0, rc, n == 0, wc, bc)
            cdf = 0.5 * (1.0 + lax.erf(c * _INV_SQRT2))
            da = da_ref[pl.ds(r0, rc), :]
            duv_ref[pl.ds(r0, rc), :] = (da * (c * cdf)).astype(bf16)
            dc = da * uv_ref[pl.ds(r0, rc), :] * (cdf + c * (_INV_SQRT2PI * jnp.exp(-0.5 * c * c)))
            dc_ref[pl.ds(r0, rc), :] = dc
            red = lambda t: jnp.sum(t, axis=0, keepdims=True)
            return gw0 + red(dc * s2), gw1 + red(dc * s1), gw2 + red(dc * cur), gb + red(dc)

        z = jnp.zeros((1, LANES), f32)
        gw0, gw1, gw2, gb = lax.fori_loop(0, nc, chunk, (z, z, z, z))
        gw_ref[0, 0:1, :] = gw0
        gw_ref[0, 1:2, :] = gw1
        gw_ref[0, 2:3, :] = gw2
        gb_ref[0] = gb

        def chunk2(n, _):
            r0 = pl.multiple_of(n * rc, rc)
            cur = dc_ref[pl.ds(r0, rc), :]
            nxt = dc_ref[pl.ds(pl.multiple_of(jnp.minimum(r0 + rc, S - 8), 8), 8), :]
            nxt = jnp.where(n == nc - 1, jnp.zeros_like(nxt), nxt)
            rid = lax.broadcasted_iota(jnp.int32, (rc, LANES), 0)
            a1 = jnp.where(rid == rc - 1, nxt[0:1, :], pltpu.roll(cur, rc - 1, 0))
            a2 = jnp.where(rid == rc - 1, nxt[1:2, :], jnp.where(rid == rc - 2, nxt[0:1, :], pltpu.roll(cur, rc - 2, 0)))
            dug_ref[pl.ds(r0, rc), :] = (wc[2:3, :] * cur + wc[1:2, :] * a1 + wc[0:1, :] * a2).astype(bf16)
            return 0

        lax.fori_loop(0, nc, chunk2, 0)

    blk = lambda off: pl.BlockSpec((S, LANES), lambda b, j: (b, off + j))
    return _pcall(body, name="glu_bwd", grid=(B, nf),
                  in_specs=[blk(0), blk(nf), blk(0), pl.BlockSpec((3, LANES), lambda b, j: (0, j)),
                            pl.BlockSpec((1, LANES), lambda b, j: (0, j))],
                  out_specs=[blk(0), blk(0), pl.BlockSpec((1, 3, LANES), lambda b, j: (b, 0, j)),
                             pl.BlockSpec((1, 1, LANES), lambda b, j: (b, 0, j))],
                  out_shape=[jax.ShapeDtypeStruct((B * S, F), bf16), jax.ShapeDtypeStruct((B * S, F), bf16),
                             jax.ShapeDtypeStruct((B, 3, F), f32), jax.ShapeDtypeStruct((B, 1, F), f32)],
                  scratch_shapes=[pltpu.VMEM((S, LANES), f32)],
                  compiler_params=_params(("arbitrary", "arbitrary")))(u, u, da, w_conv, b_conv)


def _down_loss(a, w_down, xhat1, ln1_g, ln1_b, ln2_g, ln2_b, tgt, tm):
    T, D = xhat1.shape
    F = a.shape[1]
    tm = _tile(T, tm, 8)

    def body(a_ref, w_ref, xh_ref, g1_ref, b1_ref, g2_ref, b2_ref, t_ref, dr_ref, st_ref):
        @pl.when(pl.program_id(0) == 0)
        def _():
            st_ref[...] = jnp.zeros_like(st_ref)
        x1 = xh_ref[...] * g1_ref[...] + b1_ref[...]
        r = ALPHA * x1 + _dot(a_ref[...], w_ref[...])
        mean = jnp.mean(r, axis=1, keepdims=True)
        cen = r - mean
        rstd = lax.rsqrt(jnp.mean(cen * cen, axis=1, keepdims=True) + LN_EPS)
        xh = cen * rstd
        err = (xh * g2_ref[...] + b2_ref[...]) - t_ref[...]
        dy = err * (1.0 / D)
        dxh = dy * g2_ref[...]
        m1 = jnp.mean(dxh, axis=1, keepdims=True)
        m2 = jnp.mean(dxh * xh, axis=1, keepdims=True)
        dr_ref[...] = rstd * (dxh - m1 - xh * m2)
        st_ref[0:1, :] += jnp.sum(dy * xh, axis=0, keepdims=True)
        st_ref[1:2, :] += jnp.sum(dy, axis=0, keepdims=True)
        st_ref[2:3, :] += jnp.sum(err * err, axis=0, keepdims=True)

    rows = lambda w: pl.BlockSpec((tm, w), lambda i: (i, 0))
    vec = pl.BlockSpec((1, D), lambda i: (0, 0))
    return _pcall(body, name="down_loss", grid=(T // tm,),
                  in_specs=[rows(F), pl.BlockSpec((F, D), lambda i: (0, 0)), rows(D), vec, vec, vec, vec, rows(D)],
                  out_specs=[rows(D), pl.BlockSpec((8, D), lambda i: (0, 0))],
                  out_shape=[jax.ShapeDtypeStruct((T, D), f32), jax.ShapeDtypeStruct((8, D), f32)],
                  compiler_params=_params(("arbitrary",)))(a, w_down, xhat1, ln1_g, ln1_b, ln2_g, ln2_b, tgt)


def _local_step(x, tgt, w_in, b_in, ln1_g, ln1_b, b_conv, ln2_g, ln2_b, late_shards, late_weights, reducer=None,
                tail=None, tq=512, tm=256):
    B, S, D = x.shape
    T = B * S
    E = (w_in.shape[1] - 2 * D) * HEAD_DIM // (6 * HEAD_DIM + 1)
    n_pairs = E // LANES
    NH = E // HEAD_DIM
    x2 = x.reshape(T, D)
    tgt2 = tgt.reshape(T, D)
    tq = _tile(S, tq, 8)

    c_f, c_g = 6 * E, 6 * E + NH
    w_qkv = w_in[:, :c_f]
    qscale = jnp.concatenate([jnp.full((E,), Q_SCALE, f32), jnp.ones((2 * E,), f32)] * 2)
    w_qkv_s = (w_qkv.astype(f32) * qscale).astype(bf16)
    b_qkv_s = b_in[:, :c_f] * qscale
    w_f = jnp.pad(w_in[:, c_f:c_g], ((0, 0), (0, LANES - NH)))
    b_f = jnp.pad(b_in[:, c_f:c_g], ((0, 0), (0, LANES - NH)))
    w_g = w_in[:, c_g:]
    b_g = b_in[:, c_g:]

    xb = x2.astype(bf16)
    qkv = _matmul(xb, w_qkv_s, bias=b_qkv_s, out_dtype=bf16, tm=1024, tn=1536, name="proj_qkv")
    g = _matmul(xb, w_g, bias=b_g, tm=1024, tn=1024, name="proj_gate")
    fl = _matmul(xb, w_f, bias=b_f, tm=1024, name="proj_forget")
    c = _cumlogf(fl, B, S)
    cT = c.reshape(B, S, LANES)[:, :, :NH].transpose(0, 2, 1).reshape(B * NH, 1, S)
    P = n_pairs
    o_sb, lt_sb, stacks = _sb_fwd(qkv, B, S, P, 0, P, 2 * P, tq, late_shards)
    wp_sb, wp_fx, w_out, w_up, w_conv, w_down = late_weights(stacks)
    F = w_down.shape[0]
    o_fx, o_fx32, lse_fx = _fox_fwd(qkv, c, cT, B, S, P, 3 * P, 4 * P, 5 * P, tq)
    xhat1, rstd1, x1b, merged = _mix_fwd(o_sb, o_fx, g, x2, wp_sb, wp_fx, w_out, ln1_g, ln1_b, tm)
    u = _matmul(x1b, w_up, tm=1024, tn=1408, name="ffn_up")
    a = _glu_fwd(u, w_conv, b_conv, B, S)
    dr2, st2 = _down_loss(a, w_down, xhat1, ln1_g, ln1_b, ln2_g, ln2_b, tgt2, tm)

    grads = {}
    grads["ln2_g"], grads["ln2_b"] = st2[0:1], st2[1:2]
    sq_err = st2[2:3]
    da = _matmul(dr2, w_down, tb=True, tm=1024, tn=1408, name="ffn_da")
    grads["w_down"] = _matmul(a, dr2, ta=True, tm=1408, tn=1024, tk=1024, name="grad_w_down")
    du_g, du_v, gwc, gbc = _glu_bwd(u, da, w_conv, b_conv, B, S)
    grads["w_conv"] = jnp.sum(gwc, axis=0)
    grads["b_conv"] = jnp.sum(gbc, axis=0)
    grads["w_up"] = jnp.concatenate(
        [_matmul(x1b, du_g, ta=True, tm=512, tn=2816, tk=1024, name="grad_w_up_gate"),
         _matmul(x1b, du_v, ta=True, tm=512, tn=2816, tk=1024, name="grad_w_up_val")], axis=1)
    dx1 = _matmul(du_g, w_up[:, :F], tb=True, tm=1024, tn=1024, tk=F, name="ffn_dx_gate")
    dx1 = _matmul(du_v, w_up[:, F:], tb=True, addend=dx1, tm=1024, tn=1024, tk=F, name="ffn_dx_val")
    dr1, st1 = _ln_bwd(dx1, dr2, ALPHA, xhat1, rstd1, ln1_g, tm)
    grads["ln1_g"], grads["ln1_b"] = st1[0:1], st1[1:2]

    dy_sb, dy_fx, dg, do_sb, do_fx, gsum_sb, gsum_fx = _mix_bwd(dr1, o_sb, o_fx, g, wp_sb, wp_fx, w_out, tm)
    grads["w_out"] = _matmul(merged, dr1, ta=True, tm=512, tn=1024, tk=1024, name="grad_w_out")
    grads["w_proj_sb"] = _matmul(o_sb, dy_sb, ta=True, tm=512, tn=1024, tk=1024, name="grad_w_proj_sb")
    grads["w_proj_fox"] = _matmul(o_fx, dy_fx, ta=True, tm=512, tn=1024, tk=1024, name="grad_w_proj_fox")
    (dq_f, dk_f, dv_f, dcT), got = _fox_bwd(qkv, c, cT, do_fx, o_fx32, lse_fx, B, S, P, 3 * P, 4 * P, 5 * P, tq,
                                            reducer.to_sibling(grads) if reducer else None)
    (dq_s, dk_s, dv_s), got = _sb_bwd(qkv, do_sb, lt_sb, B, S, P, 0, P, 2 * P, tq, reducer.to_chips(got) if reducer else None)
    if reducer:
        reducer.from_chips(got)
    dc = jnp.pad(dcT.reshape(B, NH, S).transpose(0, 2, 1), ((0, 0), (0, 0), (0, LANES - NH))).reshape(T, LANES)
    dfl = _cumlogf_bwd(dc, fl, B, S)
    dqkv = jnp.concatenate([dq_s, dk_s, dv_s, dq_f, dk_f, dv_f], axis=1)
    gw_qkv, gb_qkv = _matmul(x2, dqkv, ta=True, colsum=True, tm=512, tn=3072, tk=512, name="grad_w_qkv")
    gw_f, gb_f = _matmul(x2, dfl, ta=True, colsum=True, tm=512, tk=1024, name="grad_w_forget")
    gw_g = _matmul(x2, dg, ta=True, tm=512, tn=2048, tk=1024, name="grad_w_gate")
    grads["w_in"] = jnp.concatenate([gw_qkv, gw_f[:, :NH], gw_g], axis=1)
    grads["b_in"] = jnp.concatenate([gb_qkv, gb_f[:, :NH], gsum_sb[0:1], gsum_fx[0:1]], axis=1)
    dx = _matmul(dfl, w_f, tb=True, addend=dr1, addend_scale=ALPHA, tm=1024, tn=1024, name="dx_forget",
                 hosted=tail.to_sibling(grads, sq_err) if tail else None)
    if tail:
        dx, got = dx
        to_chips = tail.to_chips(got)
    dx = _matmul(dqkv, w_qkv, tb=True, addend=dx, tm=512, tn=1024, tk=c_f, name="dx_qkv", hosted=to_chips if tail else None)
    if tail:
        dx, got = dx
        tail.from_chips(got)
    dx = _matmul(dg, w_g, tb=True, addend=dx, tm=1024, tn=1024, tk=2 * D, name="dx_gate")
    return sq_err, dx.reshape(B, S, D), grads


_ANY = pl.BlockSpec(memory_space=pl.ANY)
_MESH = pl.DeviceIdType.MESH


def _pos():
    return lax.axis_index("x"), lax.axis_index("y"), lax.axis_index("c")


def _other_chips(x, y):
    return [(1 - x, y), (x, 1 - y), (1 - x, 1 - y)]


def _gather_split(shard):
    h = shard.shape[0] // 2

    def body(src_ref, dst_ref, send_sems, recv_sems, pass_send, pass_recv):
        x, y, c = _pos()
        k = 2 * x + y
        chips = _other_chips(x, y)
        slots = [2 * chip[0] + chip[1] for chip in chips]

        def rows(ref, half):
            return ref.at[pl.ds(pl.multiple_of(half * h, 16), h), :]

        def fetch(j, slot):
            return pltpu.make_async_remote_copy(src_ref=rows(src_ref, c), dst_ref=rows(dst_ref.at[slot], c),
                                                send_sem=send_sems.at[j], recv_sem=recv_sems.at[j],
                                                device_id=(*chips[j], c), device_id_type=_MESH)

        def pass_on(j, half):
            blk = rows(dst_ref.at[slots[j]], half)
            return pltpu.make_async_remote_copy(src_ref=blk, dst_ref=blk, send_sem=pass_send.at[j], recv_sem=pass_recv.at[j],
                                                device_id=(x, y, 1 - c), device_id_type=_MESH)

        for j in range(3):
            fetch(j, k).start()
        for j in range(3):
            fetch(j, slots[j]).wait_recv()
            pass_on(j, c).start()
        for j in range(3):
            pass_on(j, 1 - c).wait_recv()
        for j in range(3):
            fetch(j, k).wait_send()
            pass_on(j, c).wait_send()

    sems = pltpu.SemaphoreType.DMA((3,))
    return _pcall(body, name="gather_w_in", in_specs=[_ANY], out_specs=_ANY,
                  out_shape=jax.ShapeDtypeStruct((4,) + shard.shape, shard.dtype), scratch_shapes=[sems] * 4,
                  compiler_params=pltpu.CompilerParams(has_side_effects=True))(shard)


def _gather_shapes(shards):
    return [jax.ShapeDtypeStruct((4,) + s.shape, s.dtype) for s in shards]


def _gather_sems(n):
    return [pltpu.SemaphoreType.DMA((n, 3)), pltpu.SemaphoreType.DMA((n, 3)), pltpu.SemaphoreType.DMA((n,))]


def _gather_copies(srcs, dsts, send_sems, recv_sems, local_sems):
    n = len(srcs)
    x, y, c = _pos()
    k = 2 * x + y
    chips = _other_chips(x, y)

    def copy(a, j, chip, slot):
        return pltpu.make_async_remote_copy(src_ref=srcs[a], dst_ref=dsts[a].at[slot], send_sem=send_sems.at[a, j],
                                            recv_sem=recv_sems.at[a, j], device_id=(*chip, c), device_id_type=_MESH)

    def mine():
        return ([pltpu.make_async_copy(srcs[a], dsts[a].at[k], local_sems.at[a]) for a in range(n)],
                [copy(a, j, chip, k) for a in range(n) for j, chip in enumerate(chips)])

    def start():
        local, sends = mine()
        for cp in local + sends:
            cp.start()

    def finish():
        local, sends = mine()
        for a in range(n):
            for j, chip in enumerate(chips):
                copy(a, j, chip, 2 * chip[0] + chip[1]).wait_recv()
        for cp in sends:
            cp.wait_send()
        for cp in local:
            cp.wait()

    return start, finish


class _Hosted:
    def __init__(self, inputs, outputs, sems, copies):
        self.inputs, self.outputs, self.sems, self.copies = list(inputs), list(outputs), list(sems), copies


def _host(hosted, refs, n_in, n_out, step, last):
    if hosted is None:
        return refs[:n_in], refs[n_in:n_in + n_out], lambda: None
    hi, ho = len(hosted.inputs), len(hosted.outputs)
    own_in, h_in = refs[:n_in], refs[n_in:n_in + hi]
    own_out = refs[n_in + hi:n_in + hi + n_out]
    h_out = refs[n_in + hi + n_out:n_in + hi + n_out + ho]
    start, finish = hosted.copies(h_in, h_out, *refs[n_in + hi + n_out + ho:])
    pl.when(step == 0)(start)
    return own_in, own_out, lambda: pl.when(step == last)(finish)


def _host_call(body, name, grid, in_specs, args, out_specs, out_shape, hosted):
    h_in = hosted.inputs if hosted else []
    h_out = hosted.outputs if hosted else []
    res = _pcall(body, name=name, grid=grid, in_specs=list(in_specs) + [_ANY] * len(h_in),
                 out_specs=list(out_specs) + [_ANY] * len(h_out), out_shape=list(out_shape) + list(h_out),
                 scratch_shapes=hosted.sems if hosted else [],
                 compiler_params=_params(("arbitrary",) * len(grid)))(*args, *h_in)
    return list(res[:len(out_shape)]), list(res[len(out_shape):])


def _sibling_copies(shapes):
    n = len(shapes)

    def copies(p_refs, got_refs, send_sems, recv_sems):
        x, y, c = _pos()

        def copy(a):
            h = shapes[a][1] // 2
            src = p_refs[a].at[:, pl.ds(pl.multiple_of((1 - c) * h, 8), h), :]
            return pltpu.make_async_remote_copy(src_ref=src, dst_ref=got_refs[a], send_sem=send_sems.at[a],
                                                recv_sem=recv_sems.at[a], device_id=(x, y, 1 - c), device_id_type=_MESH)

        def start():
            for a in range(n):
                copy(a).start()

        def finish():
            for a in range(n):
                copy(a).wait()

        return start, finish

    return copies


def _hosted_sibling(pieces):
    n = len(pieces)
    return _Hosted(pieces, [jax.ShapeDtypeStruct((4, p.shape[1] // 2, p.shape[2]), p.dtype) for p in pieces],
                   [pltpu.SemaphoreType.DMA((n,)), pltpu.SemaphoreType.DMA((n,))], _sibling_copies([p.shape for p in pieces]))


def _chips_copies(n):
    def copies(p_refs, got_refs, send_sems, recv_sems):
        x, y, c = _pos()
        k = 2 * x + y
        chips = _other_chips(x, y)

        def copy(a, j, chip, piece, slot):
            return pltpu.make_async_remote_copy(src_ref=p_refs[a].at[piece], dst_ref=got_refs[a].at[slot],
                                                send_sem=send_sems.at[a, j], recv_sem=recv_sems.at[a, j],
                                                device_id=(*chip, c), device_id_type=_MESH)

        def start():
            for a in range(n):
                for j, chip in enumerate(chips):
                    copy(a, j, chip, 2 * chip[0] + chip[1], k).start()

        def finish():
            for a in range(n):
                for j, chip in enumerate(chips):
                    copy(a, j, chip, k, 2 * chip[0] + chip[1]).wait_recv()
            for a in range(n):
                for j, chip in enumerate(chips):
                    copy(a, j, chip, 2 * chip[0] + chip[1], k).wait_send()

        return start, finish

    return copies


def _hosted_chips(pieces):
    n = len(pieces)
    return _Hosted(pieces, [jax.ShapeDtypeStruct(p.shape, p.dtype) for p in pieces],
                   [pltpu.SemaphoreType.DMA((n, 3)), pltpu.SemaphoreType.DMA((n, 3))], _chips_copies(n))


def _small_copies(sm_ref, sg_ref, send_sems, recv_sems, local_sem):
    x, y, c = _pos()
    me = 4 * x + 2 * y + c
    flip = lambda v, bit: 1 - v if bit else v
    peers = [(flip(x, r & 4), flip(y, r & 2), flip(c, r & 1)) for r in range(1, 8)]

    def copy(j, slot):
        return pltpu.make_async_remote_copy(src_ref=sm_ref, dst_ref=sg_ref.at[slot], send_sem=send_sems.at[j],
                                            recv_sem=recv_sems.at[j], device_id=peers[j], device_id_type=_MESH)

    def local():
        return pltpu.make_async_copy(sm_ref, sg_ref.at[me], local_sem)

    def start():
        local().start()
        for j in range(7):
            copy(j, me).start()

    def finish():
        for j, (px, py, pc) in enumerate(peers):
            copy(j, 4 * px + 2 * py + pc).wait_recv()
        for j in range(7):
            copy(j, me).wait_send()
        local().wait()

    return start, finish


def _hosted_sibling_and_small(pieces, small):
    n = len(pieces)
    sibling = _hosted_sibling(pieces)

    def copies(in_refs, out_refs, big_send, big_recv, send_sems, recv_sems, local_sem):
        start_big, finish_big = sibling.copies(in_refs[:n], out_refs[:n], big_send, big_recv)
        start_small, finish_small = _small_copies(in_refs[n], out_refs[n], send_sems, recv_sems, local_sem)

        def start():
            start_big()
            start_small()

        def finish():
            finish_small()
            finish_big()

        return start, finish

    return _Hosted(pieces + [small], sibling.outputs + [jax.ShapeDtypeStruct((8,) + small.shape, small.dtype)],
                   sibling.sems + [pltpu.SemaphoreType.DMA((7,)), pltpu.SemaphoreType.DMA((7,)), pltpu.SemaphoreType.DMA(())],
                   copies)


def _share_halves(shards):
    n = len(shards)

    def body(*refs):
        full_refs = refs[n:2 * n]
        send_sems, recv_sems = refs[2 * n:]
        x, y, c = _pos()

        def copy(a, half):
            h = shards[a].shape[0] // 2
            rows = full_refs[a].at[pl.ds(pl.multiple_of(half * h, 8), h), :]
            return pltpu.make_async_remote_copy(src_ref=rows, dst_ref=rows, send_sem=send_sems.at[a],
                                                recv_sem=recv_sems.at[a], device_id=(x, y, 1 - c), device_id_type=_MESH)

        sends = [copy(a, c) for a in range(n)]
        for cp in sends:
            cp.start()
        for a in range(n):
            copy(a, 1 - c).wait_recv()
        for cp in sends:
            cp.wait_send()

    return _pcall(body, name="share_halves", in_specs=[_ANY] * n, out_specs=[_ANY] * n,
                  out_shape=[jax.ShapeDtypeStruct(s.shape, s.dtype) for s in shards],
                  input_output_aliases={a: a for a in range(n)},
                  scratch_shapes=[pltpu.SemaphoreType.DMA((n,)), pltpu.SemaphoreType.DMA((n,))],
                  compiler_params=pltpu.CompilerParams(has_side_effects=True))(*shards)


def _add_own_half(piece, got, core, name):
    _, r, cols = piece.shape
    h = r // 2

    def body(c_ref, a_ref, b_ref, o_ref, o16_ref):
        s = a_ref[0] + b_ref[...]
        o_ref[...] = s
        o16_ref[...] = s.astype(bf16)

    out = pl.BlockSpec((1, h, cols), lambda k, c: (k, 0, 0))
    grid_spec = pltpu.PrefetchScalarGridSpec(
        num_scalar_prefetch=1, grid=(4,),
        in_specs=[pl.BlockSpec((1, 1, h, cols), lambda k, c: (k, c[0], 0, 0)), out], out_specs=[out, out])
    return _pcall(body, name=name, grid_spec=grid_spec,
                  out_shape=[jax.ShapeDtypeStruct((4, h, cols), f32), jax.ShapeDtypeStruct((4, h, cols), bf16)],
                  compiler_params=_params(("arbitrary",)))(core, piece.reshape(4, 2, h, cols), got)


def _sum_chips(own, got, where, name):
    _, h, cols = own.shape
    t = _tile(h, 128, 16)
    nt = h // t

    def body(w_ref, own_ref, gx_ref, gy_ref, gxy_ref, o_ref):
        o_ref[...] = ((own_ref[0] + gx_ref[0].astype(f32)) + gy_ref[0].astype(f32)) + gxy_ref[0].astype(f32)

    slot = lambda j: pl.BlockSpec((1, t, cols), lambda i, w: (w[j], i, 0))
    grid_spec = pltpu.PrefetchScalarGridSpec(
        num_scalar_prefetch=1, grid=(nt,),
        in_specs=[slot(0), slot(2), slot(3), slot(4)],
        out_specs=pl.BlockSpec((t, cols), lambda i, w: (w[1] * nt + i, 0)))
    return _pcall(body, name=name, grid_spec=grid_spec, out_shape=jax.ShapeDtypeStruct((2 * h, cols), f32),
                  compiler_params=_params(("arbitrary",)))(where, own, got, got, got)


def _sum_slots(stack, name):
    k, n, cols = stack.shape
    t = _tile(n, 128, 8)

    def body(s_ref, o_ref):
        acc = s_ref[0]
        for i in range(1, k):
            acc = acc + s_ref[i]
        o_ref[...] = acc

    return _pcall(body, name=name, grid=(n // t,), in_specs=[pl.BlockSpec((k, t, cols), lambda i: (0, i, 0))],
                  out_specs=pl.BlockSpec((t, cols), lambda i: (i, 0)), out_shape=jax.ShapeDtypeStruct((n, cols), f32),
                  compiler_params=_params(("arbitrary",)))(stack)


def _adamw(w, g, m, v, name):
    n, cols = w.shape[-2:]
    t = _tile(n, 128, 8)
    c1 = 1.0 - ADAM_B1 ** ADAM_STEP
    c2 = 1.0 - ADAM_B2 ** ADAM_STEP
    at = (0,) if w.ndim == 3 else (Ellipsis,)

    def body(w_ref, g_ref, m_ref, v_ref, d_ref, nm_ref, nv_ref, g_out_ref):
        g = g_ref[...]
        nm = ADAM_B1 * m_ref[at] + (1.0 - ADAM_B1) * g
        nv = ADAM_B2 * v_ref[at] + (1.0 - ADAM_B2) * (g * g)
        d_ref[at] = -ADAM_LR * ((nm / c1) / (jnp.sqrt(nv / c2) + ADAM_EPS) + ADAM_WD * w_ref[at])
        nm_ref[at] = nm
        nv_ref[at] = nv
        g_out_ref[at] = g

    flat = pl.BlockSpec((t, cols), lambda i: (i, 0))
    spec = pl.BlockSpec((1, t, cols), lambda i: (0, i, 0)) if w.ndim == 3 else flat
    shp = jax.ShapeDtypeStruct(w.shape, f32)
    return _pcall(body, name=name, grid=(n // t,), in_specs=[spec, flat, spec, spec], out_specs=[spec] * 4,
                  out_shape=[shp] * 4, compiler_params=_params(("arbitrary",)))(w, g, m, v)


_MATS = (("w_in", 1), ("w_proj_sb", 1), ("w_proj_fox", 1), ("w_out", 0), ("w_up", 1), ("w_down", 0))
_SMALL = ("b_in", "ln1_g", "ln1_b", "b_conv", "ln2_g", "ln2_b")


def _pad_lanes(v):
    n = v.shape[-1]
    return jnp.pad(v, ((0, 0), (0, (-n) % LANES)))


def _pack_rows(vectors):
    flat = jnp.concatenate([_pad_lanes(v.reshape(1, -1)) for v in vectors], axis=1).reshape(-1, LANES)
    return jnp.pad(flat, ((0, (-flat.shape[0]) % 8), (0, 0)))


def _unpack_rows(packed, sizes):
    out, r = [], 0
    for n in sizes:
        rows = -(-n // LANES)
        out.append(packed[r:r + rows].reshape(1, rows * LANES)[:, :n])
        r += rows
    return out


def _unstack(stack, axis):
    if axis == 0:
        return stack.reshape(-1, stack.shape[2])
    return jnp.concatenate([stack[k] for k in range(4)], axis=1)


def _pieces(g, axis):
    if axis == 0:
        return g.reshape(4, g.shape[0] // 4, g.shape[1])
    cols = g.shape[1] // 4
    return jnp.stack([g[:, k * cols:(k + 1) * cols] for k in range(4)])


class _Reducer:
    def __init__(self, mats, core, where, small=None):
        self.mats, self.core, self.where, self.small = mats, core, where, small

    def to_sibling(self, grads, sq_err=None):
        self.local = [_pieces(grads[n], axis) for n, axis in self.mats]
        if self.small is None:
            return _hosted_sibling(self.local)
        return _hosted_sibling_and_small(self.local, self.small(grads, sq_err))

    def to_chips(self, got):
        if self.small is not None:
            got, self.small_all = got[:-1], got[-1]
        self.sums = [_add_own_half(p, g, self.core, "add_sibling_" + n) for (n, _), p, g in zip(self.mats, self.local, got)]
        return _hosted_chips([s16 for _, s16 in self.sums])

    def from_chips(self, got):
        self.halves = [_sum_chips(s32, r16, self.where, "sum_chips_" + n)
                       for (n, _), (s32, _), r16 in zip(self.mats, self.sums, got)]


def kernel(x, w_in, b_in, w_proj_sb, w_proj_fox, w_out, ln1_g, ln1_b, w_up, w_conv, b_conv, w_down, ln2_g, ln2_b, loss_target, m_w_in, m_b_in, m_w_proj_sb, m_w_proj_fox, m_w_out, m_ln1_g, m_ln1_b, m_w_up, m_w_conv, m_b_conv, m_w_down, m_ln2_g, m_ln2_b, v_w_in, v_b_in, v_w_proj_sb, v_w_proj_fox, v_w_out, v_ln1_g, v_ln1_b, v_w_up, v_w_conv, v_b_conv, v_w_down, v_ln2_g, v_ln2_b):
    w = dict(w_in=w_in, b_in=b_in, w_proj_sb=w_proj_sb, w_proj_fox=w_proj_fox, w_out=w_out, ln1_g=ln1_g, ln1_b=ln1_b,
             w_up=w_up, w_conv=w_conv, b_conv=b_conv, w_down=w_down, ln2_g=ln2_g, ln2_b=ln2_b)
    m = dict(w_in=m_w_in, b_in=m_b_in, w_proj_sb=m_w_proj_sb, w_proj_fox=m_w_proj_fox, w_out=m_w_out, ln1_g=m_ln1_g,
             ln1_b=m_ln1_b, w_up=m_w_up, w_conv=m_w_conv, b_conv=m_b_conv, w_down=m_w_down, ln2_g=m_ln2_g, ln2_b=m_ln2_b)
    v = dict(w_in=v_w_in, b_in=v_b_in, w_proj_sb=v_w_proj_sb, w_proj_fox=v_w_proj_fox, w_out=v_w_out, ln1_g=v_ln1_g,
             ln1_b=v_ln1_b, w_up=v_w_up, w_conv=v_w_conv, b_conv=v_b_conv, w_down=v_w_down, ln2_g=v_ln2_g, ln2_b=v_ln2_b)
    order = ["w_in", "b_in", "w_proj_sb", "w_proj_fox", "w_out", "ln1_g", "ln1_b", "w_up", "w_conv", "b_conv", "w_down",
             "ln2_g", "ln2_b"]
    x_idx, y_idx, c_idx = _pos()
    chip = 2 * x_idx + y_idx
    D = x.shape[-1]
    core = c_idx.astype(jnp.int32).reshape(1)

    w_in_own = w["w_in"][0].astype(bf16)
    w_in_stack = _gather_split(w_in_own)
    w_in_full = jnp.concatenate([jnp.where(chip == k, w_in_own, w_in_stack[k]) for k in range(4)], axis=1)
    late = (("w_proj_sb", 1), ("w_proj_fox", 1), ("w_out", 0), ("w_up", 1), ("w_conv", 1), ("w_down", 0))
    late_shards = [w[n][0] if n == "w_conv" else w[n][0].astype(bf16) for n, _ in late]
    late_weights = lambda stacks: [_unstack(s, axis) for (_, axis), s in zip(late, stacks)]

    where = jnp.stack([chip, c_idx, 2 * (1 - x_idx) + y_idx, 2 * x_idx + 1 - y_idx, 2 * (1 - x_idx) + 1 - y_idx]).astype(jnp.int32)
    small_names = list(_SMALL) + ["w_conv"]
    pack_small = lambda grads, sq_err: _pack_rows(
        [jnp.full((1, 1), (0.5 / D) * jnp.sum(sq_err), f32)] + [grads[n] for n in small_names])
    early = _Reducer(_MATS[1:], core, where)
    last = _Reducer(_MATS[:1], core, where, pack_small)
    sq_err, grad_x, grads = _local_step(x, loss_target, w_in_full, w["b_in"], w["ln1_g"], w["ln1_b"], w["b_conv"],
                                        w["ln2_g"], w["ln2_b"], late_shards, late_weights, early, last)
    g_shards = _share_halves(last.halves + early.halves)
    small_sum = _sum_slots(last.small_all, "sum_small")

    out = {"grad": {}, "delta": {}, "m": {}, "v": {}}
    for (n, _), g_ in zip(_MATS, g_shards):
        d_, m_, v_, g_out = _adamw(w[n], g_, m[n], v[n], "adamw_" + n)
        for key, t in (("grad", g_out), ("delta", d_), ("m", m_), ("v", v_)):
            out[key][n] = t.reshape(w[n].shape)
    sizes = [1] + [int(grads[n].size) for n in small_names]
    sm = _unpack_rows(small_sum, sizes)
    loss = sm[0][0, 0]
    g_small = dict(zip(small_names, sm[1:]))
    F4 = w["w_conv"].shape[-1]
    g_small["w_conv"] = lax.dynamic_slice_in_dim(g_small["w_conv"].reshape(3, -1), chip * F4, F4, axis=1)
    pack_s = lambda d: _pack_rows([d[n].reshape(1, -1) for n in small_names])
    gs_packed = _pack_rows([g_small[n].reshape(1, -1) for n in small_names])
    s_delta, s_m, s_v, _ = _adamw(pack_s(w), gs_packed, pack_s(m), pack_s(v), "adamw_small")
    s_sizes = [int(w[n].size) for n in small_names]

    for key, packed_s in (("grad", gs_packed), ("delta", s_delta), ("m", s_m), ("v", s_v)):
        for n, t in zip(small_names, _unpack_rows(packed_s, s_sizes)):
            out[key][n] = t.reshape(w[n].shape)
    return (loss, grad_x, *[out["grad"][n] for n in order], *[out["delta"][n] for n in order],
            *[out["m"][n] for n in order], *[out["v"][n] for n in order])
```

```python
import functools
import math

import jax
import jax.numpy as jnp
from jax import lax
from jax.experimental import pallas as pl
from jax.experimental.pallas import tpu as pltpu

f32, bf16 = jnp.float32, jnp.bfloat16

HEAD_DIM = 64
LANES = 128
LN_EPS = 1e-5
ALPHA = 2.0 ** 0.25
Q_SCALE = HEAD_DIM ** -0.5
ADAM_LR, ADAM_B1, ADAM_B2, ADAM_EPS, ADAM_WD, ADAM_STEP = 0.001, 0.9, 0.999, 1e-08, 0.01, 10
VMEM_LIMIT = 56 * 1024 * 1024
NEG = -1e30

_pcall = pl.pallas_call
_NT = (((1,), (1,)), ((), ()))
_TN = (((0,), (0,)), ((), ()))


def _params(sem=None):
    return pltpu.CompilerParams(dimension_semantics=sem, vmem_limit_bytes=VMEM_LIMIT)


def _tile(dim, target, unit=LANES):
    if dim <= target:
        return dim
    t = (target // unit) * unit
    while t > unit and dim % t:
        t -= unit
    assert dim % t == 0, (dim, target)
    return t


def _dot(a, b, dn=None):
    if dn is None:
        return jnp.dot(a, b, preferred_element_type=f32)
    return lax.dot_general(a, b, dn, preferred_element_type=f32)


def _split_dot(x, tri):
    hi = x.astype(bf16)
    lo = (x - hi.astype(f32)).astype(bf16)
    return _dot(hi, tri) + _dot(lo, tri)


SCAN_BLOCK = 256


def _scan_cols(x, tri, reverse, split=True):
    cb = tri.shape[0]
    nb = x.shape[1] // cb
    blocks = [x[:, b * cb:(b + 1) * cb] for b in range(nb)]
    outs, run = [None] * nb, None
    for b in (reversed(range(nb)) if reverse else range(nb)):
        o = _split_dot(blocks[b], tri) if split else _dot(blocks[b].astype(bf16), tri)
        s = jnp.sum(blocks[b], axis=1, keepdims=True)
        outs[b] = o if run is None else o + run
        run = s if run is None else run + s
    return (outs[0] if nb == 1 else jnp.concatenate(outs, axis=1)), run


def _tri(cb, rel):
    row = lax.broadcasted_iota(jnp.int32, (cb, cb), 0)
    col = lax.broadcasted_iota(jnp.int32, (cb, cb), 1)
    return rel(row, col).astype(bf16)


def _matmul(a, b, *, name, ta=False, tb=False, bias=None, addend=None, addend_scale=1.0, colsum=False,
            out_dtype=f32, tm=512, tn=512, tk=1024, hosted=None):
    M, K = (a.shape[1], a.shape[0]) if ta else a.shape
    N = b.shape[0] if tb else b.shape[1]
    assert K == (b.shape[1] if tb else b.shape[0])
    assert not (colsum and tb)
    tm, tn, tk = _tile(M, tm), _tile(N, tn), _tile(K, tk)
    nk = K // tk
    n_in = 2 + (bias is not None) + (addend is not None)
    n_out = 1 + colsum
    grid = (M // tm, N // tn, nk)

    def body(*refs):
        k = pl.program_id(2)
        step = (pl.program_id(0) * grid[1] + pl.program_id(1)) * nk + k
        scratch = refs[len(refs) - n_out - (len(hosted.sems) if hosted else 0):]
        own_in, own_out, finish = _host(hosted, refs[:len(refs) - len(scratch)] + scratch[n_out:], n_in, n_out, step,
                                        grid[0] * grid[1] * nk - 1)
        a_ref, b_ref = own_in[0], own_in[1]
        bias_ref = own_in[2] if bias is not None else None
        add_ref = own_in[n_in - 1] if addend is not None else None
        o_ref = own_out[0]
        cs_ref = own_out[1] if colsum else None
        acc = scratch[0]
        cs_acc = scratch[1] if colsum else None

        @pl.when(k == 0)
        def _():
            acc[...] = jnp.zeros_like(acc)
            if colsum:
                cs_acc[...] = jnp.zeros_like(cs_acc)

        dn = (((0 if ta else 1,), (1 if tb else 0,)), ((), ()))
        acc[...] += lax.dot_general(a_ref[...].astype(bf16), b_ref[...].astype(bf16), dn, preferred_element_type=f32)
        if colsum:
            cs_acc[...] += jnp.sum(b_ref[...].astype(f32), axis=0, keepdims=True)

        @pl.when(k == nk - 1)
        def _():
            r = acc[...]
            if bias is not None:
                r = r + bias_ref[...]
            if addend is not None:
                r = r + addend_scale * add_ref[...].astype(f32)
            o_ref[...] = r.astype(out_dtype)
            if colsum:
                cs_ref[0] = cs_acc[...]

        finish()

    a_spec = pl.BlockSpec((tk, tm), lambda i, j, k: (k, i)) if ta else pl.BlockSpec((tm, tk), lambda i, j, k: (i, k))
    b_spec = pl.BlockSpec((tn, tk), lambda i, j, k: (j, k)) if tb else pl.BlockSpec((tk, tn), lambda i, j, k: (k, j))
    in_specs, args = [a_spec, b_spec], [a, b]
    if bias is not None:
        in_specs.append(pl.BlockSpec((1, tn), lambda i, j, k: (0, j)))
        args.append(bias.reshape(1, N).astype(f32))
    if addend is not None:
        in_specs.append(pl.BlockSpec((tm, tn), lambda i, j, k: (i, j)))
        args.append(addend)
    out_shape = [jax.ShapeDtypeStruct((M, N), out_dtype)]
    out_specs = [pl.BlockSpec((tm, tn), lambda i, j, k: (i, j))]
    scratch = [pltpu.VMEM((tm, tn), f32)]
    if colsum:
        out_shape.append(jax.ShapeDtypeStruct((M // tm, 1, N), f32))
        out_specs.append(pl.BlockSpec((1, 1, tn), lambda i, j, k: (i, 0, j)))
        scratch.append(pltpu.VMEM((1, tn), f32))
    h_in = hosted.inputs if hosted else []
    h_out = hosted.outputs if hosted else []
    res = _pcall(body, name=name, grid=grid, in_specs=in_specs + [_ANY] * len(h_in),
                 out_specs=out_specs + [_ANY] * len(h_out), out_shape=out_shape + h_out,
                 scratch_shapes=scratch + (hosted.sems if hosted else []),
                 compiler_params=_params(("arbitrary", "arbitrary", "arbitrary")))(*args, *h_in)
    own = (res[0], res[1][0]) if colsum else res[0]
    return (own, list(res[n_out:])) if hosted else own


def _cumlogf(fl, B, S):
    t = _tile(S, 256, 8)

    def body(fl_ref, c_ref, carry):
        @pl.when(pl.program_id(1) == 0)
        def _():
            carry[...] = jnp.zeros_like(carry)
        z = fl_ref[...]
        ls = jnp.minimum(z, 0.0) - jnp.log(1.0 + jnp.exp(-jnp.abs(z)))
        row = lax.broadcasted_iota(jnp.int32, (t, t), 0)
        col = lax.broadcasted_iota(jnp.int32, (t, t), 1)
        lower = (col <= row).astype(f32)
        c = jnp.dot(lower, ls, precision=lax.Precision.HIGHEST, preferred_element_type=f32) + carry[...]
        c_ref[...] = c
        carry[...] = c[t - 1:t, :]

    return _pcall(body, name="cumlogf", grid=(B, S // t),
                  in_specs=[pl.BlockSpec((t, LANES), lambda b, i: (b * (S // t) + i, 0))],
                  out_specs=pl.BlockSpec((t, LANES), lambda b, i: (b * (S // t) + i, 0)),
                  out_shape=jax.ShapeDtypeStruct(fl.shape, f32), scratch_shapes=[pltpu.VMEM((1, LANES), f32)],
                  compiler_params=_params(("arbitrary", "arbitrary")))(fl)


def _cumlogf_bwd(dc, fl, B, S):
    t = _tile(S, 256, 8)
    n = S // t

    def body(dc_ref, fl_ref, o_ref, carry):
        @pl.when(pl.program_id(1) == 0)
        def _():
            carry[...] = jnp.zeros_like(carry)
        row = lax.broadcasted_iota(jnp.int32, (t, t), 0)
        col = lax.broadcasted_iota(jnp.int32, (t, t), 1)
        upper = (col >= row).astype(f32)
        r = jnp.dot(upper, dc_ref[...], precision=lax.Precision.HIGHEST, preferred_element_type=f32) + carry[...]
        carry[...] = r[0:1, :]
        z = fl_ref[...]
        o_ref[...] = r / (1.0 + jnp.exp(z))

    spec = pl.BlockSpec((t, LANES), lambda b, i: (b * n + n - 1 - i, 0))
    return _pcall(body, name="cumlogf_bwd", grid=(B, n), in_specs=[spec, spec], out_specs=spec,
                  out_shape=jax.ShapeDtypeStruct(fl.shape, f32), scratch_shapes=[pltpu.VMEM((1, LANES), f32)],
                  compiler_params=_params(("arbitrary", "arbitrary")))(dc, fl)


def _head_masks():
    lane = lax.broadcasted_iota(jnp.int32, (1, LANES), 1)
    return lane < HEAD_DIM


def _by_head(m0, t):
    z = jnp.zeros_like(t)
    return [jnp.where(m0, t, z), jnp.where(m0, z, t)]


def _sb_terms(z):
    softplus = jnp.maximum(z, 0.0) + jnp.log(1.0 + jnp.exp(-jnp.abs(z)))
    return z - softplus, -softplus


STRIP_ROWS = 32


def _strip_rows(tq):
    return STRIP_ROWS if tq % STRIP_ROWS == 0 else tq


def _strict(r, rs, tq):
    row = lax.broadcasted_iota(jnp.int32, (rs, tq), 0) + r
    col = lax.broadcasted_iota(jnp.int32, (rs, tq), 1)
    return col < row


def _score_scratch(tq, n_f32, n_bf16, n_sums):
    return ([pltpu.VMEM((tq, tq), f32)] * (2 * n_f32) + [pltpu.VMEM((tq, tq), bf16)] * (2 * n_bf16)
            + [pltpu.VMEM((tq, LANES), f32)] * (2 * n_sums))


def _by_pairs(refs):
    return [refs[i:i + 2] for i in range(0, len(refs), 2)]


def _sb_fwd(qkv, B, S, n_pairs, qcol, kcol, vcol, tq, shards=()):
    nq = S // tq
    T = B * S
    n = len(shards)

    cb = min(tq, SCAN_BLOCK)
    nb = tq // cb
    rs = _strip_rows(tq)

    def body(q_ref, k_ref, v_ref, *rest):
        o_ref, lt_ref = rest[n], rest[n + 1]
        z_s, suf_s, hi_s, lo_s, w_s, sum_s = _by_pairs(rest[len(rest) - 12:])
        i = pl.program_id(2)
        if n:
            start, finish = _gather_copies(rest[:n], rest[n + 2:2 * n + 2], *rest[2 * n + 2:2 * n + 5])
            step = (pl.program_id(0) * n_pairs + pl.program_id(1)) * nq + i
            pl.when(step == 0)(start)
        m0 = _head_masks()
        qh = _by_head(m0, q_ref[...])
        later = _tri(cb, lambda j, s: j > s)
        lane = lax.broadcasted_iota(jnp.int32, (1, LANES), 1)

        def tile(s0, R, acc, diag):
            kb = k_ref[pl.ds(s0, tq), :]
            vh = _by_head(m0, v_ref[pl.ds(s0, tq), :])
            R = list(R)
            for h in range(2):
                z_s[h][...] = _dot(qh[h], kb, _NT)
            for h in range(2):
                for r in range(0, tq, rs):
                    lb, l1m = _sb_terms(z_s[h][r:r + rs, :])
                    z_s[h][r:r + rs, :] = lb
                    if diag:
                        l1m = jnp.where(_strict(r, rs, tq), l1m, 0.0)
                    hi = l1m.astype(bf16)
                    hi_s[h][r:r + rs, :] = hi
                    lo_s[h][r:r + rs, :] = (l1m - hi.astype(f32)).astype(bf16)
                    sums = jnp.zeros((rs, LANES), f32)
                    for b in range(nb):
                        sums = jnp.where(lane == b, jnp.sum(l1m[:, b * cb:(b + 1) * cb], axis=1, keepdims=True), sums)
                    sum_s[h][r:r + rs, :] = sums
            for h in range(2):
                for b in range(nb):
                    blk = slice(b * cb, (b + 1) * cb)
                    suf_s[h][:, blk] = _dot(hi_s[h][:, blk], later) + _dot(lo_s[h][:, blk], later)
            for h in range(2):
                for r in range(0, tq, rs):
                    sums = sum_s[h][r:r + rs, :]
                    after = R[h][r:r + rs]
                    for b in reversed(range(nb)):
                        blk = slice(b * cb, (b + 1) * cb)
                        w = jnp.exp(z_s[h][r:r + rs, blk] + (suf_s[h][r:r + rs, blk] + after))
                        if diag:
                            w = jnp.where(_strict(r, rs, tq)[:, blk], w, 0.0)
                        w_s[h][r:r + rs, blk] = w.astype(bf16)
                        after = after + sums[:, b:b + 1]
            for h in range(2):
                acc = acc + _dot(w_s[h][...], vh[h])
                R[h] = R[h] + jnp.sum(sum_s[h][...], axis=1, keepdims=True)
            return R, acc

        zero = jnp.zeros((tq, 1), f32)
        R, acc = tile(pl.multiple_of(i * tq, tq), [zero, zero], jnp.zeros((tq, LANES), f32), True)

        def loop(n, carry):
            s0 = pl.multiple_of((i - 1 - n) * tq, tq)
            R, acc = tile(s0, carry[:2], carry[2], False)
            return R[0], R[1], acc

        R0, R1, acc = lax.fori_loop(0, i, loop, (R[0], R[1], acc))
        o_ref[...] = acc.astype(bf16)
        lt_ref[...] = jnp.where(m0, R0, R1)
        if n:
            pl.when(step == B * n_pairs * nq - 1)(finish)

    qs = lambda c: pl.BlockSpec((tq, LANES), lambda b, p, i: (b * nq + i, c + p))
    ks = lambda c: pl.BlockSpec((S, LANES), lambda b, p, i: (b, c + p))
    os_ = pl.BlockSpec((tq, LANES), lambda b, p, i: (b * nq + i, p))
    res = _pcall(body, name="sb_fwd", grid=(B, n_pairs, nq), in_specs=[qs(qcol), ks(kcol), ks(vcol)] + [_ANY] * n,
                 out_specs=[os_, os_] + [_ANY] * n,
                 out_shape=[jax.ShapeDtypeStruct((T, n_pairs * LANES), bf16), jax.ShapeDtypeStruct((T, n_pairs * LANES), f32)]
                 + _gather_shapes(shards), scratch_shapes=(_gather_sems(n) if n else []) + _score_scratch(tq, 2, 3, 1),
                 compiler_params=_params(("arbitrary", "arbitrary", "arbitrary")))(qkv, qkv, qkv, *shards)
    return res[0], res[1], res[2:]


def _sb_bwd(qkv, do, lt, B, S, n_pairs, qcol, kcol, vcol, tq, hosted=None):
    nq = S // tq
    T = B * S

    def body(*refs):
        i = pl.program_id(2)
        step = (pl.program_id(0) * n_pairs + pl.program_id(1)) * nq + i
        (q_ref, k_ref, v_ref, do_ref, lt_ref), (dq_ref, dk_ref, dv_ref), finish = _host(
            hosted, refs, 5, 3, step, B * n_pairs * nq - 1)

        @pl.when(i == 0)
        def _():
            dk_ref[...] = jnp.zeros_like(dk_ref)
            dv_ref[...] = jnp.zeros_like(dv_ref)

        m0 = _head_masks()
        qh = _by_head(m0, q_ref[...])
        doh = _by_head(m0, do_ref[...])
        lt = lt_ref[...]
        ltot = [lt[:, 0:1], lt[:, HEAD_DIM:HEAD_DIM + 1]]
        row = lax.broadcasted_iota(jnp.int32, (tq, tq), 0)
        col = lax.broadcasted_iota(jnp.int32, (tq, tq), 1)
        strict = col < row
        upto = _tri(min(tq, SCAN_BLOCK), lambda j, s: j <= s)
        before = _tri(min(tq, SCAN_BLOCK), lambda j, s: j < s)

        def tile(s0, CL, CP, dq, diag):
            kb = k_ref[pl.ds(s0, tq), :]
            vb = v_ref[pl.ds(s0, tq), :]
            kh = _by_head(m0, kb)
            CL, CP = list(CL), list(CP)
            dk = jnp.zeros((tq, LANES), f32)
            dv = jnp.zeros((tq, LANES), f32)
            for h in range(2):
                z = _dot(qh[h], kb, _NT)
                lb, l1m = _sb_terms(z)
                if diag:
                    l1m = jnp.where(strict, l1m, 0.0)
                pre, l_total = _scan_cols(l1m, upto, False)
                w = jnp.exp(lb + ((ltot[h] - CL[h]) - pre))
                if diag:
                    w = jnp.where(strict, w, 0.0)
                g = _dot(doh[h], vb, _NT) * w
                p, g_total = _scan_cols(g, before, False, split=False)
                dz = g - jnp.exp(lb) * (g + (p + CP[h]))
                if diag:
                    dz = jnp.where(strict, dz, 0.0)
                dzb = dz.astype(bf16)
                dq = dq + _dot(dzb, kh[h])
                dk = dk + _dot(dzb, qh[h], _TN)
                dv = dv + _dot(w.astype(bf16), doh[h], _TN)
                CL[h] = CL[h] + l_total
                CP[h] = CP[h] + g_total
            dk_ref[pl.ds(s0, tq), :] += dk
            dv_ref[pl.ds(s0, tq), :] += dv
            return CL, CP, dq

        zero = jnp.zeros((tq, 1), f32)

        def loop(n, carry):
            CL, CP, dq = tile(pl.multiple_of(n * tq, tq), carry[0:2], carry[2:4], carry[4], False)
            return CL[0], CL[1], CP[0], CP[1], dq

        c = lax.fori_loop(0, i, loop, (zero, zero, zero, zero, jnp.zeros((tq, LANES), f32)))
        _, _, dq = tile(pl.multiple_of(i * tq, tq), c[0:2], c[2:4], c[4], True)
        dq_ref[...] = dq * Q_SCALE
        finish()

    qs = lambda c: pl.BlockSpec((tq, LANES), lambda b, p, i: (b * nq + i, c + p))
    ks = lambda c: pl.BlockSpec((S, LANES), lambda b, p, i: (b, c + p))
    ts = pl.BlockSpec((tq, LANES), lambda b, p, i: (b * nq + i, p))
    fs = pl.BlockSpec((S, LANES), lambda b, p, i: (b, p))
    shp = jax.ShapeDtypeStruct((T, n_pairs * LANES), f32)
    return _host_call(body, "sb_bwd", (B, n_pairs, nq), [qs(qcol), ks(kcol), ks(vcol), ts, ts], [qkv, qkv, qkv, do, lt],
                      [ts, fs, fs], [shp, shp, shp], hosted)


def _fox_fwd(qkv, c, cT, B, S, n_pairs, qcol, kcol, vcol, tq):
    nq = S // tq
    T = B * S

    def body(q_ref, k_ref, v_ref, cq_ref, ck_ref, o_ref, o32_ref, lse_ref):
        p_idx = pl.program_id(1)
        i = pl.program_id(2)
        m0 = _head_masks()
        lane = lax.broadcasted_iota(jnp.int32, (1, LANES), 1)
        qh = _by_head(m0, q_ref[...])
        cq_all = cq_ref[...]
        cq = [jnp.sum(jnp.where(lane == 2 * p_idx + h, cq_all, 0.0), axis=1, keepdims=True) for h in range(2)]
        row = lax.broadcasted_iota(jnp.int32, (tq, tq), 0)
        col = lax.broadcasted_iota(jnp.int32, (tq, tq), 1)
        causal = col <= row

        def tile(s0, m, l, acc, diag):
            kb = k_ref[pl.ds(s0, tq), :]
            vh = _by_head(m0, v_ref[pl.ds(s0, tq), :])
            m, l = list(m), list(l)
            scale, add = [], []
            for h in range(2):
                z = _dot(qh[h], kb, _NT) + (cq[h] - ck_ref[h, :, pl.ds(s0, tq)])
                if diag:
                    z = jnp.where(causal, z, NEG)
                m_new = jnp.maximum(m[h], jnp.max(z, axis=1, keepdims=True))
                p = jnp.exp(z - m_new)
                a = jnp.exp(m[h] - m_new)
                l[h] = a * l[h] + jnp.sum(p, axis=1, keepdims=True)
                m[h] = m_new
                scale.append(a)
                add.append(_dot(p.astype(bf16), vh[h]))
            acc = acc * jnp.where(m0, scale[0], scale[1]) + add[0] + add[1]
            return m, l, acc

        neg = jnp.full((tq, 1), NEG, f32)
        zero = jnp.zeros((tq, 1), f32)
        m, l, acc = tile(pl.multiple_of(i * tq, tq), [neg, neg], [zero, zero], jnp.zeros((tq, LANES), f32), True)

        def loop(n, carry):
            m, l, acc = tile(pl.multiple_of(n * tq, tq), carry[0:2], carry[2:4], carry[4], False)
            return m[0], m[1], l[0], l[1], acc

        m0_, m1_, l0, l1, acc = lax.fori_loop(0, i, loop, (m[0], m[1], l[0], l[1], acc))
        o = acc * jnp.where(m0, 1.0 / l0, 1.0 / l1)
        o_ref[...] = o.astype(bf16)
        o32_ref[...] = o
        lse_ref[...] = jnp.where(m0, m0_ + jnp.log(l0), m1_ + jnp.log(l1))

    qs = lambda cc: pl.BlockSpec((tq, LANES), lambda b, p, i: (b * nq + i, cc + p))
    ks = lambda cc: pl.BlockSpec((S, LANES), lambda b, p, i: (b, cc + p))
    cqs = pl.BlockSpec((tq, LANES), lambda b, p, i: (b * nq + i, 0))
    cks = pl.BlockSpec((2, 1, S), lambda b, p, i: (b * n_pairs + p, 0, 0))
    os_ = pl.BlockSpec((tq, LANES), lambda b, p, i: (b * nq + i, p))
    shp = jax.ShapeDtypeStruct((T, n_pairs * LANES), f32)
    return _pcall(body, name="fox_fwd", grid=(B, n_pairs, nq), in_specs=[qs(qcol), ks(kcol), ks(vcol), cqs, cks],
                  out_specs=[os_, os_, os_], out_shape=[jax.ShapeDtypeStruct((T, n_pairs * LANES), bf16), shp, shp],
                  compiler_params=_params(("arbitrary", "arbitrary", "arbitrary")))(qkv, qkv, qkv, c, cT)


def _fox_bwd(qkv, c, cT, do, o, lse, B, S, n_pairs, qcol, kcol, vcol, tq, hosted=None):
    nq = S // tq
    T = B * S

    def body(*refs):
        p_idx = pl.program_id(1)
        i = pl.program_id(2)
        step = (pl.program_id(0) * n_pairs + p_idx) * nq + i
        (q_ref, k_ref, v_ref, cq_ref, ck_ref, do_ref, o_ref, lse_ref), (dq_ref, dk_ref, dv_ref, dc_ref), finish = _host(
            hosted, refs, 8, 4, step, B * n_pairs * nq - 1)

        @pl.when(i == 0)
        def _():
            dk_ref[...] = jnp.zeros_like(dk_ref)
            dv_ref[...] = jnp.zeros_like(dv_ref)
            dc_ref[...] = jnp.zeros_like(dc_ref)

        m0 = _head_masks()
        lane = lax.broadcasted_iota(jnp.int32, (1, LANES), 1)
        qh = _by_head(m0, q_ref[...])
        do2 = do_ref[...]
        doh = _by_head(m0, do2)
        prod = do2.astype(f32) * o_ref[...].astype(f32)
        delta = [jnp.sum(p, axis=1, keepdims=True) for p in _by_head(m0, prod)]
        ls = lse_ref[...]
        lse = [ls[:, 0:1], ls[:, HEAD_DIM:HEAD_DIM + 1]]
        cq_all = cq_ref[...]
        cq = [jnp.sum(jnp.where(lane == 2 * p_idx + h, cq_all, 0.0), axis=1, keepdims=True) for h in range(2)]
        row = lax.broadcasted_iota(jnp.int32, (tq, tq), 0)
        col = lax.broadcasted_iota(jnp.int32, (tq, tq), 1)
        causal = col <= row

        def tile(s0, dq, diag):
            kb = k_ref[pl.ds(s0, tq), :]
            vb = v_ref[pl.ds(s0, tq), :]
            kh = _by_head(m0, kb)
            dk = jnp.zeros((tq, LANES), f32)
            dv = jnp.zeros((tq, LANES), f32)
            for h in range(2):
                z = _dot(qh[h], kb, _NT) + (cq[h] - ck_ref[h, :, pl.ds(s0, tq)])
                p = jnp.exp(z - lse[h])
                if diag:
                    p = jnp.where(causal, p, 0.0)
                ds = p * (_dot(doh[h], vb, _NT) - delta[h])
                dsb = ds.astype(bf16)
                dq = dq + _dot(dsb, kh[h])
                dk = dk + _dot(dsb, qh[h], _TN)
                dv = dv + _dot(p.astype(bf16), doh[h], _TN)
                dc_ref[h, :, pl.ds(s0, tq)] -= jnp.sum(ds, axis=0, keepdims=True)
            dk_ref[pl.ds(s0, tq), :] += dk
            dv_ref[pl.ds(s0, tq), :] += dv
            return dq

        dq = lax.fori_loop(0, i, lambda n, dq: tile(pl.multiple_of(n * tq, tq), dq, False), jnp.zeros((tq, LANES), f32))
        dq = tile(pl.multiple_of(i * tq, tq), dq, True)
        dq_ref[...] = dq * Q_SCALE
        finish()

    qs = lambda cc: pl.BlockSpec((tq, LANES), lambda b, p, i: (b * nq + i, cc + p))
    ks = lambda cc: pl.BlockSpec((S, LANES), lambda b, p, i: (b, cc + p))
    cqs = pl.BlockSpec((tq, LANES), lambda b, p, i: (b * nq + i, 0))
    cks = pl.BlockSpec((2, 1, S), lambda b, p, i: (b * n_pairs + p, 0, 0))
    ts = pl.BlockSpec((tq, LANES), lambda b, p, i: (b * nq + i, p))
    fs = pl.BlockSpec((S, LANES), lambda b, p, i: (b, p))
    shp = jax.ShapeDtypeStruct((T, n_pairs * LANES), f32)
    return _host_call(body, "fox_bwd", (B, n_pairs, nq), [qs(qcol), ks(kcol), ks(vcol), cqs, cks, ts, ts, ts],
                      [qkv, qkv, qkv, c, cT, do, o, lse], [ts, fs, fs, cks],
                      [shp, shp, shp, jax.ShapeDtypeStruct(cT.shape, f32)], hosted)


def _sigmoid(x):
    return 1.0 / (1.0 + jnp.exp(-x))


def _mix_fwd(o_sb, o_fx, g, x, wp_sb, wp_fx, w_out, ln_g, ln_b, tm):
    T, D = x.shape
    E = o_sb.shape[1]
    tm = _tile(T, tm, 8)

    def body(osb_ref, ofx_ref, gsb_ref, gfx_ref, x_ref, wsb_ref, wfx_ref, wo_ref, lg_ref, lb_ref,
             xhat_ref, rstd_ref, x1_ref, mg_ref):
        y_sb = _dot(osb_ref[...], wsb_ref[...])
        y_fx = _dot(ofx_ref[...], wfx_ref[...])
        merged = (_sigmoid(gsb_ref[...]) * y_sb + _sigmoid(gfx_ref[...]) * y_fx).astype(bf16)
        r = ALPHA * x_ref[...] + _dot(merged, wo_ref[...])
        mean = jnp.mean(r, axis=1, keepdims=True)
        cen = r - mean
        rstd = lax.rsqrt(jnp.mean(cen * cen, axis=1, keepdims=True) + LN_EPS)
        xhat = cen * rstd
        xhat_ref[...] = xhat
        rstd_ref[...] = rstd
        x1_ref[...] = (xhat * lg_ref[...] + lb_ref[...]).astype(bf16)
        mg_ref[...] = merged

    rows = lambda w, c=0: pl.BlockSpec((tm, w), lambda i: (i, c))
    full = lambda a: pl.BlockSpec(a.shape, lambda i: (0, 0))
    return _pcall(body, name="mix_fwd", grid=(T // tm,),
                  in_specs=[rows(E), rows(E), rows(D, 0), rows(D, 1), rows(D), full(wp_sb), full(wp_fx), full(w_out),
                            full(ln_g), full(ln_b)],
                  out_specs=[rows(D), rows(1), rows(D), rows(D)],
                  out_shape=[jax.ShapeDtypeStruct((T, D), f32), jax.ShapeDtypeStruct((T, 1), f32),
                             jax.ShapeDtypeStruct((T, D), bf16), jax.ShapeDtypeStruct((T, D), bf16)],
                  compiler_params=_params(("arbitrary",)))(o_sb, o_fx, g, g, x, wp_sb, wp_fx, w_out, ln_g, ln_b)


def _mix_bwd(dr1, o_sb, o_fx, g, wp_sb, wp_fx, w_out, tm):
    T, D = dr1.shape
    E = o_sb.shape[1]
    tm = _tile(T, tm, 8)

    def body(dr_ref, osb_ref, ofx_ref, gsb_ref, gfx_ref, wsb_ref, wfx_ref, wo_ref,
             dysb_ref, dyfx_ref, dg_ref, dosb_ref, dofx_ref, sumsb_ref, sumfx_ref):
        @pl.when(pl.program_id(0) == 0)
        def _():
            sumsb_ref[...] = jnp.zeros_like(sumsb_ref)
            sumfx_ref[...] = jnp.zeros_like(sumfx_ref)
        dm = _dot(dr_ref[...].astype(bf16), wo_ref[...], _NT)
        for half, (o_ref, g_ref, w_ref, dy_ref, do_ref, sum_ref) in enumerate((
                (osb_ref, gsb_ref, wsb_ref, dysb_ref, dosb_ref, sumsb_ref),
                (ofx_ref, gfx_ref, wfx_ref, dyfx_ref, dofx_ref, sumfx_ref))):
            y = _dot(o_ref[...], w_ref[...])
            s = _sigmoid(g_ref[...])
            dy = (dm * s).astype(bf16)
            dy_ref[...] = dy
            dg = dm * y * s * (1.0 - s)
            dg_ref[:, half * D:(half + 1) * D] = dg.astype(bf16)
            sum_ref[0:1, :] += jnp.sum(dg, axis=0, keepdims=True)
            do_ref[...] = _dot(dy, w_ref[...], _NT).astype(bf16)

    rows = lambda w, c=0: pl.BlockSpec((tm, w), lambda i: (i, c))
    full = lambda a: pl.BlockSpec(a.shape, lambda i: (0, 0))
    acc = pl.BlockSpec((8, D), lambda i: (0, 0))
    res = _pcall(body, name="mix_bwd", grid=(T // tm,),
                 in_specs=[rows(D), rows(E), rows(E), rows(D, 0), rows(D, 1), full(wp_sb), full(wp_fx), full(w_out)],
                 out_specs=[rows(D), rows(D), rows(2 * D), rows(E), rows(E), acc, acc],
                 out_shape=[jax.ShapeDtypeStruct((T, D), bf16)] * 2 + [jax.ShapeDtypeStruct((T, 2 * D), bf16)]
                 + [jax.ShapeDtypeStruct((T, E), bf16)] * 2 + [jax.ShapeDtypeStruct((8, D), f32)] * 2,
                 compiler_params=_params(("arbitrary",)))(dr1, o_sb, o_fx, g, g, wp_sb, wp_fx, w_out)
    return res


def _ln_bwd(dy_a, dy_b, scale_b, xhat, rstd, ln_g, tm):
    T, D = xhat.shape
    tm = _tile(T, tm, 8)

    def body(a_ref, b_ref, xh_ref, rs_ref, g_ref, dr_ref, st_ref):
        @pl.when(pl.program_id(0) == 0)
        def _():
            st_ref[...] = jnp.zeros_like(st_ref)
        dy = a_ref[...] + scale_b * b_ref[...]
        xh = xh_ref[...]
        dxh = dy * g_ref[...]
        m1 = jnp.mean(dxh, axis=1, keepdims=True)
        m2 = jnp.mean(dxh * xh, axis=1, keepdims=True)
        dr_ref[...] = rs_ref[...] * (dxh - m1 - xh * m2)
        st_ref[0:1, :] += jnp.sum(dy * xh, axis=0, keepdims=True)
        st_ref[1:2, :] += jnp.sum(dy, axis=0, keepdims=True)

    rows = lambda w: pl.BlockSpec((tm, w), lambda i: (i, 0))
    return _pcall(body, name="ln1_bwd", grid=(T // tm,),
                  in_specs=[rows(D), rows(D), rows(D), rows(1), pl.BlockSpec((1, D), lambda i: (0, 0))],
                  out_specs=[rows(D), pl.BlockSpec((8, D), lambda i: (0, 0))],
                  out_shape=[jax.ShapeDtypeStruct((T, D), f32), jax.ShapeDtypeStruct((8, D), f32)],
                  compiler_params=_params(("arbitrary",)))(dy_a, dy_b, xhat, rstd, ln_g)


_INV_SQRT2 = 1.0 / math.sqrt(2.0)
_INV_SQRT2PI = 1.0 / math.sqrt(2.0 * math.pi)


def _conv_rows(ref, r0, rc, first, wc, bc):
    cur = ref[pl.ds(r0, rc), :].astype(f32)
    prev = ref[pl.ds(pl.multiple_of(jnp.maximum(r0 - 16, 0), 16), 16), :].astype(f32)
    prev = jnp.where(first, jnp.zeros_like(prev), prev)
    rid = lax.broadcasted_iota(jnp.int32, (rc, LANES), 0)
    s1 = jnp.where(rid == 0, prev[15:16, :], pltpu.roll(cur, 1, 0))
    s2 = jnp.where(rid == 0, prev[14:15, :], jnp.where(rid == 1, prev[15:16, :], pltpu.roll(cur, 2, 0)))
    conv = bc + wc[0:1, :] * s2 + wc[1:2, :] * s1 + wc[2:3, :] * cur
    return conv, (s2, s1, cur)


def _glu_fwd(u, w_conv, b_conv, B, S, rc=512):
    F = u.shape[1] // 2
    nf = F // LANES
    rc = _tile(S, rc, 8)

    def body(ug_ref, uv_ref, wc_ref, bc_ref, a_ref):
        wc, bc = wc_ref[...], bc_ref[...]

        def chunk(n, _):
            r0 = pl.multiple_of(n * rc, rc)
            c, _taps = _conv_rows(ug_ref, r0, rc, n == 0, wc, bc)
            gelu = 0.5 * c * (1.0 + lax.erf(c * _INV_SQRT2))
            a_ref[pl.ds(r0, rc), :] = (gelu * uv_ref[pl.ds(r0, rc), :].astype(f32)).astype(bf16)
            return 0

        lax.fori_loop(0, S // rc, chunk, 0)

    return _pcall(body, name="glu_fwd", grid=(B, nf),
                  in_specs=[pl.BlockSpec((S, LANES), lambda b, j: (b, j)), pl.BlockSpec((S, LANES), lambda b, j: (b, nf + j)),
                            pl.BlockSpec((3, LANES), lambda b, j: (0, j)), pl.BlockSpec((1, LANES), lambda b, j: (0, j))],
                  out_specs=pl.BlockSpec((S, LANES), lambda b, j: (b, j)),
                  out_shape=jax.ShapeDtypeStruct((B * S, F), bf16),
                  compiler_params=_params(("arbitrary", "arbitrary")))(u, u, w_conv, b_conv)


def _glu_bwd(u, da, w_conv, b_conv, B, S, rc=512):
    F = u.shape[1] // 2
    nf = F // LANES
    rc = _tile(S, rc, 8)
    nc = S // rc

    def body(ug_ref, uv_ref, da_ref, wc_ref, bc_ref, dug_ref, duv_ref, gw_ref, gb_ref, dc_ref):
        wc, bc = wc_ref[...], bc_ref[...]

        def chunk(n, carry):
            gw0, gw1, gw2, gb = carry
            r0 = pl.multiple_of(n * rc, rc)
            c, (s2, s1, cur) = _conv_rows(ug_ref, r0, rc, n == 0, wc, bc)
            cdf = 0.5 * (1.0 + lax.erf(c * _INV_SQRT2))
            da = da_ref[pl.ds(r0, rc), :]
            duv_ref[pl.ds(r0, rc), :] = (da * (c * cdf)).astype(bf16)
            dc = da * uv_ref[pl.ds(r0, rc), :].astype(f32) * (cdf + c * (_INV_SQRT2PI * jnp.exp(-0.5 * c * c)))
            dc_ref[pl.ds(r0, rc), :] = dc
            red = lambda t: jnp.sum(t, axis=0, keepdims=True)
            return gw0 + red(dc * s2), gw1 + red(dc * s1), gw2 + red(dc * cur), gb + red(dc)

        z = jnp.zeros((1, LANES), f32)
        gw0, gw1, gw2, gb = lax.fori_loop(0, nc, chunk, (z, z, z, z))
        gw_ref[0, 0:1, :] = gw0
        gw_ref[0, 1:2, :] = gw1
        gw_ref[0, 2:3, :] = gw2
        gb_ref[0] = gb

        def chunk2(n, _):
            r0 = pl.multiple_of(n * rc, rc)
            cur = dc_ref[pl.ds(r0, rc), :]
            nxt = dc_ref[pl.ds(pl.multiple_of(jnp.minimum(r0 + rc, S - 8), 8), 8), :]
            nxt = jnp.where(n == nc - 1, jnp.zeros_like(nxt), nxt)
            rid = lax.broadcasted_iota(jnp.int32, (rc, LANES), 0)
            a1 = jnp.where(rid == rc - 1, nxt[0:1, :], pltpu.roll(cur, rc - 1, 0))
            a2 = jnp.where(rid == rc - 1, nxt[1:2, :], jnp.where(rid == rc - 2, nxt[0:1, :], pltpu.roll(cur, rc - 2, 0)))
            dug_ref[pl.ds(r0, rc), :] = (wc[2:3, :] * cur + wc[1:2, :] * a1 + wc[0:1, :] * a2).astype(bf16)
            return 0

        lax.fori_loop(0, nc, chunk2, 0)

    blk = lambda off: pl.BlockSpec((S, LANES), lambda b, j: (b, off + j))
    return _pcall(body, name="glu_bwd", grid=(B, nf),
                  in_specs=[blk(0), blk(nf), blk(0), pl.BlockSpec((3, LANES), lambda b, j: (0, j)),
                            pl.BlockSpec((1, LANES), lambda b, j: (0, j))],
                  out_specs=[blk(0), blk(0), pl.BlockSpec((1, 3, LANES), lambda b, j: (b, 0, j)),
                             pl.BlockSpec((1, 1, LANES), lambda b, j: (b, 0, j))],
                  out_shape=[jax.ShapeDtypeStruct((B * S, F), bf16), jax.ShapeDtypeStruct((B * S, F), bf16),
                             jax.ShapeDtypeStruct((B, 3, F), f32), jax.ShapeDtypeStruct((B, 1, F), f32)],
                  scratch_shapes=[pltpu.VMEM((S, LANES), f32)],
                  compiler_params=_params(("arbitrary", "arbitrary")))(u, u, da, w_conv, b_conv)


def _down_loss(a, w_down, xhat1, ln1_g, ln1_b, ln2_g, ln2_b, tgt, tm):
    T, D = xhat1.shape
    F = a.shape[1]
    tm = _tile(T, tm, 8)

    def body(a_ref, w_ref, xh_ref, g1_ref, b1_ref, g2_ref, b2_ref, t_ref, dr_ref, st_ref):
        @pl.when(pl.program_id(0) == 0)
        def _():
            st_ref[...] = jnp.zeros_like(st_ref)
        x1 = xh_ref[...] * g1_ref[...] + b1_ref[...]
        r = ALPHA * x1 + _dot(a_ref[...], w_ref[...])
        mean = jnp.mean(r, axis=1, keepdims=True)
        cen = r - mean
        rstd = lax.rsqrt(jnp.mean(cen * cen, axis=1, keepdims=True) + LN_EPS)
        xh = cen * rstd
        err = (xh * g2_ref[...] + b2_ref[...]) - t_ref[...]
        dy = err * (1.0 / D)
        dxh = dy * g2_ref[...]
        m1 = jnp.mean(dxh, axis=1, keepdims=True)
        m2 = jnp.mean(dxh * xh, axis=1, keepdims=True)
        dr_ref[...] = rstd * (dxh - m1 - xh * m2)
        st_ref[0:1, :] += jnp.sum(dy * xh, axis=0, keepdims=True)
        st_ref[1:2, :] += jnp.sum(dy, axis=0, keepdims=True)
        st_ref[2:3, :] += jnp.sum(err * err, axis=0, keepdims=True)

    rows = lambda w: pl.BlockSpec((tm, w), lambda i: (i, 0))
    vec = pl.BlockSpec((1, D), lambda i: (0, 0))
    return _pcall(body, name="down_loss", grid=(T // tm,),
                  in_specs=[rows(F), pl.BlockSpec((F, D), lambda i: (0, 0)), rows(D), vec, vec, vec, vec, rows(D)],
                  out_specs=[rows(D), pl.BlockSpec((8, D), lambda i: (0, 0))],
                  out_shape=[jax.ShapeDtypeStruct((T, D), f32), jax.ShapeDtypeStruct((8, D), f32)],
                  compiler_params=_params(("arbitrary",)))(a, w_down, xhat1, ln1_g, ln1_b, ln2_g, ln2_b, tgt)


def _local_step(x, tgt, w_in, b_in, ln1_g, ln1_b, b_conv, ln2_g, ln2_b, late_shards, late_weights, reducer=None,
                tail=None, tq=512, tm=256):
    B, S, D = x.shape
    T = B * S
    E = (w_in.shape[1] - 2 * D) * HEAD_DIM // (6 * HEAD_DIM + 1)
    n_pairs = E // LANES
    NH = E // HEAD_DIM
    x2 = x.reshape(T, D)
    tgt2 = tgt.reshape(T, D)
    tq = _tile(S, tq, 8)

    c_f, c_g = 6 * E, 6 * E + NH
    w_qkv = w_in[:, :c_f]
    qscale = jnp.concatenate([jnp.full((E,), Q_SCALE, f32), jnp.ones((2 * E,), f32)] * 2)
    w_qkv_s = (w_qkv.astype(f32) * qscale).astype(bf16)
    b_qkv_s = b_in[:, :c_f] * qscale
    w_f = jnp.pad(w_in[:, c_f:c_g], ((0, 0), (0, LANES - NH)))
    b_f = jnp.pad(b_in[:, c_f:c_g], ((0, 0), (0, LANES - NH)))
    w_g = w_in[:, c_g:]
    b_g = b_in[:, c_g:]

    xb = x2.astype(bf16)
    qkv = _matmul(xb, w_qkv_s, bias=b_qkv_s, out_dtype=bf16, tm=1024, tn=1536, name="proj_qkv")
    g = _matmul(xb, w_g, bias=b_g, tm=1024, tn=1024, name="proj_gate")
    fl = _matmul(xb, w_f, bias=b_f, tm=1024, name="proj_forget")
    c = _cumlogf(fl, B, S)
    cT = c.reshape(B, S, LANES)[:, :, :NH].transpose(0, 2, 1).reshape(B * NH, 1, S)
    P = n_pairs
    o_sb, lt_sb, stacks = _sb_fwd(qkv, B, S, P, 0, P, 2 * P, tq, late_shards)
    wp_sb, wp_fx, w_out, w_up, w_conv, w_down = late_weights(stacks)
    F = w_down.shape[0]
    o_fx, o_fx32, lse_fx = _fox_fwd(qkv, c, cT, B, S, P, 3 * P, 4 * P, 5 * P, tq)
    xhat1, rstd1, x1b, merged = _mix_fwd(o_sb, o_fx, g, x2, wp_sb, wp_fx, w_out, ln1_g, ln1_b, tm)
    u = _matmul(x1b, w_up, out_dtype=bf16, tm=1024, tn=1408, name="ffn_up")
    a = _glu_fwd(u, w_conv, b_conv, B, S)
    dr2, st2 = _down_loss(a, w_down, xhat1, ln1_g, ln1_b, ln2_g, ln2_b, tgt2, tm)

    grads = {}
    grads["ln2_g"], grads["ln2_b"] = st2[0:1], st2[1:2]
    sq_err = st2[2:3]
    da = _matmul(dr2, w_down, tb=True, tm=1024, tn=1408, name="ffn_da")
    grads["w_down"] = _matmul(a, dr2, ta=True, tm=1408, tn=1024, tk=1024, name="grad_w_down")
    du_g, du_v, gwc, gbc = _glu_bwd(u, da, w_conv, b_conv, B, S)
    grads["w_conv"] = jnp.sum(gwc, axis=0)
    grads["b_conv"] = jnp.sum(gbc, axis=0)
    grads["w_up"] = jnp.concatenate(
        [_matmul(x1b, du_g, ta=True, tm=512, tn=2816, tk=1024, name="grad_w_up_gate"),
         _matmul(x1b, du_v, ta=True, tm=512, tn=2816, tk=1024, name="grad_w_up_val")], axis=1)
    dx1 = _matmul(du_g, w_up[:, :F], tb=True, tm=1024, tn=1024, tk=F, name="ffn_dx_gate")
    dx1 = _matmul(du_v, w_up[:, F:], tb=True, addend=dx1, tm=1024, tn=1024, tk=F, name="ffn_dx_val")
    dr1, st1 = _ln_bwd(dx1, dr2, ALPHA, xhat1, rstd1, ln1_g, tm)
    grads["ln1_g"], grads["ln1_b"] = st1[0:1], st1[1:2]

    dy_sb, dy_fx, dg, do_sb, do_fx, gsum_sb, gsum_fx = _mix_bwd(dr1, o_sb, o_fx, g, wp_sb, wp_fx, w_out, tm)
    grads["w_out"] = _matmul(merged, dr1, ta=True, tm=512, tn=1024, tk=1024, name="grad_w_out")
    grads["w_proj_sb"] = _matmul(o_sb, dy_sb, ta=True, tm=512, tn=1024, tk=1024, name="grad_w_proj_sb")
    grads["w_proj_fox"] = _matmul(o_fx, dy_fx, ta=True, tm=512, tn=1024, tk=1024, name="grad_w_proj_fox")
    (dq_f, dk_f, dv_f, dcT), got = _fox_bwd(qkv, c, cT, do_fx, o_fx32, lse_fx, B, S, P, 3 * P, 4 * P, 5 * P, tq,
                                            reducer.to_sibling(grads) if reducer else None)
    (dq_s, dk_s, dv_s), got = _sb_bwd(qkv, do_sb, lt_sb, B, S, P, 0, P, 2 * P, tq, reducer.to_chips(got) if reducer else None)
    if reducer:
        reducer.from_chips(got)
    dc = jnp.pad(dcT.reshape(B, NH, S).transpose(0, 2, 1), ((0, 0), (0, 0), (0, LANES - NH))).reshape(T, LANES)
    dfl = _cumlogf_bwd(dc, fl, B, S)
    dqkv = jnp.concatenate([dq_s, dk_s, dv_s, dq_f, dk_f, dv_f], axis=1)
    gw_qkv, gb_qkv = _matmul(x2, dqkv, ta=True, colsum=True, tm=512, tn=3072, tk=512, name="grad_w_qkv")
    gw_f, gb_f = _matmul(x2, dfl, ta=True, colsum=True, tm=512, tk=1024, name="grad_w_forget")
    gw_g = _matmul(x2, dg, ta=True, tm=512, tn=2048, tk=1024, name="grad_w_gate")
    grads["w_in"] = jnp.concatenate([gw_qkv, gw_f[:, :NH], gw_g], axis=1)
    grads["b_in"] = jnp.concatenate([gb_qkv, gb_f[:, :NH], gsum_sb[0:1], gsum_fx[0:1]], axis=1)
    dx = _matmul(dfl, w_f, tb=True, addend=dr1, addend_scale=ALPHA, tm=1024, tn=1024, name="dx_forget",
                 hosted=tail.to_sibling(grads, sq_err) if tail else None)
    if tail:
        dx, got = dx
        to_chips = tail.to_chips(got)
    dx = _matmul(dqkv, w_qkv, tb=True, addend=dx, tm=512, tn=1024, tk=c_f, name="dx_qkv", hosted=to_chips if tail else None)
    if tail:
        dx, got = dx
        tail.from_chips(got)
    dx = _matmul(dg, w_g, tb=True, addend=dx, tm=1024, tn=1024, tk=2 * D, name="dx_gate")
    return sq_err, dx.reshape(B, S, D), grads


_ANY = pl.BlockSpec(memory_space=pl.ANY)
_MESH = pl.DeviceIdType.MESH


def _pos():
    return lax.axis_index("x"), lax.axis_index("y"), lax.axis_index("c")


def _other_chips(x, y):
    return [(1 - x, y), (x, 1 - y), (1 - x, 1 - y)]


def _gather_split(shard):
    h = shard.shape[0] // 2

    def body(src_ref, dst_ref, send_sems, recv_sems, pass_send, pass_recv):
        x, y, c = _pos()
        k = 2 * x + y
        chips = _other_chips(x, y)
        slots = [2 * chip[0] + chip[1] for chip in chips]

        def rows(ref, half):
            return ref.at[pl.ds(pl.multiple_of(half * h, 16), h), :]

        def fetch(j, slot):
            return pltpu.make_async_remote_copy(src_ref=rows(src_ref, c), dst_ref=rows(dst_ref.at[slot], c),
                                                send_sem=send_sems.at[j], recv_sem=recv_sems.at[j],
                                                device_id=(*chips[j], c), device_id_type=_MESH)

        def pass_on(j, half):
            blk = rows(dst_ref.at[slots[j]], half)
            return pltpu.make_async_remote_copy(src_ref=blk, dst_ref=blk, send_sem=pass_send.at[j], recv_sem=pass_recv.at[j],
                                                device_id=(x, y, 1 - c), device_id_type=_MESH)

        for j in range(3):
            fetch(j, k).start()
        for j in range(3):
            fetch(j, slots[j]).wait_recv()
            pass_on(j, c).start()
        for j in range(3):
            pass_on(j, 1 - c).wait_recv()
        for j in range(3):
            fetch(j, k).wait_send()
            pass_on(j, c).wait_send()

    sems = pltpu.SemaphoreType.DMA((3,))
    return _pcall(body, name="gather_w_in", in_specs=[_ANY], out_specs=_ANY,
                  out_shape=jax.ShapeDtypeStruct((4,) + shard.shape, shard.dtype), scratch_shapes=[sems] * 4,
                  compiler_params=pltpu.CompilerParams(has_side_effects=True))(shard)


def _gather_shapes(shards):
    return [jax.ShapeDtypeStruct((4,) + s.shape, s.dtype) for s in shards]


def _gather_sems(n):
    return [pltpu.SemaphoreType.DMA((n, 3)), pltpu.SemaphoreType.DMA((n, 3)), pltpu.SemaphoreType.DMA((n,))]


def _gather_copies(srcs, dsts, send_sems, recv_sems, local_sems):
    n = len(srcs)
    x, y, c = _pos()
    k = 2 * x + y
    chips = _other_chips(x, y)

    def copy(a, j, chip, slot):
        return pltpu.make_async_remote_copy(src_ref=srcs[a], dst_ref=dsts[a].at[slot], send_sem=send_sems.at[a, j],
                                            recv_sem=recv_sems.at[a, j], device_id=(*chip, c), device_id_type=_MESH)

    def mine():
        return ([pltpu.make_async_copy(srcs[a], dsts[a].at[k], local_sems.at[a]) for a in range(n)],
                [copy(a, j, chip, k) for a in range(n) for j, chip in enumerate(chips)])

    def start():
        local, sends = mine()
        for cp in local + sends:
            cp.start()

    def finish():
        local, sends = mine()
        for a in range(n):
            for j, chip in enumerate(chips):
                copy(a, j, chip, 2 * chip[0] + chip[1]).wait_recv()
        for cp in sends:
            cp.wait_send()
        for cp in local:
            cp.wait()

    return start, finish


class _Hosted:
    def __init__(self, inputs, outputs, sems, copies):
        self.inputs, self.outputs, self.sems, self.copies = list(inputs), list(outputs), list(sems), copies


def _host(hosted, refs, n_in, n_out, step, last):
    if hosted is None:
        return refs[:n_in], refs[n_in:n_in + n_out], lambda: None
    hi, ho = len(hosted.inputs), len(hosted.outputs)
    own_in, h_in = refs[:n_in], refs[n_in:n_in + hi]
    own_out = refs[n_in + hi:n_in + hi + n_out]
    h_out = refs[n_in + hi + n_out:n_in + hi + n_out + ho]
    start, finish = hosted.copies(h_in, h_out, *refs[n_in + hi + n_out + ho:])
    pl.when(step == 0)(start)
    return own_in, own_out, lambda: pl.when(step == last)(finish)


def _host_call(body, name, grid, in_specs, args, out_specs, out_shape, hosted):
    h_in = hosted.inputs if hosted else []
    h_out = hosted.outputs if hosted else []
    res = _pcall(body, name=name, grid=grid, in_specs=list(in_specs) + [_ANY] * len(h_in),
                 out_specs=list(out_specs) + [_ANY] * len(h_out), out_shape=list(out_shape) + list(h_out),
                 scratch_shapes=hosted.sems if hosted else [],
                 compiler_params=_params(("arbitrary",) * len(grid)))(*args, *h_in)
    return list(res[:len(out_shape)]), list(res[len(out_shape):])


def _sibling_copies(shapes):
    n = len(shapes)

    def copies(p_refs, got_refs, send_sems, recv_sems):
        x, y, c = _pos()

        def copy(a):
            h = shapes[a][1] // 2
            src = p_refs[a].at[:, pl.ds(pl.multiple_of((1 - c) * h, 8), h), :]
            return pltpu.make_async_remote_copy(src_ref=src, dst_ref=got_refs[a], send_sem=send_sems.at[a],
                                                recv_sem=recv_sems.at[a], device_id=(x, y, 1 - c), device_id_type=_MESH)

        def start():
            for a in range(n):
                copy(a).start()

        def finish():
            for a in range(n):
                copy(a).wait()

        return start, finish

    return copies


def _hosted_sibling(pieces):
    n = len(pieces)
    return _Hosted(pieces, [jax.ShapeDtypeStruct((4, p.shape[1] // 2, p.shape[2]), p.dtype) for p in pieces],
                   [pltpu.SemaphoreType.DMA((n,)), pltpu.SemaphoreType.DMA((n,))], _sibling_copies([p.shape for p in pieces]))


def _chips_copies(n):
    def copies(p_refs, got_refs, send_sems, recv_sems):
        x, y, c = _pos()
        k = 2 * x + y
        chips = _other_chips(x, y)

        def copy(a, j, chip, piece, slot):
            return pltpu.make_async_remote_copy(src_ref=p_refs[a].at[piece], dst_ref=got_refs[a].at[slot],
                                                send_sem=send_sems.at[a, j], recv_sem=recv_sems.at[a, j],
                                                device_id=(*chip, c), device_id_type=_MESH)

        def start():
            for a in range(n):
                for j, chip in enumerate(chips):
                    copy(a, j, chip, 2 * chip[0] + chip[1], k).start()

        def finish():
            for a in range(n):
                for j, chip in enumerate(chips):
                    copy(a, j, chip, k, 2 * chip[0] + chip[1]).wait_recv()
            for a in range(n):
                for j, chip in enumerate(chips):
                    copy(a, j, chip, 2 * chip[0] + chip[1], k).wait_send()

        return start, finish

    return copies


def _hosted_chips(pieces):
    n = len(pieces)
    return _Hosted(pieces, [jax.ShapeDtypeStruct(p.shape, p.dtype) for p in pieces],
                   [pltpu.SemaphoreType.DMA((n, 3)), pltpu.SemaphoreType.DMA((n, 3))], _chips_copies(n))


def _small_copies(sm_ref, sg_ref, send_sems, recv_sems, local_sem):
    x, y, c = _pos()
    me = 4 * x + 2 * y + c
    flip = lambda v, bit: 1 - v if bit else v
    peers = [(flip(x, r & 4), flip(y, r & 2), flip(c, r & 1)) for r in range(1, 8)]

    def copy(j, slot):
        return pltpu.make_async_remote_copy(src_ref=sm_ref, dst_ref=sg_ref.at[slot], send_sem=send_sems.at[j],
                                            recv_sem=recv_sems.at[j], device_id=peers[j], device_id_type=_MESH)

    def local():
        return pltpu.make_async_copy(sm_ref, sg_ref.at[me], local_sem)

    def start():
        local().start()
        for j in range(7):
            copy(j, me).start()

    def finish():
        for j, (px, py, pc) in enumerate(peers):
            copy(j, 4 * px + 2 * py + pc).wait_recv()
        for j in range(7):
            copy(j, me).wait_send()
        local().wait()

    return start, finish


def _hosted_sibling_and_small(pieces, small):
    n = len(pieces)
    sibling = _hosted_sibling(pieces)

    def copies(in_refs, out_refs, big_send, big_recv, send_sems, recv_sems, local_sem):
        start_big, finish_big = sibling.copies(in_refs[:n], out_refs[:n], big_send, big_recv)
        start_small, finish_small = _small_copies(in_refs[n], out_refs[n], send_sems, recv_sems, local_sem)

        def start():
            start_big()
            start_small()

        def finish():
            finish_small()
            finish_big()

        return start, finish

    return _Hosted(pieces + [small], sibling.outputs + [jax.ShapeDtypeStruct((8,) + small.shape, small.dtype)],
                   sibling.sems + [pltpu.SemaphoreType.DMA((7,)), pltpu.SemaphoreType.DMA((7,)), pltpu.SemaphoreType.DMA(())],
                   copies)


def _share_halves(shards):
    n = len(shards)

    def body(*refs):
        full_refs = refs[n:2 * n]
        send_sems, recv_sems = refs[2 * n:]
        x, y, c = _pos()

        def copy(a, half):
            h = shards[a].shape[0] // 2
            rows = full_refs[a].at[pl.ds(pl.multiple_of(half * h, 8), h), :]
            return pltpu.make_async_remote_copy(src_ref=rows, dst_ref=rows, send_sem=send_sems.at[a],
                                                recv_sem=recv_sems.at[a], device_id=(x, y, 1 - c), device_id_type=_MESH)

        sends = [copy(a, c) for a in range(n)]
        for cp in sends:
            cp.start()
        for a in range(n):
            copy(a, 1 - c).wait_recv()
        for cp in sends:
            cp.wait_send()

    return _pcall(body, name="share_halves", in_specs=[_ANY] * n, out_specs=[_ANY] * n,
                  out_shape=[jax.ShapeDtypeStruct(s.shape, s.dtype) for s in shards],
                  input_output_aliases={a: a for a in range(n)},
                  scratch_shapes=[pltpu.SemaphoreType.DMA((n,)), pltpu.SemaphoreType.DMA((n,))],
                  compiler_params=pltpu.CompilerParams(has_side_effects=True))(*shards)


def _add_own_half(piece, got, core, name):
    _, r, cols = piece.shape
    h = r // 2

    def body(c_ref, a_ref, b_ref, o_ref, o16_ref):
        s = a_ref[0] + b_ref[...]
        o_ref[...] = s
        o16_ref[...] = s.astype(bf16)

    out = pl.BlockSpec((1, h, cols), lambda k, c: (k, 0, 0))
    grid_spec = pltpu.PrefetchScalarGridSpec(
        num_scalar_prefetch=1, grid=(4,),
        in_specs=[pl.BlockSpec((1, 1, h, cols), lambda k, c: (k, c[0], 0, 0)), out], out_specs=[out, out])
    return _pcall(body, name=name, grid_spec=grid_spec,
                  out_shape=[jax.ShapeDtypeStruct((4, h, cols), f32), jax.ShapeDtypeStruct((4, h, cols), bf16)],
                  compiler_params=_params(("arbitrary",)))(core, piece.reshape(4, 2, h, cols), got)


def _sum_chips(own, got, where, name):
    _, h, cols = own.shape
    t = _tile(h, 128, 16)
    nt = h // t

    def body(w_ref, own_ref, gx_ref, gy_ref, gxy_ref, o_ref):
        o_ref[...] = ((own_ref[0] + gx_ref[0].astype(f32)) + gy_ref[0].astype(f32)) + gxy_ref[0].astype(f32)

    slot = lambda j: pl.BlockSpec((1, t, cols), lambda i, w: (w[j], i, 0))
    grid_spec = pltpu.PrefetchScalarGridSpec(
        num_scalar_prefetch=1, grid=(nt,),
        in_specs=[slot(0), slot(2), slot(3), slot(4)],
        out_specs=pl.BlockSpec((t, cols), lambda i, w: (w[1] * nt + i, 0)))
    return _pcall(body, name=name, grid_spec=grid_spec, out_shape=jax.ShapeDtypeStruct((2 * h, cols), f32),
                  compiler_params=_params(("arbitrary",)))(where, own, got, got, got)


def _sum_slots(stack, name):
    k, n, cols = stack.shape
    t = _tile(n, 128, 8)

    def body(s_ref, o_ref):
        acc = s_ref[0]
        for i in range(1, k):
            acc = acc + s_ref[i]
        o_ref[...] = acc

    return _pcall(body, name=name, grid=(n // t,), in_specs=[pl.BlockSpec((k, t, cols), lambda i: (0, i, 0))],
                  out_specs=pl.BlockSpec((t, cols), lambda i: (i, 0)), out_shape=jax.ShapeDtypeStruct((n, cols), f32),
                  compiler_params=_params(("arbitrary",)))(stack)


def _adamw(w, g, m, v, name):
    n, cols = w.shape[-2:]
    t = _tile(n, 128, 8)
    c1 = 1.0 - ADAM_B1 ** ADAM_STEP
    c2 = 1.0 - ADAM_B2 ** ADAM_STEP
    at = (0,) if w.ndim == 3 else (Ellipsis,)

    def body(w_ref, g_ref, m_ref, v_ref, d_ref, nm_ref, nv_ref, g_out_ref):
        g = g_ref[...]
        nm = ADAM_B1 * m_ref[at] + (1.0 - ADAM_B1) * g
        nv = ADAM_B2 * v_ref[at] + (1.0 - ADAM_B2) * (g * g)
        d_ref[at] = -ADAM_LR * ((nm / c1) / (jnp.sqrt(nv / c2) + ADAM_EPS) + ADAM_WD * w_ref[at])
        nm_ref[at] = nm
        nv_ref[at] = nv
        g_out_ref[at] = g

    flat = pl.BlockSpec((t, cols), lambda i: (i, 0))
    spec = pl.BlockSpec((1, t, cols), lambda i: (0, i, 0)) if w.ndim == 3 else flat
    shp = jax.ShapeDtypeStruct(w.shape, f32)
    return _pcall(body, name=name, grid=(n // t,), in_specs=[spec, flat, spec, spec], out_specs=[spec] * 4,
                  out_shape=[shp] * 4, compiler_params=_params(("arbitrary",)))(w, g, m, v)


_MATS = (("w_in", 1), ("w_proj_sb", 1), ("w_proj_fox", 1), ("w_out", 0), ("w_up", 1), ("w_down", 0))
_SMALL = ("b_in", "ln1_g", "ln1_b", "b_conv", "ln2_g", "ln2_b")


def _pad_lanes(v):
    n = v.shape[-1]
    return jnp.pad(v, ((0, 0), (0, (-n) % LANES)))


def _pack_rows(vectors):
    flat = jnp.concatenate([_pad_lanes(v.reshape(1, -1)) for v in vectors], axis=1).reshape(-1, LANES)
    return jnp.pad(flat, ((0, (-flat.shape[0]) % 8), (0, 0)))


def _unpack_rows(packed, sizes):
    out, r = [], 0
    for n in sizes:
        rows = -(-n // LANES)
        out.append(packed[r:r + rows].reshape(1, rows * LANES)[:, :n])
        r += rows
    return out


def _unstack(stack, axis):
    if axis == 0:
        return stack.reshape(-1, stack.shape[2])
    return jnp.concatenate([stack[k] for k in range(4)], axis=1)


def _pieces(g, axis):
    if axis == 0:
        return g.reshape(4, g.shape[0] // 4, g.shape[1])
    cols = g.shape[1] // 4
    return jnp.stack([g[:, k * cols:(k + 1) * cols] for k in range(4)])


class _Reducer:
    def __init__(self, mats, core, where, small=None):
        self.mats, self.core, self.where, self.small = mats, core, where, small

    def to_sibling(self, grads, sq_err=None):
        self.local = [_pieces(grads[n], axis) for n, axis in self.mats]
        if self.small is None:
            return _hosted_sibling(self.local)
        return _hosted_sibling_and_small(self.local, self.small(grads, sq_err))

    def to_chips(self, got):
        if self.small is not None:
            got, self.small_all = got[:-1], got[-1]
        self.sums = [_add_own_half(p, g, self.core, "add_sibling_" + n) for (n, _), p, g in zip(self.mats, self.local, got)]
        return _hosted_chips([s16 for _, s16 in self.sums])

    def from_chips(self, got):
        self.halves = [_sum_chips(s32, r16, self.where, "sum_chips_" + n)
                       for (n, _), (s32, _), r16 in zip(self.mats, self.sums, got)]


def kernel(x, w_in, b_in, w_proj_sb, w_proj_fox, w_out, ln1_g, ln1_b, w_up, w_conv, b_conv, w_down, ln2_g, ln2_b, loss_target, m_w_in, m_b_in, m_w_proj_sb, m_w_proj_fox, m_w_out, m_ln1_g, m_ln1_b, m_w_up, m_w_conv, m_b_conv, m_w_down, m_ln2_g, m_ln2_b, v_w_in, v_b_in, v_w_proj_sb, v_w_proj_fox, v_w_out, v_ln1_g, v_ln1_b, v_w_up, v_w_conv, v_b_conv, v_w_down, v_ln2_g, v_ln2_b):
    w = dict(w_in=w_in, b_in=b_in, w_proj_sb=w_proj_sb, w_proj_fox=w_proj_fox, w_out=w_out, ln1_g=ln1_g, ln1_b=ln1_b,
             w_up=w_up, w_conv=w_conv, b_conv=b_conv, w_down=w_down, ln2_g=ln2_g, ln2_b=ln2_b)
    m = dict(w_in=m_w_in, b_in=m_b_in, w_proj_sb=m_w_proj_sb, w_proj_fox=m_w_proj_fox, w_out=m_w_out, ln1_g=m_ln1_g,
             ln1_b=m_ln1_b, w_up=m_w_up, w_conv=m_w_conv, b_conv=m_b_conv, w_down=m_w_down, ln2_g=m_ln2_g, ln2_b=m_ln2_b)
    v = dict(w_in=v_w_in, b_in=v_b_in, w_proj_sb=v_w_proj_sb, w_proj_fox=v_w_proj_fox, w_out=v_w_out, ln1_g=v_ln1_g,
             ln1_b=v_ln1_b, w_up=v_w_up, w_conv=v_w_conv, b_conv=v_b_conv, w_down=v_w_down, ln2_g=v_ln2_g, ln2_b=v_ln2_b)
    order = ["w_in", "b_in", "w_proj_sb", "w_proj_fox", "w_out", "ln1_g", "ln1_b", "w_up", "w_conv", "b_conv", "w_down",
             "ln2_g", "ln2_b"]
    x_idx, y_idx, c_idx = _pos()
    chip = 2 * x_idx + y_idx
    D = x.shape[-1]
    core = c_idx.astype(jnp.int32).reshape(1)

    w_in_own = w["w_in"][0].astype(bf16)
    w_in_stack = _gather_split(w_in_own)
    w_in_full = jnp.concatenate([jnp.where(chip == k, w_in_own, w_in_stack[k]) for k in range(4)], axis=1)
    late = (("w_proj_sb", 1), ("w_proj_fox", 1), ("w_out", 0), ("w_up", 1), ("w_conv", 1), ("w_down", 0))
    late_shards = [w[n][0] if n == "w_conv" else w[n][0].astype(bf16) for n, _ in late]
    late_weights = lambda stacks: [_unstack(s, axis) for (_, axis), s in zip(late, stacks)]

    where = jnp.stack([chip, c_idx, 2 * (1 - x_idx) + y_idx, 2 * x_idx + 1 - y_idx, 2 * (1 - x_idx) + 1 - y_idx]).astype(jnp.int32)
    small_names = list(_SMALL) + ["w_conv"]
    pack_small = lambda grads, sq_err: _pack_rows(
        [jnp.full((1, 1), (0.5 / D) * jnp.sum(sq_err), f32)] + [grads[n] for n in small_names])
    early = _Reducer(_MATS[1:], core, where)
    last = _Reducer(_MATS[:1], core, where, pack_small)
    sq_err, grad_x, grads = _local_step(x, loss_target, w_in_full, w["b_in"], w["ln1_g"], w["ln1_b"], w["b_conv"],
                                        w["ln2_g"], w["ln2_b"], late_shards, late_weights, early, last)
    g_shards = _share_halves(last.halves + early.halves)
    small_sum = _sum_slots(last.small_all, "sum_small")

    out = {"grad": {}, "delta": {}, "m": {}, "v": {}}
    for (n, _), g_ in zip(_MATS, g_shards):
        d_, m_, v_, g_out = _adamw(w[n], g_, m[n], v[n], "adamw_" + n)
        for key, t in (("grad", g_out), ("delta", d_), ("m", m_), ("v", v_)):
            out[key][n] = t.reshape(w[n].shape)
    sizes = [1] + [int(grads[n].size) for n in small_names]
    sm = _unpack_rows(small_sum, sizes)
    loss = sm[0][0, 0]
    g_small = dict(zip(small_names, sm[1:]))
    F4 = w["w_conv"].shape[-1]
    g_small["w_conv"] = lax.dynamic_slice_in_dim(g_small["w_conv"].reshape(3, -1), chip * F4, F4, axis=1)
    pack_s = lambda d: _pack_rows([d[n].reshape(1, -1) for n in small_names])
    gs_packed = _pack_rows([g_small[n].reshape(1, -1) for n in small_names])
    s_delta, s_m, s_v, _ = _adamw(pack_s(w), gs_packed, pack_s(m), pack_s(v), "adamw_small")
    s_sizes = [int(w[n].size) for n in small_names]

    for key, packed_s in (("grad", gs_packed), ("delta", s_delta), ("m", s_m), ("v", s_v)):
        for n, t in zip(small_names, _unpack_rows(packed_s, s_sizes)):
            out[key][n] = t.reshape(w[n].shape)
    return (loss, grad_x, *[out["grad"][n] for n in order], *[out["delta"][n] for n in order],
            *[out["m"][n] for n in order], *[out["v"][n] for n in order])
```

```python
import functools
import math

import jax
import jax.numpy as jnp
from jax import lax
from jax.experimental import pallas as pl
from jax.experimental.pallas import tpu as pltpu

f32, bf16 = jnp.float32, jnp.bfloat16

HEAD_DIM = 64
LANES = 128
LN_EPS = 1e-5
ALPHA = 2.0 ** 0.25
Q_SCALE = HEAD_DIM ** -0.5
ADAM_LR, ADAM_B1, ADAM_B2, ADAM_EPS, ADAM_WD, ADAM_STEP = 0.001, 0.9, 0.999, 1e-08, 0.01, 10
VMEM_LIMIT = 56 * 1024 * 1024
NEG = -1e30

_pcall = pl.pallas_call
_NT = (((1,), (1,)), ((), ()))
_TN = (((0,), (0,)), ((), ()))


def _params(sem=None):
    return pltpu.CompilerParams(dimension_semantics=sem, vmem_limit_bytes=VMEM_LIMIT)


def _tile(dim, target, unit=LANES):
    if dim <= target:
        return dim
    t = (target // unit) * unit
    while t > unit and dim % t:
        t -= unit
    assert dim % t == 0, (dim, target)
    return t


def _dot(a, b, dn=None):
    if dn is None:
        return jnp.dot(a, b, preferred_element_type=f32)
    return lax.dot_general(a, b, dn, preferred_element_type=f32)


def _split_dot(x, tri):
    hi = x.astype(bf16)
    lo = (x - hi.astype(f32)).astype(bf16)
    return _dot(hi, tri) + _dot(lo, tri)


SCAN_BLOCK = 256


def _scan_cols(x, tri, reverse, split=True):
    cb = tri.shape[0]
    nb = x.shape[1] // cb
    blocks = [x[:, b * cb:(b + 1) * cb] for b in range(nb)]
    outs, run = [None] * nb, None
    for b in (reversed(range(nb)) if reverse else range(nb)):
        o = _split_dot(blocks[b], tri) if split else _dot(blocks[b].astype(bf16), tri)
        s = jnp.sum(blocks[b], axis=1, keepdims=True)
        outs[b] = o if run is None else o + run
        run = s if run is None else run + s
    return (outs[0] if nb == 1 else jnp.concatenate(outs, axis=1)), run


def _tri(cb, rel):
    row = lax.broadcasted_iota(jnp.int32, (cb, cb), 0)
    col = lax.broadcasted_iota(jnp.int32, (cb, cb), 1)
    return rel(row, col).astype(bf16)


def _matmul(a, b, *, name, ta=False, tb=False, bias=None, addend=None, addend_scale=1.0, colsum=False,
            out_dtype=f32, tm=512, tn=512, tk=1024, hosted=None):
    M, K = (a.shape[1], a.shape[0]) if ta else a.shape
    N = b.shape[0] if tb else b.shape[1]
    assert K == (b.shape[1] if tb else b.shape[0])
    assert not (colsum and tb)
    tm, tn, tk = _tile(M, tm), _tile(N, tn), _tile(K, tk)
    nk = K // tk
    n_in = 2 + (bias is not None) + (addend is not None)
    n_out = 1 + colsum
    grid = (M // tm, N // tn, nk)

    def body(*refs):
        k = pl.program_id(2)
        step = (pl.program_id(0) * grid[1] + pl.program_id(1)) * nk + k
        scratch = refs[len(refs) - n_out - (len(hosted.sems) if hosted else 0):]
        own_in, own_out, finish = _host(hosted, refs[:len(refs) - len(scratch)] + scratch[n_out:], n_in, n_out, step,
                                        grid[0] * grid[1] * nk - 1)
        a_ref, b_ref = own_in[0], own_in[1]
        bias_ref = own_in[2] if bias is not None else None
        add_ref = own_in[n_in - 1] if addend is not None else None
        o_ref = own_out[0]
        cs_ref = own_out[1] if colsum else None
        acc = scratch[0]
        cs_acc = scratch[1] if colsum else None

        @pl.when(k == 0)
        def _():
            acc[...] = jnp.zeros_like(acc)
            if colsum:
                cs_acc[...] = jnp.zeros_like(cs_acc)

        dn = (((0 if ta else 1,), (1 if tb else 0,)), ((), ()))
        acc[...] += lax.dot_general(a_ref[...].astype(bf16), b_ref[...].astype(bf16), dn, preferred_element_type=f32)
        if colsum:
            cs_acc[...] += jnp.sum(b_ref[...].astype(f32), axis=0, keepdims=True)

        @pl.when(k == nk - 1)
        def _():
            r = acc[...]
            if bias is not None:
                r = r + bias_ref[...]
            if addend is not None:
                r = r + addend_scale * add_ref[...].astype(f32)
            o_ref[...] = r.astype(out_dtype)
            if colsum:
                cs_ref[0] = cs_acc[...]

        finish()

    a_spec = pl.BlockSpec((tk, tm), lambda i, j, k: (k, i)) if ta else pl.BlockSpec((tm, tk), lambda i, j, k: (i, k))
    b_spec = pl.BlockSpec((tn, tk), lambda i, j, k: (j, k)) if tb else pl.BlockSpec((tk, tn), lambda i, j, k: (k, j))
    in_specs, args = [a_spec, b_spec], [a, b]
    if bias is not None:
        in_specs.append(pl.BlockSpec((1, tn), lambda i, j, k: (0, j)))
        args.append(bias.reshape(1, N).astype(f32))
    if addend is not None:
        in_specs.append(pl.BlockSpec((tm, tn), lambda i, j, k: (i, j)))
        args.append(addend)
    out_shape = [jax.ShapeDtypeStruct((M, N), out_dtype)]
    out_specs = [pl.BlockSpec((tm, tn), lambda i, j, k: (i, j))]
    scratch = [pltpu.VMEM((tm, tn), f32)]
    if colsum:
        out_shape.append(jax.ShapeDtypeStruct((M // tm, 1, N), f32))
        out_specs.append(pl.BlockSpec((1, 1, tn), lambda i, j, k: (i, 0, j)))
        scratch.append(pltpu.VMEM((1, tn), f32))
    h_in = hosted.inputs if hosted else []
    h_out = hosted.outputs if hosted else []
    res = _pcall(body, name=name, grid=grid, in_specs=in_specs + [_ANY] * len(h_in),
                 out_specs=out_specs + [_ANY] * len(h_out), out_shape=out_shape + h_out,
                 scratch_shapes=scratch + (hosted.sems if hosted else []),
                 compiler_params=_params(("arbitrary", "arbitrary", "arbitrary")))(*args, *h_in)
    own = (res[0], res[1][0]) if colsum else res[0]
    return (own, list(res[n_out:])) if hosted else own


def _cumlogf(fl, B, S):
    t = _tile(S, 256, 8)

    def body(fl_ref, c_ref, carry):
        @pl.when(pl.program_id(1) == 0)
        def _():
            carry[...] = jnp.zeros_like(carry)
        z = fl_ref[...]
        ls = jnp.minimum(z, 0.0) - jnp.log(1.0 + jnp.exp(-jnp.abs(z)))
        row = lax.broadcasted_iota(jnp.int32, (t, t), 0)
        col = lax.broadcasted_iota(jnp.int32, (t, t), 1)
        lower = (col <= row).astype(f32)
        c = jnp.dot(lower, ls, precision=lax.Precision.HIGHEST, preferred_element_type=f32) + carry[...]
        c_ref[...] = c
        carry[...] = c[t - 1:t, :]

    return _pcall(body, name="cumlogf", grid=(B, S // t),
                  in_specs=[pl.BlockSpec((t, LANES), lambda b, i: (b * (S // t) + i, 0))],
                  out_specs=pl.BlockSpec((t, LANES), lambda b, i: (b * (S // t) + i, 0)),
                  out_shape=jax.ShapeDtypeStruct(fl.shape, f32), scratch_shapes=[pltpu.VMEM((1, LANES), f32)],
                  compiler_params=_params(("arbitrary", "arbitrary")))(fl)


def _cumlogf_bwd(dc, fl, B, S):
    t = _tile(S, 256, 8)
    n = S // t

    def body(dc_ref, fl_ref, o_ref, carry):
        @pl.when(pl.program_id(1) == 0)
        def _():
            carry[...] = jnp.zeros_like(carry)
        row = lax.broadcasted_iota(jnp.int32, (t, t), 0)
        col = lax.broadcasted_iota(jnp.int32, (t, t), 1)
        upper = (col >= row).astype(f32)
        r = jnp.dot(upper, dc_ref[...], precision=lax.Precision.HIGHEST, preferred_element_type=f32) + carry[...]
        carry[...] = r[0:1, :]
        z = fl_ref[...]
        o_ref[...] = r / (1.0 + jnp.exp(z))

    spec = pl.BlockSpec((t, LANES), lambda b, i: (b * n + n - 1 - i, 0))
    return _pcall(body, name="cumlogf_bwd", grid=(B, n), in_specs=[spec, spec], out_specs=spec,
                  out_shape=jax.ShapeDtypeStruct(fl.shape, f32), scratch_shapes=[pltpu.VMEM((1, LANES), f32)],
                  compiler_params=_params(("arbitrary", "arbitrary")))(dc, fl)


def _head_masks():
    lane = lax.broadcasted_iota(jnp.int32, (1, LANES), 1)
    return lane < HEAD_DIM


def _by_head(m0, t):
    z = jnp.zeros_like(t)
    return [jnp.where(m0, t, z), jnp.where(m0, z, t)]


def _sb_terms(z):
    softplus = jnp.maximum(z, 0.0) + jnp.log(1.0 + jnp.exp(-jnp.abs(z)))
    return z - softplus, -softplus


STRIP_ROWS = 32


def _strip_rows(tq):
    return STRIP_ROWS if tq % STRIP_ROWS == 0 else tq


def _strict(r, rs, tq):
    row = lax.broadcasted_iota(jnp.int32, (rs, tq), 0) + r
    col = lax.broadcasted_iota(jnp.int32, (rs, tq), 1)
    return col < row


def _score_scratch(tq, n_f32, n_bf16, n_sums):
    return ([pltpu.VMEM((tq, tq), f32)] * (2 * n_f32) + [pltpu.VMEM((tq, tq), bf16)] * (2 * n_bf16)
            + [pltpu.VMEM((tq, LANES), f32)] * (2 * n_sums))


def _by_pairs(refs):
    return [refs[i:i + 2] for i in range(0, len(refs), 2)]


def _sb_fwd(qkv, B, S, n_pairs, qcol, kcol, vcol, tq, shards=()):
    nq = S // tq
    T = B * S
    n = len(shards)

    cb = min(tq, SCAN_BLOCK)
    nb = tq // cb
    rs = _strip_rows(tq)

    def body(q_ref, k_ref, v_ref, *rest):
        o_ref, lt_ref = rest[n], rest[n + 1]
        z_s, suf_s, hi_s, lo_s, w_s, sum_s = _by_pairs(rest[len(rest) - 12:])
        i = pl.program_id(2)
        if n:
            start, finish = _gather_copies(rest[:n], rest[n + 2:2 * n + 2], *rest[2 * n + 2:2 * n + 5])
            step = (pl.program_id(0) * n_pairs + pl.program_id(1)) * nq + i
            pl.when(step == 0)(start)
        m0 = _head_masks()
        qh = _by_head(m0, q_ref[...])
        later = _tri(cb, lambda j, s: j > s)
        lane = lax.broadcasted_iota(jnp.int32, (1, LANES), 1)

        def tile(s0, R, acc, diag):
            kb = k_ref[pl.ds(s0, tq), :]
            vh = _by_head(m0, v_ref[pl.ds(s0, tq), :])
            R = list(R)
            for h in range(2):
                z_s[h][...] = _dot(qh[h], kb, _NT)
            for h in range(2):
                for r in range(0, tq, rs):
                    lb, l1m = _sb_terms(z_s[h][r:r + rs, :])
                    z_s[h][r:r + rs, :] = lb
                    if diag:
                        l1m = jnp.where(_strict(r, rs, tq), l1m, 0.0)
                    hi = l1m.astype(bf16)
                    hi_s[h][r:r + rs, :] = hi
                    lo_s[h][r:r + rs, :] = (l1m - hi.astype(f32)).astype(bf16)
                    sums = jnp.zeros((rs, LANES), f32)
                    for b in range(nb):
                        sums = jnp.where(lane == b, jnp.sum(l1m[:, b * cb:(b + 1) * cb], axis=1, keepdims=True), sums)
                    sum_s[h][r:r + rs, :] = sums
            for h in range(2):
                for b in range(nb):
                    blk = slice(b * cb, (b + 1) * cb)
                    suf_s[h][:, blk] = _dot(hi_s[h][:, blk], later) + _dot(lo_s[h][:, blk], later)
            for h in range(2):
                for r in range(0, tq, rs):
                    sums = sum_s[h][r:r + rs, :]
                    after = R[h][r:r + rs]
                    for b in reversed(range(nb)):
                        blk = slice(b * cb, (b + 1) * cb)
                        w = jnp.exp(z_s[h][r:r + rs, blk] + (suf_s[h][r:r + rs, blk] + after))
                        if diag:
                            w = jnp.where(_strict(r, rs, tq)[:, blk], w, 0.0)
                        w_s[h][r:r + rs, blk] = w.astype(bf16)
                        after = after + sums[:, b:b + 1]
            for h in range(2):
                acc = acc + _dot(w_s[h][...], vh[h])
                R[h] = R[h] + jnp.sum(sum_s[h][...], axis=1, keepdims=True)
            return R, acc

        zero = jnp.zeros((tq, 1), f32)
        R, acc = tile(pl.multiple_of(i * tq, tq), [zero, zero], jnp.zeros((tq, LANES), f32), True)

        def loop(n, carry):
            s0 = pl.multiple_of((i - 1 - n) * tq, tq)
            R, acc = tile(s0, carry[:2], carry[2], False)
            return R[0], R[1], acc

        R0, R1, acc = lax.fori_loop(0, i, loop, (R[0], R[1], acc))
        o_ref[...] = acc.astype(bf16)
        lt_ref[...] = jnp.where(m0, R0, R1)
        if n:
            pl.when(step == B * n_pairs * nq - 1)(finish)

    qs = lambda c: pl.BlockSpec((tq, LANES), lambda b, p, i: (b * nq + i, c + p))
    ks = lambda c: pl.BlockSpec((S, LANES), lambda b, p, i: (b, c + p))
    os_ = pl.BlockSpec((tq, LANES), lambda b, p, i: (b * nq + i, p))
    res = _pcall(body, name="sb_fwd", grid=(B, n_pairs, nq), in_specs=[qs(qcol), ks(kcol), ks(vcol)] + [_ANY] * n,
                 out_specs=[os_, os_] + [_ANY] * n,
                 out_shape=[jax.ShapeDtypeStruct((T, n_pairs * LANES), bf16), jax.ShapeDtypeStruct((T, n_pairs * LANES), f32)]
                 + _gather_shapes(shards), scratch_shapes=(_gather_sems(n) if n else []) + _score_scratch(tq, 2, 3, 1),
                 compiler_params=_params(("arbitrary", "arbitrary", "arbitrary")))(qkv, qkv, qkv, *shards)
    return res[0], res[1], res[2:]


def _sb_bwd(qkv, do, lt, B, S, n_pairs, qcol, kcol, vcol, tq, hosted=None):
    nq = S // tq
    T = B * S

    def body(*refs):
        i = pl.program_id(2)
        step = (pl.program_id(0) * n_pairs + pl.program_id(1)) * nq + i
        (q_ref, k_ref, v_ref, do_ref, lt_ref), (dq_ref, dk_ref, dv_ref), finish = _host(
            hosted, refs, 5, 3, step, B * n_pairs * nq - 1)

        @pl.when(i == 0)
        def _():
            dk_ref[...] = jnp.zeros_like(dk_ref)
            dv_ref[...] = jnp.zeros_like(dv_ref)

        m0 = _head_masks()
        qh = _by_head(m0, q_ref[...])
        doh = _by_head(m0, do_ref[...])
        lt = lt_ref[...]
        ltot = [lt[:, 0:1], lt[:, HEAD_DIM:HEAD_DIM + 1]]
        row = lax.broadcasted_iota(jnp.int32, (tq, tq), 0)
        col = lax.broadcasted_iota(jnp.int32, (tq, tq), 1)
        strict = col < row
        upto = _tri(min(tq, SCAN_BLOCK), lambda j, s: j <= s)
        before = _tri(min(tq, SCAN_BLOCK), lambda j, s: j < s)

        def tile(s0, CL, CP, dq, diag):
            kb = k_ref[pl.ds(s0, tq), :]
            vb = v_ref[pl.ds(s0, tq), :]
            kh = _by_head(m0, kb)
            CL, CP = list(CL), list(CP)
            dk = jnp.zeros((tq, LANES), f32)
            dv = jnp.zeros((tq, LANES), f32)
            for h in range(2):
                z = _dot(qh[h], kb, _NT)
                lb, l1m = _sb_terms(z)
                if diag:
                    l1m = jnp.where(strict, l1m, 0.0)
                pre, l_total = _scan_cols(l1m, upto, False)
                w = jnp.exp(lb + ((ltot[h] - CL[h]) - pre))
                if diag:
                    w = jnp.where(strict, w, 0.0)
                g = _dot(doh[h], vb, _NT) * w
                p, g_total = _scan_cols(g, before, False, split=False)
                dz = g - jnp.exp(lb) * (g + (p + CP[h]))
                if diag:
                    dz = jnp.where(strict, dz, 0.0)
                dzb = dz.astype(bf16)
                dq = dq + _dot(dzb, kh[h])
                dk = dk + _dot(dzb, qh[h], _TN)
                dv = dv + _dot(w.astype(bf16), doh[h], _TN)
                CL[h] = CL[h] + l_total
                CP[h] = CP[h] + g_total
            dk_ref[pl.ds(s0, tq), :] += dk
            dv_ref[pl.ds(s0, tq), :] += dv
            return CL, CP, dq

        zero = jnp.zeros((tq, 1), f32)

        def loop(n, carry):
            CL, CP, dq = tile(pl.multiple_of(n * tq, tq), carry[0:2], carry[2:4], carry[4], False)
            return CL[0], CL[1], CP[0], CP[1], dq

        c = lax.fori_loop(0, i, loop, (zero, zero, zero, zero, jnp.zeros((tq, LANES), f32)))
        _, _, dq = tile(pl.multiple_of(i * tq, tq), c[0:2], c[2:4], c[4], True)
        dq_ref[...] = dq * Q_SCALE
        finish()

    qs = lambda c: pl.BlockSpec((tq, LANES), lambda b, p, i: (b * nq + i, c + p))
    ks = lambda c: pl.BlockSpec((S, LANES), lambda b, p, i: (b, c + p))
    ts = pl.BlockSpec((tq, LANES), lambda b, p, i: (b * nq + i, p))
    fs = pl.BlockSpec((S, LANES), lambda b, p, i: (b, p))
    shp = jax.ShapeDtypeStruct((T, n_pairs * LANES), f32)
    return _host_call(body, "sb_bwd", (B, n_pairs, nq), [qs(qcol), ks(kcol), ks(vcol), ts, ts], [qkv, qkv, qkv, do, lt],
                      [ts, fs, fs], [shp, shp, shp], hosted)


def _fox_fwd(qkv, c, cT, B, S, n_pairs, qcol, kcol, vcol, tq):
    nq = S // tq
    T = B * S

    def body(q_ref, k_ref, v_ref, cq_ref, ck_ref, o_ref, o32_ref, lse_ref):
        p_idx = pl.program_id(1)
        i = pl.program_id(2)
        m0 = _head_masks()
        lane = lax.broadcasted_iota(jnp.int32, (1, LANES), 1)
        qh = _by_head(m0, q_ref[...])
        cq_all = cq_ref[...]
        cq = [jnp.sum(jnp.where(lane == 2 * p_idx + h, cq_all, 0.0), axis=1, keepdims=True) for h in range(2)]
        row = lax.broadcasted_iota(jnp.int32, (tq, tq), 0)
        col = lax.broadcasted_iota(jnp.int32, (tq, tq), 1)
        causal = col <= row

        def tile(s0, m, l, acc, diag):
            kb = k_ref[pl.ds(s0, tq), :]
            vh = _by_head(m0, v_ref[pl.ds(s0, tq), :])
            m, l = list(m), list(l)
            scale, add = [], []
            for h in range(2):
                z = _dot(qh[h], kb, _NT) + (cq[h] - ck_ref[h, :, pl.ds(s0, tq)])
                if diag:
                    z = jnp.where(causal, z, NEG)
                m_new = jnp.maximum(m[h], jnp.max(z, axis=1, keepdims=True))
                p = jnp.exp(z - m_new)
                a = jnp.exp(m[h] - m_new)
                l[h] = a * l[h] + jnp.sum(p, axis=1, keepdims=True)
                m[h] = m_new
                scale.append(a)
                add.append(_dot(p.astype(bf16), vh[h]))
            acc = acc * jnp.where(m0, scale[0], scale[1]) + add[0] + add[1]
            return m, l, acc

        neg = jnp.full((tq, 1), NEG, f32)
        zero = jnp.zeros((tq, 1), f32)
        m, l, acc = tile(pl.multiple_of(i * tq, tq), [neg, neg], [zero, zero], jnp.zeros((tq, LANES), f32), True)

        def loop(n, carry):
            m, l, acc = tile(pl.multiple_of(n * tq, tq), carry[0:2], carry[2:4], carry[4], False)
            return m[0], m[1], l[0], l[1], acc

        m0_, m1_, l0, l1, acc = lax.fori_loop(0, i, loop, (m[0], m[1], l[0], l[1], acc))
        o = acc * jnp.where(m0, 1.0 / l0, 1.0 / l1)
        o_ref[...] = o.astype(bf16)
        o32_ref[...] = o
        lse_ref[...] = jnp.where(m0, m0_ + jnp.log(l0), m1_ + jnp.log(l1))

    qs = lambda cc: pl.BlockSpec((tq, LANES), lambda b, p, i: (b * nq + i, cc + p))
    ks = lambda cc: pl.BlockSpec((S, LANES), lambda b, p, i: (b, cc + p))
    cqs = pl.BlockSpec((tq, LANES), lambda b, p, i: (b * nq + i, 0))
    cks = pl.BlockSpec((2, 1, S), lambda b, p, i: (b * n_pairs + p, 0, 0))
    os_ = pl.BlockSpec((tq, LANES), lambda b, p, i: (b * nq + i, p))
    shp = jax.ShapeDtypeStruct((T, n_pairs * LANES), f32)
    return _pcall(body, name="fox_fwd", grid=(B, n_pairs, nq), in_specs=[qs(qcol), ks(kcol), ks(vcol), cqs, cks],
                  out_specs=[os_, os_, os_], out_shape=[jax.ShapeDtypeStruct((T, n_pairs * LANES), bf16), shp, shp],
                  compiler_params=_params(("arbitrary", "arbitrary", "arbitrary")))(qkv, qkv, qkv, c, cT)


def _fox_bwd(qkv, c, cT, do, o, lse, B, S, n_pairs, qcol, kcol, vcol, tq, hosted=None):
    nq = S // tq
    T = B * S

    def body(*refs):
        p_idx = pl.program_id(1)
        i = pl.program_id(2)
        step = (pl.program_id(0) * n_pairs + p_idx) * nq + i
        (q_ref, k_ref, v_ref, cq_ref, ck_ref, do_ref, o_ref, lse_ref), (dq_ref, dk_ref, dv_ref, dc_ref), finish = _host(
            hosted, refs, 8, 4, step, B * n_pairs * nq - 1)

        @pl.when(i == 0)
        def _():
            dk_ref[...] = jnp.zeros_like(dk_ref)
            dv_ref[...] = jnp.zeros_like(dv_ref)
            dc_ref[...] = jnp.zeros_like(dc_ref)

        m0 = _head_masks()
        lane = lax.broadcasted_iota(jnp.int32, (1, LANES), 1)
        qh = _by_head(m0, q_ref[...])
        do2 = do_ref[...]
        doh = _by_head(m0, do2)
        prod = do2.astype(f32) * o_ref[...].astype(f32)
        delta = [jnp.sum(p, axis=1, keepdims=True) for p in _by_head(m0, prod)]
        ls = lse_ref[...]
        lse = [ls[:, 0:1], ls[:, HEAD_DIM:HEAD_DIM + 1]]
        cq_all = cq_ref[...]
        cq = [jnp.sum(jnp.where(lane == 2 * p_idx + h, cq_all, 0.0), axis=1, keepdims=True) for h in range(2)]
        row = lax.broadcasted_iota(jnp.int32, (tq, tq), 0)
        col = lax.broadcasted_iota(jnp.int32, (tq, tq), 1)
        causal = col <= row

        def tile(s0, dq, diag):
            kb = k_ref[pl.ds(s0, tq), :]
            vb = v_ref[pl.ds(s0, tq), :]
            kh = _by_head(m0, kb)
            dk = jnp.zeros((tq, LANES), f32)
            dv = jnp.zeros((tq, LANES), f32)
            for h in range(2):
                z = _dot(qh[h], kb, _NT) + (cq[h] - ck_ref[h, :, pl.ds(s0, tq)])
                p = jnp.exp(z - lse[h])
                if diag:
                    p = jnp.where(causal, p, 0.0)
                ds = p * (_dot(doh[h], vb, _NT) - delta[h])
                dsb = ds.astype(bf16)
                dq = dq + _dot(dsb, kh[h])
                dk = dk + _dot(dsb, qh[h], _TN)
                dv = dv + _dot(p.astype(bf16), doh[h], _TN)
                dc_ref[h, :, pl.ds(s0, tq)] -= jnp.sum(ds, axis=0, keepdims=True)
            dk_ref[pl.ds(s0, tq), :] += dk
            dv_ref[pl.ds(s0, tq), :] += dv
            return dq

        dq = lax.fori_loop(0, i, lambda n, dq: tile(pl.multiple_of(n * tq, tq), dq, False), jnp.zeros((tq, LANES), f32))
        dq = tile(pl.multiple_of(i * tq, tq), dq, True)
        dq_ref[...] = dq * Q_SCALE
        finish()

    qs = lambda cc: pl.BlockSpec((tq, LANES), lambda b, p, i: (b * nq + i, cc + p))
    ks = lambda cc: pl.BlockSpec((S, LANES), lambda b, p, i: (b, cc + p))
    cqs = pl.BlockSpec((tq, LANES), lambda b, p, i: (b * nq + i, 0))
    cks = pl.BlockSpec((2, 1, S), lambda b, p, i: (b * n_pairs + p, 0, 0))
    ts = pl.BlockSpec((tq, LANES), lambda b, p, i: (b * nq + i, p))
    fs = pl.BlockSpec((S, LANES), lambda b, p, i: (b, p))
    shp = jax.ShapeDtypeStruct((T, n_pairs * LANES), f32)
    return _host_call(body, "fox_bwd", (B, n_pairs, nq), [qs(qcol), ks(kcol), ks(vcol), cqs, cks, ts, ts, ts],
                      [qkv, qkv, qkv, c, cT, do, o, lse], [ts, fs, fs, cks],
                      [shp, shp, shp, jax.ShapeDtypeStruct(cT.shape, f32)], hosted)


def _sigmoid(x):
    return 1.0 / (1.0 + jnp.exp(-x))


def _mix_fwd(o_sb, o_fx, g, x, wp_sb, wp_fx, w_out, ln_g, ln_b, tm):
    T, D = x.shape
    E = o_sb.shape[1]
    tm = _tile(T, tm, 8)

    def body(osb_ref, ofx_ref, gsb_ref, gfx_ref, x_ref, wsb_ref, wfx_ref, wo_ref, lg_ref, lb_ref,
             xhat_ref, rstd_ref, x1_ref, mg_ref):
        y_sb = _dot(osb_ref[...], wsb_ref[...])
        y_fx = _dot(ofx_ref[...], wfx_ref[...])
        merged = (_sigmoid(gsb_ref[...]) * y_sb + _sigmoid(gfx_ref[...]) * y_fx).astype(bf16)
        r = ALPHA * x_ref[...] + _dot(merged, wo_ref[...])
        mean = jnp.mean(r, axis=1, keepdims=True)
        cen = r - mean
        rstd = lax.rsqrt(jnp.mean(cen * cen, axis=1, keepdims=True) + LN_EPS)
        xhat = cen * rstd
        xhat_ref[...] = xhat
        rstd_ref[...] = rstd
        x1_ref[...] = (xhat * lg_ref[...] + lb_ref[...]).astype(bf16)
        mg_ref[...] = merged

    rows = lambda w, c=0: pl.BlockSpec((tm, w), lambda i: (i, c))
    full = lambda a: pl.BlockSpec(a.shape, lambda i: (0, 0))
    return _pcall(body, name="mix_fwd", grid=(T // tm,),
                  in_specs=[rows(E), rows(E), rows(D, 0), rows(D, 1), rows(D), full(wp_sb), full(wp_fx), full(w_out),
                            full(ln_g), full(ln_b)],
                  out_specs=[rows(D), rows(1), rows(D), rows(D)],
                  out_shape=[jax.ShapeDtypeStruct((T, D), f32), jax.ShapeDtypeStruct((T, 1), f32),
                             jax.ShapeDtypeStruct((T, D), bf16), jax.ShapeDtypeStruct((T, D), bf16)],
                  compiler_params=_params(("arbitrary",)))(o_sb, o_fx, g, g, x, wp_sb, wp_fx, w_out, ln_g, ln_b)


def _mix_bwd(dr1, o_sb, o_fx, g, wp_sb, wp_fx, w_out, tm):
    T, D = dr1.shape
    E = o_sb.shape[1]
    tm = _tile(T, tm, 8)

    def body(dr_ref, osb_ref, ofx_ref, gsb_ref, gfx_ref, wsb_ref, wfx_ref, wo_ref,
             dysb_ref, dyfx_ref, dg_ref, dosb_ref, dofx_ref, sumsb_ref, sumfx_ref):
        @pl.when(pl.program_id(0) == 0)
        def _():
            sumsb_ref[...] = jnp.zeros_like(sumsb_ref)
            sumfx_ref[...] = jnp.zeros_like(sumfx_ref)
        dm = _dot(dr_ref[...].astype(bf16), wo_ref[...], _NT)
        for half, (o_ref, g_ref, w_ref, dy_ref, do_ref, sum_ref) in enumerate((
                (osb_ref, gsb_ref, wsb_ref, dysb_ref, dosb_ref, sumsb_ref),
                (ofx_ref, gfx_ref, wfx_ref, dyfx_ref, dofx_ref, sumfx_ref))):
            y = _dot(o_ref[...], w_ref[...])
            s = _sigmoid(g_ref[...])
            dy = (dm * s).astype(bf16)
            dy_ref[...] = dy
            dg = dm * y * s * (1.0 - s)
            dg_ref[:, half * D:(half + 1) * D] = dg.astype(bf16)
            sum_ref[0:1, :] += jnp.sum(dg, axis=0, keepdims=True)
            do_ref[...] = _dot(dy, w_ref[...], _NT).astype(bf16)

    rows = lambda w, c=0: pl.BlockSpec((tm, w), lambda i: (i, c))
    full = lambda a: pl.BlockSpec(a.shape, lambda i: (0, 0))
    acc = pl.BlockSpec((8, D), lambda i: (0, 0))
    res = _pcall(body, name="mix_bwd", grid=(T // tm,),
                 in_specs=[rows(D), rows(E), rows(E), rows(D, 0), rows(D, 1), full(wp_sb), full(wp_fx), full(w_out)],
                 out_specs=[rows(D), rows(D), rows(2 * D), rows(E), rows(E), acc, acc],
                 out_shape=[jax.ShapeDtypeStruct((T, D), bf16)] * 2 + [jax.ShapeDtypeStruct((T, 2 * D), bf16)]
                 + [jax.ShapeDtypeStruct((T, E), bf16)] * 2 + [jax.ShapeDtypeStruct((8, D), f32)] * 2,
                 compiler_params=_params(("arbitrary",)))(dr1, o_sb, o_fx, g, g, wp_sb, wp_fx, w_out)
    return res


def _ln_bwd(dy_a, dy_b, scale_b, xhat, rstd, ln_g, tm):
    T, D = xhat.shape
    tm = _tile(T, tm, 8)

    def body(a_ref, b_ref, xh_ref, rs_ref, g_ref, dr_ref, st_ref):
        @pl.when(pl.program_id(0) == 0)
        def _():
            st_ref[...] = jnp.zeros_like(st_ref)
        dy = a_ref[...] + scale_b * b_ref[...]
        xh = xh_ref[...]
        dxh = dy * g_ref[...]
        m1 = jnp.mean(dxh, axis=1, keepdims=True)
        m2 = jnp.mean(dxh * xh, axis=1, keepdims=True)
        dr_ref[...] = rs_ref[...] * (dxh - m1 - xh * m2)
        st_ref[0:1, :] += jnp.sum(dy * xh, axis=0, keepdims=True)
        st_ref[1:2, :] += jnp.sum(dy, axis=0, keepdims=True)

    rows = lambda w: pl.BlockSpec((tm, w), lambda i: (i, 0))
    return _pcall(body, name="ln1_bwd", grid=(T // tm,),
                  in_specs=[rows(D), rows(D), rows(D), rows(1), pl.BlockSpec((1, D), lambda i: (0, 0))],
                  out_specs=[rows(D), pl.BlockSpec((8, D), lambda i: (0, 0))],
                  out_shape=[jax.ShapeDtypeStruct((T, D), f32), jax.ShapeDtypeStruct((8, D), f32)],
                  compiler_params=_params(("arbitrary",)))(dy_a, dy_b, xhat, rstd, ln_g)


_INV_SQRT2 = 1.0 / math.sqrt(2.0)
_INV_SQRT2PI = 1.0 / math.sqrt(2.0 * math.pi)


def _conv_rows(ref, r0, rc, first, wc, bc):
    cur = ref[pl.ds(r0, rc), :].astype(f32)
    prev = ref[pl.ds(pl.multiple_of(jnp.maximum(r0 - 16, 0), 16), 16), :].astype(f32)
    prev = jnp.where(first, jnp.zeros_like(prev), prev)
    rid = lax.broadcasted_iota(jnp.int32, (rc, LANES), 0)
    s1 = jnp.where(rid == 0, prev[15:16, :], pltpu.roll(cur, 1, 0))
    s2 = jnp.where(rid == 0, prev[14:15, :], jnp.where(rid == 1, prev[15:16, :], pltpu.roll(cur, 2, 0)))
    conv = bc + wc[0:1, :] * s2 + wc[1:2, :] * s1 + wc[2:3, :] * cur
    return conv, (s2, s1, cur)


def _glu_fwd(u, w_conv, b_conv, B, S, rc=512):
    F = u.shape[1] // 2
    nf = F // LANES
    rc = _tile(S, rc, 8)

    def body(ug_ref, uv_ref, wc_ref, bc_ref, a_ref):
        wc, bc = wc_ref[...], bc_ref[...]

        def chunk(n, _):
            r0 = pl.multiple_of(n * rc, rc)
            c, _taps = _conv_rows(ug_ref, r0, rc, n == 0, wc, bc)
            gelu = 0.5 * c * (1.0 + lax.erf(c * _INV_SQRT2))
            a_ref[pl.ds(r0, rc), :] = (gelu * uv_ref[pl.ds(r0, rc), :].astype(f32)).astype(bf16)
            return 0

        lax.fori_loop(0, S // rc, chunk, 0)

    return _pcall(body, name="glu_fwd", grid=(B, nf),
                  in_specs=[pl.BlockSpec((S, LANES), lambda b, j: (b, j)), pl.BlockSpec((S, LANES), lambda b, j: (b, nf + j)),
                            pl.BlockSpec((3, LANES), lambda b, j: (0, j)), pl.BlockSpec((1, LANES), lambda b, j: (0, j))],
                  out_specs=pl.BlockSpec((S, LANES), lambda b, j: (b, j)),
                  out_shape=jax.ShapeDtypeStruct((B * S, F), bf16),
                  compiler_params=_params(("arbitrary", "arbitrary")))(u, u, w_conv, b_conv)


def _glu_bwd(u, da, w_conv, b_conv, B, S, rc=512):
    F = u.shape[1] // 2
    nf = F // LANES
    rc = _tile(S, rc, 8)
    nc = S // rc

    def body(ug_ref, uv_ref, da_ref, wc_ref, bc_ref, dug_ref, duv_ref, gw_ref, gb_ref, dc_ref):
        wc, bc = wc_ref[...], bc_ref[...]

        def chunk(n, carry):
            gw0, gw1, gw2, gb = carry
            r0 = pl.multiple_of(n * rc, rc)
            c, (s2, s1, cur) = _conv_rows(ug_ref, r0, rc, n == 0, wc, bc)
            cdf = 0.5 * (1.0 + lax.erf(c * _INV_SQRT2))
            da = da_ref[pl.ds(r0, rc), :]
            duv_ref[pl.ds(r0, rc), :] = (da * (c * cdf)).astype(bf16)
            dc = da * uv_ref[pl.ds(r0, rc), :].astype(f32) * (cdf + c * (_INV_SQRT2PI * jnp.exp(-0.5 * c * c)))
            dc_ref[pl.ds(r0, rc), :] = dc
            red = lambda t: jnp.sum(t, axis=0, keepdims=True)
            return gw0 + red(dc * s2), gw1 + red(dc * s1), gw2 + red(dc * cur), gb + red(dc)

        z = jnp.zeros((1, LANES), f32)
        gw0, gw1, gw2, gb = lax.fori_loop(0, nc, chunk, (z, z, z, z))
        gw_ref[0, 0:1, :] = gw0
        gw_ref[0, 1:2, :] = gw1
        gw_ref[0, 2:3, :] = gw2
        gb_ref[0] = gb

        def chunk2(n, _):
            r0 = pl.multiple_of(n * rc, rc)
            cur = dc_ref[pl.ds(r0, rc), :]
            nxt = dc_ref[pl.ds(pl.multiple_of(jnp.minimum(r0 + rc, S - 8), 8), 8), :]
            nxt = jnp.where(n == nc - 1, jnp.zeros_like(nxt), nxt)
            rid = lax.broadcasted_iota(jnp.int32, (rc, LANES), 0)
            a1 = jnp.where(rid == rc - 1, nxt[0:1, :], pltpu.roll(cur, rc - 1, 0))
            a2 = jnp.where(rid == rc - 1, nxt[1:2, :], jnp.where(rid == rc - 2, nxt[0:1, :], pltpu.roll(cur, rc - 2, 0)))
            dug_ref[pl.ds(r0, rc), :] = (wc[2:3, :] * cur + wc[1:2, :] * a1 + wc[0:1, :] * a2).astype(bf16)
            return 0

        lax.fori_loop(0, nc, chunk2, 0)

    blk = lambda off: pl.BlockSpec((S, LANES), lambda b, j: (b, off + j))
    return _pcall(body, name="glu_bwd", grid=(B, nf),
                  in_specs=[blk(0), blk(nf), blk(0), pl.BlockSpec((3, LANES), lambda b, j: (0, j)),
                            pl.BlockSpec((1, LANES), lambda b, j: (0, j))],
                  out_specs=[blk(0), blk(0), pl.BlockSpec((1, 3, LANES), lambda b, j: (b, 0, j)),
                             pl.BlockSpec((1, 1, LANES), lambda b, j: (b, 0, j))],
                  out_shape=[jax.ShapeDtypeStruct((B * S, F), bf16), jax.ShapeDtypeStruct((B * S, F), bf16),
                             jax.ShapeDtypeStruct((B, 3, F), f32), jax.ShapeDtypeStruct((B, 1, F), f32)],
                  scratch_shapes=[pltpu.VMEM((S, LANES), f32)],
                  compiler_params=_params(("arbitrary", "arbitrary")))(u, u, da, w_conv, b_conv)


def _down_loss(a, w_down, xhat1, ln1_g, ln1_b, ln2_g, ln2_b, tgt, tm):
    T, D = xhat1.shape
    F = a.shape[1]
    tm = _tile(T, tm, 8)

    def body(a_ref, w_ref, xh_ref, g1_ref, b1_ref, g2_ref, b2_ref, t_ref, dr_ref, st_ref):
        @pl.when(pl.program_id(0) == 0)
        def _():
            st_ref[...] = jnp.zeros_like(st_ref)
        x1 = xh_ref[...] * g1_ref[...] + b1_ref[...]
        r = ALPHA * x1 + _dot(a_ref[...], w_ref[...])
        mean = jnp.mean(r, axis=1, keepdims=True)
        cen = r - mean
        rstd = lax.rsqrt(jnp.mean(cen * cen, axis=1, keepdims=True) + LN_EPS)
        xh = cen * rstd
        err = (xh * g2_ref[...] + b2_ref[...]) - t_ref[...]
        dy = err * (1.0 / D)
        dxh = dy * g2_ref[...]
        m1 = jnp.mean(dxh, axis=1, keepdims=True)
        m2 = jnp.mean(dxh * xh, axis=1, keepdims=True)
        dr_ref[...] = rstd * (dxh - m1 - xh * m2)
        st_ref[0:1, :] += jnp.sum(dy * xh, axis=0, keepdims=True)
        st_ref[1:2, :] += jnp.sum(dy, axis=0, keepdims=True)
        st_ref[2:3, :] += jnp.sum(err * err, axis=0, keepdims=True)

    rows = lambda w: pl.BlockSpec((tm, w), lambda i: (i, 0))
    vec = pl.BlockSpec((1, D), lambda i: (0, 0))
    return _pcall(body, name="down_loss", grid=(T // tm,),
                  in_specs=[rows(F), pl.BlockSpec((F, D), lambda i: (0, 0)), rows(D), vec, vec, vec, vec, rows(D)],
                  out_specs=[rows(D), pl.BlockSpec((8, D), lambda i: (0, 0))],
                  out_shape=[jax.ShapeDtypeStruct((T, D), f32), jax.ShapeDtypeStruct((8, D), f32)],
                  compiler_params=_params(("arbitrary",)))(a, w_down, xhat1, ln1_g, ln1_b, ln2_g, ln2_b, tgt)


def _local_step(x, tgt, w_in, b_in, ln1_g, ln1_b, b_conv, ln2_g, ln2_b, late_shards, late_weights, reducer=None,
                tail=None, tq=512, tm=512):
    B, S, D = x.shape
    T = B * S
    E = (w_in.shape[1] - 2 * D) * HEAD_DIM // (6 * HEAD_DIM + 1)
    n_pairs = E // LANES
    NH = E // HEAD_DIM
    x2 = x.reshape(T, D)
    tgt2 = tgt.reshape(T, D)
    tq = _tile(S, tq, 8)

    c_f, c_g = 6 * E, 6 * E + NH
    w_qkv = w_in[:, :c_f]
    qscale = jnp.concatenate([jnp.full((E,), Q_SCALE, f32), jnp.ones((2 * E,), f32)] * 2)
    w_qkv_s = (w_qkv.astype(f32) * qscale).astype(bf16)
    b_qkv_s = b_in[:, :c_f] * qscale
    w_f = jnp.pad(w_in[:, c_f:c_g], ((0, 0), (0, LANES - NH)))
    b_f = jnp.pad(b_in[:, c_f:c_g], ((0, 0), (0, LANES - NH)))
    w_g = w_in[:, c_g:]
    b_g = b_in[:, c_g:]

    xb = x2.astype(bf16)
    qkv = _matmul(xb, w_qkv_s, bias=b_qkv_s, out_dtype=bf16, tm=1024, tn=1536, name="proj_qkv")
    g = _matmul(xb, w_g, bias=b_g, tm=1024, tn=1024, name="proj_gate")
    fl = _matmul(xb, w_f, bias=b_f, tm=1024, name="proj_forget")
    c = _cumlogf(fl, B, S)
    cT = c.reshape(B, S, LANES)[:, :, :NH].transpose(0, 2, 1).reshape(B * NH, 1, S)
    P = n_pairs
    o_sb, lt_sb, stacks = _sb_fwd(qkv, B, S, P, 0, P, 2 * P, tq, late_shards)
    wp_sb, wp_fx, w_out, w_up, w_conv, w_down = late_weights(stacks)
    F = w_down.shape[0]
    o_fx, o_fx32, lse_fx = _fox_fwd(qkv, c, cT, B, S, P, 3 * P, 4 * P, 5 * P, tq)
    xhat1, rstd1, x1b, merged = _mix_fwd(o_sb, o_fx, g, x2, wp_sb, wp_fx, w_out, ln1_g, ln1_b, tm)
    u = _matmul(x1b, w_up, out_dtype=bf16, tm=1024, tn=1408, name="ffn_up")
    a = _glu_fwd(u, w_conv, b_conv, B, S)
    dr2, st2 = _down_loss(a, w_down, xhat1, ln1_g, ln1_b, ln2_g, ln2_b, tgt2, tm)

    grads = {}
    grads["ln2_g"], grads["ln2_b"] = st2[0:1], st2[1:2]
    sq_err = st2[2:3]
    da = _matmul(dr2, w_down, tb=True, tm=1024, tn=1408, name="ffn_da")
    grads["w_down"] = _matmul(a, dr2, ta=True, tm=1408, tn=1024, tk=1024, name="grad_w_down")
    du_g, du_v, gwc, gbc = _glu_bwd(u, da, w_conv, b_conv, B, S)
    grads["w_conv"] = jnp.sum(gwc, axis=0)
    grads["b_conv"] = jnp.sum(gbc, axis=0)
    grads["w_up"] = jnp.concatenate(
        [_matmul(x1b, du_g, ta=True, tm=512, tn=2816, tk=1024, name="grad_w_up_gate"),
         _matmul(x1b, du_v, ta=True, tm=512, tn=2816, tk=1024, name="grad_w_up_val")], axis=1)
    dx1 = _matmul(du_g, w_up[:, :F], tb=True, tm=1024, tn=1024, tk=F, name="ffn_dx_gate")
    dx1 = _matmul(du_v, w_up[:, F:], tb=True, addend=dx1, tm=1024, tn=1024, tk=F, name="ffn_dx_val")
    dr1, st1 = _ln_bwd(dx1, dr2, ALPHA, xhat1, rstd1, ln1_g, tm)
    grads["ln1_g"], grads["ln1_b"] = st1[0:1], st1[1:2]

    dy_sb, dy_fx, dg, do_sb, do_fx, gsum_sb, gsum_fx = _mix_bwd(dr1, o_sb, o_fx, g, wp_sb, wp_fx, w_out, tm)
    grads["w_out"] = _matmul(merged, dr1, ta=True, tm=512, tn=1024, tk=1024, name="grad_w_out")
    grads["w_proj_sb"] = _matmul(o_sb, dy_sb, ta=True, tm=512, tn=1024, tk=1024, name="grad_w_proj_sb")
    grads["w_proj_fox"] = _matmul(o_fx, dy_fx, ta=True, tm=512, tn=1024, tk=1024, name="grad_w_proj_fox")
    (dq_f, dk_f, dv_f, dcT), got = _fox_bwd(qkv, c, cT, do_fx, o_fx32, lse_fx, B, S, P, 3 * P, 4 * P, 5 * P, tq,
                                            reducer.to_sibling(grads) if reducer else None)
    (dq_s, dk_s, dv_s), got = _sb_bwd(qkv, do_sb, lt_sb, B, S, P, 0, P, 2 * P, tq, reducer.to_chips(got) if reducer else None)
    if reducer:
        reducer.from_chips(got)
    dc = jnp.pad(dcT.reshape(B, NH, S).transpose(0, 2, 1), ((0, 0), (0, 0), (0, LANES - NH))).reshape(T, LANES)
    dfl = _cumlogf_bwd(dc, fl, B, S)
    dqkv = jnp.concatenate([dq_s, dk_s, dv_s, dq_f, dk_f, dv_f], axis=1)
    gw_qkv, gb_qkv = _matmul(x2, dqkv, ta=True, colsum=True, tm=512, tn=3072, tk=512, name="grad_w_qkv")
    gw_f, gb_f = _matmul(x2, dfl, ta=True, colsum=True, tm=512, tk=1024, name="grad_w_forget")
    gw_g = _matmul(x2, dg, ta=True, tm=512, tn=2048, tk=1024, name="grad_w_gate")
    grads["w_in"] = jnp.concatenate([gw_qkv, gw_f[:, :NH], gw_g], axis=1)
    grads["b_in"] = jnp.concatenate([gb_qkv, gb_f[:, :NH], gsum_sb[0:1], gsum_fx[0:1]], axis=1)
    dx = _matmul(dfl, w_f, tb=True, addend=dr1, addend_scale=ALPHA, tm=1024, tn=1024, name="dx_forget",
                 hosted=tail.to_sibling(grads, sq_err) if tail else None)
    if tail:
        dx, got = dx
        to_chips = tail.to_chips(got)
    dx = _matmul(dqkv, w_qkv, tb=True, addend=dx, tm=512, tn=1024, tk=c_f, name="dx_qkv", hosted=to_chips if tail else None)
    if tail:
        dx, got = dx
        tail.from_chips(got)
    dx = _matmul(dg, w_g, tb=True, addend=dx, tm=1024, tn=1024, tk=2 * D, name="dx_gate")
    return sq_err, dx.reshape(B, S, D), grads


_ANY = pl.BlockSpec(memory_space=pl.ANY)
_MESH = pl.DeviceIdType.MESH


def _pos():
    return lax.axis_index("x"), lax.axis_index("y"), lax.axis_index("c")


def _other_chips(x, y):
    return [(1 - x, y), (x, 1 - y), (1 - x, 1 - y)]


def _gather_split(shard):
    h = shard.shape[0] // 2

    def body(src_ref, dst_ref, send_sems, recv_sems, pass_send, pass_recv):
        x, y, c = _pos()
        k = 2 * x + y
        chips = _other_chips(x, y)
        slots = [2 * chip[0] + chip[1] for chip in chips]

        def rows(ref, half):
            return ref.at[pl.ds(pl.multiple_of(half * h, 16), h), :]

        def fetch(j, slot):
            return pltpu.make_async_remote_copy(src_ref=rows(src_ref, c), dst_ref=rows(dst_ref.at[slot], c),
                                                send_sem=send_sems.at[j], recv_sem=recv_sems.at[j],
                                                device_id=(*chips[j], c), device_id_type=_MESH)

        def pass_on(j, half):
            blk = rows(dst_ref.at[slots[j]], half)
            return pltpu.make_async_remote_copy(src_ref=blk, dst_ref=blk, send_sem=pass_send.at[j], recv_sem=pass_recv.at[j],
                                                device_id=(x, y, 1 - c), device_id_type=_MESH)

        for j in range(3):
            fetch(j, k).start()
        for j in range(3):
            fetch(j, slots[j]).wait_recv()
            pass_on(j, c).start()
        for j in range(3):
            pass_on(j, 1 - c).wait_recv()
        for j in range(3):
            fetch(j, k).wait_send()
            pass_on(j, c).wait_send()

    sems = pltpu.SemaphoreType.DMA((3,))
    return _pcall(body, name="gather_w_in", in_specs=[_ANY], out_specs=_ANY,
                  out_shape=jax.ShapeDtypeStruct((4,) + shard.shape, shard.dtype), scratch_shapes=[sems] * 4,
                  compiler_params=pltpu.CompilerParams(has_side_effects=True))(shard)


def _gather_shapes(shards):
    return [jax.ShapeDtypeStruct((4,) + s.shape, s.dtype) for s in shards]


def _gather_sems(n):
    return [pltpu.SemaphoreType.DMA((n, 3)), pltpu.SemaphoreType.DMA((n, 3)), pltpu.SemaphoreType.DMA((n,))]


def _gather_copies(srcs, dsts, send_sems, recv_sems, local_sems):
    n = len(srcs)
    x, y, c = _pos()
    k = 2 * x + y
    chips = _other_chips(x, y)

    def copy(a, j, chip, slot):
        return pltpu.make_async_remote_copy(src_ref=srcs[a], dst_ref=dsts[a].at[slot], send_sem=send_sems.at[a, j],
                                            recv_sem=recv_sems.at[a, j], device_id=(*chip, c), device_id_type=_MESH)

    def mine():
        return ([pltpu.make_async_copy(srcs[a], dsts[a].at[k], local_sems.at[a]) for a in range(n)],
                [copy(a, j, chip, k) for a in range(n) for j, chip in enumerate(chips)])

    def start():
        local, sends = mine()
        for cp in local + sends:
            cp.start()

    def finish():
        local, sends = mine()
        for a in range(n):
            for j, chip in enumerate(chips):
                copy(a, j, chip, 2 * chip[0] + chip[1]).wait_recv()
        for cp in sends:
            cp.wait_send()
        for cp in local:
            cp.wait()

    return start, finish


class _Hosted:
    def __init__(self, inputs, outputs, sems, copies):
        self.inputs, self.outputs, self.sems, self.copies = list(inputs), list(outputs), list(sems), copies


def _host(hosted, refs, n_in, n_out, step, last):
    if hosted is None:
        return refs[:n_in], refs[n_in:n_in + n_out], lambda: None
    hi, ho = len(hosted.inputs), len(hosted.outputs)
    own_in, h_in = refs[:n_in], refs[n_in:n_in + hi]
    own_out = refs[n_in + hi:n_in + hi + n_out]
    h_out = refs[n_in + hi + n_out:n_in + hi + n_out + ho]
    start, finish = hosted.copies(h_in, h_out, *refs[n_in + hi + n_out + ho:])
    pl.when(step == 0)(start)
    return own_in, own_out, lambda: pl.when(step == last)(finish)


def _host_call(body, name, grid, in_specs, args, out_specs, out_shape, hosted):
    h_in = hosted.inputs if hosted else []
    h_out = hosted.outputs if hosted else []
    res = _pcall(body, name=name, grid=grid, in_specs=list(in_specs) + [_ANY] * len(h_in),
                 out_specs=list(out_specs) + [_ANY] * len(h_out), out_shape=list(out_shape) + list(h_out),
                 scratch_shapes=hosted.sems if hosted else [],
                 compiler_params=_params(("arbitrary",) * len(grid)))(*args, *h_in)
    return list(res[:len(out_shape)]), list(res[len(out_shape):])


def _sibling_copies(shapes):
    n = len(shapes)

    def copies(p_refs, got_refs, send_sems, recv_sems):
        x, y, c = _pos()

        def copy(a):
            h = shapes[a][1] // 2
            src = p_refs[a].at[:, pl.ds(pl.multiple_of((1 - c) * h, 8), h), :]
            return pltpu.make_async_remote_copy(src_ref=src, dst_ref=got_refs[a], send_sem=send_sems.at[a],
                                                recv_sem=recv_sems.at[a], device_id=(x, y, 1 - c), device_id_type=_MESH)

        def start():
            for a in range(n):
                copy(a).start()

        def finish():
            for a in range(n):
                copy(a).wait()

        return start, finish

    return copies


def _hosted_sibling(pieces):
    n = len(pieces)
    return _Hosted(pieces, [jax.ShapeDtypeStruct((4, p.shape[1] // 2, p.shape[2]), p.dtype) for p in pieces],
                   [pltpu.SemaphoreType.DMA((n,)), pltpu.SemaphoreType.DMA((n,))], _sibling_copies([p.shape for p in pieces]))


def _chips_copies(n):
    def copies(p_refs, got_refs, send_sems, recv_sems):
        x, y, c = _pos()
        k = 2 * x + y
        chips = _other_chips(x, y)

        def copy(a, j, chip, piece, slot):
            return pltpu.make_async_remote_copy(src_ref=p_refs[a].at[piece], dst_ref=got_refs[a].at[slot],
                                                send_sem=send_sems.at[a, j], recv_sem=recv_sems.at[a, j],
                                                device_id=(*chip, c), device_id_type=_MESH)

        def start():
            for a in range(n):
                for j, chip in enumerate(chips):
                    copy(a, j, chip, 2 * chip[0] + chip[1], k).start()

        def finish():
            for a in range(n):
                for j, chip in enumerate(chips):
                    copy(a, j, chip, k, 2 * chip[0] + chip[1]).wait_recv()
            for a in range(n):
                for j, chip in enumerate(chips):
                    copy(a, j, chip, 2 * chip[0] + chip[1], k).wait_send()

        return start, finish

    return copies


def _hosted_chips(pieces):
    n = len(pieces)
    return _Hosted(pieces, [jax.ShapeDtypeStruct(p.shape, p.dtype) for p in pieces],
                   [pltpu.SemaphoreType.DMA((n, 3)), pltpu.SemaphoreType.DMA((n, 3))], _chips_copies(n))


def _small_copies(sm_ref, sg_ref, send_sems, recv_sems, local_sem):
    x, y, c = _pos()
    me = 4 * x + 2 * y + c
    flip = lambda v, bit: 1 - v if bit else v
    peers = [(flip(x, r & 4), flip(y, r & 2), flip(c, r & 1)) for r in range(1, 8)]

    def copy(j, slot):
        return pltpu.make_async_remote_copy(src_ref=sm_ref, dst_ref=sg_ref.at[slot], send_sem=send_sems.at[j],
                                            recv_sem=recv_sems.at[j], device_id=peers[j], device_id_type=_MESH)

    def local():
        return pltpu.make_async_copy(sm_ref, sg_ref.at[me], local_sem)

    def start():
        local().start()
        for j in range(7):
            copy(j, me).start()

    def finish():
        for j, (px, py, pc) in enumerate(peers):
            copy(j, 4 * px + 2 * py + pc).wait_recv()
        for j in range(7):
            copy(j, me).wait_send()
        local().wait()

    return start, finish


def _hosted_sibling_and_small(pieces, small):
    n = len(pieces)
    sibling = _hosted_sibling(pieces)

    def copies(in_refs, out_refs, big_send, big_recv, send_sems, recv_sems, local_sem):
        start_big, finish_big = sibling.copies(in_refs[:n], out_refs[:n], big_send, big_recv)
        start_small, finish_small = _small_copies(in_refs[n], out_refs[n], send_sems, recv_sems, local_sem)

        def start():
            start_big()
            start_small()

        def finish():
            finish_small()
            finish_big()

        return start, finish

    return _Hosted(pieces + [small], sibling.outputs + [jax.ShapeDtypeStruct((8,) + small.shape, small.dtype)],
                   sibling.sems + [pltpu.SemaphoreType.DMA((7,)), pltpu.SemaphoreType.DMA((7,)), pltpu.SemaphoreType.DMA(())],
                   copies)


def _share_halves(shards):
    n = len(shards)

    def body(*refs):
        full_refs = refs[n:2 * n]
        send_sems, recv_sems = refs[2 * n:]
        x, y, c = _pos()

        def copy(a, half):
            h = shards[a].shape[0] // 2
            rows = full_refs[a].at[pl.ds(pl.multiple_of(half * h, 8), h), :]
            return pltpu.make_async_remote_copy(src_ref=rows, dst_ref=rows, send_sem=send_sems.at[a],
                                                recv_sem=recv_sems.at[a], device_id=(x, y, 1 - c), device_id_type=_MESH)

        sends = [copy(a, c) for a in range(n)]
        for cp in sends:
            cp.start()
        for a in range(n):
            copy(a, 1 - c).wait_recv()
        for cp in sends:
            cp.wait_send()

    return _pcall(body, name="share_halves", in_specs=[_ANY] * n, out_specs=[_ANY] * n,
                  out_shape=[jax.ShapeDtypeStruct(s.shape, s.dtype) for s in shards],
                  input_output_aliases={a: a for a in range(n)},
                  scratch_shapes=[pltpu.SemaphoreType.DMA((n,)), pltpu.SemaphoreType.DMA((n,))],
                  compiler_params=pltpu.CompilerParams(has_side_effects=True))(*shards)


def _add_own_half(piece, got, core, name):
    _, r, cols = piece.shape
    h = r // 2

    def body(c_ref, a_ref, b_ref, o_ref, o16_ref):
        s = a_ref[0] + b_ref[...]
        o_ref[...] = s
        o16_ref[...] = s.astype(bf16)

    out = pl.BlockSpec((1, h, cols), lambda k, c: (k, 0, 0))
    grid_spec = pltpu.PrefetchScalarGridSpec(
        num_scalar_prefetch=1, grid=(4,),
        in_specs=[pl.BlockSpec((1, 1, h, cols), lambda k, c: (k, c[0], 0, 0)), out], out_specs=[out, out])
    return _pcall(body, name=name, grid_spec=grid_spec,
                  out_shape=[jax.ShapeDtypeStruct((4, h, cols), f32), jax.ShapeDtypeStruct((4, h, cols), bf16)],
                  compiler_params=_params(("arbitrary",)))(core, piece.reshape(4, 2, h, cols), got)


def _sum_chips(own, got, where, name):
    _, h, cols = own.shape
    t = _tile(h, 128, 16)
    nt = h // t

    def body(w_ref, own_ref, gx_ref, gy_ref, gxy_ref, o_ref):
        o_ref[...] = ((own_ref[0] + gx_ref[0].astype(f32)) + gy_ref[0].astype(f32)) + gxy_ref[0].astype(f32)

    slot = lambda j: pl.BlockSpec((1, t, cols), lambda i, w: (w[j], i, 0))
    grid_spec = pltpu.PrefetchScalarGridSpec(
        num_scalar_prefetch=1, grid=(nt,),
        in_specs=[slot(0), slot(2), slot(3), slot(4)],
        out_specs=pl.BlockSpec((t, cols), lambda i, w: (w[1] * nt + i, 0)))
    return _pcall(body, name=name, grid_spec=grid_spec, out_shape=jax.ShapeDtypeStruct((2 * h, cols), f32),
                  compiler_params=_params(("arbitrary",)))(where, own, got, got, got)


def _sum_slots(stack, name):
    k, n, cols = stack.shape
    t = _tile(n, 128, 8)

    def body(s_ref, o_ref):
        acc = s_ref[0]
        for i in range(1, k):
            acc = acc + s_ref[i]
        o_ref[...] = acc

    return _pcall(body, name=name, grid=(n // t,), in_specs=[pl.BlockSpec((k, t, cols), lambda i: (0, i, 0))],
                  out_specs=pl.BlockSpec((t, cols), lambda i: (i, 0)), out_shape=jax.ShapeDtypeStruct((n, cols), f32),
                  compiler_params=_params(("arbitrary",)))(stack)


def _adamw(w, g, m, v, name):
    n, cols = w.shape[-2:]
    t = _tile(n, 128, 8)
    c1 = 1.0 - ADAM_B1 ** ADAM_STEP
    c2 = 1.0 - ADAM_B2 ** ADAM_STEP
    at = (0,) if w.ndim == 3 else (Ellipsis,)

    def body(w_ref, g_ref, m_ref, v_ref, d_ref, nm_ref, nv_ref, g_out_ref):
        g = g_ref[...]
        nm = ADAM_B1 * m_ref[at] + (1.0 - ADAM_B1) * g
        nv = ADAM_B2 * v_ref[at] + (1.0 - ADAM_B2) * (g * g)
        d_ref[at] = -ADAM_LR * ((nm / c1) / (jnp.sqrt(nv / c2) + ADAM_EPS) + ADAM_WD * w_ref[at])
        nm_ref[at] = nm
        nv_ref[at] = nv
        g_out_ref[at] = g

    flat = pl.BlockSpec((t, cols), lambda i: (i, 0))
    spec = pl.BlockSpec((1, t, cols), lambda i: (0, i, 0)) if w.ndim == 3 else flat
    shp = jax.ShapeDtypeStruct(w.shape, f32)
    return _pcall(body, name=name, grid=(n // t,), in_specs=[spec, flat, spec, spec], out_specs=[spec] * 4,
                  out_shape=[shp] * 4, compiler_params=_params(("arbitrary",)))(w, g, m, v)


_MATS = (("w_in", 1), ("w_proj_sb", 1), ("w_proj_fox", 1), ("w_out", 0), ("w_up", 1), ("w_down", 0))
_SMALL = ("b_in", "ln1_g", "ln1_b", "b_conv", "ln2_g", "ln2_b")


def _pad_lanes(v):
    n = v.shape[-1]
    return jnp.pad(v, ((0, 0), (0, (-n) % LANES)))


def _pack_rows(vectors):
    flat = jnp.concatenate([_pad_lanes(v.reshape(1, -1)) for v in vectors], axis=1).reshape(-1, LANES)
    return jnp.pad(flat, ((0, (-flat.shape[0]) % 8), (0, 0)))


def _unpack_rows(packed, sizes):
    out, r = [], 0
    for n in sizes:
        rows = -(-n // LANES)
        out.append(packed[r:r + rows].reshape(1, rows * LANES)[:, :n])
        r += rows
    return out


def _unstack(stack, axis):
    if axis == 0:
        return stack.reshape(-1, stack.shape[2])
    return jnp.concatenate([stack[k] for k in range(4)], axis=1)


def _pieces(g, axis):
    if axis == 0:
        return g.reshape(4, g.shape[0] // 4, g.shape[1])
    cols = g.shape[1] // 4
    return jnp.stack([g[:, k * cols:(k + 1) * cols] for k in range(4)])


class _Reducer:
    def __init__(self, mats, core, where, small=None):
        self.mats, self.core, self.where, self.small = mats, core, where, small

    def to_sibling(self, grads, sq_err=None):
        self.local = [_pieces(grads[n], axis) for n, axis in self.mats]
        if self.small is None:
            return _hosted_sibling(self.local)
        return _hosted_sibling_and_small(self.local, self.small(grads, sq_err))

    def to_chips(self, got):
        if self.small is not None:
            got, self.small_all = got[:-1], got[-1]
        self.sums = [_add_own_half(p, g, self.core, "add_sibling_" + n) for (n, _), p, g in zip(self.mats, self.local, got)]
        return _hosted_chips([s16 for _, s16 in self.sums])

    def from_chips(self, got):
        self.halves = [_sum_chips(s32, r16, self.where, "sum_chips_" + n)
                       for (n, _), (s32, _), r16 in zip(self.mats, self.sums, got)]


def kernel(x, w_in, b_in, w_proj_sb, w_proj_fox, w_out, ln1_g, ln1_b, w_up, w_conv, b_conv, w_down, ln2_g, ln2_b, loss_target, m_w_in, m_b_in, m_w_proj_sb, m_w_proj_fox, m_w_out, m_ln1_g, m_ln1_b, m_w_up, m_w_conv, m_b_conv, m_w_down, m_ln2_g, m_ln2_b, v_w_in, v_b_in, v_w_proj_sb, v_w_proj_fox, v_w_out, v_ln1_g, v_ln1_b, v_w_up, v_w_conv, v_b_conv, v_w_down, v_ln2_g, v_ln2_b):
    w = dict(w_in=w_in, b_in=b_in, w_proj_sb=w_proj_sb, w_proj_fox=w_proj_fox, w_out=w_out, ln1_g=ln1_g, ln1_b=ln1_b,
             w_up=w_up, w_conv=w_conv, b_conv=b_conv, w_down=w_down, ln2_g=ln2_g, ln2_b=ln2_b)
    m = dict(w_in=m_w_in, b_in=m_b_in, w_proj_sb=m_w_proj_sb, w_proj_fox=m_w_proj_fox, w_out=m_w_out, ln1_g=m_ln1_g,
             ln1_b=m_ln1_b, w_up=m_w_up, w_conv=m_w_conv, b_conv=m_b_conv, w_down=m_w_down, ln2_g=m_ln2_g, ln2_b=m_ln2_b)
    v = dict(w_in=v_w_in, b_in=v_b_in, w_proj_sb=v_w_proj_sb, w_proj_fox=v_w_proj_fox, w_out=v_w_out, ln1_g=v_ln1_g,
             ln1_b=v_ln1_b, w_up=v_w_up, w_conv=v_w_conv, b_conv=v_b_conv, w_down=v_w_down, ln2_g=v_ln2_g, ln2_b=v_ln2_b)
    order = ["w_in", "b_in", "w_proj_sb", "w_proj_fox", "w_out", "ln1_g", "ln1_b", "w_up", "w_conv", "b_conv", "w_down",
             "ln2_g", "ln2_b"]
    x_idx, y_idx, c_idx = _pos()
    chip = 2 * x_idx + y_idx
    D = x.shape[-1]
    core = c_idx.astype(jnp.int32).reshape(1)

    w_in_own = w["w_in"][0].astype(bf16)
    w_in_stack = _gather_split(w_in_own)
    w_in_full = jnp.concatenate([jnp.where(chip == k, w_in_own, w_in_stack[k]) for k in range(4)], axis=1)
    late = (("w_proj_sb", 1), ("w_proj_fox", 1), ("w_out", 0), ("w_up", 1), ("w_conv", 1), ("w_down", 0))
    late_shards = [w[n][0] if n == "w_conv" else w[n][0].astype(bf16) for n, _ in late]
    late_weights = lambda stacks: [_unstack(s, axis) for (_, axis), s in zip(late, stacks)]

    where = jnp.stack([chip, c_idx, 2 * (1 - x_idx) + y_idx, 2 * x_idx + 1 - y_idx, 2 * (1 - x_idx) + 1 - y_idx]).astype(jnp.int32)
    small_names = list(_SMALL) + ["w_conv"]
    pack_small = lambda grads, sq_err: _pack_rows(
        [jnp.full((1, 1), (0.5 / D) * jnp.sum(sq_err), f32)] + [grads[n] for n in small_names])
    early = _Reducer(_MATS[1:], core, where)
    last = _Reducer(_MATS[:1], core, where, pack_small)
    sq_err, grad_x, grads = _local_step(x, loss_target, w_in_full, w["b_in"], w["ln1_g"], w["ln1_b"], w["b_conv"],
                                        w["ln2_g"], w["ln2_b"], late_shards, late_weights, early, last)
    g_shards = _share_halves(last.halves + early.halves)
    small_sum = _sum_slots(last.small_all, "sum_small")

    out = {"grad": {}, "delta": {}, "m": {}, "v": {}}
    for (n, _), g_ in zip(_MATS, g_shards):
        d_, m_, v_, g_out = _adamw(w[n], g_, m[n], v[n], "adamw_" + n)
        for key, t in (("grad", g_out), ("delta", d_), ("m", m_), ("v", v_)):
            out[key][n] = t.reshape(w[n].shape)
    sizes = [1] + [int(grads[n].size) for n in small_names]
    sm = _unpack_rows(small_sum, sizes)
    loss = sm[0][0, 0]
    g_small = dict(zip(small_names, sm[1:]))
    F4 = w["w_conv"].shape[-1]
    g_small["w_conv"] = lax.dynamic_slice_in_dim(g_small["w_conv"].reshape(3, -1), chip * F4, F4, axis=1)
    pack_s = lambda d: _pack_rows([d[n].reshape(1, -1) for n in small_names])
    gs_packed = _pack_rows([g_small[n].reshape(1, -1) for n in small_names])
    s_delta, s_m, s_v, _ = _adamw(pack_s(w), gs_packed, pack_s(m), pack_s(v), "adamw_small")
    s_sizes = [int(w[n].size) for n in small_names]

    for key, packed_s in (("grad", gs_packed), ("delta", s_delta), ("m", s_m), ("v", s_v)):
        for n, t in zip(small_names, _unpack_rows(packed_s, s_sizes)):
            out[key][n] = t.reshape(w[n].shape)
    return (loss, grad_x, *[out["grad"][n] for n in order], *[out["delta"][n] for n in order],
            *[out["m"][n] for n in order], *[out["v"][n] for n in order])
```

```python
import functools
import math

import jax
import jax.numpy as jnp
from jax import lax
from jax.experimental import pallas as pl
from jax.experimental.pallas import tpu as pltpu

f32, bf16 = jnp.float32, jnp.bfloat16

HEAD_DIM = 64
LANES = 128
LN_EPS = 1e-5
ALPHA = 2.0 ** 0.25
Q_SCALE = HEAD_DIM ** -0.5
ADAM_LR, ADAM_B1, ADAM_B2, ADAM_EPS, ADAM_WD, ADAM_STEP = 0.001, 0.9, 0.999, 1e-08, 0.01, 10
VMEM_LIMIT = 56 * 1024 * 1024
NEG = -1e30

_pcall = pl.pallas_call
_NT = (((1,), (1,)), ((), ()))
_TN = (((0,), (0,)), ((), ()))


def _params(sem=None):
    return pltpu.CompilerParams(dimension_semantics=sem, vmem_limit_bytes=VMEM_LIMIT)


def _tile(dim, target, unit=LANES):
    if dim <= target:
        return dim
    t = (target // unit) * unit
    while t > unit and dim % t:
        t -= unit
    assert dim % t == 0, (dim, target)
    return t


def _dot(a, b, dn=None):
    if dn is None:
        return jnp.dot(a, b, preferred_element_type=f32)
    return lax.dot_general(a, b, dn, preferred_element_type=f32)


def _split_dot(x, tri):
    hi = x.astype(bf16)
    lo = (x - hi.astype(f32)).astype(bf16)
    return _dot(hi, tri) + _dot(lo, tri)


SCAN_BLOCK = 256


def _scan_cols(x, tri, reverse, split=True):
    cb = tri.shape[0]
    nb = x.shape[1] // cb
    blocks = [x[:, b * cb:(b + 1) * cb] for b in range(nb)]
    outs, run = [None] * nb, None
    for b in (reversed(range(nb)) if reverse else range(nb)):
        o = _split_dot(blocks[b], tri) if split else _dot(blocks[b].astype(bf16), tri)
        s = jnp.sum(blocks[b], axis=1, keepdims=True)
        outs[b] = o if run is None else o + run
        run = s if run is None else run + s
    return (outs[0] if nb == 1 else jnp.concatenate(outs, axis=1)), run


def _tri(cb, rel):
    row = lax.broadcasted_iota(jnp.int32, (cb, cb), 0)
    col = lax.broadcasted_iota(jnp.int32, (cb, cb), 1)
    return rel(row, col).astype(bf16)


def _matmul(a, b, *, name, ta=False, tb=False, bias=None, addend=None, addend_scale=1.0, colsum=False,
            out_dtype=f32, tm=512, tn=512, tk=1024, hosted=None):
    M, K = (a.shape[1], a.shape[0]) if ta else a.shape
    N = b.shape[0] if tb else b.shape[1]
    assert K == (b.shape[1] if tb else b.shape[0])
    assert not (colsum and tb)
    tm, tn, tk = _tile(M, tm), _tile(N, tn), _tile(K, tk)
    nk = K // tk
    n_in = 2 + (bias is not None) + (addend is not None)
    n_out = 1 + colsum
    grid = (M // tm, N // tn, nk)

    def body(*refs):
        k = pl.program_id(2)
        step = (pl.program_id(0) * grid[1] + pl.program_id(1)) * nk + k
        scratch = refs[len(refs) - n_out - (len(hosted.sems) if hosted else 0):]
        own_in, own_out, finish = _host(hosted, refs[:len(refs) - len(scratch)] + scratch[n_out:], n_in, n_out, step,
                                        grid[0] * grid[1] * nk - 1)
        a_ref, b_ref = own_in[0], own_in[1]
        bias_ref = own_in[2] if bias is not None else None
        add_ref = own_in[n_in - 1] if addend is not None else None
        o_ref = own_out[0]
        cs_ref = own_out[1] if colsum else None
        acc = scratch[0]
        cs_acc = scratch[1] if colsum else None

        @pl.when(k == 0)
        def _():
            acc[...] = jnp.zeros_like(acc)
            if colsum:
                cs_acc[...] = jnp.zeros_like(cs_acc)

        dn = (((0 if ta else 1,), (1 if tb else 0,)), ((), ()))
        acc[...] += lax.dot_general(a_ref[...].astype(bf16), b_ref[...].astype(bf16), dn, preferred_element_type=f32)
        if colsum:
            cs_acc[...] += jnp.sum(b_ref[...].astype(f32), axis=0, keepdims=True)

        @pl.when(k == nk - 1)
        def _():
            r = acc[...]
            if bias is not None:
                r = r + bias_ref[...]
            if addend is not None:
                r = r + addend_scale * add_ref[...].astype(f32)
            o_ref[...] = r.astype(out_dtype)
            if colsum:
                cs_ref[0] = cs_acc[...]

        finish()

    a_spec = pl.BlockSpec((tk, tm), lambda i, j, k: (k, i)) if ta else pl.BlockSpec((tm, tk), lambda i, j, k: (i, k))
    b_spec = pl.BlockSpec((tn, tk), lambda i, j, k: (j, k)) if tb else pl.BlockSpec((tk, tn), lambda i, j, k: (k, j))
    in_specs, args = [a_spec, b_spec], [a, b]
    if bias is not None:
        in_specs.append(pl.BlockSpec((1, tn), lambda i, j, k: (0, j)))
        args.append(bias.reshape(1, N).astype(f32))
    if addend is not None:
        in_specs.append(pl.BlockSpec((tm, tn), lambda i, j, k: (i, j)))
        args.append(addend)
    out_shape = [jax.ShapeDtypeStruct((M, N), out_dtype)]
    out_specs = [pl.BlockSpec((tm, tn), lambda i, j, k: (i, j))]
    scratch = [pltpu.VMEM((tm, tn), f32)]
    if colsum:
        out_shape.append(jax.ShapeDtypeStruct((M // tm, 1, N), f32))
        out_specs.append(pl.BlockSpec((1, 1, tn), lambda i, j, k: (i, 0, j)))
        scratch.append(pltpu.VMEM((1, tn), f32))
    h_in = hosted.inputs if hosted else []
    h_out = hosted.outputs if hosted else []
    res = _pcall(body, name=name, grid=grid, in_specs=in_specs + [_ANY] * len(h_in),
                 out_specs=out_specs + [_ANY] * len(h_out), out_shape=out_shape + h_out,
                 scratch_shapes=scratch + (hosted.sems if hosted else []),
                 compiler_params=_params(("arbitrary", "arbitrary", "arbitrary")))(*args, *h_in)
    own = (res[0], res[1][0]) if colsum else res[0]
    return (own, list(res[n_out:])) if hosted else own


def _cumlogf(fl, B, S):
    t = _tile(S, 256, 8)

    def body(fl_ref, c_ref, carry):
        @pl.when(pl.program_id(1) == 0)
        def _():
            carry[...] = jnp.zeros_like(carry)
        z = fl_ref[...]
        ls = jnp.minimum(z, 0.0) - jnp.log(1.0 + jnp.exp(-jnp.abs(z)))
        row = lax.broadcasted_iota(jnp.int32, (t, t), 0)
        col = lax.broadcasted_iota(jnp.int32, (t, t), 1)
        lower = (col <= row).astype(f32)
        c = jnp.dot(lower, ls, precision=lax.Precision.HIGHEST, preferred_element_type=f32) + carry[...]
        c_ref[...] = c
        carry[...] = c[t - 1:t, :]

    return _pcall(body, name="cumlogf", grid=(B, S // t),
                  in_specs=[pl.BlockSpec((t, LANES), lambda b, i: (b * (S // t) + i, 0))],
                  out_specs=pl.BlockSpec((t, LANES), lambda b, i: (b * (S // t) + i, 0)),
                  out_shape=jax.ShapeDtypeStruct(fl.shape, f32), scratch_shapes=[pltpu.VMEM((1, LANES), f32)],
                  compiler_params=_params(("arbitrary", "arbitrary")))(fl)


def _cumlogf_bwd(dc, fl, B, S):
    t = _tile(S, 256, 8)
    n = S // t

    def body(dc_ref, fl_ref, o_ref, carry):
        @pl.when(pl.program_id(1) == 0)
        def _():
            carry[...] = jnp.zeros_like(carry)
        row = lax.broadcasted_iota(jnp.int32, (t, t), 0)
        col = lax.broadcasted_iota(jnp.int32, (t, t), 1)
        upper = (col >= row).astype(f32)
        r = jnp.dot(upper, dc_ref[...], precision=lax.Precision.HIGHEST, preferred_element_type=f32) + carry[...]
        carry[...] = r[0:1, :]
        z = fl_ref[...]
        o_ref[...] = r / (1.0 + jnp.exp(z))

    spec = pl.BlockSpec((t, LANES), lambda b, i: (b * n + n - 1 - i, 0))
    return _pcall(body, name="cumlogf_bwd", grid=(B, n), in_specs=[spec, spec], out_specs=spec,
                  out_shape=jax.ShapeDtypeStruct(fl.shape, f32), scratch_shapes=[pltpu.VMEM((1, LANES), f32)],
                  compiler_params=_params(("arbitrary", "arbitrary")))(dc, fl)


def _head_masks():
    lane = lax.broadcasted_iota(jnp.int32, (1, LANES), 1)
    return lane < HEAD_DIM


def _by_head(m0, t):
    z = jnp.zeros_like(t)
    return [jnp.where(m0, t, z), jnp.where(m0, z, t)]


def _sb_terms(z):
    softplus = jnp.maximum(z, 0.0) + jnp.log(1.0 + jnp.exp(-jnp.abs(z)))
    return z - softplus, -softplus


STRIP_ROWS = 32


def _strip_rows(tq):
    return STRIP_ROWS if tq % STRIP_ROWS == 0 else tq


def _strict(r, rs, tq):
    row = lax.broadcasted_iota(jnp.int32, (rs, tq), 0) + r
    col = lax.broadcasted_iota(jnp.int32, (rs, tq), 1)
    return col < row


def _score_scratch(tq, n_f32, n_bf16, n_sums):
    return ([pltpu.VMEM((tq, tq), f32)] * (2 * n_f32) + [pltpu.VMEM((tq, tq), bf16)] * (2 * n_bf16)
            + [pltpu.VMEM((tq, LANES), f32)] * (2 * n_sums))


def _by_pairs(refs):
    return [refs[i:i + 2] for i in range(0, len(refs), 2)]


def _sb_fwd(qkv, B, S, n_pairs, qcol, kcol, vcol, tq, shards=()):
    nq = S // tq
    T = B * S
    n = len(shards)

    cb = min(tq, SCAN_BLOCK)
    nb = tq // cb
    rs = _strip_rows(tq)

    def body(q_ref, k_ref, v_ref, *rest):
        o_ref, lt_ref = rest[n], rest[n + 1]
        z_s, suf_s, hi_s, lo_s, w_s, sum_s = _by_pairs(rest[len(rest) - 12:])
        i = pl.program_id(2)
        if n:
            start, finish = _gather_copies(rest[:n], rest[n + 2:2 * n + 2], *rest[2 * n + 2:2 * n + 5])
            step = (pl.program_id(0) * n_pairs + pl.program_id(1)) * nq + i
            pl.when(step == 0)(start)
        m0 = _head_masks()
        qh = _by_head(m0, q_ref[...])
        later = _tri(cb, lambda j, s: j > s)
        lane = lax.broadcasted_iota(jnp.int32, (1, LANES), 1)

        def tile(s0, R, acc, diag):
            kb = k_ref[pl.ds(s0, tq), :]
            vh = _by_head(m0, v_ref[pl.ds(s0, tq), :])
            R = list(R)
            for h in range(2):
                z_s[h][...] = _dot(qh[h], kb, _NT)
            for h in range(2):
                for r in range(0, tq, rs):
                    lb, l1m = _sb_terms(z_s[h][r:r + rs, :])
                    z_s[h][r:r + rs, :] = lb
                    if diag:
                        l1m = jnp.where(_strict(r, rs, tq), l1m, 0.0)
                    hi = l1m.astype(bf16)
                    hi_s[h][r:r + rs, :] = hi
                    lo_s[h][r:r + rs, :] = (l1m - hi.astype(f32)).astype(bf16)
                    sums = jnp.zeros((rs, LANES), f32)
                    for b in range(nb):
                        sums = jnp.where(lane == b, jnp.sum(l1m[:, b * cb:(b + 1) * cb], axis=1, keepdims=True), sums)
                    sum_s[h][r:r + rs, :] = sums
            for h in range(2):
                for b in range(nb):
                    blk = slice(b * cb, (b + 1) * cb)
                    suf_s[h][:, blk] = _dot(hi_s[h][:, blk], later) + _dot(lo_s[h][:, blk], later)
            for h in range(2):
                for r in range(0, tq, rs):
                    sums = sum_s[h][r:r + rs, :]
                    after = R[h][r:r + rs]
                    for b in reversed(range(nb)):
                        blk = slice(b * cb, (b + 1) * cb)
                        w = jnp.exp(z_s[h][r:r + rs, blk] + (suf_s[h][r:r + rs, blk] + after))
                        if diag:
                            w = jnp.where(_strict(r, rs, tq)[:, blk], w, 0.0)
                        w_s[h][r:r + rs, blk] = w.astype(bf16)
                        after = after + sums[:, b:b + 1]
            for h in range(2):
                acc = acc + _dot(w_s[h][...], vh[h])
                R[h] = R[h] + jnp.sum(sum_s[h][...], axis=1, keepdims=True)
            return R, acc

        zero = jnp.zeros((tq, 1), f32)
        R, acc = tile(pl.multiple_of(i * tq, tq), [zero, zero], jnp.zeros((tq, LANES), f32), True)

        def loop(n, carry):
            s0 = pl.multiple_of((i - 1 - n) * tq, tq)
            R, acc = tile(s0, carry[:2], carry[2], False)
            return R[0], R[1], acc

        R0, R1, acc = lax.fori_loop(0, i, loop, (R[0], R[1], acc))
        o_ref[...] = acc.astype(bf16)
        lt_ref[...] = jnp.where(m0, R0, R1)
        if n:
            pl.when(step == B * n_pairs * nq - 1)(finish)

    qs = lambda c: pl.BlockSpec((tq, LANES), lambda b, p, i: (b * nq + i, c + p))
    ks = lambda c: pl.BlockSpec((S, LANES), lambda b, p, i: (b, c + p))
    os_ = pl.BlockSpec((tq, LANES), lambda b, p, i: (b * nq + i, p))
    res = _pcall(body, name="sb_fwd", grid=(B, n_pairs, nq), in_specs=[qs(qcol), ks(kcol), ks(vcol)] + [_ANY] * n,
                 out_specs=[os_, os_] + [_ANY] * n,
                 out_shape=[jax.ShapeDtypeStruct((T, n_pairs * LANES), bf16), jax.ShapeDtypeStruct((T, n_pairs * LANES), f32)]
                 + _gather_shapes(shards), scratch_shapes=(_gather_sems(n) if n else []) + _score_scratch(tq, 2, 3, 1),
                 compiler_params=_params(("arbitrary", "arbitrary", "arbitrary")))(qkv, qkv, qkv, *shards)
    return res[0], res[1], res[2:]


def _sb_bwd(qkv, do, lt, B, S, n_pairs, qcol, kcol, vcol, tq, hosted=None):
    nq = S // tq
    T = B * S

    def body(*refs):
        i = pl.program_id(2)
        step = (pl.program_id(0) * n_pairs + pl.program_id(1)) * nq + i
        (q_ref, k_ref, v_ref, do_ref, lt_ref), (dq_ref, dk_ref, dv_ref), finish = _host(
            hosted, refs, 5, 3, step, B * n_pairs * nq - 1)

        @pl.when(i == 0)
        def _():
            dk_ref[...] = jnp.zeros_like(dk_ref)
            dv_ref[...] = jnp.zeros_like(dv_ref)

        m0 = _head_masks()
        qh = _by_head(m0, q_ref[...])
        doh = _by_head(m0, do_ref[...])
        lt = lt_ref[...]
        ltot = [lt[:, 0:1], lt[:, HEAD_DIM:HEAD_DIM + 1]]
        row = lax.broadcasted_iota(jnp.int32, (tq, tq), 0)
        col = lax.broadcasted_iota(jnp.int32, (tq, tq), 1)
        strict = col < row
        upto = _tri(min(tq, SCAN_BLOCK), lambda j, s: j <= s)
        before = _tri(min(tq, SCAN_BLOCK), lambda j, s: j < s)

        def tile(s0, CL, CP, dq, diag):
            kb = k_ref[pl.ds(s0, tq), :]
            vb = v_ref[pl.ds(s0, tq), :]
            kh = _by_head(m0, kb)
            CL, CP = list(CL), list(CP)
            dk = jnp.zeros((tq, LANES), f32)
            dv = jnp.zeros((tq, LANES), f32)
            for h in range(2):
                z = _dot(qh[h], kb, _NT)
                lb, l1m = _sb_terms(z)
                if diag:
                    l1m = jnp.where(strict, l1m, 0.0)
                pre, l_total = _scan_cols(l1m, upto, False)
                w = jnp.exp(lb + ((ltot[h] - CL[h]) - pre))
                if diag:
                    w = jnp.where(strict, w, 0.0)
                g = _dot(doh[h], vb, _NT) * w
                p, g_total = _scan_cols(g, before, False, split=False)
                dz = g - jnp.exp(lb) * (g + (p + CP[h]))
                if diag:
                    dz = jnp.where(strict, dz, 0.0)
                dzb = dz.astype(bf16)
                dq = dq + _dot(dzb, kh[h])
                dk = dk + _dot(dzb, qh[h], _TN)
                dv = dv + _dot(w.astype(bf16), doh[h], _TN)
                CL[h] = CL[h] + l_total
                CP[h] = CP[h] + g_total
            dk_ref[pl.ds(s0, tq), :] += dk
            dv_ref[pl.ds(s0, tq), :] += dv
            return CL, CP, dq

        zero = jnp.zeros((tq, 1), f32)

        def loop(n, carry):
            CL, CP, dq = tile(pl.multiple_of(n * tq, tq), carry[0:2], carry[2:4], carry[4], False)
            return CL[0], CL[1], CP[0], CP[1], dq

        c = lax.fori_loop(0, i, loop, (zero, zero, zero, zero, jnp.zeros((tq, LANES), f32)))
        _, _, dq = tile(pl.multiple_of(i * tq, tq), c[0:2], c[2:4], c[4], True)
        dq_ref[...] = dq * Q_SCALE
        finish()

    qs = lambda c: pl.BlockSpec((tq, LANES), lambda b, p, i: (b * nq + i, c + p))
    ks = lambda c: pl.BlockSpec((S, LANES), lambda b, p, i: (b, c + p))
    ts = pl.BlockSpec((tq, LANES), lambda b, p, i: (b * nq + i, p))
    fs = pl.BlockSpec((S, LANES), lambda b, p, i: (b, p))
    shp = jax.ShapeDtypeStruct((T, n_pairs * LANES), f32)
    return _host_call(body, "sb_bwd", (B, n_pairs, nq), [qs(qcol), ks(kcol), ks(vcol), ts, ts], [qkv, qkv, qkv, do, lt],
                      [ts, fs, fs], [shp, shp, shp], hosted)


def _fox_fwd(qkv, c, cT, B, S, n_pairs, qcol, kcol, vcol, tq):
    nq = S // tq
    T = B * S

    def body(q_ref, k_ref, v_ref, cq_ref, ck_ref, o_ref, o32_ref, lse_ref):
        p_idx = pl.program_id(1)
        i = pl.program_id(2)
        m0 = _head_masks()
        lane = lax.broadcasted_iota(jnp.int32, (1, LANES), 1)
        qh = _by_head(m0, q_ref[...])
        cq_all = cq_ref[...]
        cq = [jnp.sum(jnp.where(lane == 2 * p_idx + h, cq_all, 0.0), axis=1, keepdims=True) for h in range(2)]
        row = lax.broadcasted_iota(jnp.int32, (tq, tq), 0)
        col = lax.broadcasted_iota(jnp.int32, (tq, tq), 1)
        causal = col <= row

        def tile(s0, m, l, acc, diag):
            kb = k_ref[pl.ds(s0, tq), :]
            vh = _by_head(m0, v_ref[pl.ds(s0, tq), :])
            m, l = list(m), list(l)
            scale, add = [], []
            for h in range(2):
                z = _dot(qh[h], kb, _NT) + (cq[h] - ck_ref[h, :, pl.ds(s0, tq)])
                if diag:
                    z = jnp.where(causal, z, NEG)
                m_new = jnp.maximum(m[h], jnp.max(z, axis=1, keepdims=True))
                p = jnp.exp(z - m_new)
                a = jnp.exp(m[h] - m_new)
                l[h] = a * l[h] + jnp.sum(p, axis=1, keepdims=True)
                m[h] = m_new
                scale.append(a)
                add.append(_dot(p.astype(bf16), vh[h]))
            acc = acc * jnp.where(m0, scale[0], scale[1]) + add[0] + add[1]
            return m, l, acc

        neg = jnp.full((tq, 1), NEG, f32)
        zero = jnp.zeros((tq, 1), f32)
        m, l, acc = tile(pl.multiple_of(i * tq, tq), [neg, neg], [zero, zero], jnp.zeros((tq, LANES), f32), True)

        def loop(n, carry):
            m, l, acc = tile(pl.multiple_of(n * tq, tq), carry[0:2], carry[2:4], carry[4], False)
            return m[0], m[1], l[0], l[1], acc

        m0_, m1_, l0, l1, acc = lax.fori_loop(0, i, loop, (m[0], m[1], l[0], l[1], acc))
        o = acc * jnp.where(m0, 1.0 / l0, 1.0 / l1)
        o_ref[...] = o.astype(bf16)
        o32_ref[...] = o
        lse_ref[...] = jnp.where(m0, m0_ + jnp.log(l0), m1_ + jnp.log(l1))

    qs = lambda cc: pl.BlockSpec((tq, LANES), lambda b, p, i: (b * nq + i, cc + p))
    ks = lambda cc: pl.BlockSpec((S, LANES), lambda b, p, i: (b, cc + p))
    cqs = pl.BlockSpec((tq, LANES), lambda b, p, i: (b * nq + i, 0))
    cks = pl.BlockSpec((2, 1, S), lambda b, p, i: (b * n_pairs + p, 0, 0))
    os_ = pl.BlockSpec((tq, LANES), lambda b, p, i: (b * nq + i, p))
    shp = jax.ShapeDtypeStruct((T, n_pairs * LANES), f32)
    return _pcall(body, name="fox_fwd", grid=(B, n_pairs, nq), in_specs=[qs(qcol), ks(kcol), ks(vcol), cqs, cks],
                  out_specs=[os_, os_, os_], out_shape=[jax.ShapeDtypeStruct((T, n_pairs * LANES), bf16), shp, shp],
                  compiler_params=_params(("arbitrary", "arbitrary", "arbitrary")))(qkv, qkv, qkv, c, cT)


def _fox_bwd(qkv, c, cT, do, o, lse, B, S, n_pairs, qcol, kcol, vcol, tq, hosted=None):
    nq = S // tq
    T = B * S

    def body(*refs):
        p_idx = pl.program_id(1)
        i = pl.program_id(2)
        step = (pl.program_id(0) * n_pairs + p_idx) * nq + i
        (q_ref, k_ref, v_ref, cq_ref, ck_ref, do_ref, o_ref, lse_ref), (dq_ref, dk_ref, dv_ref, dc_ref), finish = _host(
            hosted, refs, 8, 4, step, B * n_pairs * nq - 1)

        @pl.when(i == 0)
        def _():
            dk_ref[...] = jnp.zeros_like(dk_ref)
            dv_ref[...] = jnp.zeros_like(dv_ref)
            dc_ref[...] = jnp.zeros_like(dc_ref)

        m0 = _head_masks()
        lane = lax.broadcasted_iota(jnp.int32, (1, LANES), 1)
        qh = _by_head(m0, q_ref[...])
        do2 = do_ref[...]
        doh = _by_head(m0, do2)
        prod = do2.astype(f32) * o_ref[...].astype(f32)
        delta = [jnp.sum(p, axis=1, keepdims=True) for p in _by_head(m0, prod)]
        ls = lse_ref[...]
        lse = [ls[:, 0:1], ls[:, HEAD_DIM:HEAD_DIM + 1]]
        cq_all = cq_ref[...]
        cq = [jnp.sum(jnp.where(lane == 2 * p_idx + h, cq_all, 0.0), axis=1, keepdims=True) for h in range(2)]
        tk = tq // 2 if tq % 512 == 0 else tq
        row = lax.broadcasted_iota(jnp.int32, (tq, tk), 0)
        col = lax.broadcasted_iota(jnp.int32, (tq, tk), 1)

        def tile(s0, dq, offset):
            kb = k_ref[pl.ds(s0, tk), :]
            vb = v_ref[pl.ds(s0, tk), :]
            kh = _by_head(m0, kb)
            dk = jnp.zeros((tk, LANES), f32)
            dv = jnp.zeros((tk, LANES), f32)
            for h in range(2):
                z = _dot(qh[h], kb, _NT) + (cq[h] - ck_ref[h, :, pl.ds(s0, tk)])
                p = jnp.exp(z - lse[h])
                if offset is not None:
                    p = jnp.where(col + offset <= row, p, 0.0)
                ds = p * (_dot(doh[h], vb, _NT) - delta[h])
                dsb = ds.astype(bf16)
                dq = dq + _dot(dsb, kh[h])
                dk = dk + _dot(dsb, qh[h], _TN)
                dv = dv + _dot(p.astype(bf16), doh[h], _TN)
                dc_ref[h, :, pl.ds(s0, tk)] -= jnp.sum(ds, axis=0, keepdims=True)
            dk_ref[pl.ds(s0, tk), :] += dk
            dv_ref[pl.ds(s0, tk), :] += dv
            return dq

        dq = lax.fori_loop(0, i * (tq // tk), lambda n, dq: tile(pl.multiple_of(n * tk, tk), dq, None),
                           jnp.zeros((tq, LANES), f32))
        for sub in range(tq // tk):
            dq = tile(pl.multiple_of(i * tq + sub * tk, tk), dq, sub * tk)
        dq_ref[...] = dq * Q_SCALE
        finish()

    qs = lambda cc: pl.BlockSpec((tq, LANES), lambda b, p, i: (b * nq + i, cc + p))
    ks = lambda cc: pl.BlockSpec((S, LANES), lambda b, p, i: (b, cc + p))
    cqs = pl.BlockSpec((tq, LANES), lambda b, p, i: (b * nq + i, 0))
    cks = pl.BlockSpec((2, 1, S), lambda b, p, i: (b * n_pairs + p, 0, 0))
    ts = pl.BlockSpec((tq, LANES), lambda b, p, i: (b * nq + i, p))
    fs = pl.BlockSpec((S, LANES), lambda b, p, i: (b, p))
    shp = jax.ShapeDtypeStruct((T, n_pairs * LANES), f32)
    return _host_call(body, "fox_bwd", (B, n_pairs, nq), [qs(qcol), ks(kcol), ks(vcol), cqs, cks, ts, ts, ts],
                      [qkv, qkv, qkv, c, cT, do, o, lse], [ts, fs, fs, cks],
                      [shp, shp, shp, jax.ShapeDtypeStruct(cT.shape, f32)], hosted)


def _sigmoid(x):
    return 1.0 / (1.0 + jnp.exp(-x))


def _mix_fwd(o_sb, o_fx, g, x, wp_sb, wp_fx, w_out, ln_g, ln_b, tm):
    T, D = x.shape
    E = o_sb.shape[1]
    tm = _tile(T, tm, 8)

    def body(osb_ref, ofx_ref, gsb_ref, gfx_ref, x_ref, wsb_ref, wfx_ref, wo_ref, lg_ref, lb_ref,
             xhat_ref, rstd_ref, x1_ref, mg_ref):
        y_sb = _dot(osb_ref[...], wsb_ref[...])
        y_fx = _dot(ofx_ref[...], wfx_ref[...])
        merged = (_sigmoid(gsb_ref[...]) * y_sb + _sigmoid(gfx_ref[...]) * y_fx).astype(bf16)
        r = ALPHA * x_ref[...] + _dot(merged, wo_ref[...])
        mean = jnp.mean(r, axis=1, keepdims=True)
        cen = r - mean
        rstd = lax.rsqrt(jnp.mean(cen * cen, axis=1, keepdims=True) + LN_EPS)
        xhat = cen * rstd
        xhat_ref[...] = xhat
        rstd_ref[...] = rstd
        x1_ref[...] = (xhat * lg_ref[...] + lb_ref[...]).astype(bf16)
        mg_ref[...] = merged

    rows = lambda w, c=0: pl.BlockSpec((tm, w), lambda i: (i, c))
    full = lambda a: pl.BlockSpec(a.shape, lambda i: (0, 0))
    return _pcall(body, name="mix_fwd", grid=(T // tm,),
                  in_specs=[rows(E), rows(E), rows(D, 0), rows(D, 1), rows(D), full(wp_sb), full(wp_fx), full(w_out),
                            full(ln_g), full(ln_b)],
                  out_specs=[rows(D), rows(1), rows(D), rows(D)],
                  out_shape=[jax.ShapeDtypeStruct((T, D), f32), jax.ShapeDtypeStruct((T, 1), f32),
                             jax.ShapeDtypeStruct((T, D), bf16), jax.ShapeDtypeStruct((T, D), bf16)],
                  compiler_params=_params(("arbitrary",)))(o_sb, o_fx, g, g, x, wp_sb, wp_fx, w_out, ln_g, ln_b)


def _mix_bwd(dr1, o_sb, o_fx, g, wp_sb, wp_fx, w_out, tm):
    T, D = dr1.shape
    E = o_sb.shape[1]
    tm = _tile(T, tm, 8)

    def body(dr_ref, osb_ref, ofx_ref, gsb_ref, gfx_ref, wsb_ref, wfx_ref, wo_ref,
             dysb_ref, dyfx_ref, dg_ref, dosb_ref, dofx_ref, sumsb_ref, sumfx_ref):
        @pl.when(pl.program_id(0) == 0)
        def _():
            sumsb_ref[...] = jnp.zeros_like(sumsb_ref)
            sumfx_ref[...] = jnp.zeros_like(sumfx_ref)
        dm = _dot(dr_ref[...].astype(bf16), wo_ref[...], _NT)
        for half, (o_ref, g_ref, w_ref, dy_ref, do_ref, sum_ref) in enumerate((
                (osb_ref, gsb_ref, wsb_ref, dysb_ref, dosb_ref, sumsb_ref),
                (ofx_ref, gfx_ref, wfx_ref, dyfx_ref, dofx_ref, sumfx_ref))):
            y = _dot(o_ref[...], w_ref[...])
            s = _sigmoid(g_ref[...])
            dy = (dm * s).astype(bf16)
            dy_ref[...] = dy
            dg = dm * y * s * (1.0 - s)
            dg_ref[:, half * D:(half + 1) * D] = dg.astype(bf16)
            sum_ref[0:1, :] += jnp.sum(dg, axis=0, keepdims=True)
            do_ref[...] = _dot(dy, w_ref[...], _NT).astype(bf16)

    rows = lambda w, c=0: pl.BlockSpec((tm, w), lambda i: (i, c))
    full = lambda a: pl.BlockSpec(a.shape, lambda i: (0, 0))
    acc = pl.BlockSpec((8, D), lambda i: (0, 0))
    res = _pcall(body, name="mix_bwd", grid=(T // tm,),
                 in_specs=[rows(D), rows(E), rows(E), rows(D, 0), rows(D, 1), full(wp_sb), full(wp_fx), full(w_out)],
                 out_specs=[rows(D), rows(D), rows(2 * D), rows(E), rows(E), acc, acc],
                 out_shape=[jax.ShapeDtypeStruct((T, D), bf16)] * 2 + [jax.ShapeDtypeStruct((T, 2 * D), bf16)]
                 + [jax.ShapeDtypeStruct((T, E), bf16)] * 2 + [jax.ShapeDtypeStruct((8, D), f32)] * 2,
                 compiler_params=_params(("arbitrary",)))(dr1, o_sb, o_fx, g, g, wp_sb, wp_fx, w_out)
    return res


def _ln_bwd(dy_a, dy_b, scale_b, xhat, rstd, ln_g, tm):
    T, D = xhat.shape
    tm = _tile(T, tm, 8)

    def body(a_ref, b_ref, xh_ref, rs_ref, g_ref, dr_ref, st_ref):
        @pl.when(pl.program_id(0) == 0)
        def _():
            st_ref[...] = jnp.zeros_like(st_ref)
        dy = a_ref[...] + scale_b * b_ref[...]
        xh = xh_ref[...]
        dxh = dy * g_ref[...]
        m1 = jnp.mean(dxh, axis=1, keepdims=True)
        m2 = jnp.mean(dxh * xh, axis=1, keepdims=True)
        dr_ref[...] = rs_ref[...] * (dxh - m1 - xh * m2)
        st_ref[0:1, :] += jnp.sum(dy * xh, axis=0, keepdims=True)
        st_ref[1:2, :] += jnp.sum(dy, axis=0, keepdims=True)

    rows = lambda w: pl.BlockSpec((tm, w), lambda i: (i, 0))
    return _pcall(body, name="ln1_bwd", grid=(T // tm,),
                  in_specs=[rows(D), rows(D), rows(D), rows(1), pl.BlockSpec((1, D), lambda i: (0, 0))],
                  out_specs=[rows(D), pl.BlockSpec((8, D), lambda i: (0, 0))],
                  out_shape=[jax.ShapeDtypeStruct((T, D), f32), jax.ShapeDtypeStruct((8, D), f32)],
                  compiler_params=_params(("arbitrary",)))(dy_a, dy_b, xhat, rstd, ln_g)


_INV_SQRT2 = 1.0 / math.sqrt(2.0)
_INV_SQRT2PI = 1.0 / math.sqrt(2.0 * math.pi)


def _conv_rows(ref, r0, rc, first, wc, bc):
    cur = ref[pl.ds(r0, rc), :].astype(f32)
    prev = ref[pl.ds(pl.multiple_of(jnp.maximum(r0 - 16, 0), 16), 16), :].astype(f32)
    prev = jnp.where(first, jnp.zeros_like(prev), prev)
    rid = lax.broadcasted_iota(jnp.int32, (rc, LANES), 0)
    s1 = jnp.where(rid == 0, prev[15:16, :], pltpu.roll(cur, 1, 0))
    s2 = jnp.where(rid == 0, prev[14:15, :], jnp.where(rid == 1, prev[15:16, :], pltpu.roll(cur, 2, 0)))
    conv = bc + wc[0:1, :] * s2 + wc[1:2, :] * s1 + wc[2:3, :] * cur
    return conv, (s2, s1, cur)


def _glu_fwd(u, w_conv, b_conv, B, S, rc=512):
    F = u.shape[1] // 2
    nf = F // LANES
    rc = _tile(S, rc, 8)

    def body(ug_ref, uv_ref, wc_ref, bc_ref, a_ref):
        wc, bc = wc_ref[...], bc_ref[...]

        def chunk(n, _):
            r0 = pl.multiple_of(n * rc, rc)
            c, _taps = _conv_rows(ug_ref, r0, rc, n == 0, wc, bc)
            gelu = 0.5 * c * (1.0 + lax.erf(c * _INV_SQRT2))
            a_ref[pl.ds(r0, rc), :] = (gelu * uv_ref[pl.ds(r0, rc), :].astype(f32)).astype(bf16)
            return 0

        lax.fori_loop(0, S // rc, chunk, 0)

    return _pcall(body, name="glu_fwd", grid=(B, nf),
                  in_specs=[pl.BlockSpec((S, LANES), lambda b, j: (b, j)), pl.BlockSpec((S, LANES), lambda b, j: (b, nf + j)),
                            pl.BlockSpec((3, LANES), lambda b, j: (0, j)), pl.BlockSpec((1, LANES), lambda b, j: (0, j))],
                  out_specs=pl.BlockSpec((S, LANES), lambda b, j: (b, j)),
                  out_shape=jax.ShapeDtypeStruct((B * S, F), bf16),
                  compiler_params=_params(("arbitrary", "arbitrary")))(u, u, w_conv, b_conv)


def _glu_bwd(u, da, w_conv, b_conv, B, S, rc=512):
    F = u.shape[1] // 2
    nf = F // LANES
    rc = _tile(S, rc, 8)
    nc = S // rc

    def body(ug_ref, uv_ref, da_ref, wc_ref, bc_ref, dug_ref, duv_ref, gw_ref, gb_ref, dc_ref):
        wc, bc = wc_ref[...], bc_ref[...]

        def chunk(n, carry):
            gw0, gw1, gw2, gb = carry
            r0 = pl.multiple_of(n * rc, rc)
            c, (s2, s1, cur) = _conv_rows(ug_ref, r0, rc, n == 0, wc, bc)
            cdf = 0.5 * (1.0 + lax.erf(c * _INV_SQRT2))
            da = da_ref[pl.ds(r0, rc), :]
            duv_ref[pl.ds(r0, rc), :] = (da * (c * cdf)).astype(bf16)
            dc = da * uv_ref[pl.ds(r0, rc), :].astype(f32) * (cdf + c * (_INV_SQRT2PI * jnp.exp(-0.5 * c * c)))
            dc_ref[pl.ds(r0, rc), :] = dc
            red = lambda t: jnp.sum(t, axis=0, keepdims=True)
            return gw0 + red(dc * s2), gw1 + red(dc * s1), gw2 + red(dc * cur), gb + red(dc)

        z = jnp.zeros((1, LANES), f32)
        gw0, gw1, gw2, gb = lax.fori_loop(0, nc, chunk, (z, z, z, z))
        gw_ref[0, 0:1, :] = gw0
        gw_ref[0, 1:2, :] = gw1
        gw_ref[0, 2:3, :] = gw2
        gb_ref[0] = gb

        def chunk2(n, _):
            r0 = pl.multiple_of(n * rc, rc)
            cur = dc_ref[pl.ds(r0, rc), :]
            nxt = dc_ref[pl.ds(pl.multiple_of(jnp.minimum(r0 + rc, S - 8), 8), 8), :]
            nxt = jnp.where(n == nc - 1, jnp.zeros_like(nxt), nxt)
            rid = lax.broadcasted_iota(jnp.int32, (rc, LANES), 0)
            a1 = jnp.where(rid == rc - 1, nxt[0:1, :], pltpu.roll(cur, rc - 1, 0))
            a2 = jnp.where(rid == rc - 1, nxt[1:2, :], jnp.where(rid == rc - 2, nxt[0:1, :], pltpu.roll(cur, rc - 2, 0)))
            dug_ref[pl.ds(r0, rc), :] = (wc[2:3, :] * cur + wc[1:2, :] * a1 + wc[0:1, :] * a2).astype(bf16)
            return 0

        lax.fori_loop(0, nc, chunk2, 0)

    blk = lambda off: pl.BlockSpec((S, LANES), lambda b, j: (b, off + j))
    return _pcall(body, name="glu_bwd", grid=(B, nf),
                  in_specs=[blk(0), blk(nf), blk(0), pl.BlockSpec((3, LANES), lambda b, j: (0, j)),
                            pl.BlockSpec((1, LANES), lambda b, j: (0, j))],
                  out_specs=[blk(0), blk(0), pl.BlockSpec((1, 3, LANES), lambda b, j: (b, 0, j)),
                             pl.BlockSpec((1, 1, LANES), lambda b, j: (b, 0, j))],
                  out_shape=[jax.ShapeDtypeStruct((B * S, F), bf16), jax.ShapeDtypeStruct((B * S, F), bf16),
                             jax.ShapeDtypeStruct((B, 3, F), f32), jax.ShapeDtypeStruct((B, 1, F), f32)],
                  scratch_shapes=[pltpu.VMEM((S, LANES), f32)],
                  compiler_params=_params(("arbitrary", "arbitrary")))(u, u, da, w_conv, b_conv)


def _down_loss(a, w_down, xhat1, ln1_g, ln1_b, ln2_g, ln2_b, tgt, tm):
    T, D = xhat1.shape
    F = a.shape[1]
    tm = _tile(T, tm, 8)

    def body(a_ref, w_ref, xh_ref, g1_ref, b1_ref, g2_ref, b2_ref, t_ref, dr_ref, st_ref):
        @pl.when(pl.program_id(0) == 0)
        def _():
            st_ref[...] = jnp.zeros_like(st_ref)
        x1 = xh_ref[...] * g1_ref[...] + b1_ref[...]
        r = ALPHA * x1 + _dot(a_ref[...], w_ref[...])
        mean = jnp.mean(r, axis=1, keepdims=True)
        cen = r - mean
        rstd = lax.rsqrt(jnp.mean(cen * cen, axis=1, keepdims=True) + LN_EPS)
        xh = cen * rstd
        err = (xh * g2_ref[...] + b2_ref[...]) - t_ref[...]
        dy = err * (1.0 / D)
        dxh = dy * g2_ref[...]
        m1 = jnp.mean(dxh, axis=1, keepdims=True)
        m2 = jnp.mean(dxh * xh, axis=1, keepdims=True)
        dr_ref[...] = rstd * (dxh - m1 - xh * m2)
        st_ref[0:1, :] += jnp.sum(dy * xh, axis=0, keepdims=True)
        st_ref[1:2, :] += jnp.sum(dy, axis=0, keepdims=True)
        st_ref[2:3, :] += jnp.sum(err * err, axis=0, keepdims=True)

    rows = lambda w: pl.BlockSpec((tm, w), lambda i: (i, 0))
    vec = pl.BlockSpec((1, D), lambda i: (0, 0))
    return _pcall(body, name="down_loss", grid=(T // tm,),
                  in_specs=[rows(F), pl.BlockSpec((F, D), lambda i: (0, 0)), rows(D), vec, vec, vec, vec, rows(D)],
                  out_specs=[rows(D), pl.BlockSpec((8, D), lambda i: (0, 0))],
                  out_shape=[jax.ShapeDtypeStruct((T, D), f32), jax.ShapeDtypeStruct((8, D), f32)],
                  compiler_params=_params(("arbitrary",)))(a, w_down, xhat1, ln1_g, ln1_b, ln2_g, ln2_b, tgt)


def _local_step(x, tgt, w_in, b_in, ln1_g, ln1_b, b_conv, ln2_g, ln2_b, late_shards, late_weights, reducer=None,
                tail=None, tq=512, tm=512):
    B, S, D = x.shape
    T = B * S
    E = (w_in.shape[1] - 2 * D) * HEAD_DIM // (6 * HEAD_DIM + 1)
    n_pairs = E // LANES
    NH = E // HEAD_DIM
    x2 = x.reshape(T, D)
    tgt2 = tgt.reshape(T, D)
    tq = _tile(S, tq, 8)

    c_f, c_g = 6 * E, 6 * E + NH
    w_qkv = w_in[:, :c_f]
    qscale = jnp.concatenate([jnp.full((E,), Q_SCALE, f32), jnp.ones((2 * E,), f32)] * 2)
    w_qkv_s = (w_qkv.astype(f32) * qscale).astype(bf16)
    b_qkv_s = b_in[:, :c_f] * qscale
    w_f = jnp.pad(w_in[:, c_f:c_g], ((0, 0), (0, LANES - NH)))
    b_f = jnp.pad(b_in[:, c_f:c_g], ((0, 0), (0, LANES - NH)))
    w_g = w_in[:, c_g:]
    b_g = b_in[:, c_g:]

    xb = x2.astype(bf16)
    qkv = _matmul(xb, w_qkv_s, bias=b_qkv_s, out_dtype=bf16, tm=1024, tn=1536, name="proj_qkv")
    g = _matmul(xb, w_g, bias=b_g, tm=1024, tn=1024, name="proj_gate")
    fl = _matmul(xb, w_f, bias=b_f, tm=1024, name="proj_forget")
    c = _cumlogf(fl, B, S)
    cT = c.reshape(B, S, LANES)[:, :, :NH].transpose(0, 2, 1).reshape(B * NH, 1, S)
    P = n_pairs
    o_sb, lt_sb, stacks = _sb_fwd(qkv, B, S, P, 0, P, 2 * P, tq, late_shards)
    wp_sb, wp_fx, w_out, w_up, w_conv, w_down = late_weights(stacks)
    F = w_down.shape[0]
    o_fx, o_fx32, lse_fx = _fox_fwd(qkv, c, cT, B, S, P, 3 * P, 4 * P, 5 * P, tq)
    xhat1, rstd1, x1b, merged = _mix_fwd(o_sb, o_fx, g, x2, wp_sb, wp_fx, w_out, ln1_g, ln1_b, tm)
    u = _matmul(x1b, w_up, out_dtype=bf16, tm=1024, tn=1408, name="ffn_up")
    a = _glu_fwd(u, w_conv, b_conv, B, S)
    dr2, st2 = _down_loss(a, w_down, xhat1, ln1_g, ln1_b, ln2_g, ln2_b, tgt2, tm)

    grads = {}
    grads["ln2_g"], grads["ln2_b"] = st2[0:1], st2[1:2]
    sq_err = st2[2:3]
    da = _matmul(dr2, w_down, tb=True, tm=1024, tn=1408, name="ffn_da")
    grads["w_down"] = _matmul(a, dr2, ta=True, tm=1408, tn=1024, tk=1024, name="grad_w_down")
    du_g, du_v, gwc, gbc = _glu_bwd(u, da, w_conv, b_conv, B, S)
    grads["w_conv"] = jnp.sum(gwc, axis=0)
    grads["b_conv"] = jnp.sum(gbc, axis=0)
    grads["w_up"] = jnp.concatenate(
        [_matmul(x1b, du_g, ta=True, tm=512, tn=2816, tk=1024, name="grad_w_up_gate"),
         _matmul(x1b, du_v, ta=True, tm=512, tn=2816, tk=1024, name="grad_w_up_val")], axis=1)
    dx1 = _matmul(du_g, w_up[:, :F], tb=True, tm=1024, tn=1024, tk=F, name="ffn_dx_gate")
    dx1 = _matmul(du_v, w_up[:, F:], tb=True, addend=dx1, tm=1024, tn=1024, tk=F, name="ffn_dx_val")
    dr1, st1 = _ln_bwd(dx1, dr2, ALPHA, xhat1, rstd1, ln1_g, tm)
    grads["ln1_g"], grads["ln1_b"] = st1[0:1], st1[1:2]

    dy_sb, dy_fx, dg, do_sb, do_fx, gsum_sb, gsum_fx = _mix_bwd(dr1, o_sb, o_fx, g, wp_sb, wp_fx, w_out, tm)
    grads["w_out"] = _matmul(merged, dr1, ta=True, tm=512, tn=1024, tk=1024, name="grad_w_out")
    grads["w_proj_sb"] = _matmul(o_sb, dy_sb, ta=True, tm=512, tn=1024, tk=1024, name="grad_w_proj_sb")
    grads["w_proj_fox"] = _matmul(o_fx, dy_fx, ta=True, tm=512, tn=1024, tk=1024, name="grad_w_proj_fox")
    (dq_f, dk_f, dv_f, dcT), got = _fox_bwd(qkv, c, cT, do_fx, o_fx32, lse_fx, B, S, P, 3 * P, 4 * P, 5 * P, tq,
                                            reducer.to_sibling(grads) if reducer else None)
    (dq_s, dk_s, dv_s), got = _sb_bwd(qkv, do_sb, lt_sb, B, S, P, 0, P, 2 * P, tq, reducer.to_chips(got) if reducer else None)
    if reducer:
        reducer.from_chips(got)
    dc = jnp.pad(dcT.reshape(B, NH, S).transpose(0, 2, 1), ((0, 0), (0, 0), (0, LANES - NH))).reshape(T, LANES)
    dfl = _cumlogf_bwd(dc, fl, B, S)
    dqkv = jnp.concatenate([dq_s, dk_s, dv_s, dq_f, dk_f, dv_f], axis=1)
    gw_qkv, gb_qkv = _matmul(x2, dqkv, ta=True, colsum=True, tm=512, tn=3072, tk=512, name="grad_w_qkv")
    gw_f, gb_f = _matmul(x2, dfl, ta=True, colsum=True, tm=512, tk=1024, name="grad_w_forget")
    gw_g = _matmul(x2, dg, ta=True, tm=512, tn=2048, tk=1024, name="grad_w_gate")
    grads["w_in"] = jnp.concatenate([gw_qkv, gw_f[:, :NH], gw_g], axis=1)
    grads["b_in"] = jnp.concatenate([gb_qkv, gb_f[:, :NH], gsum_sb[0:1], gsum_fx[0:1]], axis=1)
    dx = _matmul(dfl, w_f, tb=True, addend=dr1, addend_scale=ALPHA, tm=1024, tn=1024, name="dx_forget",
                 hosted=tail.to_sibling(grads, sq_err) if tail else None)
    if tail:
        dx, got = dx
        to_chips = tail.to_chips(got)
    dx = _matmul(dqkv, w_qkv, tb=True, addend=dx, tm=512, tn=1024, tk=c_f, name="dx_qkv", hosted=to_chips if tail else None)
    if tail:
        dx, got = dx
        tail.from_chips(got)
    dx = _matmul(dg, w_g, tb=True, addend=dx, tm=1024, tn=1024, tk=2 * D, name="dx_gate")
    return sq_err, dx.reshape(B, S, D), grads


_ANY = pl.BlockSpec(memory_space=pl.ANY)
_MESH = pl.DeviceIdType.MESH


def _pos():
    return lax.axis_index("x"), lax.axis_index("y"), lax.axis_index("c")


def _other_chips(x, y):
    return [(1 - x, y), (x, 1 - y), (1 - x, 1 - y)]


def _gather_split(shard):
    h = shard.shape[0] // 2

    def body(src_ref, dst_ref, send_sems, recv_sems, pass_send, pass_recv):
        x, y, c = _pos()
        k = 2 * x + y
        chips = _other_chips(x, y)
        slots = [2 * chip[0] + chip[1] for chip in chips]

        def rows(ref, half):
            return ref.at[pl.ds(pl.multiple_of(half * h, 16), h), :]

        def fetch(j, slot):
            return pltpu.make_async_remote_copy(src_ref=rows(src_ref, c), dst_ref=rows(dst_ref.at[slot], c),
                                                send_sem=send_sems.at[j], recv_sem=recv_sems.at[j],
                                                device_id=(*chips[j], c), device_id_type=_MESH)

        def pass_on(j, half):
            blk = rows(dst_ref.at[slots[j]], half)
            return pltpu.make_async_remote_copy(src_ref=blk, dst_ref=blk, send_sem=pass_send.at[j], recv_sem=pass_recv.at[j],
                                                device_id=(x, y, 1 - c), device_id_type=_MESH)

        for j in range(3):
            fetch(j, k).start()
        for j in range(3):
            fetch(j, slots[j]).wait_recv()
            pass_on(j, c).start()
        for j in range(3):
            pass_on(j, 1 - c).wait_recv()
        for j in range(3):
            fetch(j, k).wait_send()
            pass_on(j, c).wait_send()

    sems = pltpu.SemaphoreType.DMA((3,))
    return _pcall(body, name="gather_w_in", in_specs=[_ANY], out_specs=_ANY,
                  out_shape=jax.ShapeDtypeStruct((4,) + shard.shape, shard.dtype), scratch_shapes=[sems] * 4,
                  compiler_params=pltpu.CompilerParams(has_side_effects=True))(shard)


def _gather_shapes(shards):
    return [jax.ShapeDtypeStruct((4,) + s.shape, s.dtype) for s in shards]


def _gather_sems(n):
    return [pltpu.SemaphoreType.DMA((n, 3)), pltpu.SemaphoreType.DMA((n, 3)), pltpu.SemaphoreType.DMA((n,))]


def _gather_copies(srcs, dsts, send_sems, recv_sems, local_sems):
    n = len(srcs)
    x, y, c = _pos()
    k = 2 * x + y
    chips = _other_chips(x, y)

    def copy(a, j, chip, slot):
        return pltpu.make_async_remote_copy(src_ref=srcs[a], dst_ref=dsts[a].at[slot], send_sem=send_sems.at[a, j],
                                            recv_sem=recv_sems.at[a, j], device_id=(*chip, c), device_id_type=_MESH)

    def mine():
        return ([pltpu.make_async_copy(srcs[a], dsts[a].at[k], local_sems.at[a]) for a in range(n)],
                [copy(a, j, chip, k) for a in range(n) for j, chip in enumerate(chips)])

    def start():
        local, sends = mine()
        for cp in local + sends:
            cp.start()

    def finish():
        local, sends = mine()
        for a in range(n):
            for j, chip in enumerate(chips):
                copy(a, j, chip, 2 * chip[0] + chip[1]).wait_recv()
        for cp in sends:
            cp.wait_send()
        for cp in local:
            cp.wait()

    return start, finish


class _Hosted:
    def __init__(self, inputs, outputs, sems, copies):
        self.inputs, self.outputs, self.sems, self.copies = list(inputs), list(outputs), list(sems), copies


def _host(hosted, refs, n_in, n_out, step, last):
    if hosted is None:
        return refs[:n_in], refs[n_in:n_in + n_out], lambda: None
    hi, ho = len(hosted.inputs), len(hosted.outputs)
    own_in, h_in = refs[:n_in], refs[n_in:n_in + hi]
    own_out = refs[n_in + hi:n_in + hi + n_out]
    h_out = refs[n_in + hi + n_out:n_in + hi + n_out + ho]
    start, finish = hosted.copies(h_in, h_out, *refs[n_in + hi + n_out + ho:])
    pl.when(step == 0)(start)
    return own_in, own_out, lambda: pl.when(step == last)(finish)


def _host_call(body, name, grid, in_specs, args, out_specs, out_shape, hosted):
    h_in = hosted.inputs if hosted else []
    h_out = hosted.outputs if hosted else []
    res = _pcall(body, name=name, grid=grid, in_specs=list(in_specs) + [_ANY] * len(h_in),
                 out_specs=list(out_specs) + [_ANY] * len(h_out), out_shape=list(out_shape) + list(h_out),
                 scratch_shapes=hosted.sems if hosted else [],
                 compiler_params=_params(("arbitrary",) * len(grid)))(*args, *h_in)
    return list(res[:len(out_shape)]), list(res[len(out_shape):])


def _sibling_copies(shapes):
    n = len(shapes)

    def copies(p_refs, got_refs, send_sems, recv_sems):
        x, y, c = _pos()

        def copy(a):
            h = shapes[a][1] // 2
            src = p_refs[a].at[:, pl.ds(pl.multiple_of((1 - c) * h, 8), h), :]
            return pltpu.make_async_remote_copy(src_ref=src, dst_ref=got_refs[a], send_sem=send_sems.at[a],
                                                recv_sem=recv_sems.at[a], device_id=(x, y, 1 - c), device_id_type=_MESH)

        def start():
            for a in range(n):
                copy(a).start()

        def finish():
            for a in range(n):
                copy(a).wait()

        return start, finish

    return copies


def _hosted_sibling(pieces):
    n = len(pieces)
    return _Hosted(pieces, [jax.ShapeDtypeStruct((4, p.shape[1] // 2, p.shape[2]), p.dtype) for p in pieces],
                   [pltpu.SemaphoreType.DMA((n,)), pltpu.SemaphoreType.DMA((n,))], _sibling_copies([p.shape for p in pieces]))


def _chips_copies(n):
    def copies(p_refs, got_refs, send_sems, recv_sems):
        x, y, c = _pos()
        k = 2 * x + y
        chips = _other_chips(x, y)

        def copy(a, j, chip, piece, slot):
            return pltpu.make_async_remote_copy(src_ref=p_refs[a].at[piece], dst_ref=got_refs[a].at[slot],
                                                send_sem=send_sems.at[a, j], recv_sem=recv_sems.at[a, j],
                                                device_id=(*chip, c), device_id_type=_MESH)

        def start():
            for a in range(n):
                for j, chip in enumerate(chips):
                    copy(a, j, chip, 2 * chip[0] + chip[1], k).start()

        def finish():
            for a in range(n):
                for j, chip in enumerate(chips):
                    copy(a, j, chip, k, 2 * chip[0] + chip[1]).wait_recv()
            for a in range(n):
                for j, chip in enumerate(chips):
                    copy(a, j, chip, 2 * chip[0] + chip[1], k).wait_send()

        return start, finish

    return copies


def _hosted_chips(pieces):
    n = len(pieces)
    return _Hosted(pieces, [jax.ShapeDtypeStruct(p.shape, p.dtype) for p in pieces],
                   [pltpu.SemaphoreType.DMA((n, 3)), pltpu.SemaphoreType.DMA((n, 3))], _chips_copies(n))


def _small_copies(sm_ref, sg_ref, send_sems, recv_sems, local_sem):
    x, y, c = _pos()
    me = 4 * x + 2 * y + c
    flip = lambda v, bit: 1 - v if bit else v
    peers = [(flip(x, r & 4), flip(y, r & 2), flip(c, r & 1)) for r in range(1, 8)]

    def copy(j, slot):
        return pltpu.make_async_remote_copy(src_ref=sm_ref, dst_ref=sg_ref.at[slot], send_sem=send_sems.at[j],
                                            recv_sem=recv_sems.at[j], device_id=peers[j], device_id_type=_MESH)

    def local():
        return pltpu.make_async_copy(sm_ref, sg_ref.at[me], local_sem)

    def start():
        local().start()
        for j in range(7):
            copy(j, me).start()

    def finish():
        for j, (px, py, pc) in enumerate(peers):
            copy(j, 4 * px + 2 * py + pc).wait_recv()
        for j in range(7):
            copy(j, me).wait_send()
        local().wait()

    return start, finish


def _hosted_sibling_and_small(pieces, small):
    n = len(pieces)
    sibling = _hosted_sibling(pieces)

    def copies(in_refs, out_refs, big_send, big_recv, send_sems, recv_sems, local_sem):
        start_big, finish_big = sibling.copies(in_refs[:n], out_refs[:n], big_send, big_recv)
        start_small, finish_small = _small_copies(in_refs[n], out_refs[n], send_sems, recv_sems, local_sem)

        def start():
            start_big()
            start_small()

        def finish():
            finish_small()
            finish_big()

        return start, finish

    return _Hosted(pieces + [small], sibling.outputs + [jax.ShapeDtypeStruct((8,) + small.shape, small.dtype)],
                   sibling.sems + [pltpu.SemaphoreType.DMA((7,)), pltpu.SemaphoreType.DMA((7,)), pltpu.SemaphoreType.DMA(())],
                   copies)


def _share_halves(shards):
    n = len(shards)

    def body(*refs):
        full_refs = refs[n:2 * n]
        send_sems, recv_sems = refs[2 * n:]
        x, y, c = _pos()

        def copy(a, half):
            h = shards[a].shape[0] // 2
            rows = full_refs[a].at[pl.ds(pl.multiple_of(half * h, 8), h), :]
            return pltpu.make_async_remote_copy(src_ref=rows, dst_ref=rows, send_sem=send_sems.at[a],
                                                recv_sem=recv_sems.at[a], device_id=(x, y, 1 - c), device_id_type=_MESH)

        sends = [copy(a, c) for a in range(n)]
        for cp in sends:
            cp.start()
        for a in range(n):
            copy(a, 1 - c).wait_recv()
        for cp in sends:
            cp.wait_send()

    return _pcall(body, name="share_halves", in_specs=[_ANY] * n, out_specs=[_ANY] * n,
                  out_shape=[jax.ShapeDtypeStruct(s.shape, s.dtype) for s in shards],
                  input_output_aliases={a: a for a in range(n)},
                  scratch_shapes=[pltpu.SemaphoreType.DMA((n,)), pltpu.SemaphoreType.DMA((n,))],
                  compiler_params=pltpu.CompilerParams(has_side_effects=True))(*shards)


def _add_own_half(piece, got, core, name):
    _, r, cols = piece.shape
    h = r // 2

    def body(c_ref, a_ref, b_ref, o_ref, o16_ref):
        s = a_ref[0] + b_ref[...]
        o_ref[...] = s
        o16_ref[...] = s.astype(bf16)

    out = pl.BlockSpec((1, h, cols), lambda k, c: (k, 0, 0))
    grid_spec = pltpu.PrefetchScalarGridSpec(
        num_scalar_prefetch=1, grid=(4,),
        in_specs=[pl.BlockSpec((1, 1, h, cols), lambda k, c: (k, c[0], 0, 0)), out], out_specs=[out, out])
    return _pcall(body, name=name, grid_spec=grid_spec,
                  out_shape=[jax.ShapeDtypeStruct((4, h, cols), f32), jax.ShapeDtypeStruct((4, h, cols), bf16)],
                  compiler_params=_params(("arbitrary",)))(core, piece.reshape(4, 2, h, cols), got)


def _sum_chips(own, got, where, name):
    _, h, cols = own.shape
    t = _tile(h, 128, 16)
    nt = h // t

    def body(w_ref, own_ref, gx_ref, gy_ref, gxy_ref, o_ref):
        o_ref[...] = ((own_ref[0] + gx_ref[0].astype(f32)) + gy_ref[0].astype(f32)) + gxy_ref[0].astype(f32)

    slot = lambda j: pl.BlockSpec((1, t, cols), lambda i, w: (w[j], i, 0))
    grid_spec = pltpu.PrefetchScalarGridSpec(
        num_scalar_prefetch=1, grid=(nt,),
        in_specs=[slot(0), slot(2), slot(3), slot(4)],
        out_specs=pl.BlockSpec((t, cols), lambda i, w: (w[1] * nt + i, 0)))
    return _pcall(body, name=name, grid_spec=grid_spec, out_shape=jax.ShapeDtypeStruct((2 * h, cols), f32),
                  compiler_params=_params(("arbitrary",)))(where, own, got, got, got)


def _sum_slots(stack, name):
    k, n, cols = stack.shape
    t = _tile(n, 128, 8)

    def body(s_ref, o_ref):
        acc = s_ref[0]
        for i in range(1, k):
            acc = acc + s_ref[i]
        o_ref[...] = acc

    return _pcall(body, name=name, grid=(n // t,), in_specs=[pl.BlockSpec((k, t, cols), lambda i: (0, i, 0))],
                  out_specs=pl.BlockSpec((t, cols), lambda i: (i, 0)), out_shape=jax.ShapeDtypeStruct((n, cols), f32),
                  compiler_params=_params(("arbitrary",)))(stack)


def _adamw(w, g, m, v, name):
    n, cols = w.shape[-2:]
    t = _tile(n, 128, 8)
    c1 = 1.0 - ADAM_B1 ** ADAM_STEP
    c2 = 1.0 - ADAM_B2 ** ADAM_STEP
    at = (0,) if w.ndim == 3 else (Ellipsis,)

    def body(w_ref, g_ref, m_ref, v_ref, d_ref, nm_ref, nv_ref, g_out_ref):
        g = g_ref[...]
        nm = ADAM_B1 * m_ref[at] + (1.0 - ADAM_B1) * g
        nv = ADAM_B2 * v_ref[at] + (1.0 - ADAM_B2) * (g * g)
        d_ref[at] = -ADAM_LR * ((nm / c1) / (jnp.sqrt(nv / c2) + ADAM_EPS) + ADAM_WD * w_ref[at])
        nm_ref[at] = nm
        nv_ref[at] = nv
        g_out_ref[at] = g

    flat = pl.BlockSpec((t, cols), lambda i: (i, 0))
    spec = pl.BlockSpec((1, t, cols), lambda i: (0, i, 0)) if w.ndim == 3 else flat
    shp = jax.ShapeDtypeStruct(w.shape, f32)
    return _pcall(body, name=name, grid=(n // t,), in_specs=[spec, flat, spec, spec], out_specs=[spec] * 4,
                  out_shape=[shp] * 4, compiler_params=_params(("arbitrary",)))(w, g, m, v)


_MATS = (("w_in", 1), ("w_proj_sb", 1), ("w_proj_fox", 1), ("w_out", 0), ("w_up", 1), ("w_down", 0))
_SMALL = ("b_in", "ln1_g", "ln1_b", "b_conv", "ln2_g", "ln2_b")


def _pad_lanes(v):
    n = v.shape[-1]
    return jnp.pad(v, ((0, 0), (0, (-n) % LANES)))


def _pack_rows(vectors):
    flat = jnp.concatenate([_pad_lanes(v.reshape(1, -1)) for v in vectors], axis=1).reshape(-1, LANES)
    return jnp.pad(flat, ((0, (-flat.shape[0]) % 8), (0, 0)))


def _unpack_rows(packed, sizes):
    out, r = [], 0
    for n in sizes:
        rows = -(-n // LANES)
        out.append(packed[r:r + rows].reshape(1, rows * LANES)[:, :n])
        r += rows
    return out


def _unstack(stack, axis):
    if axis == 0:
        return stack.reshape(-1, stack.shape[2])
    return jnp.concatenate([stack[k] for k in range(4)], axis=1)


def _pieces(g, axis):
    if axis == 0:
        return g.reshape(4, g.shape[0] // 4, g.shape[1])
    cols = g.shape[1] // 4
    return jnp.stack([g[:, k * cols:(k + 1) * cols] for k in range(4)])


class _Reducer:
    def __init__(self, mats, core, where, small=None):
        self.mats, self.core, self.where, self.small = mats, core, where, small

    def to_sibling(self, grads, sq_err=None):
        self.local = [_pieces(grads[n], axis) for n, axis in self.mats]
        if self.small is None:
            return _hosted_sibling(self.local)
        return _hosted_sibling_and_small(self.local, self.small(grads, sq_err))

    def to_chips(self, got):
        if self.small is not None:
            got, self.small_all = got[:-1], got[-1]
        self.sums = [_add_own_half(p, g, self.core, "add_sibling_" + n) for (n, _), p, g in zip(self.mats, self.local, got)]
        return _hosted_chips([s16 for _, s16 in self.sums])

    def from_chips(self, got):
        self.halves = [_sum_chips(s32, r16, self.where, "sum_chips_" + n)
                       for (n, _), (s32, _), r16 in zip(self.mats, self.sums, got)]


def kernel(x, w_in, b_in, w_proj_sb, w_proj_fox, w_out, ln1_g, ln1_b, w_up, w_conv, b_conv, w_down, ln2_g, ln2_b, loss_target, m_w_in, m_b_in, m_w_proj_sb, m_w_proj_fox, m_w_out, m_ln1_g, m_ln1_b, m_w_up, m_w_conv, m_b_conv, m_w_down, m_ln2_g, m_ln2_b, v_w_in, v_b_in, v_w_proj_sb, v_w_proj_fox, v_w_out, v_ln1_g, v_ln1_b, v_w_up, v_w_conv, v_b_conv, v_w_down, v_ln2_g, v_ln2_b):
    w = dict(w_in=w_in, b_in=b_in, w_proj_sb=w_proj_sb, w_proj_fox=w_proj_fox, w_out=w_out, ln1_g=ln1_g, ln1_b=ln1_b,
             w_up=w_up, w_conv=w_conv, b_conv=b_conv, w_down=w_down, ln2_g=ln2_g, ln2_b=ln2_b)
    m = dict(w_in=m_w_in, b_in=m_b_in, w_proj_sb=m_w_proj_sb, w_proj_fox=m_w_proj_fox, w_out=m_w_out, ln1_g=m_ln1_g,
             ln1_b=m_ln1_b, w_up=m_w_up, w_conv=m_w_conv, b_conv=m_b_conv, w_down=m_w_down, ln2_g=m_ln2_g, ln2_b=m_ln2_b)
    v = dict(w_in=v_w_in, b_in=v_b_in, w_proj_sb=v_w_proj_sb, w_proj_fox=v_w_proj_fox, w_out=v_w_out, ln1_g=v_ln1_g,
             ln1_b=v_ln1_b, w_up=v_w_up, w_conv=v_w_conv, b_conv=v_b_conv, w_down=v_w_down, ln2_g=v_ln2_g, ln2_b=v_ln2_b)
    order = ["w_in", "b_in", "w_proj_sb", "w_proj_fox", "w_out", "ln1_g", "ln1_b", "w_up", "w_conv", "b_conv", "w_down",
             "ln2_g", "ln2_b"]
    x_idx, y_idx, c_idx = _pos()
    chip = 2 * x_idx + y_idx
    D = x.shape[-1]
    core = c_idx.astype(jnp.int32).reshape(1)

    w_in_own = w["w_in"][0].astype(bf16)
    w_in_stack = _gather_split(w_in_own)
    w_in_full = jnp.concatenate([jnp.where(chip == k, w_in_own, w_in_stack[k]) for k in range(4)], axis=1)
    late = (("w_proj_sb", 1), ("w_proj_fox", 1), ("w_out", 0), ("w_up", 1), ("w_conv", 1), ("w_down", 0))
    late_shards = [w[n][0] if n == "w_conv" else w[n][0].astype(bf16) for n, _ in late]
    late_weights = lambda stacks: [_unstack(s, axis) for (_, axis), s in zip(late, stacks)]

    where = jnp.stack([chip, c_idx, 2 * (1 - x_idx) + y_idx, 2 * x_idx + 1 - y_idx, 2 * (1 - x_idx) + 1 - y_idx]).astype(jnp.int32)
    small_names = list(_SMALL) + ["w_conv"]
    pack_small = lambda grads, sq_err: _pack_rows(
        [jnp.full((1, 1), (0.5 / D) * jnp.sum(sq_err), f32)] + [grads[n] for n in small_names])
    early = _Reducer(_MATS[1:], core, where)
    last = _Reducer(_MATS[:1], core, where, pack_small)
    sq_err, grad_x, grads = _local_step(x, loss_target, w_in_full, w["b_in"], w["ln1_g"], w["ln1_b"], w["b_conv"],
                                        w["ln2_g"], w["ln2_b"], late_shards, late_weights, early, last)
    g_shards = _share_halves(last.halves + early.halves)
    small_sum = _sum_slots(last.small_all, "sum_small")

    out = {"grad": {}, "delta": {}, "m": {}, "v": {}}
    for (n, _), g_ in zip(_MATS, g_shards):
        d_, m_, v_, g_out = _adamw(w[n], g_, m[n], v[n], "adamw_" + n)
        for key, t in (("grad", g_out), ("delta", d_), ("m", m_), ("v", v_)):
            out[key][n] = t.reshape(w[n].shape)
    sizes = [1] + [int(grads[n].size) for n in small_names]
    sm = _unpack_rows(small_sum, sizes)
    loss = sm[0][0, 0]
    g_small = dict(zip(small_names, sm[1:]))
    F4 = w["w_conv"].shape[-1]
    g_small["w_conv"] = lax.dynamic_slice_in_dim(g_small["w_conv"].reshape(3, -1), chip * F4, F4, axis=1)
    pack_s = lambda d: _pack_rows([d[n].reshape(1, -1) for n in small_names])
    gs_packed = _pack_rows([g_small[n].reshape(1, -1) for n in small_names])
    s_delta, s_m, s_v, _ = _adamw(pack_s(w), gs_packed, pack_s(m), pack_s(v), "adamw_small")
    s_sizes = [int(w[n].size) for n in small_names]

    for key, packed_s in (("grad", gs_packed), ("delta", s_delta), ("m", s_m), ("v", s_v)):
        for n, t in zip(small_names, _unpack_rows(packed_s, s_sizes)):
            out[key][n] = t.reshape(w[n].shape)
    return (loss, grad_x, *[out["grad"][n] for n in order], *[out["delta"][n] for n in order],
            *[out["m"][n] for n in order], *[out["v"][n] for n in order])
```
